```python
import math
import jax, jax.numpy as jnp
from jax import lax
import numpy as np

D_MODEL = 1024
BATCH = 32
SEQ = 2048
DEPTH = 4

POOL_WINDOWS = (2, 4, 8, 16)
N_POOL_GROUPS = 4
POOL_GROUP = D_MODEL // N_POOL_GROUPS
HEAD_DIM = 64
N_HEADS = D_MODEL // HEAD_DIM
N_KV_HEADS = 4
GROUP_SIZE = N_HEADS // N_KV_HEADS
N_BRANCH = 3
CMP_LEN = 32
CMP_STRIDE = 16
CMP_HIDDEN = 4 * HEAD_DIM
SLC_LEN = 64
N_SEL = 4
WINDOW = 256
Q_BLOCK = 128
FORCED_SCORE = 1.0e4
N_BUCKETS = 32
MAX_DISTANCE = 128
D_FF = (7 * D_MODEL) // 2
N_EXPERTS = 8
TOP_K = 2
EPS = 1e-6
NEG = -1e30
TINY = 1e-30

kernel_name = "yoco_pool_nsa_moe_hybrid"


def rmsnorm(x, g):
    xf = x.astype(jnp.float32)
    y = xf * lax.rsqrt(jnp.mean(xf * xf, axis=-1, keepdims=True) + EPS)
    return (y * g.astype(jnp.float32)).astype(x.dtype)


def masked_softmax(s, mask):
    s = jnp.where(mask, s, NEG)
    m = jnp.max(s, axis=-1, keepdims=True)
    p = jnp.where(mask, jnp.exp(s - m), 0.0)
    return p / jnp.maximum(jnp.sum(p, axis=-1, keepdims=True), TINY)


def rel_bucket(dist):
    n = jnp.maximum(dist, 0)
    max_exact = N_BUCKETS // 2
    nf = jnp.maximum(n, 1).astype(jnp.float32)
    large = max_exact + (jnp.log(nf / max_exact) / math.log(MAX_DISTANCE / max_exact)
                         * (N_BUCKETS - max_exact)).astype(jnp.int32)
    large = jnp.minimum(large, N_BUCKETS - 1)
    return jnp.where(n < max_exact, n, large)


def pool_mixer(h, w_grp, scale):
    B, S, D = h.shape
    hg = h.reshape(B, S, N_POOL_GROUPS, POOL_GROUP)
    csum = jnp.cumsum(hg.astype(jnp.float32), axis=1)
    t = jnp.arange(S)
    pooled = []
    for gi, w in enumerate(POOL_WINDOWS):
        c = csum[:, :, gi]
        prev = jnp.pad(c, ((0, 0), (w, 0), (0, 0)))[:, :S]
        cnt = jnp.minimum(t + 1, w).astype(jnp.float32)[None, :, None]
        pooled.append((c - prev) / cnt)
    pooled = jnp.stack(pooled, axis=2)
    diff = (pooled - hg.astype(jnp.float32)).astype(h.dtype)
    y = jnp.einsum('bsgc,gcd->bsgd', diff, w_grp).reshape(B, S, D)
    return y * scale


def swiglu(h, wg, wu, wd):
    return (jax.nn.silu(h @ wg) * (h @ wu)) @ wd


def moe(h, router, wg, wu, wd):
    B, S, D = h.shape
    xf = h.reshape(-1, D)
    logits = jnp.einsum('nd,de->ne', xf, router, preferred_element_type=jnp.float32)
    top_val, top_idx = lax.top_k(logits, TOP_K)
    gate = jax.nn.softmax(top_val, axis=-1)
    flat_e = top_idx.reshape(-1)
    order = jnp.argsort(flat_e)
    tok = order // TOP_K
    sizes = jnp.bincount(flat_e, length=N_EXPERTS).astype(jnp.int32)
    xs = xf[tok]
    hid = jax.nn.silu(lax.ragged_dot(xs, wg, sizes)) * lax.ragged_dot(xs, wu, sizes)
    ys = lax.ragged_dot(hid, wd, sizes) * gate.reshape(-1)[order][:, None].astype(h.dtype)
    return jnp.zeros_like(xf).at[tok].add(ys).reshape(B, S, D)


def build_shared_kv(x, norm_kv, w_kv, g_k, cmp_pos, cmp_w1, cmp_w2):
    B, S, _ = x.shape
    y = rmsnorm(x, norm_kv)
    kv = (y @ w_kv).reshape(B, S, 2 * N_BRANCH, N_KV_HEADS, HEAD_DIM)
    k_c, v_c, k_s, v_s, k_w, v_w = [kv[:, :, i] for i in range(2 * N_BRANCH)]
    nc = (S - CMP_LEN) // CMP_STRIDE + 1
    idx = np.arange(nc)[:, None] * CMP_STRIDE + np.arange(CMP_LEN)[None, :]

    def compress(z, pos, w1, w2):
        blocks = z[:, idx] + pos[None, None, :, None, :]
        blocks = blocks.transpose(0, 1, 3, 2, 4).reshape(B, nc, N_KV_HEADS, CMP_LEN * HEAD_DIM)
        return jax.nn.silu(blocks @ w1) @ w2

    k_cmp = rmsnorm(compress(k_c, cmp_pos[0], cmp_w1[0], cmp_w2[0]), g_k[0])
    v_cmp = compress(v_c, cmp_pos[1], cmp_w1[1], cmp_w2[1])
    nsb = S // SLC_LEN

    def to_blocks(z):
        return (z.reshape(B, nsb, SLC_LEN, N_KV_HEADS, HEAD_DIM)
                 .transpose(0, 3, 1, 2, 4).reshape(B, N_KV_HEADS, nsb, SLC_LEN * HEAD_DIM))

    k_slc = to_blocks(rmsnorm(k_s, g_k[1]))
    v_slc = to_blocks(v_s)
    pad = ((0, 0), (WINDOW, 0), (0, 0), (0, 0))
    k_win = jnp.pad(rmsnorm(k_w, g_k[2]), pad)
    v_win = jnp.pad(v_w, pad)
    return (k_cmp, v_cmp, k_slc, v_slc, k_win, v_win)


def nsa_mixer(h, w_qg, g_q, w_o, rel_bias, k_cmp, v_cmp, k_slc, v_slc, k_win, v_win):
    B, S, _ = h.shape
    HD = N_HEADS * HEAD_DIM
    G, R, DH = N_KV_HEADS, GROUP_SIZE, HEAD_DIM
    qg = h @ w_qg
    q = rmsnorm(qg[..., :HD].reshape(B, S, G, R, DH), g_q) * (DH ** -0.5)
    gates = jax.nn.sigmoid(qg[..., HD:].astype(jnp.float32)).reshape(B, S, G, R, N_BRANCH).astype(h.dtype)
    nc = k_cmp.shape[1]
    nsb = S // SLC_LEN
    n_sel = min(N_SEL, nsb)
    cmp_start = np.arange(nc) * CMP_STRIDE
    cmp_end = jnp.asarray(cmp_start + CMP_LEN - 1, dtype=jnp.int32)
    slc_start = np.arange(nsb) * SLC_LEN
    overlap = np.clip(np.minimum(cmp_start[:, None] + CMP_LEN, slc_start[None, :] + SLC_LEN)
                      - np.maximum(cmp_start[:, None], slc_start[None, :]), 0, None) / CMP_LEN
    cmp_to_slc = jnp.asarray(overlap, dtype=jnp.float32)
    bias_tab = rel_bias.astype(jnp.float32).reshape(N_BUCKETS, G, R)
    bias_tab_g = bias_tab.transpose(1, 0, 2)
    g_idx = jnp.arange(G)
    jb = jnp.arange(nsb)

    def block(qb):
        t0 = qb * Q_BLOCK
        tq = t0 + jnp.arange(Q_BLOCK)
        qblk = lax.dynamic_slice_in_dim(q, t0, Q_BLOCK, axis=1)
        gblk = lax.dynamic_slice_in_dim(gates, t0, Q_BLOCK, axis=1)
        dc = tq[:, None] - cmp_end[None, :]
        s = jnp.einsum('bqgrd,bcgd->bgrqc', qblk, k_cmp, preferred_element_type=jnp.float32)
        s = s + bias_tab[rel_bucket(dc)].transpose(2, 3, 0, 1)
        p_cmp = masked_softmax(s, dc >= 0)
        o_cmp = jnp.einsum('bgrqc,bcgd->bqgrd', p_cmp.astype(v_cmp.dtype), v_cmp)
        imp = jnp.einsum('bgrqc,cj->bqgj', p_cmp, cmp_to_slc)
        blk_q = tq // SLC_LEN
        valid = jb[None, :] <= blk_q[:, None]
        forced = (jb[None, :] == 0) | (jb[None, :] == blk_q[:, None]) | (jb[None, :] == blk_q[:, None] - 1)
        score = jnp.where(forced[None, :, None, :], FORCED_SCORE,
                          jnp.where(valid[None, :, None, :], imp, NEG))
        _, sel = lax.top_k(score, n_sel)
        sel_t = sel.transpose(0, 2, 1, 3)
        flat = sel_t.reshape(B, G, Q_BLOCK * n_sel)[..., None]
        ks = jnp.take_along_axis(k_slc, flat, axis=2).reshape(B, G, Q_BLOCK, n_sel, SLC_LEN, DH)
        vs = jnp.take_along_axis(v_slc, flat, axis=2).reshape(B, G, Q_BLOCK, n_sel, SLC_LEN, DH)
        pos = sel_t[..., None] * SLC_LEN + jnp.arange(SLC_LEN)
        ds = tq[None, None, :, None, None] - pos
        bias_s = bias_tab_g[g_idx[None, :, None, None, None], rel_bucket(ds)]
        s = jnp.einsum('bqgrd,bgqnld->bgrqnl', qblk, ks, preferred_element_type=jnp.float32)
        s = (s + bias_s.transpose(0, 1, 5, 2, 3, 4)).reshape(B, G, R, Q_BLOCK, n_sel * SLC_LEN)
        mask_s = (ds >= 0).reshape(B, G, 1, Q_BLOCK, n_sel * SLC_LEN)
        p = masked_softmax(s, mask_s).reshape(B, G, R, Q_BLOCK, n_sel, SLC_LEN)
        o_slc = jnp.einsum('bgrqnl,bgqnld->bqgrd', p.astype(vs.dtype), vs)
        kw = lax.dynamic_slice_in_dim(k_win, t0, Q_BLOCK + WINDOW, axis=1)
        vw = lax.dynamic_slice_in_dim(v_win, t0, Q_BLOCK + WINDOW, axis=1)
        kpos = t0 - WINDOW + jnp.arange(Q_BLOCK + WINDOW)
        dw = tq[:, None] - kpos[None, :]
        mw = (dw >= 0) & (dw < WINDOW) & (kpos[None, :] >= 0)
        s = jnp.einsum('bqgrd,bkgd->bgrqk', qblk, kw, preferred_element_type=jnp.float32)
        s = s + bias_tab[rel_bucket(dw)].transpose(2, 3, 0, 1)
        p = masked_softmax(s, mw)
        o_win = jnp.einsum('bgrqk,bkgd->bqgrd', p.astype(vw.dtype), vw)
        return gblk[..., 0:1] * o_cmp + gblk[..., 1:2] * o_slc + gblk[..., 2:3] * o_win

    out = lax.map(block, jnp.arange(S // Q_BLOCK))
    out = out.transpose(1, 0, 2, 3, 4, 5).reshape(B, S, HD)
    return out @ w_o


def setup_inputs(seed: int = 0) -> dict:
    key = jax.random.key(seed)
    ks = iter(jax.random.split(key, 32))
    f32 = jnp.float32

    def nrm(shape, fan_in):
        return jax.random.normal(next(ks), shape, f32) * (fan_in ** -0.5)

    def gain(shape):
        return 1.0 + 0.05 * jax.random.normal(next(ks), shape, f32)

    n_a = DEPTH // 2
    n_b = DEPTH - n_a
    n_dense = (DEPTH + 1) // 2
    n_moe = DEPTH // 2
    HD = N_HEADS * HEAD_DIM
    return {
        "x": jax.random.normal(next(ks), (BATCH, SEQ, D_MODEL), f32),
        "rel_bias": 0.5 * jax.random.normal(next(ks), (N_BUCKETS, N_HEADS), f32),
        "norm_mix": gain((DEPTH, D_MODEL)),
        "norm_ffn": gain((DEPTH, D_MODEL)),
        "pool_w": nrm((n_a, N_POOL_GROUPS, POOL_GROUP, POOL_GROUP), POOL_GROUP),
        "pool_scale": gain((n_a, D_MODEL)),
        "norm_kv": gain((D_MODEL,)),
        "w_kv": nrm((D_MODEL, 2 * N_BRANCH * N_KV_HEADS * HEAD_DIM), D_MODEL),
        "g_k": gain((N_BRANCH, HEAD_DIM)),
        "cmp_pos": 0.1 * jax.random.normal(next(ks), (2, CMP_LEN, HEAD_DIM), f32),
        "cmp_w1": nrm((2, CMP_LEN * HEAD_DIM, CMP_HIDDEN), CMP_LEN * HEAD_DIM),
        "cmp_w2": nrm((2, CMP_HIDDEN, HEAD_DIM), CMP_HIDDEN),
        "w_qg": nrm((n_b, D_MODEL, HD + N_BRANCH * N_HEADS), D_MODEL),
        "g_q": gain((n_b, HEAD_DIM)),
        "w_o": nrm((n_b, HD, D_MODEL), HD),
        "ffn_wg": nrm((n_dense, D_MODEL, D_FF), D_MODEL),
        "ffn_wu": nrm((n_dense, D_MODEL, D_FF), D_MODEL),
        "ffn_wd": nrm((n_dense, D_FF, D_MODEL), D_FF),
        "router": nrm((n_moe, D_MODEL, N_EXPERTS), D_MODEL),
        "moe_wg": nrm((n_moe, N_EXPERTS, D_MODEL, D_FF), D_MODEL),
        "moe_wu": nrm((n_moe, N_EXPERTS, D_MODEL, D_FF), D_MODEL),
        "moe_wd": nrm((n_moe, N_EXPERTS, D_FF, D_MODEL), D_FF),
    }


def reference(x, rel_bias, norm_mix, norm_ffn, pool_w, pool_scale, norm_kv, w_kv, g_k,
              cmp_pos, cmp_w1, cmp_w2, w_qg, g_q, w_o, ffn_wg, ffn_wu, ffn_wd,
              router, moe_wg, moe_wu, moe_wd):
    n_a = DEPTH // 2
    shared = None
    for layer in range(DEPTH):
        h = rmsnorm(x, norm_mix[layer])
        if layer < n_a:
            x = x + pool_mixer(h, pool_w[layer], pool_scale[layer])
        else:
            j = layer - n_a
            x = x + nsa_mixer(h, w_qg[j], g_q[j], w_o[j], rel_bias, *shared)
        h = rmsnorm(x, norm_ffn[layer])
        i = layer // 2
        if layer % 2 == 0:
            x = x + swiglu(h, ffn_wg[i], ffn_wu[i], ffn_wd[i])
        else:
            x = x + moe(h, router[i], moe_wg[i], moe_wu[i], moe_wd[i])
        if layer == n_a - 1:
            shared = build_shared_kv(x, norm_kv, w_kv, g_k, cmp_pos, cmp_w1, cmp_w2)
    return x
```

```python
import functools
import math

import numpy as np
import jax
import jax.numpy as jnp
from jax import lax
from jax.experimental import pallas as pl
from jax.experimental.pallas import tpu as pltpu

f32 = jnp.float32
bf16 = jnp.bfloat16
i32 = jnp.int32
u32 = jnp.uint32

POOL_WINDOWS = (2, 4, 8, 16)
HEAD_DIM = 64
N_KV_HEADS = 4
N_BRANCH = 3
CMP_LEN = 32
CMP_STRIDE = 16
CMP_HIDDEN = 4 * HEAD_DIM
SLC_LEN = 64
N_SEL = 4
WINDOW = 256
Q_BLOCK = 128
FORCED_SCORE = 1.0e4
N_BUCKETS = 32
MAX_DISTANCE = 128
N_EXPERTS = 8
TOP_K = 2
EPS = 1e-6
NEG = -1e30
TINY = 1e-30

KEY_TILE = Q_BLOCK
POOL_HALO = 16
NH_PAD = 16
V7X_VMEM_LIMIT = 56 * 1024 * 1024

assert CMP_LEN == 2 * CMP_STRIDE and KEY_TILE == 2 * SLC_LEN and WINDOW == 2 * KEY_TILE
assert max(POOL_WINDOWS) <= POOL_HALO


def _cparams(*sem):
    return pltpu.CompilerParams(dimension_semantics=sem, vmem_limit_bytes=V7X_VMEM_LIMIT)


def _rms(xf, g):
    ms = jnp.mean(xf * xf, axis=-1, keepdims=True)
    return (xf * lax.rsqrt(ms + EPS)) * g


def _dot(a, b):
    return jnp.dot(a, b, preferred_element_type=f32)


def _dot_hilo(a, b):
    hi = a.astype(bf16)
    lo = (a - hi.astype(f32)).astype(bf16)
    return _dot(hi, b) + _dot(lo, b)


def _head_rms(z, gvec, seg, segt):
    ssq = _dot_hilo(z * z, seg)
    inv = lax.rsqrt(ssq * (1.0 / HEAD_DIM) + EPS)
    return (z * _dot_hilo(inv, segt)) * gvec


def _silu(a):
    return a * jax.nn.sigmoid(a)


def _seg_mats(width):
    heads = width // HEAD_DIM
    seg = np.zeros((width, NH_PAD), np.float32)
    seg[np.arange(width), np.arange(width) // HEAD_DIM] = 1.0
    assert heads <= NH_PAD
    return jnp.asarray(seg, bf16), jnp.asarray(seg.T, bf16)


def _pool_kernel(x_ref, halo_ref, g_ref, w_ref, scale_ref, o_ref, *, tp, cg):
    i = pl.program_id(1)
    x = x_ref[0]
    xh = jnp.concatenate([halo_ref[0], x], axis=0)
    h = _rms(xh, g_ref[...])
    row = lax.broadcasted_iota(i32, (tp + POOL_HALO, 1), 0)
    t_abs = i * tp + row - POOL_HALO
    h = jnp.where(t_abs >= 0, h, 0.0)
    outs = []
    for gi, w in enumerate(POOL_WINDOWS):
        hg = h[:, gi * cg:(gi + 1) * cg]
        s = hg
        sh = 1
        while sh < w:
            s = s + pltpu.roll(s, sh, axis=0)
            sh *= 2
        cnt = jnp.clip(t_abs + 1, 1, w).astype(f32)
        diff = (s / cnt - hg)[POOL_HALO:]
        outs.append(_dot(diff.astype(bf16), w_ref[gi]))
    y = jnp.concatenate(outs, axis=1)
    o_ref[0] = x + y * scale_ref[...]


def _pool_layer(x3, g, w_grp, scale):
    B, S, D = x3.shape
    tp = min(512, S)
    cg = D // len(POOL_WINDOWS)
    assert S % tp == 0 and tp % POOL_HALO == 0 and all(w & (w - 1) == 0 for w in POOL_WINDOWS)
    hb = tp // POOL_HALO
    return pl.pallas_call(
        functools.partial(_pool_kernel, tp=tp, cg=cg),
        out_shape=jax.ShapeDtypeStruct((B, S, D), f32),
        grid=(B, S // tp),
        in_specs=[
            pl.BlockSpec((1, tp, D), lambda b, i: (b, i, 0)),
            pl.BlockSpec((1, POOL_HALO, D), lambda b, i: (b, jnp.maximum(i * hb - 1, 0), 0)),
            pl.BlockSpec((1, D), lambda b, i: (0, 0)),
            pl.BlockSpec((len(POOL_WINDOWS), cg, cg), lambda b, i: (0, 0, 0)),
            pl.BlockSpec((1, D), lambda b, i: (0, 0)),
        ],
        out_specs=pl.BlockSpec((1, tp, D), lambda b, i: (b, i, 0)),
        compiler_params=_cparams("parallel", "arbitrary"),
        name="pool_layer",
    )(x3, x3, g.reshape(1, D), w_grp.astype(bf16), scale.reshape(1, D))


def _ffn_kernel(x_ref, g_ref, wg_ref, wu_ref, wd_ref, o_ref, h_ref, acc_ref):
    j = pl.program_id(1)

    @pl.when(j == 0)
    def _():
        x = x_ref[...]
        h_ref[...] = _rms(x, g_ref[...]).astype(bf16)
        acc_ref[...] = x

    h = h_ref[...]
    act = _silu(_dot(h, wg_ref[...])) * _dot(h, wu_ref[...])
    acc_ref[...] += _dot(act.astype(bf16), wd_ref[...])

    @pl.when(j == pl.num_programs(1) - 1)
    def _():
        o_ref[...] = acc_ref[...]


def _ffn_layer(xf, g, wg, wu, wd):
    N, D = xf.shape
    F = wg.shape[1]
    tm = min(1024, N)
    tf = 512
    assert N % tm == 0 and F % tf == 0
    return pl.pallas_call(
        _ffn_kernel,
        out_shape=jax.ShapeDtypeStruct((N, D), f32),
        grid=(N // tm, F // tf),
        in_specs=[
            pl.BlockSpec((tm, D), lambda i, j: (i, 0)),
            pl.BlockSpec((1, D), lambda i, j: (0, 0)),
            pl.BlockSpec((D, tf), lambda i, j: (0, j)),
            pl.BlockSpec((D, tf), lambda i, j: (0, j)),
            pl.BlockSpec((tf, D), lambda i, j: (j, 0)),
        ],
        out_specs=pl.BlockSpec((tm, D), lambda i, j: (i, 0)),
        scratch_shapes=[pltpu.VMEM((tm, D), bf16), pltpu.VMEM((tm, D), f32)],
        compiler_params=_cparams("parallel", "arbitrary"),
        name="ffn_dense",
    )(xf, g.reshape(1, D), wg.astype(bf16), wu.astype(bf16), wd.astype(bf16))


def _router_kernel(x_ref, g_ref, rt_ref, hp_ref, idx_ref, gate_ref):
    h = _rms(x_ref[...], g_ref[...])
    half = h.shape[1] // 2
    bits = pltpu.bitcast(h.astype(bf16).astype(f32), u32)
    hp_ref[...] = (bits[:, :half] & jnp.uint32(0xFFFF0000)) | (bits[:, half:] >> 16)
    logits = lax.dot_general(rt_ref[...], h, (((1,), (1,)), ((), ())),
                             precision=lax.Precision.HIGHEST, preferred_element_type=f32)
    ne = logits.shape[0]
    row = lax.broadcasted_iota(i32, logits.shape, 0)
    m1 = jnp.max(logits, axis=0, keepdims=True)
    i1 = jnp.min(jnp.where(logits == m1, row, ne), axis=0, keepdims=True)
    rest = jnp.where(row == i1, -jnp.inf, logits)
    m2 = jnp.max(rest, axis=0, keepdims=True)
    i2 = jnp.min(jnp.where(rest == m2, row, ne), axis=0, keepdims=True)
    e2 = jnp.exp(m2 - m1)
    den = 1.0 + e2
    idx_ref[...] = jnp.concatenate([i1, i2], axis=0)
    gate_ref[...] = jnp.concatenate([1.0 / den, e2 / den], axis=0)


def _router(xf, g, router):
    N, D = xf.shape
    E = router.shape[1]
    tr = min(1024, N)
    assert N % tr == 0 and TOP_K == 2
    return pl.pallas_call(
        _router_kernel,
        out_shape=(jax.ShapeDtypeStruct((N, D // 2), u32),
                   jax.ShapeDtypeStruct((TOP_K, N), i32),
                   jax.ShapeDtypeStruct((TOP_K, N), f32)),
        grid=(N // tr,),
        in_specs=[
            pl.BlockSpec((tr, D), lambda i: (i, 0)),
            pl.BlockSpec((1, D), lambda i: (0, 0)),
            pl.BlockSpec((E, D), lambda i: (0, 0)),
        ],
        out_specs=(pl.BlockSpec((tr, D // 2), lambda i: (i, 0)),
                   pl.BlockSpec((TOP_K, tr), lambda i: (0, i)),
                   pl.BlockSpec((TOP_K, tr), lambda i: (0, i))),
        compiler_params=_cparams("parallel"),
        name="moe_router",
    )(xf, g.reshape(1, D), router.T)


def _row_copy(idx_smem, src_hbm, dst_ref, sem, r, slot):
    return pltpu.make_async_copy(src_hbm.at[pl.ds(idx_smem[slot], 1)], dst_ref.at[pl.ds(r, 1)], sem)


def _gather_rows_kernel(idx_hbm, src_hbm, o_ref, idx_smem, sem_idx, sem, *, tg):
    i = pl.program_id(0)
    cp = pltpu.make_async_copy(idx_hbm.at[pl.ds(i * tg, tg)], idx_smem, sem_idx)
    cp.start()
    cp.wait()

    def start(r, c):
        _row_copy(idx_smem, src_hbm, o_ref, sem, r, r).start()
        return c

    def wait(r, c):
        _row_copy(idx_smem, src_hbm, o_ref, sem, r, r).wait()
        return c

    lax.fori_loop(0, tg, start, 0, unroll=8)
    lax.fori_loop(0, tg, wait, 0, unroll=8)


def _gather_rows(row_idx, src):
    P = row_idx.shape[0]
    W = src.shape[1]
    tg = min(512, P)
    assert P % tg == 0
    return pl.pallas_call(
        functools.partial(_gather_rows_kernel, tg=tg),
        out_shape=jax.ShapeDtypeStruct((P, W), src.dtype),
        grid=(P // tg,),
        in_specs=[pl.BlockSpec(memory_space=pl.ANY), pl.BlockSpec(memory_space=pl.ANY)],
        out_specs=pl.BlockSpec((tg, W), lambda i: (i, 0)),
        scratch_shapes=[pltpu.SMEM((tg,), i32), pltpu.SemaphoreType.DMA(()), pltpu.SemaphoreType.DMA(())],
        compiler_params=_cparams("arbitrary"),
        name="moe_dispatch",
    )(row_idx, src)


def _gmm_kernel(te_ref, nu_ref, xs_ref, gate_ref, wg_ref, wu_ref, wd_ref, o_ref, h_ref, acc_ref):
    i = pl.program_id(0)
    j = pl.program_id(1)

    @pl.when(i < nu_ref[0])
    def _():
        @pl.when(j == 0)
        def _():
            w = xs_ref[...]
            left = pltpu.bitcast(w & jnp.uint32(0xFFFF0000), f32)
            right = pltpu.bitcast(w << 16, f32)
            h_ref[...] = jnp.concatenate([left, right], axis=1).astype(bf16)
            acc_ref[...] = jnp.zeros_like(acc_ref)

        h = h_ref[...]
        act = _silu(_dot(h, wg_ref[0])) * _dot(h, wu_ref[0])
        acc_ref[...] += _dot(act.astype(bf16), wd_ref[0])

        @pl.when(j == pl.num_programs(1) - 1)
        def _():
            o_ref[...] = acc_ref[...] * gate_ref[...]

    @pl.when((i >= nu_ref[0]) & (j == 0))
    def _():
        o_ref[...] = jnp.zeros_like(o_ref)


def _gmm(tile_expert, n_used, xs, row_gate, wg, wu, wd, tm):
    P, half = xs.shape
    D = 2 * half
    F = wg.shape[2]
    tf = 512
    assert P % tm == 0 and F % tf == 0

    nf = F // tf

    def row_map(i, j, te, nu):
        return (jnp.minimum(i, nu[0] - 1), 0)

    def ff(i, j, nu):
        return jnp.where(i < nu[0], j, nf - 1)

    grid_spec = pltpu.PrefetchScalarGridSpec(
        num_scalar_prefetch=2,
        grid=(P // tm, nf),
        in_specs=[
            pl.BlockSpec((tm, half), row_map),
            pl.BlockSpec((tm, 1), row_map),
            pl.BlockSpec((1, D, tf), lambda i, j, te, nu: (te[i], 0, ff(i, j, nu))),
            pl.BlockSpec((1, D, tf), lambda i, j, te, nu: (te[i], 0, ff(i, j, nu))),
            pl.BlockSpec((1, tf, D), lambda i, j, te, nu: (te[i], ff(i, j, nu), 0)),
        ],
        out_specs=pl.BlockSpec((tm, D), lambda i, j, te, nu: (i, 0)),
        scratch_shapes=[pltpu.VMEM((tm, D), bf16), pltpu.VMEM((tm, D), f32)],
    )
    return pl.pallas_call(
        _gmm_kernel,
        out_shape=jax.ShapeDtypeStruct((P, D), f32),
        grid_spec=grid_spec,
        compiler_params=_cparams("arbitrary", "arbitrary"),
        name="moe_gmm",
    )(tile_expert, n_used, xs, row_gate, wg.astype(bf16), wu.astype(bf16), wd.astype(bf16))


def _combine_kernel(pos_hbm, x_ref, ys_hbm, o_ref, idx_smem, buf_ref, sem_idx, sem, *, tc, n_tok):
    i = pl.program_id(0)
    for k in range(TOP_K):
        cp = pltpu.make_async_copy(pos_hbm.at[pl.ds(k * n_tok + i * tc, tc)], idx_smem.at[pl.ds(k * tc, tc)], sem_idx)
        cp.start()
        cp.wait()

    def start(r, c):
        for k in range(TOP_K):
            _row_copy(idx_smem, ys_hbm, buf_ref.at[k], sem, r, k * tc + r).start()
        return c

    def wait(r, c):
        for k in range(TOP_K):
            _row_copy(idx_smem, ys_hbm, buf_ref.at[k], sem, r, k * tc + r).wait()
        return c

    lax.fori_loop(0, tc, start, 0, unroll=8)
    lax.fori_loop(0, tc, wait, 0, unroll=8)
    acc = x_ref[...]
    for k in range(TOP_K):
        acc = acc + buf_ref[k]
    o_ref[...] = acc


def _combine(pos_flat, xf, ys):
    N, D = xf.shape
    tc = min(256, N)
    assert N % tc == 0
    return pl.pallas_call(
        functools.partial(_combine_kernel, tc=tc, n_tok=N),
        out_shape=jax.ShapeDtypeStruct((N, D), f32),
        grid=(N // tc,),
        in_specs=[pl.BlockSpec(memory_space=pl.ANY),
                  pl.BlockSpec((tc, D), lambda i: (i, 0)),
                  pl.BlockSpec(memory_space=pl.ANY)],
        out_specs=pl.BlockSpec((tc, D), lambda i: (i, 0)),
        scratch_shapes=[pltpu.SMEM((TOP_K * tc,), i32), pltpu.VMEM((TOP_K, tc, D), f32),
                        pltpu.SemaphoreType.DMA(()), pltpu.SemaphoreType.DMA(())],
        compiler_params=_cparams("arbitrary"),
        name="moe_combine",
    )(pos_flat, xf, ys)


def _moe_layer(xf, g, router, wg, wu, wd):
    N, D = xf.shape
    E = router.shape[1]
    tm = min(1024, N)
    hp, idx, gate = _router(xf, g, router)
    e_flat = idx.reshape(-1)
    onehot = (e_flat[:, None] == jnp.arange(E, dtype=i32)[None, :]).astype(i32)
    csum = jnp.cumsum(onehot, axis=0)
    counts = csum[-1]
    rank = jnp.sum(csum * onehot, axis=1) - 1
    padded = ((counts + tm - 1) // tm) * tm
    ends = jnp.cumsum(padded)
    starts = ends - padded
    pos = (starts[e_flat] + rank).astype(i32)
    P = TOP_K * N + E * tm
    n_used = (ends[-1] // tm).astype(i32)
    tok = jnp.tile(jnp.arange(N, dtype=i32), TOP_K)
    row_tok = jnp.zeros((P,), i32).at[pos].set(tok)
    row_gate = jnp.zeros((P,), f32).at[pos].set(gate.reshape(-1))
    tile_start = jnp.minimum(jnp.arange(P // tm, dtype=i32), n_used - 1) * tm
    tile_expert = jnp.sum((ends[None, :] <= tile_start[:, None]).astype(i32), axis=1)
    xs = _gather_rows(row_tok, hp)
    ys = _gmm(tile_expert, n_used.reshape(1), xs, row_gate[:, None], wg, wu, wd, tm)
    return _combine(pos, xf, ys)


def _kv_kernel(x_ref, g_ref, w_ref, gk_ref, seg_ref, segt_ref, kc_ref, vc_ref, ks_ref, vs_ref, kw_ref, vw_ref):
    h = _rms(x_ref[...], g_ref[...]).astype(bf16)
    kv = _dot(h, w_ref[...])
    wd = kc_ref.shape[1]
    part = lambda p: kv[:, p * wd:(p + 1) * wd]
    seg, segt = seg_ref[...], segt_ref[...]
    kc_ref[...] = part(0)
    vc_ref[...] = part(1)
    ks_ref[...] = _head_rms(part(2), gk_ref[0:1, :], seg, segt).astype(bf16)
    vs_ref[...] = part(3).astype(bf16)
    kw_ref[...] = _head_rms(part(4), gk_ref[1:2, :], seg, segt).astype(bf16)
    vw_ref[...] = part(5).astype(bf16)


def _kv_project(xf, norm_kv, w_kv, g_k):
    N, D = xf.shape
    wd = N_KV_HEADS * HEAD_DIM
    tk = min(512, N)
    assert N % tk == 0 and w_kv.shape[1] == 2 * N_BRANCH * wd
    seg, segt = _seg_mats(wd)
    gk = jnp.stack([jnp.tile(g_k[1], N_KV_HEADS), jnp.tile(g_k[2], N_KV_HEADS)])
    row = lambda i: (i, 0)
    const = lambda i: (0, 0)
    return pl.pallas_call(
        _kv_kernel,
        out_shape=(jax.ShapeDtypeStruct((N, wd), f32), jax.ShapeDtypeStruct((N, wd), f32))
        + tuple(jax.ShapeDtypeStruct((N, wd), bf16) for _ in range(4)),
        grid=(N // tk,),
        in_specs=[
            pl.BlockSpec((tk, D), row),
            pl.BlockSpec((1, D), const),
            pl.BlockSpec(w_kv.shape, const),
            pl.BlockSpec((2, wd), const),
            pl.BlockSpec((wd, NH_PAD), const),
            pl.BlockSpec((NH_PAD, wd), const),
        ],
        out_specs=tuple(pl.BlockSpec((tk, wd), row) for _ in range(6)),
        compiler_params=_cparams("parallel"),
        name="kv_project",
    )(xf, norm_kv.reshape(1, D), w_kv.astype(bf16), gk, seg, segt)


def _compress_kernel(c_ref, pos_ref, w1_ref, w2_ref, gk_ref, o_ref):
    kv = pl.program_id(0)
    c = c_ref[0, 0, 0]
    a = _dot((c + pos_ref[0, 0:1, :]).astype(bf16), w1_ref[0, 0])
    b = _dot((c + pos_ref[0, 1:2, :]).astype(bf16), w1_ref[0, 1])
    n = c.shape[0]
    hid = a + pltpu.roll(b, n - 1, axis=0)
    out = _dot(_silu(hid).astype(bf16), w2_ref[0])

    @pl.when(kv == 0)
    def _():
        o_ref[0, 0, 0] = _rms(out, gk_ref[...])

    @pl.when(kv != 0)
    def _():
        o_ref[0, 0, 0] = out


def _compress(kc, vc, B, S, cmp_pos, cmp_w1, cmp_w2, g_k0):
    G, DH = N_KV_HEADS, HEAD_DIM
    nch = S // CMP_STRIDE
    cw = CMP_STRIDE * DH

    def chunks(z):
        return z.reshape(B, nch, CMP_STRIDE, G, DH).transpose(0, 3, 1, 2, 4).reshape(B, G, nch, cw)

    c = jnp.stack([chunks(kc), chunks(vc)])
    pos = cmp_pos.reshape(2, 2, cw)
    w1 = cmp_w1.reshape(2, 2, cw, CMP_HIDDEN).astype(bf16)
    return pl.pallas_call(
        _compress_kernel,
        out_shape=jax.ShapeDtypeStruct((2, B, G, nch, DH), f32),
        grid=(2, B, G),
        in_specs=[
            pl.BlockSpec((1, 1, 1, nch, cw), lambda k, b, g: (k, b, g, 0, 0)),
            pl.BlockSpec((1, 2, cw), lambda k, b, g: (k, 0, 0)),
            pl.BlockSpec((1, 2, cw, CMP_HIDDEN), lambda k, b, g: (k, 0, 0, 0)),
            pl.BlockSpec((1, CMP_HIDDEN, DH), lambda k, b, g: (k, 0, 0)),
            pl.BlockSpec((1, DH), lambda k, b, g: (0, 0)),
        ],
        out_specs=pl.BlockSpec((1, 1, 1, nch, DH), lambda k, b, g: (k, b, g, 0, 0)),
        compiler_params=_cparams("arbitrary", "arbitrary", "arbitrary"),
        name="kv_compress",
    )(c, pos, w1, cmp_w2.astype(bf16), g_k0.reshape(1, DH))


def _build_shared(xf, B, S, norm_kv, w_kv, g_k, cmp_pos, cmp_w1, cmp_w2):
    G, DH = N_KV_HEADS, HEAD_DIM
    nkt = S // KEY_TILE
    kc, vc, ks, vs, kw, vw = _kv_project(xf, norm_kv, w_kv, g_k)
    cmp = _compress(kc, vc, B, S, cmp_pos, cmp_w1, cmp_w2, g_k[0]).astype(bf16)
    kct = cmp[0].transpose(0, 1, 3, 2)
    vcm = cmp[1]

    def keys_t(z):
        return z.reshape(B, nkt, KEY_TILE, G, DH).transpose(0, 3, 1, 4, 2)

    def vals(z):
        return z.reshape(B, S, G, DH).transpose(0, 2, 1, 3)

    return kct, vcm, keys_t(ks), vals(vs), keys_t(kw), vals(vw)


def _qg_kernel(x_ref, g_ref, wq_ref, wgate_ref, gq_ref, seg_ref, segt_ref, q_ref, gate_ref):
    h = _rms(x_ref[...], g_ref[...]).astype(bf16)
    q = _head_rms(_dot(h, wq_ref[...]), gq_ref[...], seg_ref[...], segt_ref[...])
    q_ref[...] = (q * (HEAD_DIM ** -0.5)).astype(bf16)
    gate_ref[...] = jax.nn.sigmoid(_dot(h, wgate_ref[...]))


def _qg_project(xf, g, w_qg, g_q):
    N, D = xf.shape
    HD = D
    ng = w_qg.shape[1] - HD
    tq = min(512, N)
    assert N % tq == 0
    seg, segt = _seg_mats(HD)
    row = lambda i: (i, 0)
    const = lambda i: (0, 0)
    return pl.pallas_call(
        _qg_kernel,
        out_shape=(jax.ShapeDtypeStruct((N, HD), bf16), jax.ShapeDtypeStruct((N, ng), f32)),
        grid=(N // tq,),
        in_specs=[
            pl.BlockSpec((tq, D), row),
            pl.BlockSpec((1, D), const),
            pl.BlockSpec((D, HD), const),
            pl.BlockSpec((D, ng), const),
            pl.BlockSpec((1, HD), const),
            pl.BlockSpec((HD, NH_PAD), const),
            pl.BlockSpec((NH_PAD, HD), const),
        ],
        out_specs=(pl.BlockSpec((tq, HD), row), pl.BlockSpec((tq, ng), row)),
        compiler_params=_cparams("parallel"),
        name="qg_project",
    )(xf, g.reshape(1, D), w_qg[:, :HD].astype(bf16), w_qg[:, HD:].astype(bf16),
      jnp.tile(g_q, HD // HEAD_DIM).reshape(1, HD), seg, segt)


def _stack_heads(z, r):
    return jnp.concatenate([z[:, k * HEAD_DIM:(k + 1) * HEAD_DIM] for k in range(r)], axis=0)


def _softmax_rows(s, mask):
    s = jnp.where(mask, s, NEG)
    m = jnp.max(s, axis=-1, keepdims=True)
    p = jnp.where(mask, jnp.exp(s - m), 0.0)
    return p / jnp.maximum(jnp.sum(p, axis=-1, keepdims=True), TINY)


def _nsa_kernel(q_ref, gt_ref, kct_ref, vc_ref, kst_ref, vs_ref, kwt_ref, vw_ref, cb_ref, tb_ref, c2s_ref, exp_ref,
                o_ref, sel_ref, *, r, nsb, n_sel):
    qb = pl.program_id(2)
    t0 = qb * Q_BLOCK
    T = Q_BLOCK
    RT = r * T
    nc = kct_ref.shape[3]
    nkt = kst_ref.shape[2]
    qs = _stack_heads(q_ref[...], r)

    qpos_c = t0 + (lax.broadcasted_iota(i32, (RT, nc), 0) & (T - 1))
    cend = lax.broadcasted_iota(i32, (RT, nc), 1) * CMP_STRIDE + (CMP_LEN - 1)
    s = _dot(qs, kct_ref[0, 0]) + cb_ref[0].reshape(RT, nc)
    p_cmp = _softmax_rows(s, qpos_c >= cend)
    o_cmp = _dot(p_cmp.astype(bf16), vc_ref[0, 0])

    psum = p_cmp[0:T]
    for k in range(1, r):
        psum = psum + p_cmp[k * T:(k + 1) * T]
    imp = _dot_hilo(psum, c2s_ref[...])
    jb = lax.broadcasted_iota(i32, (T, nsb), 1)
    blk_q = jnp.right_shift(t0 + lax.broadcasted_iota(i32, (T, nsb), 0), SLC_LEN.bit_length() - 1)
    forced = (jb == 0) | (jb == blk_q) | (jb == blk_q - 1)
    score = jnp.where(forced, FORCED_SCORE, jnp.where(jb <= blk_q, imp, NEG))
    sel = jnp.zeros((T, nsb), f32)
    for _ in range(n_sel):
        mx = jnp.max(score, axis=-1, keepdims=True)
        first = jnp.min(jnp.where(score == mx, jb, nsb), axis=-1, keepdims=True)
        hit = jb == first
        sel = jnp.where(hit, 1.0, sel)
        score = jnp.where(hit, -jnp.inf, score)
    selk = _dot(sel.astype(bf16), exp_ref[...])
    for kt in range(nkt):
        sel_ref[kt] = selk[:, kt * KEY_TILE:(kt + 1) * KEY_TILE]

    qi = lax.broadcasted_iota(i32, (RT, KEY_TILE), 0) & (T - 1)
    kj = lax.broadcasted_iota(i32, (RT, KEY_TILE), 1)

    def slc_step(kt, carry):
        m, l, acc = carry
        s = _dot(qs, kst_ref[0, 0, kt]) + tb_ref[0, jnp.minimum(qb - kt, 2)].reshape(RT, KEY_TILE)
        chosen = jnp.concatenate([sel_ref[kt]] * r, axis=0) > 0.5
        mask = chosen & ((qb - kt) * KEY_TILE + qi >= kj)
        s = jnp.where(mask, s, NEG)
        m_new = jnp.maximum(m, jnp.max(s, axis=-1, keepdims=True))
        p = jnp.where(mask, jnp.exp(s - m_new), 0.0)
        alpha = jnp.exp(m - m_new)
        l = alpha * l + jnp.sum(p, axis=-1, keepdims=True)
        v = vs_ref[0, 0, pl.ds(pl.multiple_of(kt * KEY_TILE, KEY_TILE), KEY_TILE), :]
        acc = alpha * acc + _dot(p.astype(bf16), v)
        return m_new, l, acc

    init = (jnp.full((RT, 1), NEG, f32), jnp.zeros((RT, 1), f32), jnp.zeros((RT, HEAD_DIM), f32))
    _, l, acc = lax.fori_loop(0, qb + 1, slc_step, init)
    o_slc = acc / jnp.maximum(l, TINY)

    nwt = WINDOW // KEY_TILE + 1
    scores, masks, vals = [], [], []
    for d in range(nwt):
        kt = qb - d
        ktc = jnp.maximum(kt, 0)
        dw = d * KEY_TILE + qi - kj
        scores.append(_dot(qs, kwt_ref[0, 0, ktc]) + tb_ref[0, min(d, 2)].reshape(RT, KEY_TILE))
        masks.append((dw >= 0) & (dw < WINDOW) & (kt * KEY_TILE + kj >= 0))
        vals.append(vw_ref[0, 0, pl.ds(pl.multiple_of(ktc * KEY_TILE, KEY_TILE), KEY_TILE), :])
    p_win = _softmax_rows(jnp.concatenate(scores, axis=1), jnp.concatenate(masks, axis=1))
    o_win = _dot(p_win[:, 0:KEY_TILE].astype(bf16), vals[0])
    for d in range(1, nwt):
        o_win = o_win + _dot(p_win[:, d * KEY_TILE:(d + 1) * KEY_TILE].astype(bf16), vals[d])

    gt = gt_ref[0]
    gcol = lambda br: jnp.concatenate([gt[:, N_BRANCH * k + br:N_BRANCH * k + br + 1] for k in range(r)], axis=0)
    o = gcol(0) * o_cmp + gcol(1) * o_slc + gcol(2) * o_win
    o_ref[...] = jnp.concatenate([o[k * T:(k + 1) * T] for k in range(r)], axis=1)


def _bias_tables(rel_bias, S):
    G = N_KV_HEADS
    H = rel_bias.shape[1]
    r = H // G
    T = KEY_TILE
    n = jnp.arange(S + 2 * T, dtype=i32)
    max_exact = N_BUCKETS // 2
    nf = jnp.maximum(n, 1).astype(f32)
    large = max_exact + (jnp.log(nf / max_exact) / math.log(MAX_DISTANCE / max_exact)
                         * (N_BUCKETS - max_exact)).astype(i32)
    bucket = jnp.where(n < max_exact, n, jnp.minimum(large, N_BUCKETS - 1))
    bias1d = rel_bias.astype(f32)[bucket]
    nch = S // CMP_STRIDE
    dc = np.maximum(np.arange(S)[:, None] - (np.arange(nch)[None, :] * CMP_STRIDE + CMP_LEN - 1), 0)
    cb = bias1d[dc].transpose(2, 0, 1).reshape(G, r, S, nch)
    ii, jj = np.arange(T)[:, None], np.arange(T)[None, :]
    dt = np.stack([np.maximum(d * T + ii - jj, 0) for d in range(3)])
    tb = bias1d[dt].transpose(3, 0, 1, 2).reshape(G, r, 3, T, T).transpose(0, 2, 1, 3, 4)
    return cb, tb


def _nsa_attention(q, gates_t, shared, cb, tb, B, S):
    kct, vcm, kst, vs, kwt, vw = shared
    N, HD = q.shape
    G, DH, T = N_KV_HEADS, HEAD_DIM, Q_BLOCK
    r = HD // (G * DH)
    nqb = S // T
    nch = S // CMP_STRIDE
    nsb = S // SLC_LEN
    nkt = S // KEY_TILE
    cmp_start = np.arange(nch) * CMP_STRIDE
    slc_start = np.arange(nsb) * SLC_LEN
    overlap = np.clip(np.minimum(cmp_start[:, None] + CMP_LEN, slc_start[None, :] + SLC_LEN)
                      - np.maximum(cmp_start[:, None], slc_start[None, :]), 0, None) / CMP_LEN
    overlap[nch - 1] = 0.0
    c2s = jnp.asarray(overlap, bf16)
    expand = jnp.asarray(np.arange(S)[None, :] // SLC_LEN == np.arange(nsb)[:, None], bf16)
    bg = lambda b, g, i: (b, g, 0, 0)
    return pl.pallas_call(
        functools.partial(_nsa_kernel, r=r, nsb=nsb, n_sel=min(N_SEL, nsb)),
        out_shape=jax.ShapeDtypeStruct((N, HD), f32),
        grid=(B, G, nqb),
        in_specs=[
            pl.BlockSpec((T, r * DH), lambda b, g, i: (b * nqb + i, g)),
            pl.BlockSpec((1, T, r * N_BRANCH), lambda b, g, i: (g, b * nqb + i, 0)),
            pl.BlockSpec((1, 1, DH, nch), bg),
            pl.BlockSpec((1, 1, nch, DH), bg),
            pl.BlockSpec((1, 1, nkt, DH, KEY_TILE), lambda b, g, i: (b, g, 0, 0, 0)),
            pl.BlockSpec((1, 1, S, DH), bg),
            pl.BlockSpec((1, 1, nkt, DH, KEY_TILE), lambda b, g, i: (b, g, 0, 0, 0)),
            pl.BlockSpec((1, 1, S, DH), bg),
            pl.BlockSpec((1, r, T, nch), lambda b, g, i: (g, 0, i, 0)),
            pl.BlockSpec((1, 3, r, T, T), lambda b, g, i: (g, 0, 0, 0, 0)),
            pl.BlockSpec((nch, nsb), lambda b, g, i: (0, 0)),
            pl.BlockSpec((nsb, S), lambda b, g, i: (0, 0)),
        ],
        out_specs=pl.BlockSpec((T, r * DH), lambda b, g, i: (b * nqb + i, g)),
        scratch_shapes=[pltpu.VMEM((nkt, T, KEY_TILE), f32)],
        compiler_params=_cparams("parallel", "parallel", "arbitrary"),
        name="nsa_attention",
    )(q, gates_t, kct, vcm, kst, vs, kwt, vw, cb, tb, c2s, expand)


def _oproj_kernel(x_ref, a_ref, w_ref, o_ref):
    o_ref[...] = x_ref[...] + _dot(a_ref[...].astype(bf16), w_ref[...])


def _out_project(xf, attn, w_o):
    N, D = xf.shape
    to = min(1024, N)
    assert N % to == 0
    row = lambda i: (i, 0)
    return pl.pallas_call(
        _oproj_kernel,
        out_shape=jax.ShapeDtypeStruct((N, D), f32),
        grid=(N // to,),
        in_specs=[pl.BlockSpec((to, D), row), pl.BlockSpec((to, attn.shape[1]), row),
                  pl.BlockSpec(w_o.shape, lambda i: (0, 0))],
        out_specs=pl.BlockSpec((to, D), row),
        compiler_params=_cparams("parallel"),
        name="out_project",
    )(xf, attn, w_o.astype(bf16))


def _nsa_layer(xf, B, S, g, w_qg, g_q, w_o, shared, cb, tb):
    N = xf.shape[0]
    G = N_KV_HEADS
    q, gates = _qg_project(xf, g, w_qg, g_q)
    gates_t = gates.reshape(N, G, gates.shape[1] // G).transpose(1, 0, 2)
    attn = _nsa_attention(q, gates_t, shared, cb, tb, B, S)
    return _out_project(xf, attn, w_o)


def kernel(x, rel_bias, norm_mix, norm_ffn, pool_w, pool_scale, norm_kv, w_kv, g_k, cmp_pos, cmp_w1, cmp_w2,
           w_qg, g_q, w_o, ffn_wg, ffn_wu, ffn_wd, router, moe_wg, moe_wu, moe_wd):
    B, S, D = x.shape
    depth = norm_mix.shape[0]
    n_a = depth // 2
    assert S % Q_BLOCK == 0
    xf = x.reshape(B * S, D)
    shared = None
    cb, tb = _bias_tables(rel_bias, S)
    for layer in range(depth):
        if layer < n_a:
            xf = _pool_layer(xf.reshape(B, S, D), norm_mix[layer], pool_w[layer], pool_scale[layer]).reshape(B * S, D)
        else:
            j = layer - n_a
            xf = _nsa_layer(xf, B, S, norm_mix[layer], w_qg[j], g_q[j], w_o[j], shared, cb, tb)
        i = layer // 2
        if layer % 2 == 0:
            xf = _ffn_layer(xf, norm_ffn[layer], ffn_wg[i], ffn_wu[i], ffn_wd[i])
        else:
            xf = _moe_layer(xf, norm_ffn[layer], router[i], moe_wg[i], moe_wu[i], moe_wd[i])
        if layer == n_a - 1:
            shared = _build_shared(xf, B, S, norm_kv, w_kv, g_k, cmp_pos, cmp_w1, cmp_w2)
    return xf.reshape(B, S, D)
```

```python
import functools
import math

import numpy as np
import jax
import jax.numpy as jnp
from jax import lax
from jax.experimental import pallas as pl
from jax.experimental.pallas import tpu as pltpu

f32 = jnp.float32
bf16 = jnp.bfloat16
i32 = jnp.int32
u32 = jnp.uint32

POOL_WINDOWS = (2, 4, 8, 16)
HEAD_DIM = 64
N_KV_HEADS = 4
N_BRANCH = 3
CMP_LEN = 32
CMP_STRIDE = 16
CMP_HIDDEN = 4 * HEAD_DIM
SLC_LEN = 64
N_SEL = 4
WINDOW = 256
Q_BLOCK = 128
FORCED_SCORE = 1.0e4
N_BUCKETS = 32
MAX_DISTANCE = 128
N_EXPERTS = 8
TOP_K = 2
EPS = 1e-6
NEG = -1e30
TINY = 1e-30
LOG2E = math.log2(math.e)
TB_OWN, TB_PREV, TB_WIN2, TB_NONE = range(4)

KEY_TILE = Q_BLOCK
SLC_UNROLL = 4
POOL_HALO = 16
NH_PAD = 16
V7X_VMEM_LIMIT = 56 * 1024 * 1024

CMP_PER_Q = Q_BLOCK // CMP_STRIDE
CMP_PAD = 2 * CMP_PER_Q
CMP_NEAR = 3 * CMP_PER_Q

assert CMP_LEN == 2 * CMP_STRIDE and KEY_TILE == 2 * SLC_LEN and WINDOW == 2 * KEY_TILE
assert max(POOL_WINDOWS) <= POOL_HALO
assert 2 * KEY_TILE - (Q_BLOCK - 1) >= MAX_DISTANCE
assert (CMP_PAD + 1) * CMP_STRIDE - (CMP_LEN - 1) >= MAX_DISTANCE and CMP_PER_Q % 8 == 0


def _cparams(*sem):
    return pltpu.CompilerParams(dimension_semantics=sem, vmem_limit_bytes=V7X_VMEM_LIMIT)


def _rms(xf, g):
    ms = jnp.mean(xf * xf, axis=-1, keepdims=True)
    return (xf * lax.rsqrt(ms + EPS)) * g


def _dot(a, b):
    return jnp.dot(a, b, preferred_element_type=f32)


def _dot_hilo(a, b):
    hi = a.astype(bf16)
    lo = (a - hi.astype(f32)).astype(bf16)
    return _dot(hi, b) + _dot(lo, b)


def _head_rms(z, gvec, seg, segt):
    ssq = _dot_hilo(z * z, seg)
    inv = lax.rsqrt(ssq * (1.0 / HEAD_DIM) + EPS)
    return (z * _dot_hilo(inv, segt)) * gvec


def _silu(a):
    return a * jax.nn.sigmoid(a)


def _seg_mats(width):
    heads = width // HEAD_DIM
    seg = np.zeros((width, NH_PAD), np.float32)
    seg[np.arange(width), np.arange(width) // HEAD_DIM] = 1.0
    assert heads <= NH_PAD
    return jnp.asarray(seg, bf16), jnp.asarray(seg.T, bf16)


def _pool_kernel(x_ref, halo_ref, g_ref, w_ref, scale_ref, o_ref, *, tp, cg):
    i = pl.program_id(1)
    x = x_ref[0]
    xh = jnp.concatenate([halo_ref[0], x], axis=0)
    h = _rms(xh, g_ref[...])
    row = lax.broadcasted_iota(i32, (tp + POOL_HALO, 1), 0)
    t_abs = i * tp + row - POOL_HALO
    h = jnp.where(t_abs >= 0, h, 0.0)
    outs = []
    for gi, w in enumerate(POOL_WINDOWS):
        hg = h[:, gi * cg:(gi + 1) * cg]
        s = hg
        sh = 1
        while sh < w:
            s = s + pltpu.roll(s, sh, axis=0)
            sh *= 2
        cnt = jnp.clip(t_abs + 1, 1, w).astype(f32)
        diff = (s / cnt - hg)[POOL_HALO:]
        outs.append(_dot(diff.astype(bf16), w_ref[gi]))
    y = jnp.concatenate(outs, axis=1)
    o_ref[0] = x + y * scale_ref[...]


def _pool_layer(x3, g, w_grp, scale):
    B, S, D = x3.shape
    tp = min(512, S)
    cg = D // len(POOL_WINDOWS)
    assert S % tp == 0 and tp % POOL_HALO == 0 and all(w & (w - 1) == 0 for w in POOL_WINDOWS)
    hb = tp // POOL_HALO
    return pl.pallas_call(
        functools.partial(_pool_kernel, tp=tp, cg=cg),
        out_shape=jax.ShapeDtypeStruct((B, S, D), f32),
        grid=(B, S // tp),
        in_specs=[
            pl.BlockSpec((1, tp, D), lambda b, i: (b, i, 0)),
            pl.BlockSpec((1, POOL_HALO, D), lambda b, i: (b, jnp.maximum(i * hb - 1, 0), 0)),
            pl.BlockSpec((1, D), lambda b, i: (0, 0)),
            pl.BlockSpec((len(POOL_WINDOWS), cg, cg), lambda b, i: (0, 0, 0)),
            pl.BlockSpec((1, D), lambda b, i: (0, 0)),
        ],
        out_specs=pl.BlockSpec((1, tp, D), lambda b, i: (b, i, 0)),
        compiler_params=_cparams("parallel", "arbitrary"),
        name="pool_layer",
    )(x3, x3, g.reshape(1, D), w_grp.astype(bf16), scale.reshape(1, D))


def _ffn_kernel(x_ref, g_ref, wg_ref, wu_ref, wd_ref, o_ref, h_ref, acc_ref):
    j = pl.program_id(1)

    @pl.when(j == 0)
    def _():
        x = x_ref[...]
        h_ref[...] = _rms(x, g_ref[...]).astype(bf16)
        acc_ref[...] = x

    h = h_ref[...]
    act = _silu(_dot(h, wg_ref[...])) * _dot(h, wu_ref[...])
    acc_ref[...] += _dot(act.astype(bf16), wd_ref[...])

    @pl.when(j == pl.num_programs(1) - 1)
    def _():
        o_ref[...] = acc_ref[...]


def _ffn_layer(xf, g, wg, wu, wd):
    N, D = xf.shape
    F = wg.shape[1]
    tm = min(1024, N)
    tf = 512
    assert N % tm == 0 and F % tf == 0
    return pl.pallas_call(
        _ffn_kernel,
        out_shape=jax.ShapeDtypeStruct((N, D), f32),
        grid=(N // tm, F // tf),
        in_specs=[
            pl.BlockSpec((tm, D), lambda i, j: (i, 0)),
            pl.BlockSpec((1, D), lambda i, j: (0, 0)),
            pl.BlockSpec((D, tf), lambda i, j: (0, j)),
            pl.BlockSpec((D, tf), lambda i, j: (0, j)),
            pl.BlockSpec((tf, D), lambda i, j: (j, 0)),
        ],
        out_specs=pl.BlockSpec((tm, D), lambda i, j: (i, 0)),
        scratch_shapes=[pltpu.VMEM((tm, D), bf16), pltpu.VMEM((tm, D), f32)],
        compiler_params=_cparams("parallel", "arbitrary"),
        name="ffn_dense",
    )(xf, g.reshape(1, D), wg.astype(bf16), wu.astype(bf16), wd.astype(bf16))


def _router_kernel(x_ref, g_ref, rt_ref, hp_ref, idx_ref, gate_ref):
    h = _rms(x_ref[...], g_ref[...])
    half = h.shape[1] // 2
    bits = pltpu.bitcast(h.astype(bf16).astype(f32), u32)
    hp_ref[...] = (bits[:, :half] & jnp.uint32(0xFFFF0000)) | (bits[:, half:] >> 16)
    logits = lax.dot_general(rt_ref[...], h, (((1,), (1,)), ((), ())),
                             precision=lax.Precision.HIGHEST, preferred_element_type=f32)
    ne = logits.shape[0]
    row = lax.broadcasted_iota(i32, logits.shape, 0)
    m1 = jnp.max(logits, axis=0, keepdims=True)
    i1 = jnp.min(jnp.where(logits == m1, row, ne), axis=0, keepdims=True)
    rest = jnp.where(row == i1, -jnp.inf, logits)
    m2 = jnp.max(rest, axis=0, keepdims=True)
    i2 = jnp.min(jnp.where(rest == m2, row, ne), axis=0, keepdims=True)
    e2 = jnp.exp(m2 - m1)
    den = 1.0 + e2
    idx_ref[...] = jnp.concatenate([i1, i2], axis=0)
    gate_ref[...] = jnp.concatenate([1.0 / den, e2 / den], axis=0)


def _router(xf, g, router):
    N, D = xf.shape
    E = router.shape[1]
    tr = min(1024, N)
    assert N % tr == 0 and TOP_K == 2
    return pl.pallas_call(
        _router_kernel,
        out_shape=(jax.ShapeDtypeStruct((N, D // 2), u32),
                   jax.ShapeDtypeStruct((TOP_K, N), i32),
                   jax.ShapeDtypeStruct((TOP_K, N), f32)),
        grid=(N // tr,),
        in_specs=[
            pl.BlockSpec((tr, D), lambda i: (i, 0)),
            pl.BlockSpec((1, D), lambda i: (0, 0)),
            pl.BlockSpec((E, D), lambda i: (0, 0)),
        ],
        out_specs=(pl.BlockSpec((tr, D // 2), lambda i: (i, 0)),
                   pl.BlockSpec((TOP_K, tr), lambda i: (0, i)),
                   pl.BlockSpec((TOP_K, tr), lambda i: (0, i))),
        compiler_params=_cparams("parallel"),
        name="moe_router",
    )(xf, g.reshape(1, D), router.T)


def _fetch_slots(pos_hbm, idx_smem, sem_idx, i, tt, n_tok):
    for k in range(TOP_K):
        cp = pltpu.make_async_copy(pos_hbm.at[pl.ds(k * n_tok + i * tt, tt)], idx_smem.at[pl.ds(k * tt, tt)], sem_idx)
        cp.start()
        cp.wait()


def _gather_copy(idx_smem, src_hbm, dst_ref, sem, r, slot):
    return pltpu.make_async_copy(src_hbm.at[pl.ds(idx_smem[slot], 1)], dst_ref.at[pl.ds(r, 1)], sem)


def _dispatch_kernel(tok_hbm, hp_hbm, xs_ref, idx_smem, sem_idx, sem, *, tg):
    cp = pltpu.make_async_copy(tok_hbm.at[pl.ds(pl.program_id(0) * tg, tg)], idx_smem, sem_idx)
    cp.start()
    cp.wait()

    def start(r, c):
        _gather_copy(idx_smem, hp_hbm, xs_ref, sem, r, r).start()
        return c

    def wait(r, c):
        _gather_copy(idx_smem, hp_hbm, xs_ref, sem, r, r).wait()
        return c

    lax.fori_loop(0, tg, start, 0, unroll=8)
    lax.fori_loop(0, tg, wait, 0, unroll=8)


def _dispatch(slot_tok, hp):
    P = slot_tok.shape[0]
    W = hp.shape[1]
    tg = min(1024, P)
    assert P % tg == 0
    return pl.pallas_call(
        functools.partial(_dispatch_kernel, tg=tg),
        out_shape=jax.ShapeDtypeStruct((P, W), hp.dtype),
        grid=(P // tg,),
        in_specs=[pl.BlockSpec(memory_space=pl.ANY), pl.BlockSpec(memory_space=pl.ANY)],
        out_specs=pl.BlockSpec((tg, W), lambda i: (i, 0)),
        scratch_shapes=[pltpu.SMEM((tg,), i32), pltpu.SemaphoreType.DMA(()), pltpu.SemaphoreType.DMA(())],
        compiler_params=_cparams("arbitrary"),
        name="moe_dispatch",
    )(slot_tok, hp)


def _gmm_kernel(te_ref, nu_ref, xs_ref, wg_ref, wu_ref, wd_ref, o_ref, h_ref, acc_ref):
    i = pl.program_id(0)
    j = pl.program_id(1)

    @pl.when(i < nu_ref[0])
    def _():
        @pl.when(j == 0)
        def _():
            w = xs_ref[...]
            left = pltpu.bitcast(w & jnp.uint32(0xFFFF0000), f32)
            right = pltpu.bitcast(w << 16, f32)
            h_ref[...] = jnp.concatenate([left, right], axis=1).astype(bf16)
            acc_ref[...] = jnp.zeros_like(acc_ref)

        h = h_ref[...]
        act = _silu(_dot(h, wg_ref[0])) * _dot(h, wu_ref[0])
        acc_ref[...] += _dot(act.astype(bf16), wd_ref[0])

        @pl.when(j == pl.num_programs(1) - 1)
        def _():
            o_ref[...] = acc_ref[...]

    @pl.when((i >= nu_ref[0]) & (j == 0))
    def _():
        o_ref[...] = jnp.zeros_like(o_ref)


def _gmm(tile_expert, n_used, xs, wg, wu, wd, tm):
    P, half = xs.shape
    D = 2 * half
    F = wg.shape[2]
    tf = 512
    assert P % tm == 0 and F % tf == 0

    nf = F // tf

    def row_map(i, j, te, nu):
        return (jnp.minimum(i, nu[0] - 1), 0)

    def ff(i, j, nu):
        return jnp.where(i < nu[0], j, nf - 1)

    grid_spec = pltpu.PrefetchScalarGridSpec(
        num_scalar_prefetch=2,
        grid=(P // tm, nf),
        in_specs=[
            pl.BlockSpec((tm, half), row_map),
            pl.BlockSpec((1, D, tf), lambda i, j, te, nu: (te[i], 0, ff(i, j, nu))),
            pl.BlockSpec((1, D, tf), lambda i, j, te, nu: (te[i], 0, ff(i, j, nu))),
            pl.BlockSpec((1, tf, D), lambda i, j, te, nu: (te[i], ff(i, j, nu), 0)),
        ],
        out_specs=pl.BlockSpec((tm, D), lambda i, j, te, nu: (i, 0)),
        scratch_shapes=[pltpu.VMEM((tm, D), bf16), pltpu.VMEM((tm, D), f32)],
    )
    return pl.pallas_call(
        _gmm_kernel,
        out_shape=jax.ShapeDtypeStruct((P, D), f32),
        grid_spec=grid_spec,
        compiler_params=_cparams("arbitrary", "arbitrary"),
        name="moe_gmm",
    )(tile_expert, n_used, xs, wg.astype(bf16), wu.astype(bf16), wd.astype(bf16))


def _combine_kernel(pos_hbm, x_ref, gate_ref, ys_hbm, o_ref, idx_smem, buf_ref, sem_idx, sem, *, tt, n_tok):
    _fetch_slots(pos_hbm, idx_smem, sem_idx, pl.program_id(0), tt, n_tok)

    def start(r, c):
        for k in range(TOP_K):
            _gather_copy(idx_smem, ys_hbm, buf_ref.at[k], sem, r, k * tt + r).start()
        return c

    def wait(r, c):
        for k in range(TOP_K):
            _gather_copy(idx_smem, ys_hbm, buf_ref.at[k], sem, r, k * tt + r).wait()
        return c

    lax.fori_loop(0, tt, start, 0, unroll=8)
    lax.fori_loop(0, tt, wait, 0, unroll=8)
    gate = gate_ref[...]
    acc = x_ref[...]
    for k in range(TOP_K):
        acc = acc + buf_ref[k] * gate[:, k:k + 1]
    o_ref[...] = acc


def _combine(pos_flat, xf, gate_tk, ys):
    N, D = xf.shape
    tt = min(512, N)
    assert N % tt == 0
    return pl.pallas_call(
        functools.partial(_combine_kernel, tt=tt, n_tok=N),
        out_shape=jax.ShapeDtypeStruct((N, D), f32),
        grid=(N // tt,),
        in_specs=[pl.BlockSpec(memory_space=pl.ANY),
                  pl.BlockSpec((tt, D), lambda i: (i, 0)),
                  pl.BlockSpec((tt, TOP_K), lambda i: (i, 0)),
                  pl.BlockSpec(memory_space=pl.ANY)],
        out_specs=pl.BlockSpec((tt, D), lambda i: (i, 0)),
        scratch_shapes=[pltpu.SMEM((TOP_K * tt,), i32), pltpu.VMEM((TOP_K, tt, D), f32),
                        pltpu.SemaphoreType.DMA(()), pltpu.SemaphoreType.DMA(())],
        compiler_params=_cparams("arbitrary"),
        name="moe_combine",
    )(pos_flat, xf, gate_tk, ys)


def _moe_layer(xf, g, router, wg, wu, wd):
    N, D = xf.shape
    E = router.shape[1]
    tm = min(1024, N)
    hp, idx, gate = _router(xf, g, router)
    e_flat = idx.reshape(-1)
    onehot = (e_flat[:, None] == jnp.arange(E, dtype=i32)[None, :]).astype(i32)
    csum = jnp.cumsum(onehot, axis=0)
    counts = csum[-1]
    padded = ((counts + tm - 1) // tm) * tm
    ends = jnp.cumsum(padded)
    starts = ends - padded
    pos = (jnp.sum((csum + starts[None, :]) * onehot, axis=1) - 1).astype(i32)
    P = TOP_K * N + E * tm
    n_used = (ends[-1] // tm).astype(i32)
    tile_start = jnp.minimum(jnp.arange(P // tm, dtype=i32), n_used - 1) * tm
    tile_expert = jnp.sum((ends[None, :] <= tile_start[:, None]).astype(i32), axis=1)
    slot_tok = jnp.zeros((P,), i32).at[pos].set(jnp.tile(jnp.arange(N, dtype=i32), TOP_K))
    xs = _dispatch(slot_tok, hp)
    ys = _gmm(tile_expert, n_used.reshape(1), xs, wg, wu, wd, tm)
    return _combine(pos, xf, gate.T, ys)


def _kv_kernel(x_ref, g_ref, w_ref, gk_ref, seg_ref, segt_ref, kc_ref, vc_ref, ks_ref, vs_ref, kw_ref, vw_ref):
    h = _rms(x_ref[...], g_ref[...]).astype(bf16)
    kv = _dot(h, w_ref[...])
    wd = kc_ref.shape[1]
    part = lambda p: kv[:, p * wd:(p + 1) * wd]
    seg, segt = seg_ref[...], segt_ref[...]
    kc_ref[...] = part(0)
    vc_ref[...] = part(1)
    ks_ref[...] = _head_rms(part(2), gk_ref[0:1, :], seg, segt).astype(bf16)
    vs_ref[...] = part(3).astype(bf16)
    kw_ref[...] = _head_rms(part(4), gk_ref[1:2, :], seg, segt).astype(bf16)
    vw_ref[...] = part(5).astype(bf16)


def _kv_project(xf, norm_kv, w_kv, g_k):
    N, D = xf.shape
    wd = N_KV_HEADS * HEAD_DIM
    tk = min(512, N)
    assert N % tk == 0 and w_kv.shape[1] == 2 * N_BRANCH * wd
    seg, segt = _seg_mats(wd)
    gk = jnp.stack([jnp.tile(g_k[1], N_KV_HEADS), jnp.tile(g_k[2], N_KV_HEADS)])
    row = lambda i: (i, 0)
    const = lambda i: (0, 0)
    return pl.pallas_call(
        _kv_kernel,
        out_shape=(jax.ShapeDtypeStruct((N, wd), f32), jax.ShapeDtypeStruct((N, wd), f32))
        + tuple(jax.ShapeDtypeStruct((N, wd), bf16) for _ in range(4)),
        grid=(N // tk,),
        in_specs=[
            pl.BlockSpec((tk, D), row),
            pl.BlockSpec((1, D), const),
            pl.BlockSpec(w_kv.shape, const),
            pl.BlockSpec((2, wd), const),
            pl.BlockSpec((wd, NH_PAD), const),
            pl.BlockSpec((NH_PAD, wd), const),
        ],
        out_specs=tuple(pl.BlockSpec((tk, wd), row) for _ in range(6)),
        compiler_params=_cparams("parallel"),
        name="kv_project",
    )(xf, norm_kv.reshape(1, D), w_kv.astype(bf16), gk, seg, segt)


def _compress_kernel(c_ref, pos_ref, w1_ref, w2_ref, gk_ref, o_ref):
    kv = pl.program_id(0)
    c = c_ref[0, 0, 0]
    a = _dot((c + pos_ref[0, 0:1, :]).astype(bf16), w1_ref[0, 0])
    b = _dot((c + pos_ref[0, 1:2, :]).astype(bf16), w1_ref[0, 1])
    n = c.shape[0]
    hid = a + pltpu.roll(b, n - 1, axis=0)
    out = _dot(_silu(hid).astype(bf16), w2_ref[0])

    @pl.when(kv == 0)
    def _():
        o_ref[0, 0, 0] = _rms(out, gk_ref[...])

    @pl.when(kv != 0)
    def _():
        o_ref[0, 0, 0] = out


def _compress(kc, vc, B, S, cmp_pos, cmp_w1, cmp_w2, g_k0):
    G, DH = N_KV_HEADS, HEAD_DIM
    nch = S // CMP_STRIDE
    cw = CMP_STRIDE * DH

    def chunks(z):
        return z.reshape(B, nch, CMP_STRIDE, G, DH).transpose(0, 3, 1, 2, 4).reshape(B, G, nch, cw)

    c = jnp.stack([chunks(kc), chunks(vc)])
    pos = cmp_pos.reshape(2, 2, cw)
    w1 = cmp_w1.reshape(2, 2, cw, CMP_HIDDEN).astype(bf16)
    return pl.pallas_call(
        _compress_kernel,
        out_shape=jax.ShapeDtypeStruct((2, B, G, nch, DH), f32),
        grid=(2, B, G),
        in_specs=[
            pl.BlockSpec((1, 1, 1, nch, cw), lambda k, b, g: (k, b, g, 0, 0)),
            pl.BlockSpec((1, 2, cw), lambda k, b, g: (k, 0, 0)),
            pl.BlockSpec((1, 2, cw, CMP_HIDDEN), lambda k, b, g: (k, 0, 0, 0)),
            pl.BlockSpec((1, CMP_HIDDEN, DH), lambda k, b, g: (k, 0, 0)),
            pl.BlockSpec((1, DH), lambda k, b, g: (0, 0)),
        ],
        out_specs=pl.BlockSpec((1, 1, 1, nch, DH), lambda k, b, g: (k, b, g, 0, 0)),
        compiler_params=_cparams("arbitrary", "arbitrary", "arbitrary"),
        name="kv_compress",
    )(c, pos, w1, cmp_w2.astype(bf16), g_k0.reshape(1, DH))


def _build_shared(xf, B, S, norm_kv, w_kv, g_k, cmp_pos, cmp_w1, cmp_w2):
    G, DH = N_KV_HEADS, HEAD_DIM
    nkt = S // KEY_TILE
    kc, vc, ks, vs, kw, vw = _kv_project(xf, norm_kv, w_kv, g_k)
    cmp = _compress(kc, vc, B, S, cmp_pos, cmp_w1, cmp_w2, g_k[0]).astype(bf16)
    kcm = cmp[0]
    vct = cmp[1].transpose(0, 1, 3, 2)

    def vals_t(z):
        return z.reshape(B, nkt, KEY_TILE, G, DH).transpose(0, 3, 1, 4, 2)

    def keys(z):
        return z.reshape(B, S, G, DH).transpose(0, 2, 1, 3)

    return kcm, vct, keys(ks), vals_t(vs), keys(kw), vals_t(vw)


def _qg_kernel(x_ref, g_ref, wq_ref, wgate_ref, gq_ref, seg_ref, segt_ref, q_ref, gate_ref):
    h = _rms(x_ref[...], g_ref[...]).astype(bf16)
    q = _head_rms(_dot(h, wq_ref[...]), gq_ref[...], seg_ref[...], segt_ref[...])
    q = q * (HEAD_DIM ** -0.5 * LOG2E)
    gate_ref[...] = jax.nn.sigmoid(_dot(h, wgate_ref[...]))
    nt, G, DH, RT = q_ref.shape
    T = q.shape[0] // nt
    r = RT // T
    for t in range(nt):
        for g in range(G):
            qt = q[t * T:(t + 1) * T, g * r * DH:(g + 1) * r * DH].T
            q_ref[t, g] = jnp.concatenate([qt[k * DH:(k + 1) * DH] for k in range(r)], axis=1).astype(bf16)


def _qg_project(xf, g, w_qg, g_q):
    N, D = xf.shape
    HD = D
    ng = w_qg.shape[1] - HD
    tq = min(512, N)
    G, T = N_KV_HEADS, Q_BLOCK
    RT = HD // (G * HEAD_DIM) * T
    assert N % tq == 0 and tq % T == 0
    seg, segt = _seg_mats(HD)
    row = lambda i: (i, 0)
    const = lambda i: (0, 0)
    return pl.pallas_call(
        _qg_kernel,
        out_shape=(jax.ShapeDtypeStruct((N // T, G, HEAD_DIM, RT), bf16), jax.ShapeDtypeStruct((N, ng), f32)),
        grid=(N // tq,),
        in_specs=[
            pl.BlockSpec((tq, D), row),
            pl.BlockSpec((1, D), const),
            pl.BlockSpec((D, HD), const),
            pl.BlockSpec((D, ng), const),
            pl.BlockSpec((1, HD), const),
            pl.BlockSpec((HD, NH_PAD), const),
            pl.BlockSpec((NH_PAD, HD), const),
        ],
        out_specs=(pl.BlockSpec((tq // T, G, HEAD_DIM, RT), lambda i: (i, 0, 0, 0)), pl.BlockSpec((tq, ng), row)),
        compiler_params=_cparams("parallel"),
        name="qg_project",
    )(xf, g.reshape(1, D), w_qg[:, :HD].astype(bf16), w_qg[:, HD:].astype(bf16),
      jnp.tile(g_q, HD // HEAD_DIM).reshape(1, HD), seg, segt)


def _fold8(x, op):
    return op(x.reshape(x.shape[0] // 8, 8, x.shape[1]), axis=0)


def _nsa_kernel(q_ref, gt_ref, kc_ref, vct_ref, ks_ref, vst_ref, kw_ref, vwt_ref, cbn_ref, cstep_ref, tb_ref, c2st_ref,
                o_ref, cb_scr, pen_ref, penf_ref, s_scr, *, r, nsb, n_sel):
    qb = pl.program_id(2)
    t0 = qb * Q_BLOCK
    T = Q_BLOCK
    RT = r * T
    nch = kc_ref.shape[2]
    qT = q_ref[0, 0]

    cb_scr[...] = cstep_ref[0, pl.ds(pl.multiple_of(nch - (qb + 1) * CMP_PER_Q, 8), CMP_PAD + nch), :]
    cb_scr[pl.ds(pl.multiple_of(qb * CMP_PER_Q, 8), CMP_NEAR), :] = cbn_ref[0]
    s = _dot(kc_ref[0, 0], qT) + cb_scr[CMP_PAD:CMP_PAD + nch, :]
    p = jnp.where(s > 0.5 * NEG, jnp.exp2(s - jnp.max(s, axis=0, keepdims=True)), 0.0)
    p = p / jnp.maximum(jnp.sum(p, axis=0, keepdims=True), TINY)
    o_cmp = _dot(vct_ref[0, 0], p.astype(bf16))

    psum = p[:, 0:T]
    for k in range(1, r):
        psum = psum + p[:, k * T:(k + 1) * T]
    hi = psum.astype(bf16)
    lo = (psum - hi.astype(f32)).astype(bf16)
    imp = _dot(c2st_ref[...], hi) + _dot(c2st_ref[...], lo)
    jb = lax.broadcasted_iota(i32, (nsb, T), 0)
    blk_q = jnp.right_shift(t0 + lax.broadcasted_iota(i32, (nsb, T), 1), SLC_LEN.bit_length() - 1)
    forced = (jb == 0) | (jb == blk_q) | (jb == blk_q - 1)
    score = jnp.where(forced, FORCED_SCORE, jnp.where(jb <= blk_q, imp, NEG))
    pen = jnp.full((nsb, T), NEG, f32)
    for _ in range(n_sel):
        mx = jnp.max(score, axis=0, keepdims=True)
        first = jnp.min(jnp.where(score == mx, jb, nsb), axis=0, keepdims=True)
        hit = jb == first
        pen = jnp.where(hit, 0.0, pen)
        score = jnp.where(hit, -jnp.inf, score)
    pen = jnp.concatenate([pen] * r, axis=1)
    pen_ref[...] = pen
    penf_ref[...] = pen + cstep_ref[0, 0:1, :]

    def tile_rows(kt):
        return pl.ds(pl.multiple_of(kt * KEY_TILE, KEY_TILE), KEY_TILE)

    def add_block_pen(s, ref, kt):
        return jnp.concatenate([s[:SLC_LEN] + ref[pl.ds(2 * kt, 1), :], s[SLC_LEN:] + ref[pl.ds(2 * kt + 1, 1), :]],
                               axis=0)

    win = []
    for d in range(WINDOW // KEY_TILE, -1, -1):
        kt = qb - d
        ktc = jnp.maximum(kt, 0)
        tab = tb_ref[0, jnp.where(kt >= 0, d, TB_NONE)]
        win.append((_dot(kw_ref[0, 0, tile_rows(ktc), :], qT) + tab, vwt_ref[0, 0, ktc]))
    m8 = _fold8(win[0][0], jnp.max)
    for s_d, _ in win[1:]:
        m8 = jnp.maximum(m8, _fold8(s_d, jnp.max))
    m = jnp.max(m8, axis=0, keepdims=True)
    l8 = jnp.zeros((8, RT), f32)
    acc = jnp.zeros((HEAD_DIM, RT), f32)
    for s_d, v_d in win:
        p_d = jnp.exp2(s_d - m)
        l8 = l8 + _fold8(p_d, jnp.sum)
        acc = acc + _dot(v_d, p_d.astype(bf16))
    o_win = acc * (1.0 / jnp.maximum(jnp.sum(l8, axis=0, keepdims=True), TINY))

    nkt = s_scr.shape[0] // KEY_TILE - SLC_UNROLL
    ktp = jnp.maximum(qb - 1, 0)
    s_prev = add_block_pen(_dot(ks_ref[0, 0, tile_rows(ktp), :], qT)
                           + tb_ref[0, jnp.where(qb >= 1, TB_PREV, TB_NONE)], pen_ref, ktp)
    s_scr[tile_rows(ktp), :] = s_prev
    s_own = add_block_pen(_dot(ks_ref[0, 0, tile_rows(qb), :], qT) + tb_ref[0, TB_OWN], pen_ref, qb)
    s_scr[tile_rows(qb), :] = s_own
    m8 = jnp.maximum(_fold8(s_prev, jnp.max), _fold8(s_own, jnp.max))

    def pass_a(i, m8):
        for u in range(SLC_UNROLL):
            kt = SLC_UNROLL * i + u
            live = kt < qb - 1
            ktc = jnp.minimum(kt, nkt - 1)
            s = add_block_pen(_dot(ks_ref[0, 0, tile_rows(ktc), :], qT) + jnp.where(live, 0.0, NEG), penf_ref, ktc)
            s_scr[tile_rows(jnp.where(live, kt, nkt + u)), :] = s
            m8 = jnp.maximum(m8, _fold8(s, jnp.max))
        return m8

    m8 = lax.fori_loop(0, (jnp.maximum(qb - 1, 0) + SLC_UNROLL - 1) // SLC_UNROLL, pass_a, m8)
    m = jnp.max(m8, axis=0, keepdims=True)

    def pass_b(i, carry):
        l8, acc = carry
        for u in range(SLC_UNROLL):
            kt = SLC_UNROLL * i + u
            ktc = jnp.minimum(kt, qb)
            p_u = jnp.exp2(s_scr[tile_rows(ktc), :] - (m + jnp.where(kt <= qb, 0.0, -NEG)))
            l8 = l8 + _fold8(p_u, jnp.sum)
            acc = acc + _dot(vst_ref[0, 0, ktc], p_u.astype(bf16))
        return l8, acc

    l8, acc = lax.fori_loop(0, qb // SLC_UNROLL + 1, pass_b,
                            (jnp.zeros((8, RT), f32), jnp.zeros((HEAD_DIM, RT), f32)))
    o_slc = acc * (1.0 / jnp.maximum(jnp.sum(l8, axis=0, keepdims=True), TINY))

    gt = gt_ref[0, 0]
    o_ref[0, 0] = gt[0:1] * o_cmp + gt[1:2] * o_slc + gt[2:3] * o_win


def _bias_tables(rel_bias, S):
    G = N_KV_HEADS
    H = rel_bias.shape[1]
    r = H // G
    T = KEY_TILE
    n = jnp.arange(S + 2 * T, dtype=i32)
    max_exact = N_BUCKETS // 2
    nf = jnp.maximum(n, 1).astype(f32)
    large = max_exact + (jnp.log(nf / max_exact) / math.log(MAX_DISTANCE / max_exact)
                         * (N_BUCKETS - max_exact)).astype(i32)
    bucket = jnp.where(n < max_exact, n, jnp.minimum(large, N_BUCKETS - 1))
    bias1d = rel_bias.astype(f32)[bucket] * LOG2E

    def per_group(tab):
        lead = tab.shape[:-2]
        k = len(lead)
        t = jnp.moveaxis(tab, -1, 0).reshape((G, r) + lead + (T,))
        return jnp.moveaxis(t, 1, k + 1).reshape((G,) + lead + (r * T,))

    def masked(dist, ok):
        return jnp.where(jnp.asarray(ok)[..., None], bias1d[np.maximum(dist, 0)], NEG)

    kj, qi = np.arange(T)[:, None], np.arange(T)[None, :]
    dist = np.stack([d * T + qi - kj for d in range(3)])
    ok = np.stack([dist[0] >= 0, np.ones((T, T), bool), dist[2] < WINDOW])
    tb = jnp.concatenate([masked(dist, ok), jnp.full((1, T, T, H), NEG, f32)])
    cend = (np.arange(CMP_NEAR)[:, None] - CMP_PAD) * CMP_STRIDE + CMP_LEN - 1
    dcn = np.arange(T)[None, :] - cend
    rows = CMP_PAD + S // CMP_STRIDE
    far = jnp.broadcast_to(bias1d[-1][None, None, :], (rows, T, H))
    cstep = jnp.concatenate([far, jnp.full((rows, T, H), NEG, f32)])
    return per_group(tb), per_group(masked(dcn, dcn >= 0)), per_group(cstep)


def _nsa_attention(q, gates_t, shared, tables, B, S):
    kcm, vct, ks, vst, kw, vwt = shared
    tb, cbn, cstep = tables
    G, DH, T = N_KV_HEADS, HEAD_DIM, Q_BLOCK
    RT = q.shape[3]
    r = RT // T
    qmap = lambda b, g, i: (b * nqb + i, g, 0, 0)
    nqb = S // T
    nch = S // CMP_STRIDE
    nsb = S // SLC_LEN
    nkt = S // KEY_TILE
    cmp_start = np.arange(nch) * CMP_STRIDE
    slc_start = np.arange(nsb) * SLC_LEN
    overlap = np.clip(np.minimum(cmp_start[:, None] + CMP_LEN, slc_start[None, :] + SLC_LEN)
                      - np.maximum(cmp_start[:, None], slc_start[None, :]), 0, None) / CMP_LEN
    overlap[nch - 1] = 0.0
    c2st = jnp.asarray(overlap.T, bf16)
    bg = lambda b, g, i: (b, g, 0, 0)
    grp = lambda b, g, i: (g, 0, 0)
    return pl.pallas_call(
        functools.partial(_nsa_kernel, r=r, nsb=nsb, n_sel=min(N_SEL, nsb)),
        out_shape=jax.ShapeDtypeStruct(q.shape, f32),
        grid=(B, G, nqb),
        in_specs=[
            pl.BlockSpec((1, 1, DH, RT), qmap),
            pl.BlockSpec((1, 1, N_BRANCH, RT), qmap),
            pl.BlockSpec((1, 1, nch, DH), bg),
            pl.BlockSpec((1, 1, DH, nch), bg),
            pl.BlockSpec((1, 1, S, DH), bg),
            pl.BlockSpec((1, 1, nkt, DH, KEY_TILE), lambda b, g, i: (b, g, 0, 0, 0)),
            pl.BlockSpec((1, 1, S, DH), bg),
            pl.BlockSpec((1, 1, nkt, DH, KEY_TILE), lambda b, g, i: (b, g, 0, 0, 0)),
            pl.BlockSpec((1, CMP_NEAR, RT), grp),
            pl.BlockSpec((1, 2 * (CMP_PAD + nch), RT), grp),
            pl.BlockSpec((1, 4, KEY_TILE, RT), lambda b, g, i: (g, 0, 0, 0)),
            pl.BlockSpec((nsb, nch), lambda b, g, i: (0, 0)),
        ],
        out_specs=pl.BlockSpec((1, 1, DH, RT), qmap),
        scratch_shapes=[pltpu.VMEM((CMP_PAD + nch, RT), f32), pltpu.VMEM((nsb, RT), f32), pltpu.VMEM((nsb, RT), f32),
                        pltpu.VMEM(((nkt + SLC_UNROLL) * KEY_TILE, RT), f32)],
        compiler_params=_cparams("parallel", "parallel", "arbitrary"),
        name="nsa_attention",
    )(q, gates_t, kcm, vct, ks, vst, kw, vwt, cbn, cstep, tb, c2st)


def _oproj_kernel(x_ref, a_ref, w_ref, o_ref):
    nt, G, DH, RT = a_ref.shape
    T = x_ref.shape[0] // nt
    r = RT // T
    rows = []
    for t in range(nt):
        cols = []
        for g in range(G):
            a = a_ref[t, g]
            cols.append(jnp.concatenate([a[:, k * T:(k + 1) * T] for k in range(r)], axis=0).T)
        rows.append(jnp.concatenate(cols, axis=1))
    attn = jnp.concatenate(rows, axis=0)
    o_ref[...] = x_ref[...] + _dot(attn.astype(bf16), w_ref[...])


def _out_project(xf, attn, w_o):
    N, D = xf.shape
    nt_all, G, DH, RT = attn.shape
    T = N // nt_all
    to = min(512, N)
    assert N % to == 0 and to % T == 0
    row = lambda i: (i, 0)
    return pl.pallas_call(
        _oproj_kernel,
        out_shape=jax.ShapeDtypeStruct((N, D), f32),
        grid=(N // to,),
        in_specs=[pl.BlockSpec((to, D), row), pl.BlockSpec((to // T, G, DH, RT), lambda i: (i, 0, 0, 0)),
                  pl.BlockSpec(w_o.shape, lambda i: (0, 0))],
        out_specs=pl.BlockSpec((to, D), row),
        compiler_params=_cparams("parallel"),
        name="out_project",
    )(xf, attn, w_o.astype(bf16))


def _nsa_layer(xf, B, S, g, w_qg, g_q, w_o, shared, tables):
    N = xf.shape[0]
    G, T = N_KV_HEADS, Q_BLOCK
    q, gates = _qg_project(xf, g, w_qg, g_q)
    r = gates.shape[1] // (G * N_BRANCH)
    gates_t = gates.reshape(N // T, T, G, r, N_BRANCH).transpose(0, 2, 4, 3, 1).reshape(N // T, G, N_BRANCH, r * T)
    attn = _nsa_attention(q, gates_t, shared, tables, B, S)
    return _out_project(xf, attn, w_o)


def kernel(x, rel_bias, norm_mix, norm_ffn, pool_w, pool_scale, norm_kv, w_kv, g_k, cmp_pos, cmp_w1, cmp_w2,
           w_qg, g_q, w_o, ffn_wg, ffn_wu, ffn_wd, router, moe_wg, moe_wu, moe_wd):
    B, S, D = x.shape
    depth = norm_mix.shape[0]
    n_a = depth // 2
    assert S % Q_BLOCK == 0
    xf = x.reshape(B * S, D)
    shared = None
    tables = _bias_tables(rel_bias, S)
    for layer in range(depth):
        if layer < n_a:
            xf = _pool_layer(xf.reshape(B, S, D), norm_mix[layer], pool_w[layer], pool_scale[layer]).reshape(B * S, D)
        else:
            j = layer - n_a
            xf = _nsa_layer(xf, B, S, norm_mix[layer], w_qg[j], g_q[j], w_o[j], shared, tables)
        i = layer // 2
        if layer % 2 == 0:
            xf = _ffn_layer(xf, norm_ffn[layer], ffn_wg[i], ffn_wu[i], ffn_wd[i])
        else:
            xf = _moe_layer(xf, norm_ffn[layer], router[i], moe_wg[i], moe_wu[i], moe_wd[i])
        if layer == n_a - 1:
            shared = _build_shared(xf, B, S, norm_kv, w_kv, g_k, cmp_pos, cmp_w1, cmp_w2)
    return xf.reshape(B, S, D)
```

```python
import functools
import math

import numpy as np
import jax
import jax.numpy as jnp
from jax import lax
from jax.experimental import pallas as pl
from jax.experimental.pallas import tpu as pltpu

f32 = jnp.float32
bf16 = jnp.bfloat16
i32 = jnp.int32
u32 = jnp.uint32

POOL_WINDOWS = (2, 4, 8, 16)
HEAD_DIM = 64
N_KV_HEADS = 4
N_BRANCH = 3
CMP_LEN = 32
CMP_STRIDE = 16
CMP_HIDDEN = 4 * HEAD_DIM
SLC_LEN = 64
N_SEL = 4
WINDOW = 256
Q_BLOCK = 128
FORCED_SCORE = 1.0e4
N_BUCKETS = 32
MAX_DISTANCE = 128
N_EXPERTS = 8
TOP_K = 2
EPS = 1e-6
NEG = -1e30
TINY = 1e-30
LOG2E = math.log2(math.e)
TB_OWN, TB_PREV, TB_WIN2, TB_NONE = range(4)

KEY_TILE = Q_BLOCK
SLC_UNROLL = 2
POOL_HALO = 16
NH_PAD = 16
V7X_VMEM_LIMIT = 56 * 1024 * 1024
DMA_QUEUES = 2

CMP_PER_Q = Q_BLOCK // CMP_STRIDE
CMP_PAD = 2 * CMP_PER_Q
CMP_NEAR = 3 * CMP_PER_Q

assert CMP_LEN == 2 * CMP_STRIDE and KEY_TILE == 2 * SLC_LEN and WINDOW == 2 * KEY_TILE
assert max(POOL_WINDOWS) <= POOL_HALO
assert 2 * KEY_TILE - (Q_BLOCK - 1) >= MAX_DISTANCE
assert (CMP_PAD + 1) * CMP_STRIDE - (CMP_LEN - 1) >= MAX_DISTANCE and CMP_PER_Q % 8 == 0


def _cparams(*sem):
    return pltpu.CompilerParams(dimension_semantics=sem, vmem_limit_bytes=V7X_VMEM_LIMIT)


def _rms(xf, g):
    ms = jnp.mean(xf * xf, axis=-1, keepdims=True)
    return (xf * lax.rsqrt(ms + EPS)) * g


def _dot(a, b):
    return jnp.dot(a, b, preferred_element_type=f32)


def _dot_hilo(a, b):
    hi = a.astype(bf16)
    lo = (a - hi.astype(f32)).astype(bf16)
    return _dot(hi, b) + _dot(lo, b)


def _head_rms(z, gvec, seg, segt):
    ssq = _dot_hilo(z * z, seg)
    inv = lax.rsqrt(ssq * (1.0 / HEAD_DIM) + EPS)
    return (z * _dot_hilo(inv, segt)) * gvec


def _silu(a):
    return a * jax.nn.sigmoid(a)


def _seg_mats(width):
    heads = width // HEAD_DIM
    seg = np.zeros((width, NH_PAD), np.float32)
    seg[np.arange(width), np.arange(width) // HEAD_DIM] = 1.0
    assert heads <= NH_PAD
    return jnp.asarray(seg, bf16), jnp.asarray(seg.T, bf16)


def _pool_kernel(x_ref, halo_ref, g_ref, w_ref, scale_ref, o_ref, *, tp, cg):
    i = pl.program_id(1)
    x = x_ref[0]
    xh = jnp.concatenate([halo_ref[0], x], axis=0)
    h = _rms(xh, g_ref[...])
    row = lax.broadcasted_iota(i32, (tp + POOL_HALO, 1), 0)
    t_abs = i * tp + row - POOL_HALO
    h = jnp.where(t_abs >= 0, h, 0.0)
    outs = []
    for gi, w in enumerate(POOL_WINDOWS):
        hg = h[:, gi * cg:(gi + 1) * cg]
        s = hg
        sh = 1
        while sh < w:
            s = s + pltpu.roll(s, sh, axis=0)
            sh *= 2
        cnt = jnp.clip(t_abs + 1, 1, w).astype(f32)
        diff = (s / cnt - hg)[POOL_HALO:]
        outs.append(_dot(diff.astype(bf16), w_ref[gi]))
    y = jnp.concatenate(outs, axis=1)
    o_ref[0] = x + y * scale_ref[...]


def _pool_layer(x3, g, w_grp, scale):
    B, S, D = x3.shape
    tp = min(512, S)
    cg = D // len(POOL_WINDOWS)
    assert S % tp == 0 and tp % POOL_HALO == 0 and all(w & (w - 1) == 0 for w in POOL_WINDOWS)
    hb = tp // POOL_HALO
    return pl.pallas_call(
        functools.partial(_pool_kernel, tp=tp, cg=cg),
        out_shape=jax.ShapeDtypeStruct((B, S, D), f32),
        grid=(B, S // tp),
        in_specs=[
            pl.BlockSpec((1, tp, D), lambda b, i: (b, i, 0)),
            pl.BlockSpec((1, POOL_HALO, D), lambda b, i: (b, jnp.maximum(i * hb - 1, 0), 0)),
            pl.BlockSpec((1, D), lambda b, i: (0, 0)),
            pl.BlockSpec((len(POOL_WINDOWS), cg, cg), lambda b, i: (0, 0, 0)),
            pl.BlockSpec((1, D), lambda b, i: (0, 0)),
        ],
        out_specs=pl.BlockSpec((1, tp, D), lambda b, i: (b, i, 0)),
        compiler_params=_cparams("parallel", "arbitrary"),
        name="pool_layer",
    )(x3, x3, g.reshape(1, D), w_grp.astype(bf16), scale.reshape(1, D))


def _ffn_kernel(x_ref, g_ref, wg_ref, wu_ref, wd_ref, o_ref, h_ref, acc_ref):
    j = pl.program_id(1)

    @pl.when(j == 0)
    def _():
        x = x_ref[...]
        h_ref[...] = _rms(x, g_ref[...]).astype(bf16)
        acc_ref[...] = x

    h = h_ref[...]
    act = _silu(_dot(h, wg_ref[...])) * _dot(h, wu_ref[...])
    acc_ref[...] += _dot(act.astype(bf16), wd_ref[...])

    @pl.when(j == pl.num_programs(1) - 1)
    def _():
        o_ref[...] = acc_ref[...]


def _ffn_layer(xf, g, wg, wu, wd):
    N, D = xf.shape
    F = wg.shape[1]
    tm = min(1024, N)
    tf = 512
    assert N % tm == 0 and F % tf == 0
    return pl.pallas_call(
        _ffn_kernel,
        out_shape=jax.ShapeDtypeStruct((N, D), f32),
        grid=(N // tm, F // tf),
        in_specs=[
            pl.BlockSpec((tm, D), lambda i, j: (i, 0)),
            pl.BlockSpec((1, D), lambda i, j: (0, 0)),
            pl.BlockSpec((D, tf), lambda i, j: (0, j)),
            pl.BlockSpec((D, tf), lambda i, j: (0, j)),
            pl.BlockSpec((tf, D), lambda i, j: (j, 0)),
        ],
        out_specs=pl.BlockSpec((tm, D), lambda i, j: (i, 0)),
        scratch_shapes=[pltpu.VMEM((tm, D), bf16), pltpu.VMEM((tm, D), f32)],
        compiler_params=_cparams("parallel", "arbitrary"),
        name="ffn_dense",
    )(xf, g.reshape(1, D), wg.astype(bf16), wu.astype(bf16), wd.astype(bf16))


def _router_kernel(x_ref, g_ref, rt_ref, hp_ref, idx_ref, gate_ref):
    h = _rms(x_ref[...], g_ref[...])
    half = h.shape[1] // 2
    bits = pltpu.bitcast(h.astype(bf16).astype(f32), u32)
    hp_ref[...] = (bits[:, :half] & jnp.uint32(0xFFFF0000)) | (bits[:, half:] >> 16)
    logits = lax.dot_general(rt_ref[...], h, (((1,), (1,)), ((), ())),
                             precision=lax.Precision.HIGHEST, preferred_element_type=f32)
    ne = logits.shape[0]
    row = lax.broadcasted_iota(i32, logits.shape, 0)
    m1 = jnp.max(logits, axis=0, keepdims=True)
    i1 = jnp.min(jnp.where(logits == m1, row, ne), axis=0, keepdims=True)
    rest = jnp.where(row == i1, -jnp.inf, logits)
    m2 = jnp.max(rest, axis=0, keepdims=True)
    i2 = jnp.min(jnp.where(rest == m2, row, ne), axis=0, keepdims=True)
    e2 = jnp.exp(m2 - m1)
    den = 1.0 + e2
    idx_ref[...] = jnp.concatenate([i1, i2], axis=0)
    gate_ref[...] = jnp.concatenate([1.0 / den, e2 / den], axis=0)


def _router(xf, g, router):
    N, D = xf.shape
    E = router.shape[1]
    tr = min(1024, N)
    assert N % tr == 0 and TOP_K == 2
    return pl.pallas_call(
        _router_kernel,
        out_shape=(jax.ShapeDtypeStruct((N, D // 2), u32),
                   jax.ShapeDtypeStruct((TOP_K, N), i32),
                   jax.ShapeDtypeStruct((TOP_K, N), f32)),
        grid=(N // tr,),
        in_specs=[
            pl.BlockSpec((tr, D), lambda i: (i, 0)),
            pl.BlockSpec((1, D), lambda i: (0, 0)),
            pl.BlockSpec((E, D), lambda i: (0, 0)),
        ],
        out_specs=(pl.BlockSpec((tr, D // 2), lambda i: (i, 0)),
                   pl.BlockSpec((TOP_K, tr), lambda i: (0, i)),
                   pl.BlockSpec((TOP_K, tr), lambda i: (0, i))),
        compiler_params=_cparams("parallel"),
        name="moe_router",
    )(xf, g.reshape(1, D), router.T)


def _fetch_slots(pos_hbm, idx_smem, sem_idx, i, tt, n_tok):
    for k in range(TOP_K):
        cp = pltpu.make_async_copy(pos_hbm.at[pl.ds(k * n_tok + i * tt, tt)], idx_smem.at[pl.ds(k * tt, tt)], sem_idx)
        cp.start()
        cp.wait()


def _gather_copy(idx_smem, src_hbm, dst_ref, sem, r, slot):
    return pltpu.make_async_copy(src_hbm.at[pl.ds(idx_smem[slot], 1)], dst_ref.at[pl.ds(r, 1)], sem)


def _dispatch_kernel(tok_hbm, hp_hbm, xs_ref, idx_smem, sem_idx, sem, *, tg):
    cp = pltpu.make_async_copy(tok_hbm.at[pl.ds(pl.program_id(0) * tg, tg)], idx_smem, sem_idx)
    cp.start()
    cp.wait()

    def start(i, c):
        for t in range(DMA_QUEUES):
            r = DMA_QUEUES * i + t
            _gather_copy(idx_smem, hp_hbm, xs_ref, sem, r, r).start(priority=t)
        return c

    def wait(r, c):
        _gather_copy(idx_smem, hp_hbm, xs_ref, sem, r, r).wait()
        return c

    lax.fori_loop(0, tg // DMA_QUEUES, start, 0, unroll=4)
    lax.fori_loop(0, tg, wait, 0, unroll=8)


def _dispatch(slot_tok, hp):
    P = slot_tok.shape[0]
    W = hp.shape[1]
    tg = min(1024, P)
    assert P % tg == 0
    return pl.pallas_call(
        functools.partial(_dispatch_kernel, tg=tg),
        out_shape=jax.ShapeDtypeStruct((P, W), hp.dtype),
        grid=(P // tg,),
        in_specs=[pl.BlockSpec(memory_space=pl.ANY), pl.BlockSpec(memory_space=pl.ANY)],
        out_specs=pl.BlockSpec((tg, W), lambda i: (i, 0)),
        scratch_shapes=[pltpu.SMEM((tg,), i32), pltpu.SemaphoreType.DMA(()), pltpu.SemaphoreType.DMA(())],
        compiler_params=_cparams("arbitrary"),
        name="moe_dispatch",
    )(slot_tok, hp)


def _gmm_kernel(te_ref, nu_ref, xs_ref, wg_ref, wu_ref, wd_ref, o_ref, h_ref, acc_ref):
    i = pl.program_id(0)
    j = pl.program_id(1)

    @pl.when(i < nu_ref[0])
    def _():
        @pl.when(j == 0)
        def _():
            w = xs_ref[...]
            left = pltpu.bitcast(w & jnp.uint32(0xFFFF0000), f32)
            right = pltpu.bitcast(w << 16, f32)
            h_ref[...] = jnp.concatenate([left, right], axis=1).astype(bf16)
            acc_ref[...] = jnp.zeros_like(acc_ref)

        h = h_ref[...]
        act = _silu(_dot(h, wg_ref[0])) * _dot(h, wu_ref[0])
        acc_ref[...] += _dot(act.astype(bf16), wd_ref[0])

        @pl.when(j == pl.num_programs(1) - 1)
        def _():
            o_ref[...] = acc_ref[...]

    @pl.when((i >= nu_ref[0]) & (j == 0))
    def _():
        o_ref[...] = jnp.zeros_like(o_ref)


def _gmm(tile_expert, n_used, xs, wg, wu, wd, tm):
    P, half = xs.shape
    D = 2 * half
    F = wg.shape[2]
    tf = 512
    assert P % tm == 0 and F % tf == 0

    nf = F // tf

    def row_map(i, j, te, nu):
        return (jnp.minimum(i, nu[0] - 1), 0)

    def ff(i, j, nu):
        return jnp.where(i < nu[0], j, nf - 1)

    grid_spec = pltpu.PrefetchScalarGridSpec(
        num_scalar_prefetch=2,
        grid=(P // tm, nf),
        in_specs=[
            pl.BlockSpec((tm, half), row_map),
            pl.BlockSpec((1, D, tf), lambda i, j, te, nu: (te[i], 0, ff(i, j, nu))),
            pl.BlockSpec((1, D, tf), lambda i, j, te, nu: (te[i], 0, ff(i, j, nu))),
            pl.BlockSpec((1, tf, D), lambda i, j, te, nu: (te[i], ff(i, j, nu), 0)),
        ],
        out_specs=pl.BlockSpec((tm, D), lambda i, j, te, nu: (i, 0)),
        scratch_shapes=[pltpu.VMEM((tm, D), bf16), pltpu.VMEM((tm, D), f32)],
    )
    return pl.pallas_call(
        _gmm_kernel,
        out_shape=jax.ShapeDtypeStruct((P, D), f32),
        grid_spec=grid_spec,
        compiler_params=_cparams("arbitrary", "arbitrary"),
        name="moe_gmm",
    )(tile_expert, n_used, xs, wg.astype(bf16), wu.astype(bf16), wd.astype(bf16))


def _combine_kernel(pos_hbm, x_ref, gate_ref, ys_hbm, o_ref, idx_smem, buf_ref, sem_idx, sem, *, tt, n_tok):
    _fetch_slots(pos_hbm, idx_smem, sem_idx, pl.program_id(0), tt, n_tok)

    def start(r, c):
        for k in range(TOP_K):
            _gather_copy(idx_smem, ys_hbm, buf_ref.at[k], sem, r, k * tt + r).start(priority=k % DMA_QUEUES)
        return c

    def wait(r, c):
        for k in range(TOP_K):
            _gather_copy(idx_smem, ys_hbm, buf_ref.at[k], sem, r, k * tt + r).wait()
        return c

    lax.fori_loop(0, tt, start, 0, unroll=8)
    lax.fori_loop(0, tt, wait, 0, unroll=8)
    gate = gate_ref[...]
    acc = x_ref[...]
    for k in range(TOP_K):
        acc = acc + buf_ref[k] * gate[:, k:k + 1]
    o_ref[...] = acc


def _combine(pos_flat, xf, gate_tk, ys):
    N, D = xf.shape
    tt = min(512, N)
    assert N % tt == 0
    return pl.pallas_call(
        functools.partial(_combine_kernel, tt=tt, n_tok=N),
        out_shape=jax.ShapeDtypeStruct((N, D), f32),
        grid=(N // tt,),
        in_specs=[pl.BlockSpec(memory_space=pl.ANY),
                  pl.BlockSpec((tt, D), lambda i: (i, 0)),
                  pl.BlockSpec((tt, TOP_K), lambda i: (i, 0)),
                  pl.BlockSpec(memory_space=pl.ANY)],
        out_specs=pl.BlockSpec((tt, D), lambda i: (i, 0)),
        scratch_shapes=[pltpu.SMEM((TOP_K * tt,), i32), pltpu.VMEM((TOP_K, tt, D), f32),
                        pltpu.SemaphoreType.DMA(()), pltpu.SemaphoreType.DMA(())],
        compiler_params=_cparams("arbitrary"),
        name="moe_combine",
    )(pos_flat, xf, gate_tk, ys)


def _moe_layer(xf, g, router, wg, wu, wd):
    N, D = xf.shape
    E = router.shape[1]
    tm = min(1024, N)
    hp, idx, gate = _router(xf, g, router)
    e_flat = idx.reshape(-1)
    onehot = (e_flat[:, None] == jnp.arange(E, dtype=i32)[None, :]).astype(i32)
    csum = jnp.cumsum(onehot, axis=0)
    counts = csum[-1]
    padded = ((counts + tm - 1) // tm) * tm
    ends = jnp.cumsum(padded)
    starts = ends - padded
    pos = (jnp.sum((csum + starts[None, :]) * onehot, axis=1) - 1).astype(i32)
    P = TOP_K * N + E * tm
    n_used = (ends[-1] // tm).astype(i32)
    tile_start = jnp.minimum(jnp.arange(P // tm, dtype=i32), n_used - 1) * tm
    tile_expert = jnp.sum((ends[None, :] <= tile_start[:, None]).astype(i32), axis=1)
    slot_tok = jnp.zeros((P,), i32).at[pos].set(jnp.tile(jnp.arange(N, dtype=i32), TOP_K))
    xs = _dispatch(slot_tok, hp)
    ys = _gmm(tile_expert, n_used.reshape(1), xs, wg, wu, wd, tm)
    return _combine(pos, xf, gate.T, ys)


def _kv_kernel(x_ref, g_ref, w_ref, gk_ref, seg_ref, segt_ref, kc_ref, vc_ref, ks_ref, vs_ref, kw_ref, vw_ref):
    h = _rms(x_ref[...], g_ref[...]).astype(bf16)
    kv = _dot(h, w_ref[...])
    wd = kc_ref.shape[1]
    part = lambda p: kv[:, p * wd:(p + 1) * wd]
    seg, segt = seg_ref[...], segt_ref[...]
    kc_ref[...] = part(0)
    vc_ref[...] = part(1)
    G, DH = ks_ref.shape[1], ks_ref.shape[3]
    for k_ref, v_ref, pk, gain in ((ks_ref, vs_ref, 2, gk_ref[0:1, :]), (kw_ref, vw_ref, 4, gk_ref[1:2, :])):
        kn = _head_rms(part(pk), gain, seg, segt).astype(bf16)
        v = part(pk + 1)
        for g in range(G):
            k_ref[0, g] = kn[:, g * DH:(g + 1) * DH]
        for t in range(v_ref.shape[2]):
            vt = v[t * KEY_TILE:(t + 1) * KEY_TILE, :].T
            for g in range(G):
                v_ref[0, g, t] = vt[g * DH:(g + 1) * DH].astype(bf16)


def _kv_project(xf, B, S, norm_kv, w_kv, g_k):
    N, D = xf.shape
    G, DH = N_KV_HEADS, HEAD_DIM
    wd = G * DH
    tk = min(512, S)
    spb = S // tk
    tpk = tk // KEY_TILE
    assert S % tk == 0 and tk % KEY_TILE == 0 and w_kv.shape[1] == 2 * N_BRANCH * wd
    seg, segt = _seg_mats(wd)
    gk = jnp.stack([jnp.tile(g_k[1], G), jnp.tile(g_k[2], G)])
    row = lambda i: (i, 0)
    const = lambda i: (0, 0)
    k_shape = jax.ShapeDtypeStruct((B, G, S, DH), bf16)
    v_shape = jax.ShapeDtypeStruct((B, G, S // KEY_TILE, DH, KEY_TILE), bf16)
    k_spec = pl.BlockSpec((1, G, tk, DH), lambda i: (i // spb, 0, i % spb, 0))
    v_spec = pl.BlockSpec((1, G, tpk, DH, KEY_TILE), lambda i: (i // spb, 0, i % spb, 0, 0))
    return pl.pallas_call(
        _kv_kernel,
        out_shape=(jax.ShapeDtypeStruct((N, wd), f32), jax.ShapeDtypeStruct((N, wd), f32),
                   k_shape, v_shape, k_shape, v_shape),
        grid=(N // tk,),
        in_specs=[
            pl.BlockSpec((tk, D), row),
            pl.BlockSpec((1, D), const),
            pl.BlockSpec(w_kv.shape, const),
            pl.BlockSpec((2, wd), const),
            pl.BlockSpec((wd, NH_PAD), const),
            pl.BlockSpec((NH_PAD, wd), const),
        ],
        out_specs=(pl.BlockSpec((tk, wd), row), pl.BlockSpec((tk, wd), row), k_spec, v_spec, k_spec, v_spec),
        compiler_params=_cparams("parallel"),
        name="kv_project",
    )(xf, norm_kv.reshape(1, D), w_kv.astype(bf16), gk, seg, segt)


def _compress_kernel(c_ref, pos_ref, w1_ref, w2_ref, gk_ref, o_ref):
    kv = pl.program_id(0)
    c = c_ref[0, 0, 0]
    a = _dot((c + pos_ref[0, 0:1, :]).astype(bf16), w1_ref[0, 0])
    b = _dot((c + pos_ref[0, 1:2, :]).astype(bf16), w1_ref[0, 1])
    n = c.shape[0]
    hid = a + pltpu.roll(b, n - 1, axis=0)
    out = _dot(_silu(hid).astype(bf16), w2_ref[0])

    @pl.when(kv == 0)
    def _():
        o_ref[0, 0, 0] = _rms(out, gk_ref[...])

    @pl.when(kv != 0)
    def _():
        o_ref[0, 0, 0] = out


def _compress(kc, vc, B, S, cmp_pos, cmp_w1, cmp_w2, g_k0):
    G, DH = N_KV_HEADS, HEAD_DIM
    nch = S // CMP_STRIDE
    cw = CMP_STRIDE * DH

    def chunks(z):
        return z.reshape(B, nch, CMP_STRIDE, G, DH).transpose(0, 3, 1, 2, 4).reshape(B, G, nch, cw)

    c = jnp.stack([chunks(kc), chunks(vc)])
    pos = cmp_pos.reshape(2, 2, cw)
    w1 = cmp_w1.reshape(2, 2, cw, CMP_HIDDEN).astype(bf16)
    return pl.pallas_call(
        _compress_kernel,
        out_shape=jax.ShapeDtypeStruct((2, B, G, nch, DH), f32),
        grid=(2, B, G),
        in_specs=[
            pl.BlockSpec((1, 1, 1, nch, cw), lambda k, b, g: (k, b, g, 0, 0)),
            pl.BlockSpec((1, 2, cw), lambda k, b, g: (k, 0, 0)),
            pl.BlockSpec((1, 2, cw, CMP_HIDDEN), lambda k, b, g: (k, 0, 0, 0)),
            pl.BlockSpec((1, CMP_HIDDEN, DH), lambda k, b, g: (k, 0, 0)),
            pl.BlockSpec((1, DH), lambda k, b, g: (0, 0)),
        ],
        out_specs=pl.BlockSpec((1, 1, 1, nch, DH), lambda k, b, g: (k, b, g, 0, 0)),
        compiler_params=_cparams("arbitrary", "arbitrary", "arbitrary"),
        name="kv_compress",
    )(c, pos, w1, cmp_w2.astype(bf16), g_k0.reshape(1, DH))


def _build_shared(xf, B, S, norm_kv, w_kv, g_k, cmp_pos, cmp_w1, cmp_w2):
    kc, vc, ks, vst, kw, vwt = _kv_project(xf, B, S, norm_kv, w_kv, g_k)
    cmp = _compress(kc, vc, B, S, cmp_pos, cmp_w1, cmp_w2, g_k[0]).astype(bf16)
    kcm = cmp[0]
    vct = cmp[1].transpose(0, 1, 3, 2)
    return kcm, vct, ks, vst, kw, vwt


def _qg_kernel(x_ref, g_ref, wq_ref, wgate_ref, gq_ref, seg_ref, segt_ref, q_ref, gate_ref):
    h = _rms(x_ref[...], g_ref[...]).astype(bf16)
    q = _head_rms(_dot(h, wq_ref[...]), gq_ref[...], seg_ref[...], segt_ref[...])
    q = q * (HEAD_DIM ** -0.5 * LOG2E)
    gate_ref[...] = jax.nn.sigmoid(_dot(h, wgate_ref[...]))
    nt, G, DH, RT = q_ref.shape
    T = q.shape[0] // nt
    r = RT // T
    for t in range(nt):
        for g in range(G):
            qt = q[t * T:(t + 1) * T, g * r * DH:(g + 1) * r * DH].T
            q_ref[t, g] = jnp.concatenate([qt[k * DH:(k + 1) * DH] for k in range(r)], axis=1).astype(bf16)


def _qg_project(xf, g, w_qg, g_q):
    N, D = xf.shape
    HD = D
    ng = w_qg.shape[1] - HD
    tq = min(512, N)
    G, T = N_KV_HEADS, Q_BLOCK
    RT = HD // (G * HEAD_DIM) * T
    assert N % tq == 0 and tq % T == 0
    seg, segt = _seg_mats(HD)
    row = lambda i: (i, 0)
    const = lambda i: (0, 0)
    return pl.pallas_call(
        _qg_kernel,
        out_shape=(jax.ShapeDtypeStruct((N // T, G, HEAD_DIM, RT), bf16), jax.ShapeDtypeStruct((N, ng), f32)),
        grid=(N // tq,),
        in_specs=[
            pl.BlockSpec((tq, D), row),
            pl.BlockSpec((1, D), const),
            pl.BlockSpec((D, HD), const),
            pl.BlockSpec((D, ng), const),
            pl.BlockSpec((1, HD), const),
            pl.BlockSpec((HD, NH_PAD), const),
            pl.BlockSpec((NH_PAD, HD), const),
        ],
        out_specs=(pl.BlockSpec((tq // T, G, HEAD_DIM, RT), lambda i: (i, 0, 0, 0)), pl.BlockSpec((tq, ng), row)),
        compiler_params=_cparams("parallel"),
        name="qg_project",
    )(xf, g.reshape(1, D), w_qg[:, :HD].astype(bf16), w_qg[:, HD:].astype(bf16),
      jnp.tile(g_q, HD // HEAD_DIM).reshape(1, HD), seg, segt)


def _fold8(x, op):
    return op(x.reshape(x.shape[0] // 8, 8, x.shape[1]), axis=0)


def _nsa_kernel(q_ref, gt_ref, kc_ref, vct_ref, ks_ref, vst_ref, kw_ref, vwt_ref, cbn_ref, cstep_ref, tb_ref, c2st_ref,
                o_ref, cb_scr, pen_ref, penf_ref, s_scr, m8_scr, l8_scr, acc_scr, *, r, nsb, n_sel):
    qb = pl.program_id(1)
    t0 = qb * Q_BLOCK
    T = Q_BLOCK
    RT = r * T
    G = q_ref.shape[1]
    nch = kc_ref.shape[2]
    nkt = s_scr.shape[1] // KEY_TILE - SLC_UNROLL

    def tile_rows(kt):
        return pl.ds(pl.multiple_of(kt * KEY_TILE, KEY_TILE), KEY_TILE)

    def add_block_pen(s, ref, g, kt):
        return jnp.concatenate([s[:SLC_LEN] + ref[g, pl.ds(2 * kt, 1), :],
                                s[SLC_LEN:] + ref[g, pl.ds(2 * kt + 1, 1), :]], axis=0)

    for g in range(G):
        qT = q_ref[0, g]

        cb_scr[g] = cstep_ref[g, pl.ds(pl.multiple_of(nch - (qb + 1) * CMP_PER_Q, 8), CMP_PAD + nch), :]
        cb_scr[g, pl.ds(pl.multiple_of(qb * CMP_PER_Q, 8), CMP_NEAR), :] = cbn_ref[g]
        s = _dot(kc_ref[0, g], qT) + cb_scr[g, CMP_PAD:CMP_PAD + nch, :]
        p = jnp.where(s > 0.5 * NEG, jnp.exp2(s - jnp.max(s, axis=0, keepdims=True)), 0.0)
        p = p / jnp.maximum(jnp.sum(p, axis=0, keepdims=True), TINY)
        o_cmp = _dot(vct_ref[0, g], p.astype(bf16))

        psum = p[:, 0:T]
        for k in range(1, r):
            psum = psum + p[:, k * T:(k + 1) * T]
        hi = psum.astype(bf16)
        lo = (psum - hi.astype(f32)).astype(bf16)
        imp = _dot(c2st_ref[...], hi) + _dot(c2st_ref[...], lo)
        jb = lax.broadcasted_iota(i32, (nsb, T), 0)
        blk_q = jnp.right_shift(t0 + lax.broadcasted_iota(i32, (nsb, T), 1), SLC_LEN.bit_length() - 1)
        forced = (jb == 0) | (jb == blk_q) | (jb == blk_q - 1)
        score = jnp.where(forced, FORCED_SCORE, jnp.where(jb <= blk_q, imp, NEG))
        pen = jnp.full((nsb, T), NEG, f32)
        for _ in range(n_sel):
            mx = jnp.max(score, axis=0, keepdims=True)
            first = jnp.min(jnp.where(score == mx, jb, nsb), axis=0, keepdims=True)
            hit = jb == first
            pen = jnp.where(hit, 0.0, pen)
            score = jnp.where(hit, -jnp.inf, score)
        pen = jnp.concatenate([pen] * r, axis=1)
        pen_ref[g] = pen
        penf_ref[g] = pen + cstep_ref[g, 0:1, :]

        win = []
        for d in range(WINDOW // KEY_TILE, -1, -1):
            kt = qb - d
            ktc = jnp.maximum(kt, 0)
            tab = tb_ref[g, jnp.where(kt >= 0, d, TB_NONE)]
            win.append((_dot(kw_ref[0, g, tile_rows(ktc), :], qT) + tab, vwt_ref[0, g, ktc]))
        m8 = _fold8(win[0][0], jnp.max)
        for s_d, _ in win[1:]:
            m8 = jnp.maximum(m8, _fold8(s_d, jnp.max))
        m = jnp.max(m8, axis=0, keepdims=True)
        l8 = jnp.zeros((8, RT), f32)
        acc = jnp.zeros((HEAD_DIM, RT), f32)
        for s_d, v_d in win:
            p_d = jnp.exp2(s_d - m)
            l8 = l8 + _fold8(p_d, jnp.sum)
            acc = acc + _dot(v_d, p_d.astype(bf16))
        o_win = acc * (1.0 / jnp.maximum(jnp.sum(l8, axis=0, keepdims=True), TINY))

        gt = gt_ref[0, g]
        o_ref[0, g] = gt[0:1] * o_cmp + gt[2:3] * o_win

        ktp = jnp.maximum(qb - 1, 0)
        s_prev = add_block_pen(_dot(ks_ref[0, g, tile_rows(ktp), :], qT)
                               + tb_ref[g, jnp.where(qb >= 1, TB_PREV, TB_NONE)], pen_ref, g, ktp)
        s_scr[g, tile_rows(ktp), :] = s_prev
        s_own = add_block_pen(_dot(ks_ref[0, g, tile_rows(qb), :], qT) + tb_ref[g, TB_OWN], pen_ref, g, qb)
        s_scr[g, tile_rows(qb), :] = s_own
        m8_scr[g] = jnp.maximum(_fold8(s_prev, jnp.max), _fold8(s_own, jnp.max))
        l8_scr[g] = jnp.zeros((8, RT), f32)
        acc_scr[g] = jnp.zeros((HEAD_DIM, RT), f32)

    def pass_a(i, c):
        for g in range(G):
            qT = q_ref[0, g]
            m8 = m8_scr[g]
            for u in range(SLC_UNROLL):
                kt = SLC_UNROLL * i + u
                live = kt < qb - 1
                ktc = jnp.minimum(kt, nkt - 1)
                s = add_block_pen(_dot(ks_ref[0, g, tile_rows(ktc), :], qT) + jnp.where(live, 0.0, NEG),
                                  penf_ref, g, ktc)
                s_scr[g, tile_rows(jnp.where(live, kt, nkt + u)), :] = s
                m8 = jnp.maximum(m8, _fold8(s, jnp.max))
            m8_scr[g] = m8
        return c

    lax.fori_loop(0, (jnp.maximum(qb - 1, 0) + SLC_UNROLL - 1) // SLC_UNROLL, pass_a, 0)

    def pass_b(i, c):
        for g in range(G):
            m = jnp.max(m8_scr[g], axis=0, keepdims=True)
            l8 = l8_scr[g]
            acc = acc_scr[g]
            for u in range(SLC_UNROLL):
                kt = SLC_UNROLL * i + u
                ktc = jnp.minimum(kt, qb)
                p_u = jnp.exp2(s_scr[g, tile_rows(ktc), :] - (m + jnp.where(kt <= qb, 0.0, -NEG)))
                l8 = l8 + _fold8(p_u, jnp.sum)
                acc = acc + _dot(vst_ref[0, g, ktc], p_u.astype(bf16))
            l8_scr[g] = l8
            acc_scr[g] = acc
        return c

    lax.fori_loop(0, qb // SLC_UNROLL + 1, pass_b, 0)

    for g in range(G):
        o_slc = acc_scr[g] * (1.0 / jnp.maximum(jnp.sum(l8_scr[g], axis=0, keepdims=True), TINY))
        o_ref[0, g] = o_ref[0, g] + gt_ref[0, g][1:2] * o_slc


def _bias_tables(rel_bias, S):
    G = N_KV_HEADS
    H = rel_bias.shape[1]
    r = H // G
    T = KEY_TILE
    n = jnp.arange(S + 2 * T, dtype=i32)
    max_exact = N_BUCKETS // 2
    nf = jnp.maximum(n, 1).astype(f32)
    large = max_exact + (jnp.log(nf / max_exact) / math.log(MAX_DISTANCE / max_exact)
                         * (N_BUCKETS - max_exact)).astype(i32)
    bucket = jnp.where(n < max_exact, n, jnp.minimum(large, N_BUCKETS - 1))
    bias1d = rel_bias.astype(f32)[bucket] * LOG2E

    def per_group(tab):
        lead = tab.shape[:-2]
        k = len(lead)
        t = jnp.moveaxis(tab, -1, 0).reshape((G, r) + lead + (T,))
        return jnp.moveaxis(t, 1, k + 1).reshape((G,) + lead + (r * T,))

    def masked(dist, ok):
        return jnp.where(jnp.asarray(ok)[..., None], bias1d[np.maximum(dist, 0)], NEG)

    kj, qi = np.arange(T)[:, None], np.arange(T)[None, :]
    dist = np.stack([d * T + qi - kj for d in range(3)])
    ok = np.stack([dist[0] >= 0, np.ones((T, T), bool), dist[2] < WINDOW])
    tb = jnp.concatenate([masked(dist, ok), jnp.full((1, T, T, H), NEG, f32)])
    cend = (np.arange(CMP_NEAR)[:, None] - CMP_PAD) * CMP_STRIDE + CMP_LEN - 1
    dcn = np.arange(T)[None, :] - cend
    rows = CMP_PAD + S // CMP_STRIDE
    far = jnp.broadcast_to(bias1d[-1][None, None, :], (rows, T, H))
    cstep = jnp.concatenate([far, jnp.full((rows, T, H), NEG, f32)])
    return per_group(tb), per_group(masked(dcn, dcn >= 0)), per_group(cstep)


def _nsa_attention(q, gates_t, shared, tables, B, S):
    kcm, vct, ks, vst, kw, vwt = shared
    tb, cbn, cstep = tables
    G, DH, T = N_KV_HEADS, HEAD_DIM, Q_BLOCK
    RT = q.shape[3]
    r = RT // T
    nqb = S // T
    nch = S // CMP_STRIDE
    nsb = S // SLC_LEN
    nkt = S // KEY_TILE
    cmp_start = np.arange(nch) * CMP_STRIDE
    slc_start = np.arange(nsb) * SLC_LEN
    overlap = np.clip(np.minimum(cmp_start[:, None] + CMP_LEN, slc_start[None, :] + SLC_LEN)
                      - np.maximum(cmp_start[:, None], slc_start[None, :]), 0, None) / CMP_LEN
    overlap[nch - 1] = 0.0
    c2st = jnp.asarray(overlap.T, bf16)
    qmap = lambda b, i: (b * nqb + i, 0, 0, 0)
    bat = lambda b, i: (b, 0, 0, 0)
    bat5 = lambda b, i: (b, 0, 0, 0, 0)
    once = pl.Buffered(1)
    return pl.pallas_call(
        functools.partial(_nsa_kernel, r=r, nsb=nsb, n_sel=min(N_SEL, nsb)),
        out_shape=jax.ShapeDtypeStruct(q.shape, f32),
        grid=(B, nqb),
        in_specs=[
            pl.BlockSpec((1, G, DH, RT), qmap),
            pl.BlockSpec((1, G, N_BRANCH, RT), qmap),
            pl.BlockSpec((1, G, nch, DH), bat),
            pl.BlockSpec((1, G, DH, nch), bat),
            pl.BlockSpec((1, G, S, DH), bat),
            pl.BlockSpec((1, G, nkt, DH, KEY_TILE), bat5),
            pl.BlockSpec((1, G, S, DH), bat),
            pl.BlockSpec((1, G, nkt, DH, KEY_TILE), bat5),
            pl.BlockSpec(cbn.shape, lambda b, i: (0, 0, 0), pipeline_mode=once),
            pl.BlockSpec(cstep.shape, lambda b, i: (0, 0, 0), pipeline_mode=once),
            pl.BlockSpec(tb.shape, lambda b, i: (0, 0, 0, 0), pipeline_mode=once),
            pl.BlockSpec((nsb, nch), lambda b, i: (0, 0), pipeline_mode=once),
        ],
        out_specs=pl.BlockSpec((1, G, DH, RT), qmap),
        scratch_shapes=[pltpu.VMEM((G, CMP_PAD + nch, RT), f32), pltpu.VMEM((G, nsb, RT), f32),
                        pltpu.VMEM((G, nsb, RT), f32), pltpu.VMEM((G, (nkt + SLC_UNROLL) * KEY_TILE, RT), f32),
                        pltpu.VMEM((G, 8, RT), f32), pltpu.VMEM((G, 8, RT), f32), pltpu.VMEM((G, DH, RT), f32)],
        compiler_params=_cparams("parallel", "arbitrary"),
        name="nsa_attention",
    )(q, gates_t, kcm, vct, ks, vst, kw, vwt, cbn, cstep, tb, c2st)


def _oproj_kernel(x_ref, a_ref, w_ref, o_ref):
    nt, G, DH, RT = a_ref.shape
    T = x_ref.shape[0] // nt
    r = RT // T
    rows = []
    for t in range(nt):
        cols = []
        for g in range(G):
            a = a_ref[t, g]
            cols.append(jnp.concatenate([a[:, k * T:(k + 1) * T] for k in range(r)], axis=0).T)
        rows.append(jnp.concatenate(cols, axis=1))
    attn = jnp.concatenate(rows, axis=0)
    o_ref[...] = x_ref[...] + _dot(attn.astype(bf16), w_ref[...])


def _out_project(xf, attn, w_o):
    N, D = xf.shape
    nt_all, G, DH, RT = attn.shape
    T = N // nt_all
    to = min(512, N)
    assert N % to == 0 and to % T == 0
    row = lambda i: (i, 0)
    return pl.pallas_call(
        _oproj_kernel,
        out_shape=jax.ShapeDtypeStruct((N, D), f32),
        grid=(N // to,),
        in_specs=[pl.BlockSpec((to, D), row), pl.BlockSpec((to // T, G, DH, RT), lambda i: (i, 0, 0, 0)),
                  pl.BlockSpec(w_o.shape, lambda i: (0, 0))],
        out_specs=pl.BlockSpec((to, D), row),
        compiler_params=_cparams("parallel"),
        name="out_project",
    )(xf, attn, w_o.astype(bf16))


def _nsa_layer(xf, B, S, g, w_qg, g_q, w_o, shared, tables):
    N = xf.shape[0]
    G, T = N_KV_HEADS, Q_BLOCK
    q, gates = _qg_project(xf, g, w_qg, g_q)
    r = gates.shape[1] // (G * N_BRANCH)
    gates_t = gates.reshape(N // T, T, G, r, N_BRANCH).transpose(0, 2, 4, 3, 1).reshape(N // T, G, N_BRANCH, r * T)
    attn = _nsa_attention(q, gates_t, shared, tables, B, S)
    return _out_project(xf, attn, w_o)


def kernel(x, rel_bias, norm_mix, norm_ffn, pool_w, pool_scale, norm_kv, w_kv, g_k, cmp_pos, cmp_w1, cmp_w2,
           w_qg, g_q, w_o, ffn_wg, ffn_wu, ffn_wd, router, moe_wg, moe_wu, moe_wd):
    B, S, D = x.shape
    depth = norm_mix.shape[0]
    n_a = depth // 2
    assert S % Q_BLOCK == 0
    xf = x.reshape(B * S, D)
    shared = None
    tables = _bias_tables(rel_bias, S)
    for layer in range(depth):
        if layer < n_a:
            xf = _pool_layer(xf.reshape(B, S, D), norm_mix[layer], pool_w[layer], pool_scale[layer]).reshape(B * S, D)
        else:
            j = layer - n_a
            xf = _nsa_layer(xf, B, S, norm_mix[layer], w_qg[j], g_q[j], w_o[j], shared, tables)
        i = layer // 2
        if layer % 2 == 0:
            xf = _ffn_layer(xf, norm_ffn[layer], ffn_wg[i], ffn_wu[i], ffn_wd[i])
        else:
            xf = _moe_layer(xf, norm_ffn[layer], router[i], moe_wg[i], moe_wu[i], moe_wd[i])
        if layer == n_a - 1:
            shared = _build_shared(xf, B, S, norm_kv, w_kv, g_k, cmp_pos, cmp_w1, cmp_w2)
    return xf.reshape(B, S, D)
```

```python
import functools
import math

import numpy as np
import jax
import jax.numpy as jnp
from jax import lax
from jax.experimental import pallas as pl
from jax.experimental.pallas import tpu as pltpu
from jax.experimental.pallas import tpu_sc as plsc

f32 = jnp.float32
bf16 = jnp.bfloat16
i32 = jnp.int32
u32 = jnp.uint32

POOL_WINDOWS = (2, 4, 8, 16)
HEAD_DIM = 64
N_KV_HEADS = 4
N_BRANCH = 3
CMP_LEN = 32
CMP_STRIDE = 16
CMP_HIDDEN = 4 * HEAD_DIM
SLC_LEN = 64
N_SEL = 4
WINDOW = 256
Q_BLOCK = 128
FORCED_SCORE = 1.0e4
N_BUCKETS = 32
MAX_DISTANCE = 128
N_EXPERTS = 8
TOP_K = 2
EPS = 1e-6
NEG = -1e30
TINY = 1e-30
LOG2E = math.log2(math.e)
TB_OWN, TB_PREV, TB_WIN2, TB_NONE = range(4)

KEY_TILE = Q_BLOCK
SLC_UNROLL = 2
POOL_HALO = 16
NH_PAD = 16
V7X_VMEM_LIMIT = 56 * 1024 * 1024
V7X_SC_CORES, V7X_SC_SUBCORES = 2, 16
SC_MAX_INDEX_VECTOR = 128
SC_DISPATCH_CHUNK = 64
SC_COMBINE_CHUNK = 32

CMP_PER_Q = Q_BLOCK // CMP_STRIDE
CMP_PAD = 2 * CMP_PER_Q
CMP_NEAR = 3 * CMP_PER_Q

assert CMP_LEN == 2 * CMP_STRIDE and KEY_TILE == 2 * SLC_LEN and WINDOW == 2 * KEY_TILE
assert max(POOL_WINDOWS) <= POOL_HALO
assert 2 * KEY_TILE - (Q_BLOCK - 1) >= MAX_DISTANCE
assert (CMP_PAD + 1) * CMP_STRIDE - (CMP_LEN - 1) >= MAX_DISTANCE and CMP_PER_Q % 8 == 0


def _cparams(*sem):
    return pltpu.CompilerParams(dimension_semantics=sem, vmem_limit_bytes=V7X_VMEM_LIMIT)


def _rms(xf, g):
    ms = jnp.mean(xf * xf, axis=-1, keepdims=True)
    return (xf * lax.rsqrt(ms + EPS)) * g


def _dot(a, b):
    return jnp.dot(a, b, preferred_element_type=f32)


def _dot_hilo(a, b):
    hi = a.astype(bf16)
    lo = (a - hi.astype(f32)).astype(bf16)
    return _dot(hi, b) + _dot(lo, b)


def _head_rms(z, gvec, seg, segt):
    ssq = _dot_hilo(z * z, seg)
    inv = lax.rsqrt(ssq * (1.0 / HEAD_DIM) + EPS)
    return (z * _dot_hilo(inv, segt)) * gvec


def _silu(a):
    return a * jax.nn.sigmoid(a)


def _seg_mats(width):
    heads = width // HEAD_DIM
    seg = np.zeros((width, NH_PAD), np.float32)
    seg[np.arange(width), np.arange(width) // HEAD_DIM] = 1.0
    assert heads <= NH_PAD
    return jnp.asarray(seg, bf16), jnp.asarray(seg.T, bf16)


def _pool_kernel(x_ref, halo_ref, g_ref, w_ref, scale_ref, o_ref, *, tp, cg):
    i = pl.program_id(1)
    x = x_ref[0]
    xh = jnp.concatenate([halo_ref[0], x], axis=0)
    h = _rms(xh, g_ref[...])
    row = lax.broadcasted_iota(i32, (tp + POOL_HALO, 1), 0)
    t_abs = i * tp + row - POOL_HALO
    h = jnp.where(t_abs >= 0, h, 0.0)
    outs = []
    for gi, w in enumerate(POOL_WINDOWS):
        hg = h[:, gi * cg:(gi + 1) * cg]
        s = hg
        sh = 1
        while sh < w:
            s = s + pltpu.roll(s, sh, axis=0)
            sh *= 2
        cnt = jnp.clip(t_abs + 1, 1, w).astype(f32)
        diff = (s / cnt - hg)[POOL_HALO:]
        outs.append(_dot(diff.astype(bf16), w_ref[gi]))
    y = jnp.concatenate(outs, axis=1)
    o_ref[0] = x + y * scale_ref[...]


def _pool_layer(x3, g, w_grp, scale):
    B, S, D = x3.shape
    tp = min(512, S)
    cg = D // len(POOL_WINDOWS)
    assert S % tp == 0 and tp % POOL_HALO == 0 and all(w & (w - 1) == 0 for w in POOL_WINDOWS)
    hb = tp // POOL_HALO
    return pl.pallas_call(
        functools.partial(_pool_kernel, tp=tp, cg=cg),
        out_shape=jax.ShapeDtypeStruct((B, S, D), f32),
        grid=(B, S // tp),
        in_specs=[
            pl.BlockSpec((1, tp, D), lambda b, i: (b, i, 0)),
            pl.BlockSpec((1, POOL_HALO, D), lambda b, i: (b, jnp.maximum(i * hb - 1, 0), 0)),
            pl.BlockSpec((1, D), lambda b, i: (0, 0)),
            pl.BlockSpec((len(POOL_WINDOWS), cg, cg), lambda b, i: (0, 0, 0)),
            pl.BlockSpec((1, D), lambda b, i: (0, 0)),
        ],
        out_specs=pl.BlockSpec((1, tp, D), lambda b, i: (b, i, 0)),
        compiler_params=_cparams("parallel", "arbitrary"),
        name="pool_layer",
    )(x3, x3, g.reshape(1, D), w_grp.astype(bf16), scale.reshape(1, D))


def _ffn_kernel(x_ref, g_ref, wg_ref, wu_ref, wd_ref, o_ref, h_ref, acc_ref):
    j = pl.program_id(1)

    @pl.when(j == 0)
    def _():
        x = x_ref[...]
        h_ref[...] = _rms(x, g_ref[...]).astype(bf16)
        acc_ref[...] = x

    h = h_ref[...]
    act = _silu(_dot(h, wg_ref[...])) * _dot(h, wu_ref[...])
    acc_ref[...] += _dot(act.astype(bf16), wd_ref[...])

    @pl.when(j == pl.num_programs(1) - 1)
    def _():
        o_ref[...] = acc_ref[...]


def _ffn_layer(xf, g, wg, wu, wd):
    N, D = xf.shape
    F = wg.shape[1]
    tm = min(1024, N)
    tf = 512
    assert N % tm == 0 and F % tf == 0
    return pl.pallas_call(
        _ffn_kernel,
        out_shape=jax.ShapeDtypeStruct((N, D), f32),
        grid=(N // tm, F // tf),
        in_specs=[
            pl.BlockSpec((tm, D), lambda i, j: (i, 0)),
            pl.BlockSpec((1, D), lambda i, j: (0, 0)),
            pl.BlockSpec((D, tf), lambda i, j: (0, j)),
            pl.BlockSpec((D, tf), lambda i, j: (0, j)),
            pl.BlockSpec((tf, D), lambda i, j: (j, 0)),
        ],
        out_specs=pl.BlockSpec((tm, D), lambda i, j: (i, 0)),
        scratch_shapes=[pltpu.VMEM((tm, D), bf16), pltpu.VMEM((tm, D), f32)],
        compiler_params=_cparams("parallel", "arbitrary"),
        name="ffn_dense",
    )(xf, g.reshape(1, D), wg.astype(bf16), wu.astype(bf16), wd.astype(bf16))


def _router_kernel(x_ref, g_ref, rt_ref, hp_ref, idx_ref, gate_ref):
    h = _rms(x_ref[...], g_ref[...])
    half = h.shape[1] // 2
    bits = pltpu.bitcast(h.astype(bf16).astype(f32), u32)
    hp_ref[...] = (bits[:, :half] & jnp.uint32(0xFFFF0000)) | (bits[:, half:] >> 16)
    logits = lax.dot_general(rt_ref[...], h, (((1,), (1,)), ((), ())),
                             precision=lax.Precision.HIGHEST, preferred_element_type=f32)
    ne = logits.shape[0]
    row = lax.broadcasted_iota(i32, logits.shape, 0)
    m1 = jnp.max(logits, axis=0, keepdims=True)
    i1 = jnp.min(jnp.where(logits == m1, row, ne), axis=0, keepdims=True)
    rest = jnp.where(row == i1, -jnp.inf, logits)
    m2 = jnp.max(rest, axis=0, keepdims=True)
    i2 = jnp.min(jnp.where(rest == m2, row, ne), axis=0, keepdims=True)
    e2 = jnp.exp(m2 - m1)
    den = 1.0 + e2
    idx_ref[...] = jnp.concatenate([i1, i2], axis=0)
    gate_ref[...] = jnp.concatenate([1.0 / den, e2 / den], axis=0)


def _router(xf, g, router):
    N, D = xf.shape
    E = router.shape[1]
    tr = min(1024, N)
    assert N % tr == 0 and TOP_K == 2
    return pl.pallas_call(
        _router_kernel,
        out_shape=(jax.ShapeDtypeStruct((N, D // 2), u32),
                   jax.ShapeDtypeStruct((TOP_K, N), i32),
                   jax.ShapeDtypeStruct((TOP_K, N), f32)),
        grid=(N // tr,),
        in_specs=[
            pl.BlockSpec((tr, D), lambda i: (i, 0)),
            pl.BlockSpec((1, D), lambda i: (0, 0)),
            pl.BlockSpec((E, D), lambda i: (0, 0)),
        ],
        out_specs=(pl.BlockSpec((tr, D // 2), lambda i: (i, 0)),
                   pl.BlockSpec((TOP_K, tr), lambda i: (0, i)),
                   pl.BlockSpec((TOP_K, tr), lambda i: (0, i))),
        compiler_params=_cparams("parallel"),
        name="moe_router",
    )(xf, g.reshape(1, D), router.T)


def _sc_gather_rows(table, idx, chunk):
    B = idx.shape[0]
    D = table.shape[1]
    workers = V7X_SC_CORES * V7X_SC_SUBCORES
    per_w = B // workers
    assert B % (8 * workers) == 0 and per_w % chunk == 0 and chunk % 8 == 0 and chunk <= SC_MAX_INDEX_VECTOR
    mesh = plsc.VectorSubcoreMesh(core_axis_name="c", subcore_axis_name="s")

    @functools.partial(
        pl.kernel, mesh=mesh, out_type=jax.ShapeDtypeStruct((B, D), table.dtype),
        scratch_types=[pltpu.VMEM((chunk,), i32), pltpu.VMEM((chunk, D), table.dtype), pltpu.SemaphoreType.DMA])
    def gather(table_hbm, idx_hbm, out_hbm, idx_v, rows_v, sem):
        base = (lax.axis_index("s") * V7X_SC_CORES + lax.axis_index("c")) * per_w

        @pl.loop(0, per_w // chunk)
        def _(c):
            off = pl.multiple_of(base + c * chunk, 8)
            pltpu.sync_copy(idx_hbm.at[pl.ds(off, chunk)], idx_v)
            pltpu.async_copy(table_hbm.at[idx_v], rows_v, sem).wait()
            pltpu.sync_copy(rows_v, out_hbm.at[pl.ds(off, chunk)])

    return gather(table, idx)


def _gmm_kernel(te_ref, nu_ref, xs_ref, wg_ref, wu_ref, wd_ref, o_ref, h_ref, acc_ref):
    i = pl.program_id(0)
    j = pl.program_id(1)

    @pl.when(i < nu_ref[0])
    def _():
        @pl.when(j == 0)
        def _():
            w = xs_ref[...]
            left = pltpu.bitcast(w & jnp.uint32(0xFFFF0000), f32)
            right = pltpu.bitcast(w << 16, f32)
            h_ref[...] = jnp.concatenate([left, right], axis=1).astype(bf16)
            acc_ref[...] = jnp.zeros_like(acc_ref)

        h = h_ref[...]
        act = _silu(_dot(h, wg_ref[0])) * _dot(h, wu_ref[0])
        acc_ref[...] += _dot(act.astype(bf16), wd_ref[0])

        @pl.when(j == pl.num_programs(1) - 1)
        def _():
            o_ref[...] = acc_ref[...]

    @pl.when((i >= nu_ref[0]) & (j == 0))
    def _():
        o_ref[...] = jnp.zeros_like(o_ref)


def _gmm(tile_expert, n_used, xs, wg, wu, wd, tm):
    P, half = xs.shape
    D = 2 * half
    F = wg.shape[2]
    tf = 512
    assert P % tm == 0 and F % tf == 0

    nf = F // tf

    def row_map(i, j, te, nu):
        return (jnp.minimum(i, nu[0] - 1), 0)

    def ff(i, j, nu):
        return jnp.where(i < nu[0], j, nf - 1)

    grid_spec = pltpu.PrefetchScalarGridSpec(
        num_scalar_prefetch=2,
        grid=(P // tm, nf),
        in_specs=[
            pl.BlockSpec((tm, half), row_map),
            pl.BlockSpec((1, D, tf), lambda i, j, te, nu: (te[i], 0, ff(i, j, nu))),
            pl.BlockSpec((1, D, tf), lambda i, j, te, nu: (te[i], 0, ff(i, j, nu))),
            pl.BlockSpec((1, tf, D), lambda i, j, te, nu: (te[i], ff(i, j, nu), 0)),
        ],
        out_specs=pl.BlockSpec((tm, D), lambda i, j, te, nu: (i, 0)),
        scratch_shapes=[pltpu.VMEM((tm, D), bf16), pltpu.VMEM((tm, D), f32)],
    )
    return pl.pallas_call(
        _gmm_kernel,
        out_shape=jax.ShapeDtypeStruct((P, D), f32),
        grid_spec=grid_spec,
        compiler_params=_cparams("arbitrary", "arbitrary"),
        name="moe_gmm",
    )(tile_expert, n_used, xs, wg.astype(bf16), wu.astype(bf16), wd.astype(bf16))


def _combine_kernel(x_ref, gate_ref, y_ref, o_ref):
    gate = gate_ref[...]
    acc = x_ref[...]
    for k in range(TOP_K):
        acc = acc + y_ref[k] * gate[:, k:k + 1]
    o_ref[...] = acc


def _combine(xf, gate_tk, y_tok):
    N, D = xf.shape
    tt = min(1024, N)
    assert N % tt == 0
    return pl.pallas_call(
        _combine_kernel,
        out_shape=jax.ShapeDtypeStruct((N, D), f32),
        grid=(N // tt,),
        in_specs=[pl.BlockSpec((tt, D), lambda i: (i, 0)),
                  pl.BlockSpec((tt, TOP_K), lambda i: (i, 0)),
                  pl.BlockSpec((TOP_K, tt, D), lambda i: (0, i, 0))],
        out_specs=pl.BlockSpec((tt, D), lambda i: (i, 0)),
        compiler_params=_cparams("parallel"),
        name="moe_combine",
    )(xf, gate_tk, y_tok)


def _moe_layer(xf, g, router, wg, wu, wd):
    N, D = xf.shape
    E = router.shape[1]
    tm = min(1024, N)
    hp, idx, gate = _router(xf, g, router)
    e_flat = idx.reshape(-1)
    onehot = (e_flat[:, None] == jnp.arange(E, dtype=i32)[None, :]).astype(i32)
    csum = jnp.cumsum(onehot, axis=0)
    counts = csum[-1]
    padded = ((counts + tm - 1) // tm) * tm
    ends = jnp.cumsum(padded)
    starts = ends - padded
    pos = (jnp.sum((csum + starts[None, :]) * onehot, axis=1) - 1).astype(i32)
    P = TOP_K * N + E * tm
    n_used = (ends[-1] // tm).astype(i32)
    tile_start = jnp.minimum(jnp.arange(P // tm, dtype=i32), n_used - 1) * tm
    tile_expert = jnp.sum((ends[None, :] <= tile_start[:, None]).astype(i32), axis=1)
    slot_tok = jnp.zeros((P,), i32).at[pos].set(jnp.tile(jnp.arange(N, dtype=i32), TOP_K))
    xs = _sc_gather_rows(lax.bitcast_convert_type(hp, i32), slot_tok, SC_DISPATCH_CHUNK)
    ys = _gmm(tile_expert, n_used.reshape(1), lax.bitcast_convert_type(xs, u32), wg, wu, wd, tm)
    y_tok = _sc_gather_rows(ys, pos, SC_COMBINE_CHUNK).reshape(TOP_K, N, D)
    return _combine(xf, gate.T, y_tok)


def _kv_kernel(x_ref, g_ref, w_ref, gk_ref, seg_ref, segt_ref, kc_ref, vc_ref, ks_ref, vs_ref, kw_ref, vw_ref):
    h = _rms(x_ref[...], g_ref[...]).astype(bf16)
    kv = _dot(h, w_ref[...])
    wd = kc_ref.shape[1]
    part = lambda p: kv[:, p * wd:(p + 1) * wd]
    seg, segt = seg_ref[...], segt_ref[...]
    kc_ref[...] = part(0)
    vc_ref[...] = part(1)
    G, DH = ks_ref.shape[1], ks_ref.shape[3]
    for k_ref, v_ref, pk, gain in ((ks_ref, vs_ref, 2, gk_ref[0:1, :]), (kw_ref, vw_ref, 4, gk_ref[1:2, :])):
        kn = _head_rms(part(pk), gain, seg, segt).astype(bf16)
        v = part(pk + 1)
        for g in range(G):
            k_ref[0, g] = kn[:, g * DH:(g + 1) * DH]
        for t in range(v_ref.shape[2]):
            vt = v[t * KEY_TILE:(t + 1) * KEY_TILE, :].T
            for g in range(G):
                v_ref[0, g, t] = vt[g * DH:(g + 1) * DH].astype(bf16)


def _kv_project(xf, B, S, norm_kv, w_kv, g_k):
    N, D = xf.shape
    G, DH = N_KV_HEADS, HEAD_DIM
    wd = G * DH
    tk = min(512, S)
    spb = S // tk
    tpk = tk // KEY_TILE
    assert S % tk == 0 and tk % KEY_TILE == 0 and w_kv.shape[1] == 2 * N_BRANCH * wd
    seg, segt = _seg_mats(wd)
    gk = jnp.stack([jnp.tile(g_k[1], G), jnp.tile(g_k[2], G)])
    row = lambda i: (i, 0)
    const = lambda i: (0, 0)
    k_shape = jax.ShapeDtypeStruct((B, G, S, DH), bf16)
    v_shape = jax.ShapeDtypeStruct((B, G, S // KEY_TILE, DH, KEY_TILE), bf16)
    k_spec = pl.BlockSpec((1, G, tk, DH), lambda i: (i // spb, 0, i % spb, 0))
    v_spec = pl.BlockSpec((1, G, tpk, DH, KEY_TILE), lambda i: (i // spb, 0, i % spb, 0, 0))
    return pl.pallas_call(
        _kv_kernel,
        out_shape=(jax.ShapeDtypeStruct((N, wd), f32), jax.ShapeDtypeStruct((N, wd), f32),
                   k_shape, v_shape, k_shape, v_shape),
        grid=(N // tk,),
        in_specs=[
            pl.BlockSpec((tk, D), row),
            pl.BlockSpec((1, D), const),
            pl.BlockSpec(w_kv.shape, const),
            pl.BlockSpec((2, wd), const),
            pl.BlockSpec((wd, NH_PAD), const),
            pl.BlockSpec((NH_PAD, wd), const),
        ],
        out_specs=(pl.BlockSpec((tk, wd), row), pl.BlockSpec((tk, wd), row), k_spec, v_spec, k_spec, v_spec),
        compiler_params=_cparams("parallel"),
        name="kv_project",
    )(xf, norm_kv.reshape(1, D), w_kv.astype(bf16), gk, seg, segt)


def _compress_kernel(c_ref, pos_ref, w1_ref, w2_ref, gk_ref, o_ref):
    kv = pl.program_id(0)
    c = c_ref[0, 0, 0]
    a = _dot((c + pos_ref[0, 0:1, :]).astype(bf16), w1_ref[0, 0])
    b = _dot((c + pos_ref[0, 1:2, :]).astype(bf16), w1_ref[0, 1])
    n = c.shape[0]
    hid = a + pltpu.roll(b, n - 1, axis=0)
    out = _dot(_silu(hid).astype(bf16), w2_ref[0])

    @pl.when(kv == 0)
    def _():
        o_ref[0, 0, 0] = _rms(out, gk_ref[...])

    @pl.when(kv != 0)
    def _():
        o_ref[0, 0, 0] = out


def _compress(kc, vc, B, S, cmp_pos, cmp_w1, cmp_w2, g_k0):
    G, DH = N_KV_HEADS, HEAD_DIM
    nch = S // CMP_STRIDE
    cw = CMP_STRIDE * DH

    def chunks(z):
        return z.reshape(B, nch, CMP_STRIDE, G, DH).transpose(0, 3, 1, 2, 4).reshape(B, G, nch, cw)

    c = jnp.stack([chunks(kc), chunks(vc)])
    pos = cmp_pos.reshape(2, 2, cw)
    w1 = cmp_w1.reshape(2, 2, cw, CMP_HIDDEN).astype(bf16)
    return pl.pallas_call(
        _compress_kernel,
        out_shape=jax.ShapeDtypeStruct((2, B, G, nch, DH), f32),
        grid=(2, B, G),
        in_specs=[
            pl.BlockSpec((1, 1, 1, nch, cw), lambda k, b, g: (k, b, g, 0, 0)),
            pl.BlockSpec((1, 2, cw), lambda k, b, g: (k, 0, 0)),
            pl.BlockSpec((1, 2, cw, CMP_HIDDEN), lambda k, b, g: (k, 0, 0, 0)),
            pl.BlockSpec((1, CMP_HIDDEN, DH), lambda k, b, g: (k, 0, 0)),
            pl.BlockSpec((1, DH), lambda k, b, g: (0, 0)),
        ],
        out_specs=pl.BlockSpec((1, 1, 1, nch, DH), lambda k, b, g: (k, b, g, 0, 0)),
        compiler_params=_cparams("arbitrary", "arbitrary", "arbitrary"),
        name="kv_compress",
    )(c, pos, w1, cmp_w2.astype(bf16), g_k0.reshape(1, DH))


def _build_shared(xf, B, S, norm_kv, w_kv, g_k, cmp_pos, cmp_w1, cmp_w2):
    kc, vc, ks, vst, kw, vwt = _kv_project(xf, B, S, norm_kv, w_kv, g_k)
    cmp = _compress(kc, vc, B, S, cmp_pos, cmp_w1, cmp_w2, g_k[0]).astype(bf16)
    kcm = cmp[0]
    vct = cmp[1].transpose(0, 1, 3, 2)
    return kcm, vct, ks, vst, kw, vwt


def _qg_kernel(x_ref, g_ref, wq_ref, wgate_ref, gq_ref, seg_ref, segt_ref, q_ref, gate_ref):
    h = _rms(x_ref[...], g_ref[...]).astype(bf16)
    q = _head_rms(_dot(h, wq_ref[...]), gq_ref[...], seg_ref[...], segt_ref[...])
    q = q * (HEAD_DIM ** -0.5 * LOG2E)
    gate_ref[...] = jax.nn.sigmoid(_dot(h, wgate_ref[...]))
    nt, G, DH, RT = q_ref.shape
    T = q.shape[0] // nt
    r = RT // T
    for t in range(nt):
        for g in range(G):
            qt = q[t * T:(t + 1) * T, g * r * DH:(g + 1) * r * DH].T
            q_ref[t, g] = jnp.concatenate([qt[k * DH:(k + 1) * DH] for k in range(r)], axis=1).astype(bf16)


def _qg_project(xf, g, w_qg, g_q):
    N, D = xf.shape
    HD = D
    ng = w_qg.shape[1] - HD
    tq = min(512, N)
    G, T = N_KV_HEADS, Q_BLOCK
    RT = HD // (G * HEAD_DIM) * T
    assert N % tq == 0 and tq % T == 0
    seg, segt = _seg_mats(HD)
    row = lambda i: (i, 0)
    const = lambda i: (0, 0)
    return pl.pallas_call(
        _qg_kernel,
        out_shape=(jax.ShapeDtypeStruct((N // T, G, HEAD_DIM, RT), bf16), jax.ShapeDtypeStruct((N, ng), f32)),
        grid=(N // tq,),
        in_specs=[
            pl.BlockSpec((tq, D), row),
            pl.BlockSpec((1, D), const),
            pl.BlockSpec((D, HD), const),
            pl.BlockSpec((D, ng), const),
            pl.BlockSpec((1, HD), const),
            pl.BlockSpec((HD, NH_PAD), const),
            pl.BlockSpec((NH_PAD, HD), const),
        ],
        out_specs=(pl.BlockSpec((tq // T, G, HEAD_DIM, RT), lambda i: (i, 0, 0, 0)), pl.BlockSpec((tq, ng), row)),
        compiler_params=_cparams("parallel"),
        name="qg_project",
    )(xf, g.reshape(1, D), w_qg[:, :HD].astype(bf16), w_qg[:, HD:].astype(bf16),
      jnp.tile(g_q, HD // HEAD_DIM).reshape(1, HD), seg, segt)


def _fold8(x, op):
    return op(x.reshape(x.shape[0] // 8, 8, x.shape[1]), axis=0)


def _nsa_kernel(q_ref, gt_ref, kc_ref, vct_ref, ks_ref, vst_ref, kw_ref, vwt_ref, cbn_ref, cstep_ref, tb_ref, c2st_ref,
                o_ref, cb_scr, pen_ref, penf_ref, s_scr, m8_scr, l8_scr, acc_scr, *, r, nsb, n_sel):
    qb = pl.program_id(1)
    t0 = qb * Q_BLOCK
    T = Q_BLOCK
    RT = r * T
    G = q_ref.shape[1]
    nch = kc_ref.shape[2]
    nkt = s_scr.shape[1] // KEY_TILE - SLC_UNROLL

    def tile_rows(kt):
        return pl.ds(pl.multiple_of(kt * KEY_TILE, KEY_TILE), KEY_TILE)

    def add_block_pen(s, ref, g, kt):
        return jnp.concatenate([s[:SLC_LEN] + ref[g, pl.ds(2 * kt, 1), :],
                                s[SLC_LEN:] + ref[g, pl.ds(2 * kt + 1, 1), :]], axis=0)

    for g in range(G):
        qT = q_ref[0, g]

        cb_scr[g] = cstep_ref[g, pl.ds(pl.multiple_of(nch - (qb + 1) * CMP_PER_Q, 8), CMP_PAD + nch), :]
        cb_scr[g, pl.ds(pl.multiple_of(qb * CMP_PER_Q, 8), CMP_NEAR), :] = cbn_ref[g]
        s = _dot(kc_ref[0, g], qT) + cb_scr[g, CMP_PAD:CMP_PAD + nch, :]
        p = jnp.where(s > 0.5 * NEG, jnp.exp2(s - jnp.max(s, axis=0, keepdims=True)), 0.0)
        p = p / jnp.maximum(jnp.sum(p, axis=0, keepdims=True), TINY)
        o_cmp = _dot(vct_ref[0, g], p.astype(bf16))

        psum = p[:, 0:T]
        for k in range(1, r):
            psum = psum + p[:, k * T:(k + 1) * T]
        hi = psum.astype(bf16)
        lo = (psum - hi.astype(f32)).astype(bf16)
        imp = _dot(c2st_ref[...], hi) + _dot(c2st_ref[...], lo)
        jb = lax.broadcasted_iota(i32, (nsb, T), 0)
        blk_q = jnp.right_shift(t0 + lax.broadcasted_iota(i32, (nsb, T), 1), SLC_LEN.bit_length() - 1)
        forced = (jb == 0) | (jb == blk_q) | (jb == blk_q - 1)
        score = jnp.where(forced, FORCED_SCORE, jnp.where(jb <= blk_q, imp, NEG))
        pen = jnp.full((nsb, T), NEG, f32)
        for _ in range(n_sel):
            mx = jnp.max(score, axis=0, keepdims=True)
            first = jnp.min(jnp.where(score == mx, jb, nsb), axis=0, keepdims=True)
            hit = jb == first
            pen = jnp.where(hit, 0.0, pen)
            score = jnp.where(hit, -jnp.inf, score)
        pen = jnp.concatenate([pen] * r, axis=1)
        pen_ref[g] = pen
        penf_ref[g] = pen + cstep_ref[g, 0:1, :]

        win = []
        for d in range(WINDOW // KEY_TILE, -1, -1):
            kt = qb - d
            ktc = jnp.maximum(kt, 0)
            tab = tb_ref[g, jnp.where(kt >= 0, d, TB_NONE)]
            win.append((_dot(kw_ref[0, g, tile_rows(ktc), :], qT) + tab, vwt_ref[0, g, ktc]))
        m8 = _fold8(win[0][0], jnp.max)
        for s_d, _ in win[1:]:
            m8 = jnp.maximum(m8, _fold8(s_d, jnp.max))
        m = jnp.max(m8, axis=0, keepdims=True)
        l8 = jnp.zeros((8, RT), f32)
        acc = jnp.zeros((HEAD_DIM, RT), f32)
        for s_d, v_d in win:
            p_d = jnp.exp2(s_d - m)
            l8 = l8 + _fold8(p_d, jnp.sum)
            acc = acc + _dot(v_d, p_d.astype(bf16))
        o_win = acc * (1.0 / jnp.maximum(jnp.sum(l8, axis=0, keepdims=True), TINY))

        gt = gt_ref[0, g]
        o_ref[0, g] = gt[0:1] * o_cmp + gt[2:3] * o_win

        ktp = jnp.maximum(qb - 1, 0)
        s_prev = add_block_pen(_dot(ks_ref[0, g, tile_rows(ktp), :], qT)
                               + tb_ref[g, jnp.where(qb >= 1, TB_PREV, TB_NONE)], pen_ref, g, ktp)
        s_scr[g, tile_rows(ktp), :] = s_prev
        s_own = add_block_pen(_dot(ks_ref[0, g, tile_rows(qb), :], qT) + tb_ref[g, TB_OWN], pen_ref, g, qb)
        s_scr[g, tile_rows(qb), :] = s_own
        m8_scr[g] = jnp.maximum(_fold8(s_prev, jnp.max), _fold8(s_own, jnp.max))
        l8_scr[g] = jnp.zeros((8, RT), f32)
        acc_scr[g] = jnp.zeros((HEAD_DIM, RT), f32)

    def pass_a(i, c):
        for g in range(G):
            qT = q_ref[0, g]
            m8 = m8_scr[g]
            for u in range(SLC_UNROLL):
                kt = SLC_UNROLL * i + u
                live = kt < qb - 1
                ktc = jnp.minimum(kt, nkt - 1)
                s = add_block_pen(_dot(ks_ref[0, g, tile_rows(ktc), :], qT) + jnp.where(live, 0.0, NEG),
                                  penf_ref, g, ktc)
                s_scr[g, tile_rows(jnp.where(live, kt, nkt + u)), :] = s
                m8 = jnp.maximum(m8, _fold8(s, jnp.max))
            m8_scr[g] = m8
        return c

    lax.fori_loop(0, (jnp.maximum(qb - 1, 0) + SLC_UNROLL - 1) // SLC_UNROLL, pass_a, 0)

    def pass_b(i, c):
        for g in range(G):
            m = jnp.max(m8_scr[g], axis=0, keepdims=True)
            l8 = l8_scr[g]
            acc = acc_scr[g]
            for u in range(SLC_UNROLL):
                kt = SLC_UNROLL * i + u
                ktc = jnp.minimum(kt, qb)
                p_u = jnp.exp2(s_scr[g, tile_rows(ktc), :] - (m + jnp.where(kt <= qb, 0.0, -NEG)))
                l8 = l8 + _fold8(p_u, jnp.sum)
                acc = acc + _dot(vst_ref[0, g, ktc], p_u.astype(bf16))
            l8_scr[g] = l8
            acc_scr[g] = acc
        return c

    lax.fori_loop(0, qb // SLC_UNROLL + 1, pass_b, 0)

    for g in range(G):
        o_slc = acc_scr[g] * (1.0 / jnp.maximum(jnp.sum(l8_scr[g], axis=0, keepdims=True), TINY))
        o_ref[0, g] = o_ref[0, g] + gt_ref[0, g][1:2] * o_slc


def _bias_tables(rel_bias, S):
    G = N_KV_HEADS
    H = rel_bias.shape[1]
    r = H // G
    T = KEY_TILE
    n = jnp.arange(S + 2 * T, dtype=i32)
    max_exact = N_BUCKETS // 2
    nf = jnp.maximum(n, 1).astype(f32)
    large = max_exact + (jnp.log(nf / max_exact) / math.log(MAX_DISTANCE / max_exact)
                         * (N_BUCKETS - max_exact)).astype(i32)
    bucket = jnp.where(n < max_exact, n, jnp.minimum(large, N_BUCKETS - 1))
    bias1d = rel_bias.astype(f32)[bucket] * LOG2E

    def per_group(tab):
        lead = tab.shape[:-2]
        k = len(lead)
        t = jnp.moveaxis(tab, -1, 0).reshape((G, r) + lead + (T,))
        return jnp.moveaxis(t, 1, k + 1).reshape((G,) + lead + (r * T,))

    def masked(dist, ok):
        return jnp.where(jnp.asarray(ok)[..., None], bias1d[np.maximum(dist, 0)], NEG)

    kj, qi = np.arange(T)[:, None], np.arange(T)[None, :]
    dist = np.stack([d * T + qi - kj for d in range(3)])
    ok = np.stack([dist[0] >= 0, np.ones((T, T), bool), dist[2] < WINDOW])
    tb = jnp.concatenate([masked(dist, ok), jnp.full((1, T, T, H), NEG, f32)])
    cend = (np.arange(CMP_NEAR)[:, None] - CMP_PAD) * CMP_STRIDE + CMP_LEN - 1
    dcn = np.arange(T)[None, :] - cend
    rows = CMP_PAD + S // CMP_STRIDE
    far = jnp.broadcast_to(bias1d[-1][None, None, :], (rows, T, H))
    cstep = jnp.concatenate([far, jnp.full((rows, T, H), NEG, f32)])
    return per_group(tb), per_group(masked(dcn, dcn >= 0)), per_group(cstep)


def _nsa_attention(q, gates_t, shared, tables, B, S):
    kcm, vct, ks, vst, kw, vwt = shared
    tb, cbn, cstep = tables
    G, DH, T = N_KV_HEADS, HEAD_DIM, Q_BLOCK
    RT = q.shape[3]
    r = RT // T
    nqb = S // T
    nch = S // CMP_STRIDE
    nsb = S // SLC_LEN
    nkt = S // KEY_TILE
    cmp_start = np.arange(nch) * CMP_STRIDE
    slc_start = np.arange(nsb) * SLC_LEN
    overlap = np.clip(np.minimum(cmp_start[:, None] + CMP_LEN, slc_start[None, :] + SLC_LEN)
                      - np.maximum(cmp_start[:, None], slc_start[None, :]), 0, None) / CMP_LEN
    overlap[nch - 1] = 0.0
    c2st = jnp.asarray(overlap.T, bf16)
    qmap = lambda b, i: (b * nqb + i, 0, 0, 0)
    bat = lambda b, i: (b, 0, 0, 0)
    bat5 = lambda b, i: (b, 0, 0, 0, 0)
    once = pl.Buffered(1)
    return pl.pallas_call(
        functools.partial(_nsa_kernel, r=r, nsb=nsb, n_sel=min(N_SEL, nsb)),
        out_shape=jax.ShapeDtypeStruct(q.shape, f32),
        grid=(B, nqb),
        in_specs=[
            pl.BlockSpec((1, G, DH, RT), qmap),
            pl.BlockSpec((1, G, N_BRANCH, RT), qmap),
            pl.BlockSpec((1, G, nch, DH), bat),
            pl.BlockSpec((1, G, DH, nch), bat),
            pl.BlockSpec((1, G, S, DH), bat),
            pl.BlockSpec((1, G, nkt, DH, KEY_TILE), bat5),
            pl.BlockSpec((1, G, S, DH), bat),
            pl.BlockSpec((1, G, nkt, DH, KEY_TILE), bat5),
            pl.BlockSpec(cbn.shape, lambda b, i: (0, 0, 0), pipeline_mode=once),
            pl.BlockSpec(cstep.shape, lambda b, i: (0, 0, 0), pipeline_mode=once),
            pl.BlockSpec(tb.shape, lambda b, i: (0, 0, 0, 0), pipeline_mode=once),
            pl.BlockSpec((nsb, nch), lambda b, i: (0, 0), pipeline_mode=once),
        ],
        out_specs=pl.BlockSpec((1, G, DH, RT), qmap),
        scratch_shapes=[pltpu.VMEM((G, CMP_PAD + nch, RT), f32), pltpu.VMEM((G, nsb, RT), f32),
                        pltpu.VMEM((G, nsb, RT), f32), pltpu.VMEM((G, (nkt + SLC_UNROLL) * KEY_TILE, RT), f32),
                        pltpu.VMEM((G, 8, RT), f32), pltpu.VMEM((G, 8, RT), f32), pltpu.VMEM((G, DH, RT), f32)],
        compiler_params=_cparams("parallel", "arbitrary"),
        name="nsa_attention",
    )(q, gates_t, kcm, vct, ks, vst, kw, vwt, cbn, cstep, tb, c2st)


def _oproj_kernel(x_ref, a_ref, w_ref, o_ref):
    nt, G, DH, RT = a_ref.shape
    T = x_ref.shape[0] // nt
    r = RT // T
    rows = []
    for t in range(nt):
        cols = []
        for g in range(G):
            a = a_ref[t, g]
            cols.append(jnp.concatenate([a[:, k * T:(k + 1) * T] for k in range(r)], axis=0).T)
        rows.append(jnp.concatenate(cols, axis=1))
    attn = jnp.concatenate(rows, axis=0)
    o_ref[...] = x_ref[...] + _dot(attn.astype(bf16), w_ref[...])


def _out_project(xf, attn, w_o):
    N, D = xf.shape
    nt_all, G, DH, RT = attn.shape
    T = N // nt_all
    to = min(512, N)
    assert N % to == 0 and to % T == 0
    row = lambda i: (i, 0)
    return pl.pallas_call(
        _oproj_kernel,
        out_shape=jax.ShapeDtypeStruct((N, D), f32),
        grid=(N // to,),
        in_specs=[pl.BlockSpec((to, D), row), pl.BlockSpec((to // T, G, DH, RT), lambda i: (i, 0, 0, 0)),
                  pl.BlockSpec(w_o.shape, lambda i: (0, 0))],
        out_specs=pl.BlockSpec((to, D), row),
        compiler_params=_cparams("parallel"),
        name="out_project",
    )(xf, attn, w_o.astype(bf16))


def _nsa_layer(xf, B, S, g, w_qg, g_q, w_o, shared, tables):
    N = xf.shape[0]
    G, T = N_KV_HEADS, Q_BLOCK
    q, gates = _qg_project(xf, g, w_qg, g_q)
    r = gates.shape[1] // (G * N_BRANCH)
    gates_t = gates.reshape(N // T, T, G, r, N_BRANCH).transpose(0, 2, 4, 3, 1).reshape(N // T, G, N_BRANCH, r * T)
    attn = _nsa_attention(q, gates_t, shared, tables, B, S)
    return _out_project(xf, attn, w_o)


def kernel(x, rel_bias, norm_mix, norm_ffn, pool_w, pool_scale, norm_kv, w_kv, g_k, cmp_pos, cmp_w1, cmp_w2,
           w_qg, g_q, w_o, ffn_wg, ffn_wu, ffn_wd, router, moe_wg, moe_wu, moe_wd):
    B, S, D = x.shape
    depth = norm_mix.shape[0]
    n_a = depth // 2
    assert S % Q_BLOCK == 0
    xf = x.reshape(B * S, D)
    shared = None
    tables = _bias_tables(rel_bias, S)
    for layer in range(depth):
        if layer < n_a:
            xf = _pool_layer(xf.reshape(B, S, D), norm_mix[layer], pool_w[layer], pool_scale[layer]).reshape(B * S, D)
        else:
            j = layer - n_a
            xf = _nsa_layer(xf, B, S, norm_mix[layer], w_qg[j], g_q[j], w_o[j], shared, tables)
        i = layer // 2
        if layer % 2 == 0:
            xf = _ffn_layer(xf, norm_ffn[layer], ffn_wg[i], ffn_wu[i], ffn_wd[i])
        else:
            xf = _moe_layer(xf, norm_ffn[layer], router[i], moe_wg[i], moe_wu[i], moe_wd[i])
        if layer == n_a - 1:
            shared = _build_shared(xf, B, S, norm_kv, w_kv, g_k, cmp_pos, cmp_w1, cmp_w2)
    return xf.reshape(B, S, D)
```

```python
import functools
import math

import numpy as np
import jax
import jax.numpy as jnp
from jax import lax
from jax.experimental import pallas as pl
from jax.experimental.pallas import tpu as pltpu
from jax.experimental.pallas import tpu_sc as plsc

f32 = jnp.float32
bf16 = jnp.bfloat16
i32 = jnp.int32
u32 = jnp.uint32

POOL_WINDOWS = (2, 4, 8, 16)
HEAD_DIM = 64
N_KV_HEADS = 4
N_BRANCH = 3
CMP_LEN = 32
CMP_STRIDE = 16
CMP_HIDDEN = 4 * HEAD_DIM
SLC_LEN = 64
N_SEL = 4
WINDOW = 256
Q_BLOCK = 128
FORCED_SCORE = 1.0e4
N_BUCKETS = 32
MAX_DISTANCE = 128
N_EXPERTS = 8
TOP_K = 2
EPS = 1e-6
NEG = -1e30
TINY = 1e-30
LOG2E = math.log2(math.e)
TB_OWN, TB_PREV, TB_WIN2, TB_NONE = range(4)

KEY_TILE = Q_BLOCK
SLC_UNROLL = 2
POOL_HALO = 16
NH_PAD = 16
V7X_VMEM_LIMIT = 56 * 1024 * 1024
V7X_SC_CORES, V7X_SC_SUBCORES = 2, 16
SC_MAX_INDEX_VECTOR = 128
SC_GATHER_CHUNK = 32

CMP_PER_Q = Q_BLOCK // CMP_STRIDE
CMP_PAD = 2 * CMP_PER_Q
CMP_NEAR = 3 * CMP_PER_Q

assert CMP_LEN == 2 * CMP_STRIDE and KEY_TILE == 2 * SLC_LEN and WINDOW == 2 * KEY_TILE
assert max(POOL_WINDOWS) <= POOL_HALO
assert 2 * KEY_TILE - (Q_BLOCK - 1) >= MAX_DISTANCE
assert (CMP_PAD + 1) * CMP_STRIDE - (CMP_LEN - 1) >= MAX_DISTANCE and CMP_PER_Q % 8 == 0


def _cparams(*sem):
    return pltpu.CompilerParams(dimension_semantics=sem, vmem_limit_bytes=V7X_VMEM_LIMIT)


def _rms(xf, g):
    ms = jnp.mean(xf * xf, axis=-1, keepdims=True)
    return (xf * lax.rsqrt(ms + EPS)) * g


def _dot(a, b):
    return jnp.dot(a, b, preferred_element_type=f32)


def _dot_hilo(a, b):
    hi = a.astype(bf16)
    lo = (a - hi.astype(f32)).astype(bf16)
    return _dot(hi, b) + _dot(lo, b)


def _head_rms(z, gvec, seg, segt):
    ssq = _dot_hilo(z * z, seg)
    inv = lax.rsqrt(ssq * (1.0 / HEAD_DIM) + EPS)
    return (z * _dot_hilo(inv, segt)) * gvec


def _silu(a):
    return a * jax.nn.sigmoid(a)


def _seg_mats(width):
    heads = width // HEAD_DIM
    seg = np.zeros((width, NH_PAD), np.float32)
    seg[np.arange(width), np.arange(width) // HEAD_DIM] = 1.0
    assert heads <= NH_PAD
    return jnp.asarray(seg, bf16), jnp.asarray(seg.T, bf16)


def _pool_kernel(x_ref, halo_ref, g_ref, w_ref, scale_ref, o_ref, *, tp, cg):
    i = pl.program_id(1)
    x = x_ref[0]
    xh = jnp.concatenate([halo_ref[0], x], axis=0)
    h = _rms(xh, g_ref[...])
    row = lax.broadcasted_iota(i32, (tp + POOL_HALO, 1), 0)
    t_abs = i * tp + row - POOL_HALO
    h = jnp.where(t_abs >= 0, h, 0.0)
    outs = []
    for gi, w in enumerate(POOL_WINDOWS):
        hg = h[:, gi * cg:(gi + 1) * cg]
        s = hg
        sh = 1
        while sh < w:
            s = s + pltpu.roll(s, sh, axis=0)
            sh *= 2
        cnt = jnp.clip(t_abs + 1, 1, w).astype(f32)
        diff = (s / cnt - hg)[POOL_HALO:]
        outs.append(_dot(diff.astype(bf16), w_ref[gi]))
    y = jnp.concatenate(outs, axis=1)
    o_ref[0] = x + y * scale_ref[...]


def _pool_layer(x3, g, w_grp, scale):
    B, S, D = x3.shape
    tp = min(512, S)
    cg = D // len(POOL_WINDOWS)
    assert S % tp == 0 and tp % POOL_HALO == 0 and all(w & (w - 1) == 0 for w in POOL_WINDOWS)
    hb = tp // POOL_HALO
    return pl.pallas_call(
        functools.partial(_pool_kernel, tp=tp, cg=cg),
        out_shape=jax.ShapeDtypeStruct((B, S, D), f32),
        grid=(B, S // tp),
        in_specs=[
            pl.BlockSpec((1, tp, D), lambda b, i: (b, i, 0)),
            pl.BlockSpec((1, POOL_HALO, D), lambda b, i: (b, jnp.maximum(i * hb - 1, 0), 0)),
            pl.BlockSpec((1, D), lambda b, i: (0, 0)),
            pl.BlockSpec((len(POOL_WINDOWS), cg, cg), lambda b, i: (0, 0, 0)),
            pl.BlockSpec((1, D), lambda b, i: (0, 0)),
        ],
        out_specs=pl.BlockSpec((1, tp, D), lambda b, i: (b, i, 0)),
        compiler_params=_cparams("parallel", "arbitrary"),
        name="pool_layer",
    )(x3, x3, g.reshape(1, D), w_grp.astype(bf16), scale.reshape(1, D))


def _ffn_kernel(x_ref, g_ref, wg_ref, wu_ref, wd_ref, o_ref, h_ref, acc_ref):
    j = pl.program_id(1)

    @pl.when(j == 0)
    def _():
        x = x_ref[...]
        h_ref[...] = _rms(x, g_ref[...]).astype(bf16)
        acc_ref[...] = x

    h = h_ref[...]
    act = _silu(_dot(h, wg_ref[...])) * _dot(h, wu_ref[...])
    acc_ref[...] += _dot(act.astype(bf16), wd_ref[...])

    @pl.when(j == pl.num_programs(1) - 1)
    def _():
        o_ref[...] = acc_ref[...]


def _ffn_layer(xf, g, wg, wu, wd):
    N, D = xf.shape
    F = wg.shape[1]
    tm = min(1024, N)
    tf = 512
    assert N % tm == 0 and F % tf == 0
    return pl.pallas_call(
        _ffn_kernel,
        out_shape=jax.ShapeDtypeStruct((N, D), f32),
        grid=(N // tm, F // tf),
        in_specs=[
            pl.BlockSpec((tm, D), lambda i, j: (i, 0)),
            pl.BlockSpec((1, D), lambda i, j: (0, 0)),
            pl.BlockSpec((D, tf), lambda i, j: (0, j)),
            pl.BlockSpec((D, tf), lambda i, j: (0, j)),
            pl.BlockSpec((tf, D), lambda i, j: (j, 0)),
        ],
        out_specs=pl.BlockSpec((tm, D), lambda i, j: (i, 0)),
        scratch_shapes=[pltpu.VMEM((tm, D), bf16), pltpu.VMEM((tm, D), f32)],
        compiler_params=_cparams("parallel", "arbitrary"),
        name="ffn_dense",
    )(xf, g.reshape(1, D), wg.astype(bf16), wu.astype(bf16), wd.astype(bf16))


def _router_kernel(x_ref, g_ref, rt_ref, hp_ref, idx_ref, gate_ref):
    h = _rms(x_ref[...], g_ref[...])
    half = h.shape[1] // 2
    bits = pltpu.bitcast(h.astype(bf16).astype(f32), u32)
    hp_ref[...] = pltpu.bitcast((bits[:, :half] & jnp.uint32(0xFFFF0000)) | (bits[:, half:] >> 16), i32)
    logits = lax.dot_general(rt_ref[...], h, (((1,), (1,)), ((), ())),
                             precision=lax.Precision.HIGHEST, preferred_element_type=f32)
    ne = logits.shape[0]
    row = lax.broadcasted_iota(i32, logits.shape, 0)
    m1 = jnp.max(logits, axis=0, keepdims=True)
    i1 = jnp.min(jnp.where(logits == m1, row, ne), axis=0, keepdims=True)
    rest = jnp.where(row == i1, -jnp.inf, logits)
    m2 = jnp.max(rest, axis=0, keepdims=True)
    i2 = jnp.min(jnp.where(rest == m2, row, ne), axis=0, keepdims=True)
    e2 = jnp.exp(m2 - m1)
    den = 1.0 + e2
    idx_ref[...] = jnp.concatenate([i1, i2], axis=0)
    gate_ref[...] = jnp.concatenate([1.0 / den, e2 / den], axis=0)


def _router(xf, g, router):
    N, D = xf.shape
    E = router.shape[1]
    tr = min(1024, N)
    assert N % tr == 0 and TOP_K == 2
    return pl.pallas_call(
        _router_kernel,
        out_shape=(jax.ShapeDtypeStruct((N, D // 2), i32),
                   jax.ShapeDtypeStruct((TOP_K, N), i32),
                   jax.ShapeDtypeStruct((TOP_K, N), f32)),
        grid=(N // tr,),
        in_specs=[
            pl.BlockSpec((tr, D), lambda i: (i, 0)),
            pl.BlockSpec((1, D), lambda i: (0, 0)),
            pl.BlockSpec((E, D), lambda i: (0, 0)),
        ],
        out_specs=(pl.BlockSpec((tr, D // 2), lambda i: (i, 0)),
                   pl.BlockSpec((TOP_K, tr), lambda i: (0, i)),
                   pl.BlockSpec((TOP_K, tr), lambda i: (0, i))),
        compiler_params=_cparams("parallel"),
        name="moe_router",
    )(xf, g.reshape(1, D), router.T)


def _sc_gather_rows(table, idx, chunk):
    B = idx.shape[0]
    D = table.shape[1]
    workers = V7X_SC_CORES * V7X_SC_SUBCORES
    per_w = B // workers
    cpw = per_w // chunk
    assert B % (8 * workers) == 0 and per_w % (2 * chunk) == 0 and cpw % 8 == 0
    assert chunk % 8 == 0 and chunk <= SC_MAX_INDEX_VECTOR
    mesh = plsc.VectorSubcoreMesh(core_axis_name="c", subcore_axis_name="s")

    @functools.partial(
        pl.kernel, mesh=mesh, out_type=jax.ShapeDtypeStruct((B, D), table.dtype),
        scratch_types=[pltpu.VMEM((cpw, chunk), i32), pltpu.VMEM((2, chunk, D), table.dtype),
                       pltpu.SemaphoreType.DMA, pltpu.SemaphoreType.DMA])
    def gather(table_hbm, idx_hbm, out_hbm, idx_v, rows_v, sem0, sem1):
        wid = lax.axis_index("s") * V7X_SC_CORES + lax.axis_index("c")
        pltpu.sync_copy(idx_hbm.at[pl.ds(pl.multiple_of(wid * cpw, 8), cpw)], idx_v)
        sems = (sem0, sem1)

        def fetch(c, b):
            return pltpu.make_async_copy(table_hbm.at[idx_v.at[c]], rows_v.at[b], sems[b])

        fetch(0, 0).start()

        @pl.loop(0, cpw, step=2)
        def _(c):
            for b in range(2):
                cur = c + b
                fetch(cur, b).wait()

                @pl.when(cur + 1 < cpw)
                def _():
                    fetch(cur + 1, 1 - b).start()

                pltpu.sync_copy(rows_v.at[b], out_hbm.at[pl.ds(pl.multiple_of(wid * per_w + cur * chunk, 8), chunk)])

    return gather(table, idx.reshape(B // chunk, chunk))


def _gmm_kernel(te_ref, nu_ref, xs_ref, wg_ref, wu_ref, wd_ref, o_ref, h_ref, acc_ref):
    i = pl.program_id(0)
    j = pl.program_id(1)

    @pl.when(i < nu_ref[0])
    def _():
        @pl.when(j == 0)
        def _():
            w = pltpu.bitcast(xs_ref[...], u32)
            left = pltpu.bitcast(w & jnp.uint32(0xFFFF0000), f32)
            right = pltpu.bitcast(w << 16, f32)
            h_ref[...] = jnp.concatenate([left, right], axis=1).astype(bf16)
            acc_ref[...] = jnp.zeros_like(acc_ref)

        h = h_ref[...]
        act = _silu(_dot(h, wg_ref[0])) * _dot(h, wu_ref[0])
        acc_ref[...] += _dot(act.astype(bf16), wd_ref[0])

        @pl.when(j == pl.num_programs(1) - 1)
        def _():
            o_ref[...] = acc_ref[...]

    @pl.when((i >= nu_ref[0]) & (j == 0))
    def _():
        o_ref[...] = jnp.zeros_like(o_ref)


def _gmm(tile_expert, n_used, xs, wg, wu, wd, tm):
    P, half = xs.shape
    D = 2 * half
    F = wg.shape[2]
    tf = 512
    assert P % tm == 0 and F % tf == 0

    nf = F // tf

    def row_map(i, j, te, nu):
        return (jnp.minimum(i, nu[0] - 1), 0)

    def ff(i, j, nu):
        return jnp.where(i < nu[0], j, nf - 1)

    grid_spec = pltpu.PrefetchScalarGridSpec(
        num_scalar_prefetch=2,
        grid=(P // tm, nf),
        in_specs=[
            pl.BlockSpec((tm, half), row_map),
            pl.BlockSpec((1, D, tf), lambda i, j, te, nu: (te[i], 0, ff(i, j, nu))),
            pl.BlockSpec((1, D, tf), lambda i, j, te, nu: (te[i], 0, ff(i, j, nu))),
            pl.BlockSpec((1, tf, D), lambda i, j, te, nu: (te[i], ff(i, j, nu), 0)),
        ],
        out_specs=pl.BlockSpec((tm, D), lambda i, j, te, nu: (i, 0)),
        scratch_shapes=[pltpu.VMEM((tm, D), bf16), pltpu.VMEM((tm, D), f32)],
    )
    return pl.pallas_call(
        _gmm_kernel,
        out_shape=jax.ShapeDtypeStruct((P, D), f32),
        grid_spec=grid_spec,
        compiler_params=_cparams("arbitrary", "arbitrary"),
        name="moe_gmm",
    )(tile_expert, n_used, xs, wg.astype(bf16), wu.astype(bf16), wd.astype(bf16))


def _combine_kernel(x_ref, gate_ref, y_ref, o_ref):
    gate = gate_ref[...]
    acc = x_ref[...]
    for k in range(TOP_K):
        acc = acc + y_ref[k] * gate[:, k:k + 1]
    o_ref[...] = acc


def _combine(xf, gate_tk, y_tok):
    N, D = xf.shape
    tt = min(1024, N)
    assert N % tt == 0
    return pl.pallas_call(
        _combine_kernel,
        out_shape=jax.ShapeDtypeStruct((N, D), f32),
        grid=(N // tt,),
        in_specs=[pl.BlockSpec((tt, D), lambda i: (i, 0)),
                  pl.BlockSpec((tt, TOP_K), lambda i: (i, 0)),
                  pl.BlockSpec((TOP_K, tt, D), lambda i: (0, i, 0))],
        out_specs=pl.BlockSpec((tt, D), lambda i: (i, 0)),
        compiler_params=_cparams("parallel"),
        name="moe_combine",
    )(xf, gate_tk, y_tok)


def _moe_layer(xf, g, router, wg, wu, wd):
    N, D = xf.shape
    E = router.shape[1]
    tm = min(1024, N)
    hp, idx, gate = _router(xf, g, router)
    e_flat = idx.reshape(-1)
    onehot = (e_flat[:, None] == jnp.arange(E, dtype=i32)[None, :]).astype(i32)
    csum = jnp.cumsum(onehot, axis=0)
    counts = csum[-1]
    padded = ((counts + tm - 1) // tm) * tm
    ends = jnp.cumsum(padded)
    starts = ends - padded
    pos = (jnp.sum((csum + starts[None, :]) * onehot, axis=1) - 1).astype(i32)
    P = TOP_K * N + E * tm
    n_used = (ends[-1] // tm).astype(i32)
    tile_start = jnp.minimum(jnp.arange(P // tm, dtype=i32), n_used - 1) * tm
    tile_expert = jnp.sum((ends[None, :] <= tile_start[:, None]).astype(i32), axis=1)
    slot_tok = jnp.zeros((P,), i32).at[pos].set(jnp.tile(jnp.arange(N, dtype=i32), TOP_K))
    xs = _sc_gather_rows(hp, slot_tok, SC_GATHER_CHUNK)
    ys = _gmm(tile_expert, n_used.reshape(1), xs, wg, wu, wd, tm)
    y_tok = _sc_gather_rows(ys, pos, SC_GATHER_CHUNK).reshape(TOP_K, N, D)
    return _combine(xf, gate.T, y_tok)


def _kv_kernel(x_ref, g_ref, w_ref, gk_ref, seg_ref, segt_ref, kc_ref, vc_ref, ks_ref, vs_ref, kw_ref, vw_ref):
    h = _rms(x_ref[...], g_ref[...]).astype(bf16)
    kv = _dot(h, w_ref[...])
    wd = kc_ref.shape[1]
    part = lambda p: kv[:, p * wd:(p + 1) * wd]
    seg, segt = seg_ref[...], segt_ref[...]
    kc_ref[...] = part(0)
    vc_ref[...] = part(1)
    G, DH = ks_ref.shape[1], ks_ref.shape[3]
    for k_ref, v_ref, pk, gain in ((ks_ref, vs_ref, 2, gk_ref[0:1, :]), (kw_ref, vw_ref, 4, gk_ref[1:2, :])):
        kn = _head_rms(part(pk), gain, seg, segt).astype(bf16)
        v = part(pk + 1)
        for g in range(G):
            k_ref[0, g] = kn[:, g * DH:(g + 1) * DH]
        for t in range(v_ref.shape[2]):
            vt = v[t * KEY_TILE:(t + 1) * KEY_TILE, :].T
            for g in range(G):
                v_ref[0, g, t] = vt[g * DH:(g + 1) * DH].astype(bf16)


def _kv_project(xf, B, S, norm_kv, w_kv, g_k):
    N, D = xf.shape
    G, DH = N_KV_HEADS, HEAD_DIM
    wd = G * DH
    tk = min(512, S)
    spb = S // tk
    tpk = tk // KEY_TILE
    assert S % tk == 0 and tk % KEY_TILE == 0 and w_kv.shape[1] == 2 * N_BRANCH * wd
    seg, segt = _seg_mats(wd)
    gk = jnp.stack([jnp.tile(g_k[1], G), jnp.tile(g_k[2], G)])
    row = lambda i: (i, 0)
    const = lambda i: (0, 0)
    k_shape = jax.ShapeDtypeStruct((B, G, S, DH), bf16)
    v_shape = jax.ShapeDtypeStruct((B, G, S // KEY_TILE, DH, KEY_TILE), bf16)
    k_spec = pl.BlockSpec((1, G, tk, DH), lambda i: (i // spb, 0, i % spb, 0))
    v_spec = pl.BlockSpec((1, G, tpk, DH, KEY_TILE), lambda i: (i // spb, 0, i % spb, 0, 0))
    return pl.pallas_call(
        _kv_kernel,
        out_shape=(jax.ShapeDtypeStruct((N, wd), f32), jax.ShapeDtypeStruct((N, wd), f32),
                   k_shape, v_shape, k_shape, v_shape),
        grid=(N // tk,),
        in_specs=[
            pl.BlockSpec((tk, D), row),
            pl.BlockSpec((1, D), const),
            pl.BlockSpec(w_kv.shape, const),
            pl.BlockSpec((2, wd), const),
            pl.BlockSpec((wd, NH_PAD), const),
            pl.BlockSpec((NH_PAD, wd), const),
        ],
        out_specs=(pl.BlockSpec((tk, wd), row), pl.BlockSpec((tk, wd), row), k_spec, v_spec, k_spec, v_spec),
        compiler_params=_cparams("parallel"),
        name="kv_project",
    )(xf, norm_kv.reshape(1, D), w_kv.astype(bf16), gk, seg, segt)


def _compress_kernel(c_ref, pos_ref, w1_ref, w2_ref, gk_ref, o_ref):
    kv = pl.program_id(0)
    c = c_ref[0, 0, 0]
    a = _dot((c + pos_ref[0, 0:1, :]).astype(bf16), w1_ref[0, 0])
    b = _dot((c + pos_ref[0, 1:2, :]).astype(bf16), w1_ref[0, 1])
    n = c.shape[0]
    hid = a + pltpu.roll(b, n - 1, axis=0)
    out = _dot(_silu(hid).astype(bf16), w2_ref[0])

    @pl.when(kv == 0)
    def _():
        o_ref[0, 0, 0] = _rms(out, gk_ref[...])

    @pl.when(kv != 0)
    def _():
        o_ref[0, 0, 0] = out


def _compress(kc, vc, B, S, cmp_pos, cmp_w1, cmp_w2, g_k0):
    G, DH = N_KV_HEADS, HEAD_DIM
    nch = S // CMP_STRIDE
    cw = CMP_STRIDE * DH

    def chunks(z):
        return z.reshape(B, nch, CMP_STRIDE, G, DH).transpose(0, 3, 1, 2, 4).reshape(B, G, nch, cw)

    c = jnp.stack([chunks(kc), chunks(vc)])
    pos = cmp_pos.reshape(2, 2, cw)
    w1 = cmp_w1.reshape(2, 2, cw, CMP_HIDDEN).astype(bf16)
    return pl.pallas_call(
        _compress_kernel,
        out_shape=jax.ShapeDtypeStruct((2, B, G, nch, DH), f32),
        grid=(2, B, G),
        in_specs=[
            pl.BlockSpec((1, 1, 1, nch, cw), lambda k, b, g: (k, b, g, 0, 0)),
            pl.BlockSpec((1, 2, cw), lambda k, b, g: (k, 0, 0)),
            pl.BlockSpec((1, 2, cw, CMP_HIDDEN), lambda k, b, g: (k, 0, 0, 0)),
            pl.BlockSpec((1, CMP_HIDDEN, DH), lambda k, b, g: (k, 0, 0)),
            pl.BlockSpec((1, DH), lambda k, b, g: (0, 0)),
        ],
        out_specs=pl.BlockSpec((1, 1, 1, nch, DH), lambda k, b, g: (k, b, g, 0, 0)),
        compiler_params=_cparams("arbitrary", "arbitrary", "arbitrary"),
        name="kv_compress",
    )(c, pos, w1, cmp_w2.astype(bf16), g_k0.reshape(1, DH))


def _build_shared(xf, B, S, norm_kv, w_kv, g_k, cmp_pos, cmp_w1, cmp_w2):
    kc, vc, ks, vst, kw, vwt = _kv_project(xf, B, S, norm_kv, w_kv, g_k)
    cmp = _compress(kc, vc, B, S, cmp_pos, cmp_w1, cmp_w2, g_k[0]).astype(bf16)
    kcm = cmp[0]
    vct = cmp[1].transpose(0, 1, 3, 2)
    return kcm, vct, ks, vst, kw, vwt


def _qg_kernel(x_ref, g_ref, wq_ref, wgate_ref, gq_ref, seg_ref, segt_ref, q_ref, gate_ref):
    h = _rms(x_ref[...], g_ref[...]).astype(bf16)
    q = _head_rms(_dot(h, wq_ref[...]), gq_ref[...], seg_ref[...], segt_ref[...])
    q = q * (HEAD_DIM ** -0.5 * LOG2E)
    gate_ref[...] = jax.nn.sigmoid(_dot(h, wgate_ref[...]))
    nt, G, DH, RT = q_ref.shape
    T = q.shape[0] // nt
    r = RT // T
    for t in range(nt):
        for g in range(G):
            qt = q[t * T:(t + 1) * T, g * r * DH:(g + 1) * r * DH].T
            q_ref[t, g] = jnp.concatenate([qt[k * DH:(k + 1) * DH] for k in range(r)], axis=1).astype(bf16)


def _qg_project(xf, g, w_qg, g_q):
    N, D = xf.shape
    HD = D
    ng = w_qg.shape[1] - HD
    tq = min(512, N)
    G, T = N_KV_HEADS, Q_BLOCK
    RT = HD // (G * HEAD_DIM) * T
    assert N % tq == 0 and tq % T == 0
    seg, segt = _seg_mats(HD)
    row = lambda i: (i, 0)
    const = lambda i: (0, 0)
    return pl.pallas_call(
        _qg_kernel,
        out_shape=(jax.ShapeDtypeStruct((N // T, G, HEAD_DIM, RT), bf16), jax.ShapeDtypeStruct((N, ng), f32)),
        grid=(N // tq,),
        in_specs=[
            pl.BlockSpec((tq, D), row),
            pl.BlockSpec((1, D), const),
            pl.BlockSpec((D, HD), const),
            pl.BlockSpec((D, ng), const),
            pl.BlockSpec((1, HD), const),
            pl.BlockSpec((HD, NH_PAD), const),
            pl.BlockSpec((NH_PAD, HD), const),
        ],
        out_specs=(pl.BlockSpec((tq // T, G, HEAD_DIM, RT), lambda i: (i, 0, 0, 0)), pl.BlockSpec((tq, ng), row)),
        compiler_params=_cparams("parallel"),
        name="qg_project",
    )(xf, g.reshape(1, D), w_qg[:, :HD].astype(bf16), w_qg[:, HD:].astype(bf16),
      jnp.tile(g_q, HD // HEAD_DIM).reshape(1, HD), seg, segt)


def _fold8(x, op):
    return op(x.reshape(x.shape[0] // 8, 8, x.shape[1]), axis=0)


def _nsa_kernel(q_ref, gt_ref, kc_ref, vct_ref, ks_ref, vst_ref, kw_ref, vwt_ref, cbn_ref, cstep_ref, tb_ref, c2st_ref,
                o_ref, cb_scr, pen_ref, penf_ref, s_scr, m8_scr, l8_scr, acc_scr, *, r, nsb, n_sel):
    qb = pl.program_id(1)
    t0 = qb * Q_BLOCK
    T = Q_BLOCK
    RT = r * T
    G = q_ref.shape[1]
    nch = kc_ref.shape[2]
    nkt = s_scr.shape[1] // KEY_TILE - SLC_UNROLL

    def tile_rows(kt):
        return pl.ds(pl.multiple_of(kt * KEY_TILE, KEY_TILE), KEY_TILE)

    def add_block_pen(s, ref, g, kt):
        return jnp.concatenate([s[:SLC_LEN] + ref[g, pl.ds(2 * kt, 1), :],
                                s[SLC_LEN:] + ref[g, pl.ds(2 * kt + 1, 1), :]], axis=0)

    for g in range(G):
        qT = q_ref[0, g]

        cb_scr[g] = cstep_ref[g, pl.ds(pl.multiple_of(nch - (qb + 1) * CMP_PER_Q, 8), CMP_PAD + nch), :]
        cb_scr[g, pl.ds(pl.multiple_of(qb * CMP_PER_Q, 8), CMP_NEAR), :] = cbn_ref[g]
        s = _dot(kc_ref[0, g], qT) + cb_scr[g, CMP_PAD:CMP_PAD + nch, :]
        p = jnp.where(s > 0.5 * NEG, jnp.exp2(s - jnp.max(s, axis=0, keepdims=True)), 0.0)
        p = p / jnp.maximum(jnp.sum(p, axis=0, keepdims=True), TINY)
        o_cmp = _dot(vct_ref[0, g], p.astype(bf16))

        psum = p[:, 0:T]
        for k in range(1, r):
            psum = psum + p[:, k * T:(k + 1) * T]
        hi = psum.astype(bf16)
        lo = (psum - hi.astype(f32)).astype(bf16)
        imp = _dot(c2st_ref[...], hi) + _dot(c2st_ref[...], lo)
        jb = lax.broadcasted_iota(i32, (nsb, T), 0)
        blk_q = jnp.right_shift(t0 + lax.broadcasted_iota(i32, (nsb, T), 1), SLC_LEN.bit_length() - 1)
        forced = (jb == 0) | (jb == blk_q) | (jb == blk_q - 1)
        score = jnp.where(forced, FORCED_SCORE, jnp.where(jb <= blk_q, imp, NEG))
        pen = jnp.full((nsb, T), NEG, f32)
        for _ in range(n_sel):
            mx = jnp.max(score, axis=0, keepdims=True)
            first = jnp.min(jnp.where(score == mx, jb, nsb), axis=0, keepdims=True)
            hit = jb == first
            pen = jnp.where(hit, 0.0, pen)
            score = jnp.where(hit, -jnp.inf, score)
        pen = jnp.concatenate([pen] * r, axis=1)
        pen_ref[g] = pen
        penf_ref[g] = pen + cstep_ref[g, 0:1, :]

        win = []
        for d in range(WINDOW // KEY_TILE, -1, -1):
            kt = qb - d
            ktc = jnp.maximum(kt, 0)
            tab = tb_ref[g, jnp.where(kt >= 0, d, TB_NONE)]
            win.append((_dot(kw_ref[0, g, tile_rows(ktc), :], qT) + tab, vwt_ref[0, g, ktc]))
        m8 = _fold8(win[0][0], jnp.max)
        for s_d, _ in win[1:]:
            m8 = jnp.maximum(m8, _fold8(s_d, jnp.max))
        m = jnp.max(m8, axis=0, keepdims=True)
        l8 = jnp.zeros((8, RT), f32)
        acc = jnp.zeros((HEAD_DIM, RT), f32)
        for s_d, v_d in win:
            p_d = jnp.exp2(s_d - m)
            l8 = l8 + _fold8(p_d, jnp.sum)
            acc = acc + _dot(v_d, p_d.astype(bf16))
        o_win = acc * (1.0 / jnp.maximum(jnp.sum(l8, axis=0, keepdims=True), TINY))

        gt = gt_ref[0, g]
        o_ref[0, g] = gt[0:1] * o_cmp + gt[2:3] * o_win

        ktp = jnp.maximum(qb - 1, 0)
        s_prev = add_block_pen(_dot(ks_ref[0, g, tile_rows(ktp), :], qT)
                               + tb_ref[g, jnp.where(qb >= 1, TB_PREV, TB_NONE)], pen_ref, g, ktp)
        s_scr[g, tile_rows(ktp), :] = s_prev
        s_own = add_block_pen(_dot(ks_ref[0, g, tile_rows(qb), :], qT) + tb_ref[g, TB_OWN], pen_ref, g, qb)
        s_scr[g, tile_rows(qb), :] = s_own
        m8_scr[g] = jnp.maximum(_fold8(s_prev, jnp.max), _fold8(s_own, jnp.max))
        l8_scr[g] = jnp.zeros((8, RT), f32)
        acc_scr[g] = jnp.zeros((HEAD_DIM, RT), f32)

    def pass_a(i, c):
        for g in range(G):
            qT = q_ref[0, g]
            m8 = m8_scr[g]
            for u in range(SLC_UNROLL):
                kt = SLC_UNROLL * i + u
                live = kt < qb - 1
                ktc = jnp.minimum(kt, nkt - 1)
                s = add_block_pen(_dot(ks_ref[0, g, tile_rows(ktc), :], qT) + jnp.where(live, 0.0, NEG),
                                  penf_ref, g, ktc)
                s_scr[g, tile_rows(jnp.where(live, kt, nkt + u)), :] = s
                m8 = jnp.maximum(m8, _fold8(s, jnp.max))
            m8_scr[g] = m8
        return c

    lax.fori_loop(0, (jnp.maximum(qb - 1, 0) + SLC_UNROLL - 1) // SLC_UNROLL, pass_a, 0)

    def pass_b(i, c):
        for g in range(G):
            m = jnp.max(m8_scr[g], axis=0, keepdims=True)
            l8 = l8_scr[g]
            acc = acc_scr[g]
            for u in range(SLC_UNROLL):
                kt = SLC_UNROLL * i + u
                ktc = jnp.minimum(kt, qb)
                p_u = jnp.exp2(s_scr[g, tile_rows(ktc), :] - (m + jnp.where(kt <= qb, 0.0, -NEG)))
                l8 = l8 + _fold8(p_u, jnp.sum)
                acc = acc + _dot(vst_ref[0, g, ktc], p_u.astype(bf16))
            l8_scr[g] = l8
            acc_scr[g] = acc
        return c

    lax.fori_loop(0, qb // SLC_UNROLL + 1, pass_b, 0)

    for g in range(G):
        o_slc = acc_scr[g] * (1.0 / jnp.maximum(jnp.sum(l8_scr[g], axis=0, keepdims=True), TINY))
        o_ref[0, g] = o_ref[0, g] + gt_ref[0, g][1:2] * o_slc


def _bias_tables(rel_bias, S):
    G = N_KV_HEADS
    H = rel_bias.shape[1]
    r = H // G
    T = KEY_TILE
    n = jnp.arange(S + 2 * T, dtype=i32)
    max_exact = N_BUCKETS // 2
    nf = jnp.maximum(n, 1).astype(f32)
    large = max_exact + (jnp.log(nf / max_exact) / math.log(MAX_DISTANCE / max_exact)
                         * (N_BUCKETS - max_exact)).astype(i32)
    bucket = jnp.where(n < max_exact, n, jnp.minimum(large, N_BUCKETS - 1))
    bias1d = rel_bias.astype(f32)[bucket] * LOG2E

    def per_group(tab):
        lead = tab.shape[:-2]
        k = len(lead)
        t = jnp.moveaxis(tab, -1, 0).reshape((G, r) + lead + (T,))
        return jnp.moveaxis(t, 1, k + 1).reshape((G,) + lead + (r * T,))

    def masked(dist, ok):
        return jnp.where(jnp.asarray(ok)[..., None], bias1d[np.maximum(dist, 0)], NEG)

    kj, qi = np.arange(T)[:, None], np.arange(T)[None, :]
    dist = np.stack([d * T + qi - kj for d in range(3)])
    ok = np.stack([dist[0] >= 0, np.ones((T, T), bool), dist[2] < WINDOW])
    tb = jnp.concatenate([masked(dist, ok), jnp.full((1, T, T, H), NEG, f32)])
    cend = (np.arange(CMP_NEAR)[:, None] - CMP_PAD) * CMP_STRIDE + CMP_LEN - 1
    dcn = np.arange(T)[None, :] - cend
    rows = CMP_PAD + S // CMP_STRIDE
    far = jnp.broadcast_to(bias1d[-1][None, None, :], (rows, T, H))
    cstep = jnp.concatenate([far, jnp.full((rows, T, H), NEG, f32)])
    return per_group(tb), per_group(masked(dcn, dcn >= 0)), per_group(cstep)


def _nsa_attention(q, gates_t, shared, tables, B, S):
    kcm, vct, ks, vst, kw, vwt = shared
    tb, cbn, cstep = tables
    G, DH, T = N_KV_HEADS, HEAD_DIM, Q_BLOCK
    RT = q.shape[3]
    r = RT // T
    nqb = S // T
    nch = S // CMP_STRIDE
    nsb = S // SLC_LEN
    nkt = S // KEY_TILE
    cmp_start = np.arange(nch) * CMP_STRIDE
    slc_start = np.arange(nsb) * SLC_LEN
    overlap = np.clip(np.minimum(cmp_start[:, None] + CMP_LEN, slc_start[None, :] + SLC_LEN)
                      - np.maximum(cmp_start[:, None], slc_start[None, :]), 0, None) / CMP_LEN
    overlap[nch - 1] = 0.0
    c2st = jnp.asarray(overlap.T, bf16)
    qmap = lambda b, i: (b * nqb + i, 0, 0, 0)
    bat = lambda b, i: (b, 0, 0, 0)
    bat5 = lambda b, i: (b, 0, 0, 0, 0)
    once = pl.Buffered(1)
    return pl.pallas_call(
        functools.partial(_nsa_kernel, r=r, nsb=nsb, n_sel=min(N_SEL, nsb)),
        out_shape=jax.ShapeDtypeStruct(q.shape, f32),
        grid=(B, nqb),
        in_specs=[
            pl.BlockSpec((1, G, DH, RT), qmap),
            pl.BlockSpec((1, G, N_BRANCH, RT), qmap),
            pl.BlockSpec((1, G, nch, DH), bat),
            pl.BlockSpec((1, G, DH, nch), bat),
            pl.BlockSpec((1, G, S, DH), bat),
            pl.BlockSpec((1, G, nkt, DH, KEY_TILE), bat5),
            pl.BlockSpec((1, G, S, DH), bat),
            pl.BlockSpec((1, G, nkt, DH, KEY_TILE), bat5),
            pl.BlockSpec(cbn.shape, lambda b, i: (0, 0, 0), pipeline_mode=once),
            pl.BlockSpec(cstep.shape, lambda b, i: (0, 0, 0), pipeline_mode=once),
            pl.BlockSpec(tb.shape, lambda b, i: (0, 0, 0, 0), pipeline_mode=once),
            pl.BlockSpec((nsb, nch), lambda b, i: (0, 0), pipeline_mode=once),
        ],
        out_specs=pl.BlockSpec((1, G, DH, RT), qmap),
        scratch_shapes=[pltpu.VMEM((G, CMP_PAD + nch, RT), f32), pltpu.VMEM((G, nsb, RT), f32),
                        pltpu.VMEM((G, nsb, RT), f32), pltpu.VMEM((G, (nkt + SLC_UNROLL) * KEY_TILE, RT), f32),
                        pltpu.VMEM((G, 8, RT), f32), pltpu.VMEM((G, 8, RT), f32), pltpu.VMEM((G, DH, RT), f32)],
        compiler_params=_cparams("parallel", "arbitrary"),
        name="nsa_attention",
    )(q, gates_t, kcm, vct, ks, vst, kw, vwt, cbn, cstep, tb, c2st)


def _oproj_kernel(x_ref, a_ref, w_ref, o_ref):
    nt, G, DH, RT = a_ref.shape
    T = x_ref.shape[0] // nt
    r = RT // T
    rows = []
    for t in range(nt):
        cols = []
        for g in range(G):
            a = a_ref[t, g]
            cols.append(jnp.concatenate([a[:, k * T:(k + 1) * T] for k in range(r)], axis=0).T)
        rows.append(jnp.concatenate(cols, axis=1))
    attn = jnp.concatenate(rows, axis=0)
    o_ref[...] = x_ref[...] + _dot(attn.astype(bf16), w_ref[...])


def _out_project(xf, attn, w_o):
    N, D = xf.shape
    nt_all, G, DH, RT = attn.shape
    T = N // nt_all
    to = min(512, N)
    assert N % to == 0 and to % T == 0
    row = lambda i: (i, 0)
    return pl.pallas_call(
        _oproj_kernel,
        out_shape=jax.ShapeDtypeStruct((N, D), f32),
        grid=(N // to,),
        in_specs=[pl.BlockSpec((to, D), row), pl.BlockSpec((to // T, G, DH, RT), lambda i: (i, 0, 0, 0)),
                  pl.BlockSpec(w_o.shape, lambda i: (0, 0))],
        out_specs=pl.BlockSpec((to, D), row),
        compiler_params=_cparams("parallel"),
        name="out_project",
    )(xf, attn, w_o.astype(bf16))


def _nsa_layer(xf, B, S, g, w_qg, g_q, w_o, shared, tables):
    N = xf.shape[0]
    G, T = N_KV_HEADS, Q_BLOCK
    q, gates = _qg_project(xf, g, w_qg, g_q)
    r = gates.shape[1] // (G * N_BRANCH)
    gates_t = gates.reshape(N // T, T, G, r, N_BRANCH).transpose(0, 2, 4, 3, 1).reshape(N // T, G, N_BRANCH, r * T)
    attn = _nsa_attention(q, gates_t, shared, tables, B, S)
    return _out_project(xf, attn, w_o)


def kernel(x, rel_bias, norm_mix, norm_ffn, pool_w, pool_scale, norm_kv, w_kv, g_k, cmp_pos, cmp_w1, cmp_w2,
           w_qg, g_q, w_o, ffn_wg, ffn_wu, ffn_wd, router, moe_wg, moe_wu, moe_wd):
    B, S, D = x.shape
    depth = norm_mix.shape[0]
    n_a = depth // 2
    assert S % Q_BLOCK == 0
    xf = x.reshape(B * S, D)
    shared = None
    tables = _bias_tables(rel_bias, S)
    for layer in range(depth):
        if layer < n_a:
            xf = _pool_layer(xf.reshape(B, S, D), norm_mix[layer], pool_w[layer], pool_scale[layer]).reshape(B * S, D)
        else:
            j = layer - n_a
            xf = _nsa_layer(xf, B, S, norm_mix[layer], w_qg[j], g_q[j], w_o[j], shared, tables)
        i = layer // 2
        if layer % 2 == 0:
            xf = _ffn_layer(xf, norm_ffn[layer], ffn_wg[i], ffn_wu[i], ffn_wd[i])
        else:
            xf = _moe_layer(xf, norm_ffn[layer], router[i], moe_wg[i], moe_wu[i], moe_wd[i])
        if layer == n_a - 1:
            shared = _build_shared(xf, B, S, norm_kv, w_kv, g_k, cmp_pos, cmp_w1, cmp_w2)
    return xf.reshape(B, S, D)
```

```python
import functools
import math

import numpy as np
import jax
import jax.numpy as jnp
from jax import lax
from jax.experimental import pallas as pl
from jax.experimental.pallas import tpu as pltpu
from jax.experimental.pallas import tpu_sc as plsc

f32 = jnp.float32
bf16 = jnp.bfloat16
i32 = jnp.int32
u32 = jnp.uint32

POOL_WINDOWS = (2, 4, 8, 16)
HEAD_DIM = 64
N_KV_HEADS = 4
N_BRANCH = 3
CMP_LEN = 32
CMP_STRIDE = 16
CMP_HIDDEN = 4 * HEAD_DIM
SLC_LEN = 64
N_SEL = 4
WINDOW = 256
Q_BLOCK = 128
FORCED_SCORE = 1.0e4
N_BUCKETS = 32
MAX_DISTANCE = 128
N_EXPERTS = 8
TOP_K = 2
EPS = 1e-6
NEG = -1e30
TINY = 1e-30
LOG2E = math.log2(math.e)
TB_OWN, TB_PREV, TB_WIN2, TB_NONE = range(4)

KEY_TILE = Q_BLOCK
SLC_UNROLL = 2
POOL_HALO = 16
NH_PAD = 16
V7X_VMEM_LIMIT = 56 * 1024 * 1024
V7X_SC_CORES, V7X_SC_SUBCORES = 2, 16
SC_MAX_INDEX_VECTOR = 128
SC_GATHER_CHUNK = 32

CMP_PER_Q = Q_BLOCK // CMP_STRIDE
CMP_PAD = 2 * CMP_PER_Q
CMP_NEAR = 3 * CMP_PER_Q

assert CMP_LEN == 2 * CMP_STRIDE and KEY_TILE == 2 * SLC_LEN and WINDOW == 2 * KEY_TILE
assert max(POOL_WINDOWS) <= POOL_HALO
assert 2 * KEY_TILE - (Q_BLOCK - 1) >= MAX_DISTANCE
assert (CMP_PAD + 1) * CMP_STRIDE - (CMP_LEN - 1) >= MAX_DISTANCE and CMP_PER_Q % 8 == 0


def _cparams(*sem):
    return pltpu.CompilerParams(dimension_semantics=sem, vmem_limit_bytes=V7X_VMEM_LIMIT)


def _rms(xf, g):
    ms = jnp.mean(xf * xf, axis=-1, keepdims=True)
    return (xf * lax.rsqrt(ms + EPS)) * g


def _dot(a, b):
    return jnp.dot(a, b, preferred_element_type=f32)


def _dot_hilo(a, b):
    hi = a.astype(bf16)
    lo = (a - hi.astype(f32)).astype(bf16)
    return _dot(hi, b) + _dot(lo, b)


def _head_rms(z, gvec, seg, segt):
    ssq = _dot_hilo(z * z, seg)
    inv = lax.rsqrt(ssq * (1.0 / HEAD_DIM) + EPS)
    return (z * _dot_hilo(inv, segt)) * gvec


def _silu(a):
    return a * jax.nn.sigmoid(a)


def _seg_mats(width):
    heads = width // HEAD_DIM
    seg = np.zeros((width, NH_PAD), np.float32)
    seg[np.arange(width), np.arange(width) // HEAD_DIM] = 1.0
    assert heads <= NH_PAD
    return jnp.asarray(seg, bf16), jnp.asarray(seg.T, bf16)


def _pool_kernel(x_ref, halo_ref, g_ref, w_ref, scale_ref, o_ref, *, tp, cg):
    i = pl.program_id(1)
    x = x_ref[0]
    xh = jnp.concatenate([halo_ref[0], x], axis=0)
    h = _rms(xh, g_ref[...])
    row = lax.broadcasted_iota(i32, (tp + POOL_HALO, 1), 0)
    t_abs = i * tp + row - POOL_HALO
    h = jnp.where(t_abs >= 0, h, 0.0)
    outs = []
    for gi, w in enumerate(POOL_WINDOWS):
        hg = h[:, gi * cg:(gi + 1) * cg]
        s = hg
        sh = 1
        while sh < w:
            s = s + pltpu.roll(s, sh, axis=0)
            sh *= 2
        cnt = jnp.clip(t_abs + 1, 1, w).astype(f32)
        diff = (s / cnt - hg)[POOL_HALO:]
        outs.append(_dot(diff.astype(bf16), w_ref[gi]))
    y = jnp.concatenate(outs, axis=1)
    o_ref[0] = x + y * scale_ref[...]


def _pool_layer(x3, g, w_grp, scale):
    B, S, D = x3.shape
    tp = min(512, S)
    cg = D // len(POOL_WINDOWS)
    assert S % tp == 0 and tp % POOL_HALO == 0 and all(w & (w - 1) == 0 for w in POOL_WINDOWS)
    hb = tp // POOL_HALO
    return pl.pallas_call(
        functools.partial(_pool_kernel, tp=tp, cg=cg),
        out_shape=jax.ShapeDtypeStruct((B, S, D), f32),
        grid=(B, S // tp),
        in_specs=[
            pl.BlockSpec((1, tp, D), lambda b, i: (b, i, 0)),
            pl.BlockSpec((1, POOL_HALO, D), lambda b, i: (b, jnp.maximum(i * hb - 1, 0), 0)),
            pl.BlockSpec((1, D), lambda b, i: (0, 0)),
            pl.BlockSpec((len(POOL_WINDOWS), cg, cg), lambda b, i: (0, 0, 0)),
            pl.BlockSpec((1, D), lambda b, i: (0, 0)),
        ],
        out_specs=pl.BlockSpec((1, tp, D), lambda b, i: (b, i, 0)),
        compiler_params=_cparams("parallel", "arbitrary"),
        name="pool_layer",
    )(x3, x3, g.reshape(1, D), w_grp.astype(bf16), scale.reshape(1, D))


def _ffn_kernel(x_ref, g_ref, wg_ref, wu_ref, wd_ref, o_ref, h_ref, acc_ref):
    j = pl.program_id(1)

    @pl.when(j == 0)
    def _():
        x = x_ref[...]
        h_ref[...] = _rms(x, g_ref[...]).astype(bf16)
        acc_ref[...] = x

    h = h_ref[...]
    act = _silu(_dot(h, wg_ref[...])) * _dot(h, wu_ref[...])
    acc_ref[...] += _dot(act.astype(bf16), wd_ref[...])

    @pl.when(j == pl.num_programs(1) - 1)
    def _():
        o_ref[...] = acc_ref[...]


def _ffn_layer(xf, g, wg, wu, wd):
    N, D = xf.shape
    F = wg.shape[1]
    tm = min(1024, N)
    tf = 512
    assert N % tm == 0 and F % tf == 0
    return pl.pallas_call(
        _ffn_kernel,
        out_shape=jax.ShapeDtypeStruct((N, D), f32),
        grid=(N // tm, F // tf),
        in_specs=[
            pl.BlockSpec((tm, D), lambda i, j: (i, 0)),
            pl.BlockSpec((1, D), lambda i, j: (0, 0)),
            pl.BlockSpec((D, tf), lambda i, j: (0, j)),
            pl.BlockSpec((D, tf), lambda i, j: (0, j)),
            pl.BlockSpec((tf, D), lambda i, j: (j, 0)),
        ],
        out_specs=pl.BlockSpec((tm, D), lambda i, j: (i, 0)),
        scratch_shapes=[pltpu.VMEM((tm, D), bf16), pltpu.VMEM((tm, D), f32)],
        compiler_params=_cparams("parallel", "arbitrary"),
        name="ffn_dense",
    )(xf, g.reshape(1, D), wg.astype(bf16), wu.astype(bf16), wd.astype(bf16))


def _router_kernel(x_ref, g_ref, rt_ref, hp_ref, idx_ref, gate_ref):
    h = _rms(x_ref[...], g_ref[...])
    half = h.shape[1] // 2
    bits = pltpu.bitcast(h.astype(bf16).astype(f32), u32)
    hp_ref[...] = pltpu.bitcast((bits[:, :half] & jnp.uint32(0xFFFF0000)) | (bits[:, half:] >> 16), i32)
    logits = lax.dot_general(rt_ref[...], h, (((1,), (1,)), ((), ())),
                             precision=lax.Precision.HIGHEST, preferred_element_type=f32)
    ne = logits.shape[0]
    row = lax.broadcasted_iota(i32, logits.shape, 0)
    m1 = jnp.max(logits, axis=0, keepdims=True)
    i1 = jnp.min(jnp.where(logits == m1, row, ne), axis=0, keepdims=True)
    rest = jnp.where(row == i1, -jnp.inf, logits)
    m2 = jnp.max(rest, axis=0, keepdims=True)
    i2 = jnp.min(jnp.where(rest == m2, row, ne), axis=0, keepdims=True)
    e2 = jnp.exp(m2 - m1)
    den = 1.0 + e2
    idx_ref[...] = jnp.concatenate([i1, i2], axis=0)
    gate_ref[...] = jnp.concatenate([1.0 / den, e2 / den], axis=0)


def _router(xf, g, router):
    N, D = xf.shape
    E = router.shape[1]
    tr = min(1024, N)
    assert N % tr == 0 and TOP_K == 2
    return pl.pallas_call(
        _router_kernel,
        out_shape=(jax.ShapeDtypeStruct((N, D // 2), i32),
                   jax.ShapeDtypeStruct((TOP_K, N), i32),
                   jax.ShapeDtypeStruct((TOP_K, N), f32)),
        grid=(N // tr,),
        in_specs=[
            pl.BlockSpec((tr, D), lambda i: (i, 0)),
            pl.BlockSpec((1, D), lambda i: (0, 0)),
            pl.BlockSpec((E, D), lambda i: (0, 0)),
        ],
        out_specs=(pl.BlockSpec((tr, D // 2), lambda i: (i, 0)),
                   pl.BlockSpec((TOP_K, tr), lambda i: (0, i)),
                   pl.BlockSpec((TOP_K, tr), lambda i: (0, i))),
        compiler_params=_cparams("parallel"),
        name="moe_router",
    )(xf, g.reshape(1, D), router.T)


def _sc_gather_rows(table, idx, chunk):
    B = idx.shape[0]
    D = table.shape[1]
    workers = V7X_SC_CORES * V7X_SC_SUBCORES
    per_w = B // workers
    cpw = per_w // chunk
    assert B % (8 * workers) == 0 and per_w % (2 * chunk) == 0 and cpw % 8 == 0
    assert chunk % 8 == 0 and chunk <= SC_MAX_INDEX_VECTOR
    mesh = plsc.VectorSubcoreMesh(core_axis_name="c", subcore_axis_name="s")

    @functools.partial(
        pl.kernel, mesh=mesh, out_type=jax.ShapeDtypeStruct((B, D), table.dtype),
        scratch_types=[pltpu.VMEM((cpw, chunk), i32), pltpu.VMEM((2, chunk, D), table.dtype),
                       pltpu.SemaphoreType.DMA, pltpu.SemaphoreType.DMA])
    def gather(table_hbm, idx_hbm, out_hbm, idx_v, rows_v, sem0, sem1):
        wid = lax.axis_index("s") * V7X_SC_CORES + lax.axis_index("c")
        pltpu.sync_copy(idx_hbm.at[pl.ds(pl.multiple_of(wid * cpw, 8), cpw)], idx_v)
        sems = (sem0, sem1)

        def fetch(c, b):
            return pltpu.make_async_copy(table_hbm.at[idx_v.at[c]], rows_v.at[b], sems[b])

        fetch(0, 0).start()

        @pl.loop(0, cpw, step=2)
        def _(c):
            for b in range(2):
                cur = c + b
                fetch(cur, b).wait()

                @pl.when(cur + 1 < cpw)
                def _():
                    fetch(cur + 1, 1 - b).start()

                pltpu.sync_copy(rows_v.at[b], out_hbm.at[pl.ds(pl.multiple_of(wid * per_w + cur * chunk, 8), chunk)])

    return gather(table, idx.reshape(B // chunk, chunk))


def _sc_scatter_rows(src, slots, n_slots, chunk):
    K, N = slots.shape
    D = src.shape[1]
    workers = V7X_SC_CORES * V7X_SC_SUBCORES
    per_w = N // workers
    cpw = per_w // chunk
    assert N % (8 * workers) == 0 and per_w % (2 * chunk) == 0 and cpw % 8 == 0
    assert chunk % 8 == 0 and chunk <= SC_MAX_INDEX_VECTOR
    mesh = plsc.VectorSubcoreMesh(core_axis_name="c", subcore_axis_name="s")

    @functools.partial(
        pl.kernel, mesh=mesh, out_type=jax.ShapeDtypeStruct((n_slots, D), src.dtype),
        scratch_types=[pltpu.VMEM((K, cpw, chunk), i32), pltpu.VMEM((2, chunk, D), src.dtype),
                       pltpu.SemaphoreType.DMA, pltpu.SemaphoreType.DMA, pltpu.SemaphoreType.DMA,
                       pltpu.SemaphoreType.DMA])
    def scatter(src_hbm, slots_hbm, out_hbm, slot_v, rows_v, lsem0, lsem1, ssem0, ssem1):
        wid = lax.axis_index("s") * V7X_SC_CORES + lax.axis_index("c")
        for k in range(K):
            pltpu.sync_copy(slots_hbm.at[k, pl.ds(pl.multiple_of(wid * cpw, 8), cpw)], slot_v.at[k])
        lsems, ssems = (lsem0, lsem1), (ssem0, ssem1)

        def load(c, b):
            return pltpu.make_async_copy(
                src_hbm.at[pl.ds(pl.multiple_of(wid * per_w + c * chunk, 8), chunk)], rows_v.at[b], lsems[b])

        def store(c, b, k):
            return pltpu.make_async_copy(rows_v.at[b], out_hbm.at[slot_v.at[k].at[c]], ssems[b])

        load(0, 0).start()

        @pl.loop(0, cpw, step=2)
        def _(c):
            for b in range(2):
                cur = c + b
                load(cur, b).wait()
                for k in range(K):
                    store(cur, b, k).start()

                @pl.when(cur >= 1)
                def _():
                    for k in range(K):
                        store(cur - 1, 1 - b, k).wait()

                @pl.when(cur + 1 < cpw)
                def _():
                    load(cur + 1, 1 - b).start()

        for k in range(K):
            store(cpw - 1, 1, k).wait()

    return scatter(src, slots.reshape(K, N // chunk, chunk))


def _gmm_kernel(te_ref, nu_ref, xs_ref, wg_ref, wu_ref, wd_ref, o_ref, h_ref, acc_ref):
    i = pl.program_id(0)
    j = pl.program_id(1)

    @pl.when(i < nu_ref[0])
    def _():
        @pl.when(j == 0)
        def _():
            w = pltpu.bitcast(xs_ref[...], u32)
            left = pltpu.bitcast(w & jnp.uint32(0xFFFF0000), f32)
            right = pltpu.bitcast(w << 16, f32)
            h = jnp.concatenate([left, right], axis=1)
            filled = lax.broadcasted_iota(i32, (h.shape[0], 1), 0) < nu_ref[1 + i]
            h_ref[...] = jnp.where(filled, h, 0.0).astype(bf16)
            acc_ref[...] = jnp.zeros_like(acc_ref)

        h = h_ref[...]
        act = _silu(_dot(h, wg_ref[0])) * _dot(h, wu_ref[0])
        acc_ref[...] += _dot(act.astype(bf16), wd_ref[0])

        @pl.when(j == pl.num_programs(1) - 1)
        def _():
            o_ref[...] = acc_ref[...]

    @pl.when((i >= nu_ref[0]) & (j == 0))
    def _():
        o_ref[...] = jnp.zeros_like(o_ref)


def _gmm(tile_expert, n_used, xs, wg, wu, wd, tm):
    P, half = xs.shape
    D = 2 * half
    F = wg.shape[2]
    tf = 512
    assert P % tm == 0 and F % tf == 0

    nf = F // tf

    def row_map(i, j, te, nu):
        return (jnp.minimum(i, nu[0] - 1), 0)

    def ff(i, j, nu):
        return jnp.where(i < nu[0], j, nf - 1)

    grid_spec = pltpu.PrefetchScalarGridSpec(
        num_scalar_prefetch=2,
        grid=(P // tm, nf),
        in_specs=[
            pl.BlockSpec((tm, half), row_map),
            pl.BlockSpec((1, D, tf), lambda i, j, te, nu: (te[i], 0, ff(i, j, nu))),
            pl.BlockSpec((1, D, tf), lambda i, j, te, nu: (te[i], 0, ff(i, j, nu))),
            pl.BlockSpec((1, tf, D), lambda i, j, te, nu: (te[i], ff(i, j, nu), 0)),
        ],
        out_specs=pl.BlockSpec((tm, D), lambda i, j, te, nu: (i, 0)),
        scratch_shapes=[pltpu.VMEM((tm, D), bf16), pltpu.VMEM((tm, D), f32)],
    )
    return pl.pallas_call(
        _gmm_kernel,
        out_shape=jax.ShapeDtypeStruct((P, D), f32),
        grid_spec=grid_spec,
        compiler_params=_cparams("arbitrary", "arbitrary"),
        name="moe_gmm",
    )(tile_expert, n_used, xs, wg.astype(bf16), wu.astype(bf16), wd.astype(bf16))


def _combine_kernel(x_ref, gate_ref, y_ref, o_ref):
    gate = gate_ref[...]
    acc = x_ref[...]
    for k in range(TOP_K):
        acc = acc + y_ref[k] * gate[:, k:k + 1]
    o_ref[...] = acc


def _combine(xf, gate_tk, y_tok):
    N, D = xf.shape
    tt = min(1024, N)
    assert N % tt == 0
    return pl.pallas_call(
        _combine_kernel,
        out_shape=jax.ShapeDtypeStruct((N, D), f32),
        grid=(N // tt,),
        in_specs=[pl.BlockSpec((tt, D), lambda i: (i, 0)),
                  pl.BlockSpec((tt, TOP_K), lambda i: (i, 0)),
                  pl.BlockSpec((TOP_K, tt, D), lambda i: (0, i, 0))],
        out_specs=pl.BlockSpec((tt, D), lambda i: (i, 0)),
        compiler_params=_cparams("parallel"),
        name="moe_combine",
    )(xf, gate_tk, y_tok)


def _moe_layer(xf, g, router, wg, wu, wd):
    N, D = xf.shape
    E = router.shape[1]
    tm = min(1024, N)
    hp, idx, gate = _router(xf, g, router)
    e_flat = idx.reshape(-1)
    onehot = (e_flat[:, None] == jnp.arange(E, dtype=i32)[None, :]).astype(i32)
    csum = jnp.cumsum(onehot, axis=0)
    counts = csum[-1]
    padded = ((counts + tm - 1) // tm) * tm
    ends = jnp.cumsum(padded)
    starts = ends - padded
    pos = (jnp.sum((csum + starts[None, :]) * onehot, axis=1) - 1).astype(i32)
    P = TOP_K * N + E * tm
    n_used = (ends[-1] // tm).astype(i32)
    tile_start = jnp.minimum(jnp.arange(P // tm, dtype=i32), n_used - 1) * tm
    tile_expert = jnp.sum((ends[None, :] <= tile_start[:, None]).astype(i32), axis=1)
    group_end = (starts + counts)[tile_expert]
    tile_rows = jnp.clip(group_end - tile_start, 0, tm).astype(i32)
    xs = _sc_scatter_rows(hp, pos.reshape(TOP_K, N), P, SC_GATHER_CHUNK)
    ys = _gmm(tile_expert, jnp.concatenate([n_used.reshape(1), tile_rows]), xs, wg, wu, wd, tm)
    y_tok = _sc_gather_rows(ys, pos, SC_GATHER_CHUNK).reshape(TOP_K, N, D)
    return _combine(xf, gate.T, y_tok)


def _kv_kernel(x_ref, g_ref, w_ref, gk_ref, seg_ref, segt_ref, kc_ref, vc_ref, ks_ref, vs_ref, kw_ref, vw_ref):
    h = _rms(x_ref[...], g_ref[...]).astype(bf16)
    kv = _dot(h, w_ref[...])
    wd = kc_ref.shape[1]
    part = lambda p: kv[:, p * wd:(p + 1) * wd]
    seg, segt = seg_ref[...], segt_ref[...]
    kc_ref[...] = part(0)
    vc_ref[...] = part(1)
    G, DH = ks_ref.shape[1], ks_ref.shape[3]
    for k_ref, v_ref, pk, gain in ((ks_ref, vs_ref, 2, gk_ref[0:1, :]), (kw_ref, vw_ref, 4, gk_ref[1:2, :])):
        kn = _head_rms(part(pk), gain, seg, segt).astype(bf16)
        v = part(pk + 1)
        for g in range(G):
            k_ref[0, g] = kn[:, g * DH:(g + 1) * DH]
        for t in range(v_ref.shape[2]):
            vt = v[t * KEY_TILE:(t + 1) * KEY_TILE, :].T
            for g in range(G):
                v_ref[0, g, t] = vt[g * DH:(g + 1) * DH].astype(bf16)


def _kv_project(xf, B, S, norm_kv, w_kv, g_k):
    N, D = xf.shape
    G, DH = N_KV_HEADS, HEAD_DIM
    wd = G * DH
    tk = min(512, S)
    spb = S // tk
    tpk = tk // KEY_TILE
    assert S % tk == 0 and tk % KEY_TILE == 0 and w_kv.shape[1] == 2 * N_BRANCH * wd
    seg, segt = _seg_mats(wd)
    gk = jnp.stack([jnp.tile(g_k[1], G), jnp.tile(g_k[2], G)])
    row = lambda i: (i, 0)
    const = lambda i: (0, 0)
    k_shape = jax.ShapeDtypeStruct((B, G, S, DH), bf16)
    v_shape = jax.ShapeDtypeStruct((B, G, S // KEY_TILE, DH, KEY_TILE), bf16)
    k_spec = pl.BlockSpec((1, G, tk, DH), lambda i: (i // spb, 0, i % spb, 0))
    v_spec = pl.BlockSpec((1, G, tpk, DH, KEY_TILE), lambda i: (i // spb, 0, i % spb, 0, 0))
    return pl.pallas_call(
        _kv_kernel,
        out_shape=(jax.ShapeDtypeStruct((N, wd), f32), jax.ShapeDtypeStruct((N, wd), f32),
                   k_shape, v_shape, k_shape, v_shape),
        grid=(N // tk,),
        in_specs=[
            pl.BlockSpec((tk, D), row),
            pl.BlockSpec((1, D), const),
            pl.BlockSpec(w_kv.shape, const),
            pl.BlockSpec((2, wd), const),
            pl.BlockSpec((wd, NH_PAD), const),
            pl.BlockSpec((NH_PAD, wd), const),
        ],
        out_specs=(pl.BlockSpec((tk, wd), row), pl.BlockSpec((tk, wd), row), k_spec, v_spec, k_spec, v_spec),
        compiler_params=_cparams("parallel"),
        name="kv_project",
    )(xf, norm_kv.reshape(1, D), w_kv.astype(bf16), gk, seg, segt)


def _compress_kernel(c_ref, pos_ref, w1_ref, w2_ref, gk_ref, o_ref):
    kv = pl.program_id(0)
    c = c_ref[0, 0, 0]
    a = _dot((c + pos_ref[0, 0:1, :]).astype(bf16), w1_ref[0, 0])
    b = _dot((c + pos_ref[0, 1:2, :]).astype(bf16), w1_ref[0, 1])
    n = c.shape[0]
    hid = a + pltpu.roll(b, n - 1, axis=0)
    out = _dot(_silu(hid).astype(bf16), w2_ref[0])

    @pl.when(kv == 0)
    def _():
        o_ref[0, 0, 0] = _rms(out, gk_ref[...])

    @pl.when(kv != 0)
    def _():
        o_ref[0, 0, 0] = out


def _compress(kc, vc, B, S, cmp_pos, cmp_w1, cmp_w2, g_k0):
    G, DH = N_KV_HEADS, HEAD_DIM
    nch = S // CMP_STRIDE
    cw = CMP_STRIDE * DH

    def chunks(z):
        return z.reshape(B, nch, CMP_STRIDE, G, DH).transpose(0, 3, 1, 2, 4).reshape(B, G, nch, cw)

    c = jnp.stack([chunks(kc), chunks(vc)])
    pos = cmp_pos.reshape(2, 2, cw)
    w1 = cmp_w1.reshape(2, 2, cw, CMP_HIDDEN).astype(bf16)
    return pl.pallas_call(
        _compress_kernel,
        out_shape=jax.ShapeDtypeStruct((2, B, G, nch, DH), f32),
        grid=(2, B, G),
        in_specs=[
            pl.BlockSpec((1, 1, 1, nch, cw), lambda k, b, g: (k, b, g, 0, 0)),
            pl.BlockSpec((1, 2, cw), lambda k, b, g: (k, 0, 0)),
            pl.BlockSpec((1, 2, cw, CMP_HIDDEN), lambda k, b, g: (k, 0, 0, 0)),
            pl.BlockSpec((1, CMP_HIDDEN, DH), lambda k, b, g: (k, 0, 0)),
            pl.BlockSpec((1, DH), lambda k, b, g: (0, 0)),
        ],
        out_specs=pl.BlockSpec((1, 1, 1, nch, DH), lambda k, b, g: (k, b, g, 0, 0)),
        compiler_params=_cparams("arbitrary", "arbitrary", "arbitrary"),
        name="kv_compress",
    )(c, pos, w1, cmp_w2.astype(bf16), g_k0.reshape(1, DH))


def _build_shared(xf, B, S, norm_kv, w_kv, g_k, cmp_pos, cmp_w1, cmp_w2):
    kc, vc, ks, vst, kw, vwt = _kv_project(xf, B, S, norm_kv, w_kv, g_k)
    cmp = _compress(kc, vc, B, S, cmp_pos, cmp_w1, cmp_w2, g_k[0]).astype(bf16)
    kcm = cmp[0]
    vct = cmp[1].transpose(0, 1, 3, 2)
    return kcm, vct, ks, vst, kw, vwt


def _qg_kernel(x_ref, g_ref, wq_ref, wgate_ref, gq_ref, seg_ref, segt_ref, q_ref, gate_ref):
    h = _rms(x_ref[...], g_ref[...]).astype(bf16)
    q = _head_rms(_dot(h, wq_ref[...]), gq_ref[...], seg_ref[...], segt_ref[...])
    q = q * (HEAD_DIM ** -0.5 * LOG2E)
    gate_ref[...] = jax.nn.sigmoid(_dot(h, wgate_ref[...]))
    nt, G, DH, RT = q_ref.shape
    T = q.shape[0] // nt
    r = RT // T
    for t in range(nt):
        for g in range(G):
            qt = q[t * T:(t + 1) * T, g * r * DH:(g + 1) * r * DH].T
            q_ref[t, g] = jnp.concatenate([qt[k * DH:(k + 1) * DH] for k in range(r)], axis=1).astype(bf16)


def _qg_project(xf, g, w_qg, g_q):
    N, D = xf.shape
    HD = D
    ng = w_qg.shape[1] - HD
    tq = min(512, N)
    G, T = N_KV_HEADS, Q_BLOCK
    RT = HD // (G * HEAD_DIM) * T
    assert N % tq == 0 and tq % T == 0
    seg, segt = _seg_mats(HD)
    row = lambda i: (i, 0)
    const = lambda i: (0, 0)
    return pl.pallas_call(
        _qg_kernel,
        out_shape=(jax.ShapeDtypeStruct((N // T, G, HEAD_DIM, RT), bf16), jax.ShapeDtypeStruct((N, ng), f32)),
        grid=(N // tq,),
        in_specs=[
            pl.BlockSpec((tq, D), row),
            pl.BlockSpec((1, D), const),
            pl.BlockSpec((D, HD), const),
            pl.BlockSpec((D, ng), const),
            pl.BlockSpec((1, HD), const),
            pl.BlockSpec((HD, NH_PAD), const),
            pl.BlockSpec((NH_PAD, HD), const),
        ],
        out_specs=(pl.BlockSpec((tq // T, G, HEAD_DIM, RT), lambda i: (i, 0, 0, 0)), pl.BlockSpec((tq, ng), row)),
        compiler_params=_cparams("parallel"),
        name="qg_project",
    )(xf, g.reshape(1, D), w_qg[:, :HD].astype(bf16), w_qg[:, HD:].astype(bf16),
      jnp.tile(g_q, HD // HEAD_DIM).reshape(1, HD), seg, segt)


def _fold8(x, op):
    return op(x.reshape(x.shape[0] // 8, 8, x.shape[1]), axis=0)


def _nsa_kernel(q_ref, gt_ref, kc_ref, vct_ref, ks_ref, vst_ref, kw_ref, vwt_ref, cbn_ref, cstep_ref, tb_ref, c2st_ref,
                o_ref, cb_scr, pen_ref, penf_ref, s_scr, m8_scr, l8_scr, acc_scr, *, r, nsb, n_sel):
    qb = pl.program_id(1)
    t0 = qb * Q_BLOCK
    T = Q_BLOCK
    RT = r * T
    G = q_ref.shape[1]
    nch = kc_ref.shape[2]
    nkt = s_scr.shape[1] // KEY_TILE - SLC_UNROLL

    def tile_rows(kt):
        return pl.ds(pl.multiple_of(kt * KEY_TILE, KEY_TILE), KEY_TILE)

    def add_block_pen(s, ref, g, kt):
        return jnp.concatenate([s[:SLC_LEN] + ref[g, pl.ds(2 * kt, 1), :],
                                s[SLC_LEN:] + ref[g, pl.ds(2 * kt + 1, 1), :]], axis=0)

    for g in range(G):
        qT = q_ref[0, g]

        cb_scr[g] = cstep_ref[g, pl.ds(pl.multiple_of(nch - (qb + 1) * CMP_PER_Q, 8), CMP_PAD + nch), :]
        cb_scr[g, pl.ds(pl.multiple_of(qb * CMP_PER_Q, 8), CMP_NEAR), :] = cbn_ref[g]
        s = _dot(kc_ref[0, g], qT) + cb_scr[g, CMP_PAD:CMP_PAD + nch, :]
        p = jnp.where(s > 0.5 * NEG, jnp.exp2(s - jnp.max(s, axis=0, keepdims=True)), 0.0)
        p = p / jnp.maximum(jnp.sum(p, axis=0, keepdims=True), TINY)
        o_cmp = _dot(vct_ref[0, g], p.astype(bf16))

        psum = p[:, 0:T]
        for k in range(1, r):
            psum = psum + p[:, k * T:(k + 1) * T]
        hi = psum.astype(bf16)
        lo = (psum - hi.astype(f32)).astype(bf16)
        imp = _dot(c2st_ref[...], hi) + _dot(c2st_ref[...], lo)
        jb = lax.broadcasted_iota(i32, (nsb, T), 0)
        blk_q = jnp.right_shift(t0 + lax.broadcasted_iota(i32, (nsb, T), 1), SLC_LEN.bit_length() - 1)
        forced = (jb == 0) | (jb == blk_q) | (jb == blk_q - 1)
        score = jnp.where(forced, FORCED_SCORE, jnp.where(jb <= blk_q, imp, NEG))
        pen = jnp.full((nsb, T), NEG, f32)
        for _ in range(n_sel):
            mx = jnp.max(score, axis=0, keepdims=True)
            first = jnp.min(jnp.where(score == mx, jb, nsb), axis=0, keepdims=True)
            hit = jb == first
            pen = jnp.where(hit, 0.0, pen)
            score = jnp.where(hit, -jnp.inf, score)
        pen = jnp.concatenate([pen] * r, axis=1)
        pen_ref[g] = pen
        penf_ref[g] = pen + cstep_ref[g, 0:1, :]

        win = []
        for d in range(WINDOW // KEY_TILE, -1, -1):
            kt = qb - d
            ktc = jnp.maximum(kt, 0)
            tab = tb_ref[g, jnp.where(kt >= 0, d, TB_NONE)]
            win.append((_dot(kw_ref[0, g, tile_rows(ktc), :], qT) + tab, vwt_ref[0, g, ktc]))
        m8 = _fold8(win[0][0], jnp.max)
        for s_d, _ in win[1:]:
            m8 = jnp.maximum(m8, _fold8(s_d, jnp.max))
        m = jnp.max(m8, axis=0, keepdims=True)
        l8 = jnp.zeros((8, RT), f32)
        acc = jnp.zeros((HEAD_DIM, RT), f32)
        for s_d, v_d in win:
            p_d = jnp.exp2(s_d - m)
            l8 = l8 + _fold8(p_d, jnp.sum)
            acc = acc + _dot(v_d, p_d.astype(bf16))
        o_win = acc * (1.0 / jnp.maximum(jnp.sum(l8, axis=0, keepdims=True), TINY))

        gt = gt_ref[0, g]
        o_ref[0, g] = gt[0:1] * o_cmp + gt[2:3] * o_win

        ktp = jnp.maximum(qb - 1, 0)
        s_prev = add_block_pen(_dot(ks_ref[0, g, tile_rows(ktp), :], qT)
                               + tb_ref[g, jnp.where(qb >= 1, TB_PREV, TB_NONE)], pen_ref, g, ktp)
        s_scr[g, tile_rows(ktp), :] = s_prev
        s_own = add_block_pen(_dot(ks_ref[0, g, tile_rows(qb), :], qT) + tb_ref[g, TB_OWN], pen_ref, g, qb)
        s_scr[g, tile_rows(qb), :] = s_own
        m8_scr[g] = jnp.maximum(_fold8(s_prev, jnp.max), _fold8(s_own, jnp.max))
        l8_scr[g] = jnp.zeros((8, RT), f32)
        acc_scr[g] = jnp.zeros((HEAD_DIM, RT), f32)

    def pass_a(i, c):
        for g in range(G):
            qT = q_ref[0, g]
            m8 = m8_scr[g]
            for u in range(SLC_UNROLL):
                kt = SLC_UNROLL * i + u
                live = kt < qb - 1
                ktc = jnp.minimum(kt, nkt - 1)
                s = add_block_pen(_dot(ks_ref[0, g, tile_rows(ktc), :], qT) + jnp.where(live, 0.0, NEG),
                                  penf_ref, g, ktc)
                s_scr[g, tile_rows(jnp.where(live, kt, nkt + u)), :] = s
                m8 = jnp.maximum(m8, _fold8(s, jnp.max))
            m8_scr[g] = m8
        return c

    lax.fori_loop(0, (jnp.maximum(qb - 1, 0) + SLC_UNROLL - 1) // SLC_UNROLL, pass_a, 0)

    def pass_b(i, c):
        for g in range(G):
            m = jnp.max(m8_scr[g], axis=0, keepdims=True)
            l8 = l8_scr[g]
            acc = acc_scr[g]
            for u in range(SLC_UNROLL):
                kt = SLC_UNROLL * i + u
                ktc = jnp.minimum(kt, qb)
                p_u = jnp.exp2(s_scr[g, tile_rows(ktc), :] - (m + jnp.where(kt <= qb, 0.0, -NEG)))
                l8 = l8 + _fold8(p_u, jnp.sum)
                acc = acc + _dot(vst_ref[0, g, ktc], p_u.astype(bf16))
            l8_scr[g] = l8
            acc_scr[g] = acc
        return c

    lax.fori_loop(0, qb // SLC_UNROLL + 1, pass_b, 0)

    for g in range(G):
        o_slc = acc_scr[g] * (1.0 / jnp.maximum(jnp.sum(l8_scr[g], axis=0, keepdims=True), TINY))
        o_ref[0, g] = o_ref[0, g] + gt_ref[0, g][1:2] * o_slc


def _bias_tables(rel_bias, S):
    G = N_KV_HEADS
    H = rel_bias.shape[1]
    r = H // G
    T = KEY_TILE
    n = jnp.arange(S + 2 * T, dtype=i32)
    max_exact = N_BUCKETS // 2
    nf = jnp.maximum(n, 1).astype(f32)
    large = max_exact + (jnp.log(nf / max_exact) / math.log(MAX_DISTANCE / max_exact)
                         * (N_BUCKETS - max_exact)).astype(i32)
    bucket = jnp.where(n < max_exact, n, jnp.minimum(large, N_BUCKETS - 1))
    bias1d = rel_bias.astype(f32)[bucket] * LOG2E

    def per_group(tab):
        lead = tab.shape[:-2]
        k = len(lead)
        t = jnp.moveaxis(tab, -1, 0).reshape((G, r) + lead + (T,))
        return jnp.moveaxis(t, 1, k + 1).reshape((G,) + lead + (r * T,))

    def masked(dist, ok):
        return jnp.where(jnp.asarray(ok)[..., None], bias1d[np.maximum(dist, 0)], NEG)

    kj, qi = np.arange(T)[:, None], np.arange(T)[None, :]
    dist = np.stack([d * T + qi - kj for d in range(3)])
    ok = np.stack([dist[0] >= 0, np.ones((T, T), bool), dist[2] < WINDOW])
    tb = jnp.concatenate([masked(dist, ok), jnp.full((1, T, T, H), NEG, f32)])
    cend = (np.arange(CMP_NEAR)[:, None] - CMP_PAD) * CMP_STRIDE + CMP_LEN - 1
    dcn = np.arange(T)[None, :] - cend
    rows = CMP_PAD + S // CMP_STRIDE
    far = jnp.broadcast_to(bias1d[-1][None, None, :], (rows, T, H))
    cstep = jnp.concatenate([far, jnp.full((rows, T, H), NEG, f32)])
    return per_group(tb), per_group(masked(dcn, dcn >= 0)), per_group(cstep)


def _nsa_attention(q, gates_t, shared, tables, B, S):
    kcm, vct, ks, vst, kw, vwt = shared
    tb, cbn, cstep = tables
    G, DH, T = N_KV_HEADS, HEAD_DIM, Q_BLOCK
    RT = q.shape[3]
    r = RT // T
    nqb = S // T
    nch = S // CMP_STRIDE
    nsb = S // SLC_LEN
    nkt = S // KEY_TILE
    cmp_start = np.arange(nch) * CMP_STRIDE
    slc_start = np.arange(nsb) * SLC_LEN
    overlap = np.clip(np.minimum(cmp_start[:, None] + CMP_LEN, slc_start[None, :] + SLC_LEN)
                      - np.maximum(cmp_start[:, None], slc_start[None, :]), 0, None) / CMP_LEN
    overlap[nch - 1] = 0.0
    c2st = jnp.asarray(overlap.T, bf16)
    qmap = lambda b, i: (b * nqb + i, 0, 0, 0)
    bat = lambda b, i: (b, 0, 0, 0)
    bat5 = lambda b, i: (b, 0, 0, 0, 0)
    once = pl.Buffered(1)
    return pl.pallas_call(
        functools.partial(_nsa_kernel, r=r, nsb=nsb, n_sel=min(N_SEL, nsb)),
        out_shape=jax.ShapeDtypeStruct(q.shape, f32),
        grid=(B, nqb),
        in_specs=[
            pl.BlockSpec((1, G, DH, RT), qmap),
            pl.BlockSpec((1, G, N_BRANCH, RT), qmap),
            pl.BlockSpec((1, G, nch, DH), bat),
            pl.BlockSpec((1, G, DH, nch), bat),
            pl.BlockSpec((1, G, S, DH), bat),
            pl.BlockSpec((1, G, nkt, DH, KEY_TILE), bat5),
            pl.BlockSpec((1, G, S, DH), bat),
            pl.BlockSpec((1, G, nkt, DH, KEY_TILE), bat5),
            pl.BlockSpec(cbn.shape, lambda b, i: (0, 0, 0), pipeline_mode=once),
            pl.BlockSpec(cstep.shape, lambda b, i: (0, 0, 0), pipeline_mode=once),
            pl.BlockSpec(tb.shape, lambda b, i: (0, 0, 0, 0), pipeline_mode=once),
            pl.BlockSpec((nsb, nch), lambda b, i: (0, 0), pipeline_mode=once),
        ],
        out_specs=pl.BlockSpec((1, G, DH, RT), qmap),
        scratch_shapes=[pltpu.VMEM((G, CMP_PAD + nch, RT), f32), pltpu.VMEM((G, nsb, RT), f32),
                        pltpu.VMEM((G, nsb, RT), f32), pltpu.VMEM((G, (nkt + SLC_UNROLL) * KEY_TILE, RT), f32),
                        pltpu.VMEM((G, 8, RT), f32), pltpu.VMEM((G, 8, RT), f32), pltpu.VMEM((G, DH, RT), f32)],
        compiler_params=_cparams("parallel", "arbitrary"),
        name="nsa_attention",
    )(q, gates_t, kcm, vct, ks, vst, kw, vwt, cbn, cstep, tb, c2st)


def _oproj_kernel(x_ref, a_ref, w_ref, o_ref):
    nt, G, DH, RT = a_ref.shape
    T = x_ref.shape[0] // nt
    r = RT // T
    rows = []
    for t in range(nt):
        cols = []
        for g in range(G):
            a = a_ref[t, g]
            cols.append(jnp.concatenate([a[:, k * T:(k + 1) * T] for k in range(r)], axis=0).T)
        rows.append(jnp.concatenate(cols, axis=1))
    attn = jnp.concatenate(rows, axis=0)
    o_ref[...] = x_ref[...] + _dot(attn.astype(bf16), w_ref[...])


def _out_project(xf, attn, w_o):
    N, D = xf.shape
    nt_all, G, DH, RT = attn.shape
    T = N // nt_all
    to = min(512, N)
    assert N % to == 0 and to % T == 0
    row = lambda i: (i, 0)
    return pl.pallas_call(
        _oproj_kernel,
        out_shape=jax.ShapeDtypeStruct((N, D), f32),
        grid=(N // to,),
        in_specs=[pl.BlockSpec((to, D), row), pl.BlockSpec((to // T, G, DH, RT), lambda i: (i, 0, 0, 0)),
                  pl.BlockSpec(w_o.shape, lambda i: (0, 0))],
        out_specs=pl.BlockSpec((to, D), row),
        compiler_params=_cparams("parallel"),
        name="out_project",
    )(xf, attn, w_o.astype(bf16))


def _nsa_layer(xf, B, S, g, w_qg, g_q, w_o, shared, tables):
    N = xf.shape[0]
    G, T = N_KV_HEADS, Q_BLOCK
    q, gates = _qg_project(xf, g, w_qg, g_q)
    r = gates.shape[1] // (G * N_BRANCH)
    gates_t = gates.reshape(N // T, T, G, r, N_BRANCH).transpose(0, 2, 4, 3, 1).reshape(N // T, G, N_BRANCH, r * T)
    attn = _nsa_attention(q, gates_t, shared, tables, B, S)
    return _out_project(xf, attn, w_o)


def kernel(x, rel_bias, norm_mix, norm_ffn, pool_w, pool_scale, norm_kv, w_kv, g_k, cmp_pos, cmp_w1, cmp_w2,
           w_qg, g_q, w_o, ffn_wg, ffn_wu, ffn_wd, router, moe_wg, moe_wu, moe_wd):
    B, S, D = x.shape
    depth = norm_mix.shape[0]
    n_a = depth // 2
    assert S % Q_BLOCK == 0
    xf = x.reshape(B * S, D)
    shared = None
    tables = _bias_tables(rel_bias, S)
    for layer in range(depth):
        if layer < n_a:
            xf = _pool_layer(xf.reshape(B, S, D), norm_mix[layer], pool_w[layer], pool_scale[layer]).reshape(B * S, D)
        else:
            j = layer - n_a
            xf = _nsa_layer(xf, B, S, norm_mix[layer], w_qg[j], g_q[j], w_o[j], shared, tables)
        i = layer // 2
        if layer % 2 == 0:
            xf = _ffn_layer(xf, norm_ffn[layer], ffn_wg[i], ffn_wu[i], ffn_wd[i])
        else:
            xf = _moe_layer(xf, norm_ffn[layer], router[i], moe_wg[i], moe_wu[i], moe_wd[i])
        if layer == n_a - 1:
            shared = _build_shared(xf, B, S, norm_kv, w_kv, g_k, cmp_pos, cmp_w1, cmp_w2)
    return xf.reshape(B, S, D)
```

```python
import functools
import math

import numpy as np
import jax
import jax.numpy as jnp
from jax import lax
from jax.experimental import pallas as pl
from jax.experimental.pallas import tpu as pltpu
from jax.experimental.pallas import tpu_sc as plsc

f32 = jnp.float32
bf16 = jnp.bfloat16
i32 = jnp.int32
u32 = jnp.uint32

POOL_WINDOWS = (2, 4, 8, 16)
HEAD_DIM = 64
N_KV_HEADS = 4
N_BRANCH = 3
CMP_LEN = 32
CMP_STRIDE = 16
CMP_HIDDEN = 4 * HEAD_DIM
SLC_LEN = 64
N_SEL = 4
WINDOW = 256
Q_BLOCK = 128
FORCED_SCORE = 1.0e4
N_BUCKETS = 32
MAX_DISTANCE = 128
N_EXPERTS = 8
TOP_K = 2
EPS = 1e-6
NEG = -1e30
TINY = 1e-30
LOG2E = math.log2(math.e)
TB_OWN, TB_PREV, TB_WIN2, TB_NONE = range(4)

KEY_TILE = Q_BLOCK
SLC_UNROLL = 2
POOL_HALO = 16
NH_PAD = 16
V7X_VMEM_LIMIT = 56 * 1024 * 1024
V7X_SC_CORES, V7X_SC_SUBCORES = 2, 16
SC_MAX_INDEX_VECTOR = 128
SC_GATHER_CHUNK = 32

CMP_PER_Q = Q_BLOCK // CMP_STRIDE
CMP_PAD = 2 * CMP_PER_Q
CMP_NEAR = 3 * CMP_PER_Q

assert CMP_LEN == 2 * CMP_STRIDE and KEY_TILE == 2 * SLC_LEN and WINDOW == 2 * KEY_TILE
assert max(POOL_WINDOWS) <= POOL_HALO
assert 2 * KEY_TILE - (Q_BLOCK - 1) >= MAX_DISTANCE
assert (CMP_PAD + 1) * CMP_STRIDE - (CMP_LEN - 1) >= MAX_DISTANCE and CMP_PER_Q % 8 == 0


def _cparams(*sem):
    return pltpu.CompilerParams(dimension_semantics=sem, vmem_limit_bytes=V7X_VMEM_LIMIT)


def _rms(xf, g):
    ms = jnp.mean(xf * xf, axis=-1, keepdims=True)
    return (xf * lax.rsqrt(ms + EPS)) * g


def _dot(a, b):
    return jnp.dot(a, b, preferred_element_type=f32)


def _dot_hilo(a, b):
    hi = a.astype(bf16)
    lo = (a - hi.astype(f32)).astype(bf16)
    return _dot(hi, b) + _dot(lo, b)


def _head_rms(z, gvec, seg, segt):
    ssq = _dot_hilo(z * z, seg)
    inv = lax.rsqrt(ssq * (1.0 / HEAD_DIM) + EPS)
    return (z * _dot_hilo(inv, segt)) * gvec


def _silu(a):
    return a * jax.nn.sigmoid(a)


def _seg_mats(width):
    heads = width // HEAD_DIM
    seg = np.zeros((width, NH_PAD), np.float32)
    seg[np.arange(width), np.arange(width) // HEAD_DIM] = 1.0
    assert heads <= NH_PAD
    return jnp.asarray(seg, bf16), jnp.asarray(seg.T, bf16)


def _pool_kernel(x_ref, halo_ref, g_ref, w_ref, scale_ref, o_ref, *, tp, cg):
    i = pl.program_id(1)
    x = x_ref[0]
    xh = jnp.concatenate([halo_ref[0], x], axis=0)
    h = _rms(xh, g_ref[...])
    row = lax.broadcasted_iota(i32, (tp + POOL_HALO, 1), 0)
    t_abs = i * tp + row - POOL_HALO
    h = jnp.where(t_abs >= 0, h, 0.0)
    outs = []
    for gi, w in enumerate(POOL_WINDOWS):
        hg = h[:, gi * cg:(gi + 1) * cg]
        s = hg
        sh = 1
        while sh < w:
            s = s + pltpu.roll(s, sh, axis=0)
            sh *= 2
        cnt = jnp.clip(t_abs + 1, 1, w).astype(f32)
        diff = (s / cnt - hg)[POOL_HALO:]
        outs.append(_dot(diff.astype(bf16), w_ref[gi]))
    y = jnp.concatenate(outs, axis=1)
    o_ref[0] = x + y * scale_ref[...]


def _pool_layer(x3, g, w_grp, scale):
    B, S, D = x3.shape
    tp = min(512, S)
    cg = D // len(POOL_WINDOWS)
    assert S % tp == 0 and tp % POOL_HALO == 0 and all(w & (w - 1) == 0 for w in POOL_WINDOWS)
    hb = tp // POOL_HALO
    return pl.pallas_call(
        functools.partial(_pool_kernel, tp=tp, cg=cg),
        out_shape=jax.ShapeDtypeStruct((B, S, D), f32),
        grid=(B, S // tp),
        in_specs=[
            pl.BlockSpec((1, tp, D), lambda b, i: (b, i, 0)),
            pl.BlockSpec((1, POOL_HALO, D), lambda b, i: (b, jnp.maximum(i * hb - 1, 0), 0)),
            pl.BlockSpec((1, D), lambda b, i: (0, 0)),
            pl.BlockSpec((len(POOL_WINDOWS), cg, cg), lambda b, i: (0, 0, 0)),
            pl.BlockSpec((1, D), lambda b, i: (0, 0)),
        ],
        out_specs=pl.BlockSpec((1, tp, D), lambda b, i: (b, i, 0)),
        compiler_params=_cparams("parallel", "arbitrary"),
        name="pool_layer",
    )(x3, x3, g.reshape(1, D), w_grp.astype(bf16), scale.reshape(1, D))


def _ffn_kernel(x_ref, g_ref, wg_ref, wu_ref, wd_ref, o_ref, h_ref, acc_ref):
    j = pl.program_id(1)

    @pl.when(j == 0)
    def _():
        x = x_ref[...]
        h_ref[...] = _rms(x, g_ref[...]).astype(bf16)
        acc_ref[...] = x

    h = h_ref[...]
    act = _silu(_dot(h, wg_ref[...])) * _dot(h, wu_ref[...])
    acc_ref[...] += _dot(act.astype(bf16), wd_ref[...])

    @pl.when(j == pl.num_programs(1) - 1)
    def _():
        o_ref[...] = acc_ref[...]


def _ffn_layer(xf, g, wg, wu, wd):
    N, D = xf.shape
    F = wg.shape[1]
    tm = min(1024, N)
    tf = 512
    assert N % tm == 0 and F % tf == 0
    return pl.pallas_call(
        _ffn_kernel,
        out_shape=jax.ShapeDtypeStruct((N, D), f32),
        grid=(N // tm, F // tf),
        in_specs=[
            pl.BlockSpec((tm, D), lambda i, j: (i, 0)),
            pl.BlockSpec((1, D), lambda i, j: (0, 0)),
            pl.BlockSpec((D, tf), lambda i, j: (0, j)),
            pl.BlockSpec((D, tf), lambda i, j: (0, j)),
            pl.BlockSpec((tf, D), lambda i, j: (j, 0)),
        ],
        out_specs=pl.BlockSpec((tm, D), lambda i, j: (i, 0)),
        scratch_shapes=[pltpu.VMEM((tm, D), bf16), pltpu.VMEM((tm, D), f32)],
        compiler_params=_cparams("parallel", "arbitrary"),
        name="ffn_dense",
    )(xf, g.reshape(1, D), wg.astype(bf16), wu.astype(bf16), wd.astype(bf16))


def _router_kernel(x_ref, g_ref, rt_ref, hp_ref, idx_ref, gate_ref):
    h = _rms(x_ref[...], g_ref[...])
    half = h.shape[1] // 2
    bits = pltpu.bitcast(h.astype(bf16).astype(f32), u32)
    hp_ref[...] = pltpu.bitcast((bits[:, :half] & jnp.uint32(0xFFFF0000)) | (bits[:, half:] >> 16), i32)
    logits = lax.dot_general(rt_ref[...], h, (((1,), (1,)), ((), ())),
                             precision=lax.Precision.HIGHEST, preferred_element_type=f32)
    ne = logits.shape[0]
    row = lax.broadcasted_iota(i32, logits.shape, 0)
    m1 = jnp.max(logits, axis=0, keepdims=True)
    i1 = jnp.min(jnp.where(logits == m1, row, ne), axis=0, keepdims=True)
    rest = jnp.where(row == i1, -jnp.inf, logits)
    m2 = jnp.max(rest, axis=0, keepdims=True)
    i2 = jnp.min(jnp.where(rest == m2, row, ne), axis=0, keepdims=True)
    e2 = jnp.exp(m2 - m1)
    den = 1.0 + e2
    idx_ref[...] = jnp.concatenate([i1, i2], axis=0)
    gate_ref[...] = jnp.concatenate([1.0 / den, e2 / den], axis=0)


def _router(xf, g, router):
    N, D = xf.shape
    E = router.shape[1]
    tr = min(1024, N)
    assert N % tr == 0 and TOP_K == 2
    return pl.pallas_call(
        _router_kernel,
        out_shape=(jax.ShapeDtypeStruct((N, D // 2), i32),
                   jax.ShapeDtypeStruct((TOP_K, N), i32),
                   jax.ShapeDtypeStruct((TOP_K, N), f32)),
        grid=(N // tr,),
        in_specs=[
            pl.BlockSpec((tr, D), lambda i: (i, 0)),
            pl.BlockSpec((1, D), lambda i: (0, 0)),
            pl.BlockSpec((E, D), lambda i: (0, 0)),
        ],
        out_specs=(pl.BlockSpec((tr, D // 2), lambda i: (i, 0)),
                   pl.BlockSpec((TOP_K, tr), lambda i: (0, i)),
                   pl.BlockSpec((TOP_K, tr), lambda i: (0, i))),
        compiler_params=_cparams("parallel"),
        name="moe_router",
    )(xf, g.reshape(1, D), router.T)


def _sc_gather_rows(table, idx, chunk):
    B = idx.shape[0]
    D = table.shape[1]
    workers = V7X_SC_CORES * V7X_SC_SUBCORES
    per_w = B // workers
    cpw = per_w // chunk
    assert B % (8 * workers) == 0 and per_w % (2 * chunk) == 0 and cpw % 8 == 0
    assert chunk % 8 == 0 and chunk <= SC_MAX_INDEX_VECTOR
    mesh = plsc.VectorSubcoreMesh(core_axis_name="c", subcore_axis_name="s")

    @functools.partial(
        pl.kernel, mesh=mesh, out_type=jax.ShapeDtypeStruct((B, D), table.dtype),
        scratch_types=[pltpu.VMEM((cpw, chunk), i32), pltpu.VMEM((2, chunk, D), table.dtype),
                       pltpu.SemaphoreType.DMA, pltpu.SemaphoreType.DMA])
    def gather(table_hbm, idx_hbm, out_hbm, idx_v, rows_v, sem0, sem1):
        wid = lax.axis_index("s") * V7X_SC_CORES + lax.axis_index("c")
        pltpu.sync_copy(idx_hbm.at[pl.ds(pl.multiple_of(wid * cpw, 8), cpw)], idx_v)
        sems = (sem0, sem1)

        def fetch(c, b):
            return pltpu.make_async_copy(table_hbm.at[idx_v.at[c]], rows_v.at[b], sems[b])

        fetch(0, 0).start()

        @pl.loop(0, cpw, step=2)
        def _(c):
            for b in range(2):
                cur = c + b
                fetch(cur, b).wait()

                @pl.when(cur + 1 < cpw)
                def _():
                    fetch(cur + 1, 1 - b).start()

                pltpu.sync_copy(rows_v.at[b], out_hbm.at[pl.ds(pl.multiple_of(wid * per_w + cur * chunk, 8), chunk)])

    return gather(table, idx.reshape(B // chunk, chunk))


def _sc_scatter_rows(src, slots, n_slots, chunk):
    K, N = slots.shape
    D = src.shape[1]
    workers = V7X_SC_CORES * V7X_SC_SUBCORES
    per_w = N // workers
    cpw = per_w // chunk
    assert N % (8 * workers) == 0 and per_w % (2 * chunk) == 0 and cpw % 8 == 0
    assert chunk % 8 == 0 and chunk <= SC_MAX_INDEX_VECTOR
    mesh = plsc.VectorSubcoreMesh(core_axis_name="c", subcore_axis_name="s")

    @functools.partial(
        pl.kernel, mesh=mesh, out_type=jax.ShapeDtypeStruct((n_slots, D), src.dtype),
        scratch_types=[pltpu.VMEM((K, cpw, chunk), i32), pltpu.VMEM((2, chunk, D), src.dtype),
                       pltpu.SemaphoreType.DMA, pltpu.SemaphoreType.DMA, pltpu.SemaphoreType.DMA,
                       pltpu.SemaphoreType.DMA])
    def scatter(src_hbm, slots_hbm, out_hbm, slot_v, rows_v, lsem0, lsem1, ssem0, ssem1):
        wid = lax.axis_index("s") * V7X_SC_CORES + lax.axis_index("c")
        for k in range(K):
            pltpu.sync_copy(slots_hbm.at[k, pl.ds(pl.multiple_of(wid * cpw, 8), cpw)], slot_v.at[k])
        lsems, ssems = (lsem0, lsem1), (ssem0, ssem1)

        def load(c, b):
            return pltpu.make_async_copy(
                src_hbm.at[pl.ds(pl.multiple_of(wid * per_w + c * chunk, 8), chunk)], rows_v.at[b], lsems[b])

        def store(c, b, k):
            return pltpu.make_async_copy(rows_v.at[b], out_hbm.at[slot_v.at[k].at[c]], ssems[b])

        load(0, 0).start()

        @pl.loop(0, cpw, step=2)
        def _(c):
            for b in range(2):
                cur = c + b
                load(cur, b).wait()
                for k in range(K):
                    store(cur, b, k).start()

                @pl.when(cur >= 1)
                def _():
                    for k in range(K):
                        store(cur - 1, 1 - b, k).wait()

                @pl.when(cur + 1 < cpw)
                def _():
                    load(cur + 1, 1 - b).start()

        for k in range(K):
            store(cpw - 1, 1, k).wait()

    return scatter(src, slots.reshape(K, N // chunk, chunk))


def _gmm_kernel(te_ref, nu_ref, xs_ref, wg_ref, wu_ref, wd_ref, o_ref, h_ref, acc_ref):
    i = pl.program_id(0)
    j = pl.program_id(1)

    @pl.when(i < nu_ref[0])
    def _():
        @pl.when(j == 0)
        def _():
            w = pltpu.bitcast(xs_ref[...], u32)
            left = pltpu.bitcast(w & jnp.uint32(0xFFFF0000), f32)
            right = pltpu.bitcast(w << 16, f32)
            h = jnp.concatenate([left, right], axis=1)
            filled = lax.broadcasted_iota(i32, (h.shape[0], 1), 0) < nu_ref[1 + i]
            h_ref[...] = jnp.where(filled, h, 0.0).astype(bf16)
            acc_ref[...] = jnp.zeros_like(acc_ref)

        h = h_ref[...]
        act = _silu(_dot(h, wg_ref[0])) * _dot(h, wu_ref[0])
        acc_ref[...] += _dot(act.astype(bf16), wd_ref[0])

        @pl.when(j == pl.num_programs(1) - 1)
        def _():
            o_ref[...] = acc_ref[...]

    @pl.when((i >= nu_ref[0]) & (j == 0))
    def _():
        o_ref[...] = jnp.zeros_like(o_ref)


def _gmm(tile_expert, n_used, xs, wg, wu, wd, tm):
    P, half = xs.shape
    D = 2 * half
    F = wg.shape[2]
    tf = 512
    assert P % tm == 0 and F % tf == 0

    nf = F // tf

    def row_map(i, j, te, nu):
        return (jnp.minimum(i, nu[0] - 1), 0)

    def ff(i, j, nu):
        return jnp.where(i < nu[0], j, nf - 1)

    grid_spec = pltpu.PrefetchScalarGridSpec(
        num_scalar_prefetch=2,
        grid=(P // tm, nf),
        in_specs=[
            pl.BlockSpec((tm, half), row_map),
            pl.BlockSpec((1, D, tf), lambda i, j, te, nu: (te[i], 0, ff(i, j, nu))),
            pl.BlockSpec((1, D, tf), lambda i, j, te, nu: (te[i], 0, ff(i, j, nu))),
            pl.BlockSpec((1, tf, D), lambda i, j, te, nu: (te[i], ff(i, j, nu), 0)),
        ],
        out_specs=pl.BlockSpec((tm, D), lambda i, j, te, nu: (i, 0)),
        scratch_shapes=[pltpu.VMEM((tm, D), bf16), pltpu.VMEM((tm, D), f32)],
    )
    return pl.pallas_call(
        _gmm_kernel,
        out_shape=jax.ShapeDtypeStruct((P, D), f32),
        grid_spec=grid_spec,
        compiler_params=_cparams("arbitrary", "arbitrary"),
        name="moe_gmm",
    )(tile_expert, n_used, xs, wg, wu, wd)


def _combine_kernel(x_ref, gate_ref, y_ref, o_ref):
    gate = gate_ref[...]
    acc = x_ref[...]
    for k in range(TOP_K):
        acc = acc + y_ref[k] * gate[:, k:k + 1]
    o_ref[...] = acc


def _combine(xf, gate_tk, y_tok):
    N, D = xf.shape
    tt = min(1024, N)
    assert N % tt == 0
    return pl.pallas_call(
        _combine_kernel,
        out_shape=jax.ShapeDtypeStruct((N, D), f32),
        grid=(N // tt,),
        in_specs=[pl.BlockSpec((tt, D), lambda i: (i, 0)),
                  pl.BlockSpec((tt, TOP_K), lambda i: (i, 0)),
                  pl.BlockSpec((TOP_K, tt, D), lambda i: (0, i, 0))],
        out_specs=pl.BlockSpec((tt, D), lambda i: (i, 0)),
        compiler_params=_cparams("parallel"),
        name="moe_combine",
    )(xf, gate_tk, y_tok)


def _moe_layer(xf, g, router, wg, wu, wd, first_expert):
    N, D = xf.shape
    E = router.shape[1]
    tm = min(1024, N)
    hp, idx, gate = _router(xf, g, router)
    e_flat = idx.reshape(-1)
    onehot = (e_flat[:, None] == jnp.arange(E, dtype=i32)[None, :]).astype(i32)
    csum = jnp.cumsum(onehot, axis=0)
    counts = csum[-1]
    padded = ((counts + tm - 1) // tm) * tm
    ends = jnp.cumsum(padded)
    starts = ends - padded
    pos = (jnp.sum((csum + starts[None, :]) * onehot, axis=1) - 1).astype(i32)
    P = TOP_K * N + E * tm
    n_used = (ends[-1] // tm).astype(i32)
    tile_start = jnp.minimum(jnp.arange(P // tm, dtype=i32), n_used - 1) * tm
    tile_expert = jnp.sum((ends[None, :] <= tile_start[:, None]).astype(i32), axis=1)
    group_end = (starts + counts)[tile_expert]
    tile_rows = jnp.clip(group_end - tile_start, 0, tm).astype(i32)
    xs = _sc_scatter_rows(hp, pos.reshape(TOP_K, N), P, SC_GATHER_CHUNK)
    ys = _gmm(tile_expert + first_expert, jnp.concatenate([n_used.reshape(1), tile_rows]), xs, wg, wu, wd, tm)
    y_tok = _sc_gather_rows(ys, pos, SC_GATHER_CHUNK).reshape(TOP_K, N, D)
    return _combine(xf, gate.T, y_tok)


def _kv_kernel(x_ref, g_ref, w_ref, gk_ref, seg_ref, segt_ref, kc_ref, vc_ref, ks_ref, vs_ref, kw_ref, vw_ref):
    h = _rms(x_ref[...], g_ref[...]).astype(bf16)
    kv = _dot(h, w_ref[...])
    wd = kc_ref.shape[1]
    part = lambda p: kv[:, p * wd:(p + 1) * wd]
    seg, segt = seg_ref[...], segt_ref[...]
    kc_ref[...] = part(0)
    vc_ref[...] = part(1)
    G, DH = ks_ref.shape[1], ks_ref.shape[3]
    for k_ref, v_ref, pk, gain in ((ks_ref, vs_ref, 2, gk_ref[0:1, :]), (kw_ref, vw_ref, 4, gk_ref[1:2, :])):
        kn = _head_rms(part(pk), gain, seg, segt).astype(bf16)
        v = part(pk + 1)
        for g in range(G):
            k_ref[0, g] = kn[:, g * DH:(g + 1) * DH]
        for t in range(v_ref.shape[2]):
            vt = v[t * KEY_TILE:(t + 1) * KEY_TILE, :].T
            for g in range(G):
                v_ref[0, g, t] = vt[g * DH:(g + 1) * DH].astype(bf16)


def _kv_project(xf, B, S, norm_kv, w_kv, g_k):
    N, D = xf.shape
    G, DH = N_KV_HEADS, HEAD_DIM
    wd = G * DH
    tk = min(512, S)
    spb = S // tk
    tpk = tk // KEY_TILE
    assert S % tk == 0 and tk % KEY_TILE == 0 and w_kv.shape[1] == 2 * N_BRANCH * wd
    seg, segt = _seg_mats(wd)
    gk = jnp.stack([jnp.tile(g_k[1], G), jnp.tile(g_k[2], G)])
    row = lambda i: (i, 0)
    const = lambda i: (0, 0)
    k_shape = jax.ShapeDtypeStruct((B, G, S, DH), bf16)
    v_shape = jax.ShapeDtypeStruct((B, G, S // KEY_TILE, DH, KEY_TILE), bf16)
    k_spec = pl.BlockSpec((1, G, tk, DH), lambda i: (i // spb, 0, i % spb, 0))
    v_spec = pl.BlockSpec((1, G, tpk, DH, KEY_TILE), lambda i: (i // spb, 0, i % spb, 0, 0))
    return pl.pallas_call(
        _kv_kernel,
        out_shape=(jax.ShapeDtypeStruct((N, wd), f32), jax.ShapeDtypeStruct((N, wd), f32),
                   k_shape, v_shape, k_shape, v_shape),
        grid=(N // tk,),
        in_specs=[
            pl.BlockSpec((tk, D), row),
            pl.BlockSpec((1, D), const),
            pl.BlockSpec(w_kv.shape, const),
            pl.BlockSpec((2, wd), const),
            pl.BlockSpec((wd, NH_PAD), const),
            pl.BlockSpec((NH_PAD, wd), const),
        ],
        out_specs=(pl.BlockSpec((tk, wd), row), pl.BlockSpec((tk, wd), row), k_spec, v_spec, k_spec, v_spec),
        compiler_params=_cparams("parallel"),
        name="kv_project",
    )(xf, norm_kv.reshape(1, D), w_kv.astype(bf16), gk, seg, segt)


def _compress_kernel(c_ref, pos_ref, w1_ref, w2_ref, gk_ref, o_ref):
    kv = pl.program_id(0)
    c = c_ref[0, 0, 0]
    a = _dot((c + pos_ref[0, 0:1, :]).astype(bf16), w1_ref[0, 0])
    b = _dot((c + pos_ref[0, 1:2, :]).astype(bf16), w1_ref[0, 1])
    n = c.shape[0]
    hid = a + pltpu.roll(b, n - 1, axis=0)
    out = _dot(_silu(hid).astype(bf16), w2_ref[0])

    @pl.when(kv == 0)
    def _():
        o_ref[0, 0, 0] = _rms(out, gk_ref[...])

    @pl.when(kv != 0)
    def _():
        o_ref[0, 0, 0] = out


def _compress(kc, vc, B, S, cmp_pos, cmp_w1, cmp_w2, g_k0):
    G, DH = N_KV_HEADS, HEAD_DIM
    nch = S // CMP_STRIDE
    cw = CMP_STRIDE * DH

    def chunks(z):
        return z.reshape(B, nch, CMP_STRIDE, G, DH).transpose(0, 3, 1, 2, 4).reshape(B, G, nch, cw)

    c = jnp.stack([chunks(kc), chunks(vc)])
    pos = cmp_pos.reshape(2, 2, cw)
    w1 = cmp_w1.reshape(2, 2, cw, CMP_HIDDEN).astype(bf16)
    return pl.pallas_call(
        _compress_kernel,
        out_shape=jax.ShapeDtypeStruct((2, B, G, nch, DH), f32),
        grid=(2, B, G),
        in_specs=[
            pl.BlockSpec((1, 1, 1, nch, cw), lambda k, b, g: (k, b, g, 0, 0)),
            pl.BlockSpec((1, 2, cw), lambda k, b, g: (k, 0, 0)),
            pl.BlockSpec((1, 2, cw, CMP_HIDDEN), lambda k, b, g: (k, 0, 0, 0)),
            pl.BlockSpec((1, CMP_HIDDEN, DH), lambda k, b, g: (k, 0, 0)),
            pl.BlockSpec((1, DH), lambda k, b, g: (0, 0)),
        ],
        out_specs=pl.BlockSpec((1, 1, 1, nch, DH), lambda k, b, g: (k, b, g, 0, 0)),
        compiler_params=_cparams("arbitrary", "arbitrary", "arbitrary"),
        name="kv_compress",
    )(c, pos, w1, cmp_w2.astype(bf16), g_k0.reshape(1, DH))


def _build_shared(xf, B, S, norm_kv, w_kv, g_k, cmp_pos, cmp_w1, cmp_w2):
    kc, vc, ks, vst, kw, vwt = _kv_project(xf, B, S, norm_kv, w_kv, g_k)
    cmp = _compress(kc, vc, B, S, cmp_pos, cmp_w1, cmp_w2, g_k[0]).astype(bf16)
    kcm = cmp[0]
    vct = cmp[1].transpose(0, 1, 3, 2)
    return kcm, vct, ks, vst, kw, vwt


def _qg_kernel(x_ref, g_ref, wq_ref, wgate_ref, gq_ref, seg_ref, segt_ref, q_ref, gate_ref):
    h = _rms(x_ref[...], g_ref[...]).astype(bf16)
    q = _head_rms(_dot(h, wq_ref[...]), gq_ref[...], seg_ref[...], segt_ref[...])
    q = q * (HEAD_DIM ** -0.5 * LOG2E)
    gate_ref[...] = jax.nn.sigmoid(_dot(h, wgate_ref[...]))
    nt, G, DH, RT = q_ref.shape
    T = q.shape[0] // nt
    r = RT // T
    for t in range(nt):
        for g in range(G):
            qt = q[t * T:(t + 1) * T, g * r * DH:(g + 1) * r * DH].T
            q_ref[t, g] = jnp.concatenate([qt[k * DH:(k + 1) * DH] for k in range(r)], axis=1).astype(bf16)


def _qg_project(xf, g, w_qg, g_q):
    N, D = xf.shape
    HD = D
    ng = w_qg.shape[1] - HD
    tq = min(512, N)
    G, T = N_KV_HEADS, Q_BLOCK
    RT = HD // (G * HEAD_DIM) * T
    assert N % tq == 0 and tq % T == 0
    seg, segt = _seg_mats(HD)
    row = lambda i: (i, 0)
    const = lambda i: (0, 0)
    return pl.pallas_call(
        _qg_kernel,
        out_shape=(jax.ShapeDtypeStruct((N // T, G, HEAD_DIM, RT), bf16), jax.ShapeDtypeStruct((N, ng), f32)),
        grid=(N // tq,),
        in_specs=[
            pl.BlockSpec((tq, D), row),
            pl.BlockSpec((1, D), const),
            pl.BlockSpec((D, HD), const),
            pl.BlockSpec((D, ng), const),
            pl.BlockSpec((1, HD), const),
            pl.BlockSpec((HD, NH_PAD), const),
            pl.BlockSpec((NH_PAD, HD), const),
        ],
        out_specs=(pl.BlockSpec((tq // T, G, HEAD_DIM, RT), lambda i: (i, 0, 0, 0)), pl.BlockSpec((tq, ng), row)),
        compiler_params=_cparams("parallel"),
        name="qg_project",
    )(xf, g.reshape(1, D), w_qg[:, :HD].astype(bf16), w_qg[:, HD:].astype(bf16),
      jnp.tile(g_q, HD // HEAD_DIM).reshape(1, HD), seg, segt)


def _fold8(x, op):
    return op(x.reshape(x.shape[0] // 8, 8, x.shape[1]), axis=0)


def _nsa_kernel(q_ref, gt_ref, kc_ref, vct_ref, ks_ref, vst_ref, kw_ref, vwt_ref, cbn_ref, cstep_ref, tb_ref, c2st_ref,
                o_ref, cb_scr, pen_ref, penf_ref, s_scr, m8_scr, l8_scr, acc_scr, *, r, nsb, n_sel):
    qb = pl.program_id(1)
    t0 = qb * Q_BLOCK
    T = Q_BLOCK
    RT = r * T
    G = q_ref.shape[1]
    nch = kc_ref.shape[2]
    nkt = s_scr.shape[1] // KEY_TILE - SLC_UNROLL

    def tile_rows(kt):
        return pl.ds(pl.multiple_of(kt * KEY_TILE, KEY_TILE), KEY_TILE)

    def add_block_pen(s, ref, g, kt):
        return jnp.concatenate([s[:SLC_LEN] + ref[g, pl.ds(2 * kt, 1), :],
                                s[SLC_LEN:] + ref[g, pl.ds(2 * kt + 1, 1), :]], axis=0)

    for g in range(G):
        qT = q_ref[0, g]

        cb_scr[g] = cstep_ref[g, pl.ds(pl.multiple_of(nch - (qb + 1) * CMP_PER_Q, 8), CMP_PAD + nch), :]
        cb_scr[g, pl.ds(pl.multiple_of(qb * CMP_PER_Q, 8), CMP_NEAR), :] = cbn_ref[g]
        s = _dot(kc_ref[0, g], qT) + cb_scr[g, CMP_PAD:CMP_PAD + nch, :]
        p = jnp.where(s > 0.5 * NEG, jnp.exp2(s - jnp.max(s, axis=0, keepdims=True)), 0.0)
        p = p * (1.0 / jnp.maximum(jnp.sum(p, axis=0, keepdims=True), TINY))
        o_cmp = _dot(vct_ref[0, g], p.astype(bf16))

        psum = p[:, 0:T]
        for k in range(1, r):
            psum = psum + p[:, k * T:(k + 1) * T]
        hi = psum.astype(bf16)
        lo = (psum - hi.astype(f32)).astype(bf16)
        imp = _dot(c2st_ref[...], hi) + _dot(c2st_ref[...], lo)
        jb = lax.broadcasted_iota(i32, (nsb, T), 0)
        blk_q = jnp.right_shift(t0 + lax.broadcasted_iota(i32, (nsb, T), 1), SLC_LEN.bit_length() - 1)
        forced = (jb == 0) | (jb == blk_q) | (jb == blk_q - 1)
        score = jnp.where(forced, FORCED_SCORE, jnp.where(jb <= blk_q, imp, NEG))
        pen = jnp.full((nsb, T), NEG, f32)
        for _ in range(n_sel):
            mx = jnp.max(score, axis=0, keepdims=True)
            first = jnp.min(jnp.where(score == mx, jb, nsb), axis=0, keepdims=True)
            hit = jb == first
            pen = jnp.where(hit, 0.0, pen)
            score = jnp.where(hit, -jnp.inf, score)
        pen = jnp.concatenate([pen] * r, axis=1)
        pen_ref[g] = pen
        penf_ref[g] = pen + cstep_ref[g, 0:1, :]

        win = []
        for d in range(WINDOW // KEY_TILE, -1, -1):
            kt = qb - d
            ktc = jnp.maximum(kt, 0)
            tab = tb_ref[g, jnp.where(kt >= 0, d, TB_NONE)]
            win.append((_dot(kw_ref[0, g, tile_rows(ktc), :], qT) + tab, vwt_ref[0, g, ktc]))
        m8 = _fold8(win[0][0], jnp.max)
        for s_d, _ in win[1:]:
            m8 = jnp.maximum(m8, _fold8(s_d, jnp.max))
        m = jnp.max(m8, axis=0, keepdims=True)
        l8 = jnp.zeros((8, RT), f32)
        acc = jnp.zeros((HEAD_DIM, RT), f32)
        for s_d, v_d in win:
            p_d = jnp.exp2(s_d - m)
            l8 = l8 + _fold8(p_d, jnp.sum)
            acc = acc + _dot(v_d, p_d.astype(bf16))
        o_win = acc * (1.0 / jnp.maximum(jnp.sum(l8, axis=0, keepdims=True), TINY))

        gt = gt_ref[0, g]
        o_ref[0, g] = gt[0:1] * o_cmp + gt[2:3] * o_win

        ktp = jnp.maximum(qb - 1, 0)
        s_prev = add_block_pen(_dot(ks_ref[0, g, tile_rows(ktp), :], qT)
                               + tb_ref[g, jnp.where(qb >= 1, TB_PREV, TB_NONE)], pen_ref, g, ktp)
        s_scr[g, tile_rows(ktp), :] = s_prev
        s_own = add_block_pen(_dot(ks_ref[0, g, tile_rows(qb), :], qT) + tb_ref[g, TB_OWN], pen_ref, g, qb)
        s_scr[g, tile_rows(qb), :] = s_own
        m8_scr[g] = jnp.maximum(_fold8(s_prev, jnp.max), _fold8(s_own, jnp.max))
        l8_scr[g] = jnp.zeros((8, RT), f32)
        acc_scr[g] = jnp.zeros((HEAD_DIM, RT), f32)

    def pass_a(i, c):
        for g in range(G):
            qT = q_ref[0, g]
            m8 = m8_scr[g]
            for u in range(SLC_UNROLL):
                kt = SLC_UNROLL * i + u
                live = kt < qb - 1
                ktc = jnp.minimum(kt, nkt - 1)
                s = add_block_pen(_dot(ks_ref[0, g, tile_rows(ktc), :], qT) + jnp.where(live, 0.0, NEG),
                                  penf_ref, g, ktc)
                s_scr[g, tile_rows(jnp.where(live, kt, nkt + u)), :] = s
                m8 = jnp.maximum(m8, _fold8(s, jnp.max))
            m8_scr[g] = m8
        return c

    lax.fori_loop(0, (jnp.maximum(qb - 1, 0) + SLC_UNROLL - 1) // SLC_UNROLL, pass_a, 0)

    def pass_b(i, c):
        for g in range(G):
            m = jnp.max(m8_scr[g], axis=0, keepdims=True)
            l8 = l8_scr[g]
            acc = acc_scr[g]
            for u in range(SLC_UNROLL):
                kt = SLC_UNROLL * i + u
                ktc = jnp.minimum(kt, qb)
                p_u = jnp.exp2(s_scr[g, tile_rows(ktc), :] - (m + jnp.where(kt <= qb, 0.0, -NEG)))
                l8 = l8 + _fold8(p_u, jnp.sum)
                acc = acc + _dot(vst_ref[0, g, ktc], p_u.astype(bf16))
            l8_scr[g] = l8
            acc_scr[g] = acc
        return c

    lax.fori_loop(0, qb // SLC_UNROLL + 1, pass_b, 0)

    for g in range(G):
        o_slc = acc_scr[g] * (1.0 / jnp.maximum(jnp.sum(l8_scr[g], axis=0, keepdims=True), TINY))
        o_ref[0, g] = o_ref[0, g] + gt_ref[0, g][1:2] * o_slc


def _bias_tables(rel_bias, S):
    G = N_KV_HEADS
    H = rel_bias.shape[1]
    r = H // G
    T = KEY_TILE
    n = jnp.arange(S + 2 * T, dtype=i32)
    max_exact = N_BUCKETS // 2
    nf = jnp.maximum(n, 1).astype(f32)
    large = max_exact + (jnp.log(nf / max_exact) / math.log(MAX_DISTANCE / max_exact)
                         * (N_BUCKETS - max_exact)).astype(i32)
    bucket = jnp.where(n < max_exact, n, jnp.minimum(large, N_BUCKETS - 1))
    bias1d = rel_bias.astype(f32)[bucket] * LOG2E

    def per_group(tab):
        lead = tab.shape[:-2]
        k = len(lead)
        t = jnp.moveaxis(tab, -1, 0).reshape((G, r) + lead + (T,))
        return jnp.moveaxis(t, 1, k + 1).reshape((G,) + lead + (r * T,))

    def masked(dist, ok):
        return jnp.where(jnp.asarray(ok)[..., None], bias1d[np.maximum(dist, 0)], NEG)

    kj, qi = np.arange(T)[:, None], np.arange(T)[None, :]
    dist = np.stack([d * T + qi - kj for d in range(3)])
    ok = np.stack([dist[0] >= 0, np.ones((T, T), bool), dist[2] < WINDOW])
    tb = jnp.concatenate([masked(dist, ok), jnp.full((1, T, T, H), NEG, f32)])
    cend = (np.arange(CMP_NEAR)[:, None] - CMP_PAD) * CMP_STRIDE + CMP_LEN - 1
    dcn = np.arange(T)[None, :] - cend
    rows = CMP_PAD + S // CMP_STRIDE
    far = jnp.broadcast_to(bias1d[-1][None, None, :], (rows, T, H))
    cstep = jnp.concatenate([far, jnp.full((rows, T, H), NEG, f32)])
    return per_group(tb), per_group(masked(dcn, dcn >= 0)), per_group(cstep)


def _nsa_attention(q, gates_t, shared, tables, B, S):
    kcm, vct, ks, vst, kw, vwt = shared
    tb, cbn, cstep = tables
    G, DH, T = N_KV_HEADS, HEAD_DIM, Q_BLOCK
    RT = q.shape[3]
    r = RT // T
    nqb = S // T
    nch = S // CMP_STRIDE
    nsb = S // SLC_LEN
    nkt = S // KEY_TILE
    cmp_start = np.arange(nch) * CMP_STRIDE
    slc_start = np.arange(nsb) * SLC_LEN
    overlap = np.clip(np.minimum(cmp_start[:, None] + CMP_LEN, slc_start[None, :] + SLC_LEN)
                      - np.maximum(cmp_start[:, None], slc_start[None, :]), 0, None) / CMP_LEN
    overlap[nch - 1] = 0.0
    c2st = jnp.asarray(overlap.T, bf16)
    qmap = lambda b, i: (b * nqb + i, 0, 0, 0)
    bat = lambda b, i: (b, 0, 0, 0)
    bat5 = lambda b, i: (b, 0, 0, 0, 0)
    once = pl.Buffered(1)
    return pl.pallas_call(
        functools.partial(_nsa_kernel, r=r, nsb=nsb, n_sel=min(N_SEL, nsb)),
        out_shape=jax.ShapeDtypeStruct(q.shape, f32),
        grid=(B, nqb),
        in_specs=[
            pl.BlockSpec((1, G, DH, RT), qmap),
            pl.BlockSpec((1, G, N_BRANCH, RT), qmap),
            pl.BlockSpec((1, G, nch, DH), bat),
            pl.BlockSpec((1, G, DH, nch), bat),
            pl.BlockSpec((1, G, S, DH), bat),
            pl.BlockSpec((1, G, nkt, DH, KEY_TILE), bat5),
            pl.BlockSpec((1, G, S, DH), bat),
            pl.BlockSpec((1, G, nkt, DH, KEY_TILE), bat5),
            pl.BlockSpec(cbn.shape, lambda b, i: (0, 0, 0), pipeline_mode=once),
            pl.BlockSpec(cstep.shape, lambda b, i: (0, 0, 0), pipeline_mode=once),
            pl.BlockSpec(tb.shape, lambda b, i: (0, 0, 0, 0), pipeline_mode=once),
            pl.BlockSpec((nsb, nch), lambda b, i: (0, 0), pipeline_mode=once),
        ],
        out_specs=pl.BlockSpec((1, G, DH, RT), qmap),
        scratch_shapes=[pltpu.VMEM((G, CMP_PAD + nch, RT), f32), pltpu.VMEM((G, nsb, RT), f32),
                        pltpu.VMEM((G, nsb, RT), f32), pltpu.VMEM((G, (nkt + SLC_UNROLL) * KEY_TILE, RT), f32),
                        pltpu.VMEM((G, 8, RT), f32), pltpu.VMEM((G, 8, RT), f32), pltpu.VMEM((G, DH, RT), f32)],
        compiler_params=_cparams("parallel", "arbitrary"),
        name="nsa_attention",
    )(q, gates_t, kcm, vct, ks, vst, kw, vwt, cbn, cstep, tb, c2st)


def _oproj_kernel(x_ref, a_ref, w_ref, o_ref):
    nt, G, DH, RT = a_ref.shape
    T = x_ref.shape[0] // nt
    r = RT // T
    rows = []
    for t in range(nt):
        cols = []
        for g in range(G):
            a = a_ref[t, g]
            cols.append(jnp.concatenate([a[:, k * T:(k + 1) * T] for k in range(r)], axis=0).T)
        rows.append(jnp.concatenate(cols, axis=1))
    attn = jnp.concatenate(rows, axis=0)
    o_ref[...] = x_ref[...] + _dot(attn.astype(bf16), w_ref[...])


def _out_project(xf, attn, w_o):
    N, D = xf.shape
    nt_all, G, DH, RT = attn.shape
    T = N // nt_all
    to = min(512, N)
    assert N % to == 0 and to % T == 0
    row = lambda i: (i, 0)
    return pl.pallas_call(
        _oproj_kernel,
        out_shape=jax.ShapeDtypeStruct((N, D), f32),
        grid=(N // to,),
        in_specs=[pl.BlockSpec((to, D), row), pl.BlockSpec((to // T, G, DH, RT), lambda i: (i, 0, 0, 0)),
                  pl.BlockSpec(w_o.shape, lambda i: (0, 0))],
        out_specs=pl.BlockSpec((to, D), row),
        compiler_params=_cparams("parallel"),
        name="out_project",
    )(xf, attn, w_o.astype(bf16))


def _nsa_layer(xf, B, S, g, w_qg, g_q, w_o, shared, tables):
    N = xf.shape[0]
    G, T = N_KV_HEADS, Q_BLOCK
    q, gates = _qg_project(xf, g, w_qg, g_q)
    r = gates.shape[1] // (G * N_BRANCH)
    gates_t = gates.reshape(N // T, T, G, r, N_BRANCH).transpose(0, 2, 4, 3, 1).reshape(N // T, G, N_BRANCH, r * T)
    attn = _nsa_attention(q, gates_t, shared, tables, B, S)
    return _out_project(xf, attn, w_o)


def kernel(x, rel_bias, norm_mix, norm_ffn, pool_w, pool_scale, norm_kv, w_kv, g_k, cmp_pos, cmp_w1, cmp_w2,
           w_qg, g_q, w_o, ffn_wg, ffn_wu, ffn_wd, router, moe_wg, moe_wu, moe_wd):
    B, S, D = x.shape
    depth = norm_mix.shape[0]
    n_a = depth // 2
    assert S % Q_BLOCK == 0
    xf = x.reshape(B * S, D)
    shared = None
    tables = _bias_tables(rel_bias, S)
    n_exp = moe_wg.shape[1]
    moe_w = [w.reshape((-1,) + w.shape[2:]).astype(bf16) for w in (moe_wg, moe_wu, moe_wd)]
    for layer in range(depth):
        if layer < n_a:
            xf = _pool_layer(xf.reshape(B, S, D), norm_mix[layer], pool_w[layer], pool_scale[layer]).reshape(B * S, D)
        else:
            j = layer - n_a
            xf = _nsa_layer(xf, B, S, norm_mix[layer], w_qg[j], g_q[j], w_o[j], shared, tables)
        i = layer // 2
        if layer % 2 == 0:
            xf = _ffn_layer(xf, norm_ffn[layer], ffn_wg[i], ffn_wu[i], ffn_wd[i])
        else:
            xf = _moe_layer(xf, norm_ffn[layer], router[i], *moe_w, first_expert=i * n_exp)
        if layer == n_a - 1:
            shared = _build_shared(xf, B, S, norm_kv, w_kv, g_k, cmp_pos, cmp_w1, cmp_w2)
    return xf.reshape(B, S, D)
```

```python
import functools
import math

import numpy as np
import jax
import jax.numpy as jnp
from jax import lax
from jax.experimental import pallas as pl
from jax.experimental.pallas import tpu as pltpu
from jax.experimental.pallas import tpu_sc as plsc

f32 = jnp.float32
bf16 = jnp.bfloat16
i32 = jnp.int32
u32 = jnp.uint32

POOL_WINDOWS = (2, 4, 8, 16)
HEAD_DIM = 64
N_KV_HEADS = 4
N_BRANCH = 3
CMP_LEN = 32
CMP_STRIDE = 16
CMP_HIDDEN = 4 * HEAD_DIM
SLC_LEN = 64
N_SEL = 4
WINDOW = 256
Q_BLOCK = 128
FORCED_SCORE = 1.0e4
N_BUCKETS = 32
MAX_DISTANCE = 128
TOP_K = 2
EPS = 1e-6
NEG = -1e30
TINY = 1e-30
LOG2E = math.log2(math.e)
TB_OWN, TB_PREV, TB_WIN2, TB_NONE = range(4)

KEY_TILE = Q_BLOCK
SLC_UNROLL = 4
POOL_HALO = 16
NH_PAD = 16
V7X_VMEM_LIMIT = 56 * 1024 * 1024
V7X_SC_CORES, V7X_SC_SUBCORES = 2, 16
SC_MAX_INDEX_VECTOR = 128
SC_GATHER_CHUNK = 32

CMP_PER_Q = Q_BLOCK // CMP_STRIDE
CMP_PAD = 2 * CMP_PER_Q
CMP_NEAR = 3 * CMP_PER_Q

assert CMP_LEN == 2 * CMP_STRIDE and KEY_TILE == 2 * SLC_LEN and WINDOW == 2 * KEY_TILE
assert max(POOL_WINDOWS) <= POOL_HALO
assert 2 * KEY_TILE - (Q_BLOCK - 1) >= MAX_DISTANCE
assert (CMP_PAD + 1) * CMP_STRIDE - (CMP_LEN - 1) >= MAX_DISTANCE and CMP_PER_Q % 8 == 0


def _cparams(*sem):
    return pltpu.CompilerParams(dimension_semantics=sem, vmem_limit_bytes=V7X_VMEM_LIMIT)


def _rms(xf, g):
    ms = jnp.mean(xf * xf, axis=-1, keepdims=True)
    return (xf * lax.rsqrt(ms + EPS)) * g


def _dot(a, b):
    return jnp.dot(a, b, preferred_element_type=f32)


def _dot_hilo(a, b):
    hi = a.astype(bf16)
    lo = (a - hi.astype(f32)).astype(bf16)
    return _dot(hi, b) + _dot(lo, b)


def _head_rms(z, gvec, seg, segt):
    ssq = _dot_hilo(z * z, seg)
    inv = lax.rsqrt(ssq * (1.0 / HEAD_DIM) + EPS)
    return (z * _dot_hilo(inv, segt)) * gvec


def _silu(a):
    return a * jax.nn.sigmoid(a)


def _seg_mats(width):
    heads = width // HEAD_DIM
    seg = np.zeros((width, NH_PAD), np.float32)
    seg[np.arange(width), np.arange(width) // HEAD_DIM] = 1.0
    assert heads <= NH_PAD
    return jnp.asarray(seg, bf16), jnp.asarray(seg.T, bf16)


def _pool_kernel(x_ref, halo_ref, g_ref, w_ref, scale_ref, o_ref, *, tp, cg):
    i = pl.program_id(1)
    x = x_ref[0]
    xh = jnp.concatenate([halo_ref[0], x], axis=0)
    h = _rms(xh, g_ref[...])
    row = lax.broadcasted_iota(i32, (tp + POOL_HALO, 1), 0)
    t_abs = i * tp + row - POOL_HALO
    h = jnp.where(t_abs >= 0, h, 0.0)
    outs = []
    for gi, w in enumerate(POOL_WINDOWS):
        hg = h[:, gi * cg:(gi + 1) * cg]
        s = hg
        sh = 1
        while sh < w:
            s = s + pltpu.roll(s, sh, axis=0)
            sh *= 2
        cnt = jnp.clip(t_abs + 1, 1, w).astype(f32)
        diff = (s / cnt - hg)[POOL_HALO:]
        outs.append(_dot(diff.astype(bf16), w_ref[gi]))
    y = jnp.concatenate(outs, axis=1)
    o_ref[0] = x + y * scale_ref[...]


def _pool_layer(x3, g, w_grp, scale):
    B, S, D = x3.shape
    tp = min(512, S)
    cg = D // len(POOL_WINDOWS)
    assert S % tp == 0 and tp % POOL_HALO == 0 and all(w & (w - 1) == 0 for w in POOL_WINDOWS)
    hb = tp // POOL_HALO
    return pl.pallas_call(
        functools.partial(_pool_kernel, tp=tp, cg=cg),
        out_shape=jax.ShapeDtypeStruct((B, S, D), f32),
        grid=(B, S // tp),
        in_specs=[
            pl.BlockSpec((1, tp, D), lambda b, i: (b, i, 0)),
            pl.BlockSpec((1, POOL_HALO, D), lambda b, i: (b, jnp.maximum(i * hb - 1, 0), 0)),
            pl.BlockSpec((1, D), lambda b, i: (0, 0)),
            pl.BlockSpec((len(POOL_WINDOWS), cg, cg), lambda b, i: (0, 0, 0)),
            pl.BlockSpec((1, D), lambda b, i: (0, 0)),
        ],
        out_specs=pl.BlockSpec((1, tp, D), lambda b, i: (b, i, 0)),
        compiler_params=_cparams("parallel", "arbitrary"),
        name="pool_layer",
    )(x3, x3, g.reshape(1, D), w_grp.astype(bf16), scale.reshape(1, D))


def _ffn_kernel(x_ref, g_ref, wg_ref, wu_ref, wd_ref, o_ref, h_ref, acc_ref):
    j = pl.program_id(1)

    @pl.when(j == 0)
    def _():
        x = x_ref[...]
        h_ref[...] = _rms(x, g_ref[...]).astype(bf16)
        acc_ref[...] = x

    h = h_ref[...]
    act = _silu(_dot(h, wg_ref[...])) * _dot(h, wu_ref[...])
    acc_ref[...] += _dot(act.astype(bf16), wd_ref[...])

    @pl.when(j == pl.num_programs(1) - 1)
    def _():
        o_ref[...] = acc_ref[...]


def _ffn_layer(xf, g, wg, wu, wd):
    N, D = xf.shape
    F = wg.shape[1]
    tm = min(1024, N)
    tf = 512
    assert N % tm == 0 and F % tf == 0
    return pl.pallas_call(
        _ffn_kernel,
        out_shape=jax.ShapeDtypeStruct((N, D), f32),
        grid=(N // tm, F // tf),
        in_specs=[
            pl.BlockSpec((tm, D), lambda i, j: (i, 0)),
            pl.BlockSpec((1, D), lambda i, j: (0, 0)),
            pl.BlockSpec((D, tf), lambda i, j: (0, j)),
            pl.BlockSpec((D, tf), lambda i, j: (0, j)),
            pl.BlockSpec((tf, D), lambda i, j: (j, 0)),
        ],
        out_specs=pl.BlockSpec((tm, D), lambda i, j: (i, 0)),
        scratch_shapes=[pltpu.VMEM((tm, D), bf16), pltpu.VMEM((tm, D), f32)],
        compiler_params=_cparams("parallel", "arbitrary"),
        name="ffn_dense",
    )(xf, g.reshape(1, D), wg.astype(bf16), wu.astype(bf16), wd.astype(bf16))


def _router_kernel(x_ref, g_ref, rt_ref, hp_ref, idx_ref, gate_ref):
    h = _rms(x_ref[...], g_ref[...])
    half = h.shape[1] // 2
    bits = pltpu.bitcast(h.astype(bf16).astype(f32), u32)
    hp_ref[...] = pltpu.bitcast((bits[:, :half] & jnp.uint32(0xFFFF0000)) | (bits[:, half:] >> 16), i32)
    logits = lax.dot_general(rt_ref[...], h, (((1,), (1,)), ((), ())),
                             precision=lax.Precision.HIGHEST, preferred_element_type=f32)
    ne = logits.shape[0]
    row = lax.broadcasted_iota(i32, logits.shape, 0)
    m1 = jnp.max(logits, axis=0, keepdims=True)
    i1 = jnp.min(jnp.where(logits == m1, row, ne), axis=0, keepdims=True)
    rest = jnp.where(row == i1, -jnp.inf, logits)
    m2 = jnp.max(rest, axis=0, keepdims=True)
    i2 = jnp.min(jnp.where(rest == m2, row, ne), axis=0, keepdims=True)
    e2 = jnp.exp(m2 - m1)
    den = 1.0 + e2
    idx_ref[...] = jnp.concatenate([i1, i2], axis=0)
    gate_ref[...] = jnp.concatenate([1.0 / den, e2 / den], axis=0)


def _router(xf, g, router):
    N, D = xf.shape
    E = router.shape[1]
    tr = min(1024, N)
    assert N % tr == 0 and TOP_K == 2
    return pl.pallas_call(
        _router_kernel,
        out_shape=(jax.ShapeDtypeStruct((N, D // 2), i32),
                   jax.ShapeDtypeStruct((TOP_K, N), i32),
                   jax.ShapeDtypeStruct((TOP_K, N), f32)),
        grid=(N // tr,),
        in_specs=[
            pl.BlockSpec((tr, D), lambda i: (i, 0)),
            pl.BlockSpec((1, D), lambda i: (0, 0)),
            pl.BlockSpec((E, D), lambda i: (0, 0)),
        ],
        out_specs=(pl.BlockSpec((tr, D // 2), lambda i: (i, 0)),
                   pl.BlockSpec((TOP_K, tr), lambda i: (0, i)),
                   pl.BlockSpec((TOP_K, tr), lambda i: (0, i))),
        compiler_params=_cparams("parallel"),
        name="moe_router",
    )(xf, g.reshape(1, D), router.T)


def _sc_gather_rows(table, idx, chunk):
    B = idx.shape[0]
    D = table.shape[1]
    workers = V7X_SC_CORES * V7X_SC_SUBCORES
    per_w = B // workers
    cpw = per_w // chunk
    assert B % (8 * workers) == 0 and per_w % (2 * chunk) == 0 and cpw % 8 == 0
    assert chunk % 8 == 0 and chunk <= SC_MAX_INDEX_VECTOR
    mesh = plsc.VectorSubcoreMesh(core_axis_name="c", subcore_axis_name="s")

    @functools.partial(
        pl.kernel, mesh=mesh, out_type=jax.ShapeDtypeStruct((B, D), table.dtype),
        scratch_types=[pltpu.VMEM((cpw, chunk), i32), pltpu.VMEM((2, chunk, D), table.dtype),
                       pltpu.SemaphoreType.DMA, pltpu.SemaphoreType.DMA])
    def gather(table_hbm, idx_hbm, out_hbm, idx_v, rows_v, sem0, sem1):
        wid = lax.axis_index("s") * V7X_SC_CORES + lax.axis_index("c")
        pltpu.sync_copy(idx_hbm.at[pl.ds(pl.multiple_of(wid * cpw, 8), cpw)], idx_v)
        sems = (sem0, sem1)

        def fetch(c, b):
            return pltpu.make_async_copy(table_hbm.at[idx_v.at[c]], rows_v.at[b], sems[b])

        fetch(0, 0).start()

        @pl.loop(0, cpw, step=2)
        def _(c):
            for b in range(2):
                cur = c + b
                fetch(cur, b).wait()

                @pl.when(cur + 1 < cpw)
                def _():
                    fetch(cur + 1, 1 - b).start()

                pltpu.sync_copy(rows_v.at[b], out_hbm.at[pl.ds(pl.multiple_of(wid * per_w + cur * chunk, 8), chunk)])

    return gather(table, idx.reshape(B // chunk, chunk))


def _sc_scatter_rows(src, slots, n_slots, chunk):
    K, N = slots.shape
    D = src.shape[1]
    workers = V7X_SC_CORES * V7X_SC_SUBCORES
    per_w = N // workers
    cpw = per_w // chunk
    assert N % (8 * workers) == 0 and per_w % (2 * chunk) == 0 and cpw % 8 == 0
    assert chunk % 8 == 0 and chunk <= SC_MAX_INDEX_VECTOR
    mesh = plsc.VectorSubcoreMesh(core_axis_name="c", subcore_axis_name="s")

    @functools.partial(
        pl.kernel, mesh=mesh, out_type=jax.ShapeDtypeStruct((n_slots, D), src.dtype),
        scratch_types=[pltpu.VMEM((K, cpw, chunk), i32), pltpu.VMEM((2, chunk, D), src.dtype),
                       pltpu.SemaphoreType.DMA, pltpu.SemaphoreType.DMA, pltpu.SemaphoreType.DMA,
                       pltpu.SemaphoreType.DMA])
    def scatter(src_hbm, slots_hbm, out_hbm, slot_v, rows_v, lsem0, lsem1, ssem0, ssem1):
        wid = lax.axis_index("s") * V7X_SC_CORES + lax.axis_index("c")
        for k in range(K):
            pltpu.sync_copy(slots_hbm.at[k, pl.ds(pl.multiple_of(wid * cpw, 8), cpw)], slot_v.at[k])
        lsems, ssems = (lsem0, lsem1), (ssem0, ssem1)

        def load(c, b):
            return pltpu.make_async_copy(
                src_hbm.at[pl.ds(pl.multiple_of(wid * per_w + c * chunk, 8), chunk)], rows_v.at[b], lsems[b])

        def store(c, b, k):
            return pltpu.make_async_copy(rows_v.at[b], out_hbm.at[slot_v.at[k].at[c]], ssems[b])

        load(0, 0).start()

        @pl.loop(0, cpw, step=2)
        def _(c):
            for b in range(2):
                cur = c + b
                load(cur, b).wait()
                for k in range(K):
                    store(cur, b, k).start()

                @pl.when(cur >= 1)
                def _():
                    for k in range(K):
                        store(cur - 1, 1 - b, k).wait()

                @pl.when(cur + 1 < cpw)
                def _():
                    load(cur + 1, 1 - b).start()

        for k in range(K):
            store(cpw - 1, 1, k).wait()

    return scatter(src, slots.reshape(K, N // chunk, chunk))


def _gmm_kernel(te_ref, nu_ref, xs_ref, wg_ref, wu_ref, wd_ref, o_ref, h_ref, acc_ref):
    i = pl.program_id(0)
    j = pl.program_id(1)

    @pl.when(i < nu_ref[0])
    def _():
        @pl.when(j == 0)
        def _():
            w = pltpu.bitcast(xs_ref[...], u32)
            left = pltpu.bitcast(w & jnp.uint32(0xFFFF0000), f32)
            right = pltpu.bitcast(w << 16, f32)
            h = jnp.concatenate([left, right], axis=1)
            filled = lax.broadcasted_iota(i32, (h.shape[0], 1), 0) < nu_ref[1 + i]
            h_ref[...] = jnp.where(filled, h, 0.0).astype(bf16)
            acc_ref[...] = jnp.zeros_like(acc_ref)

        h = h_ref[...]
        act = _silu(_dot(h, wg_ref[0])) * _dot(h, wu_ref[0])
        acc_ref[...] += _dot(act.astype(bf16), wd_ref[0])

        @pl.when(j == pl.num_programs(1) - 1)
        def _():
            o_ref[...] = acc_ref[...]

    @pl.when((i >= nu_ref[0]) & (j == 0))
    def _():
        o_ref[...] = jnp.zeros_like(o_ref)


def _gmm(tile_expert, n_used, xs, wg, wu, wd, tm):
    P, half = xs.shape
    D = 2 * half
    F = wg.shape[2]
    tf = 512
    assert P % tm == 0 and F % tf == 0

    nf = F // tf

    def row_map(i, j, te, nu):
        return (jnp.minimum(i, nu[0] - 1), 0)

    def ff(i, j, nu):
        return jnp.where(i < nu[0], j, nf - 1)

    grid_spec = pltpu.PrefetchScalarGridSpec(
        num_scalar_prefetch=2,
        grid=(P // tm, nf),
        in_specs=[
            pl.BlockSpec((tm, half), row_map),
            pl.BlockSpec((1, D, tf), lambda i, j, te, nu: (te[i], 0, ff(i, j, nu))),
            pl.BlockSpec((1, D, tf), lambda i, j, te, nu: (te[i], 0, ff(i, j, nu))),
            pl.BlockSpec((1, tf, D), lambda i, j, te, nu: (te[i], ff(i, j, nu), 0)),
        ],
        out_specs=pl.BlockSpec((tm, D), lambda i, j, te, nu: (i, 0)),
        scratch_shapes=[pltpu.VMEM((tm, D), bf16), pltpu.VMEM((tm, D), f32)],
    )
    return pl.pallas_call(
        _gmm_kernel,
        out_shape=jax.ShapeDtypeStruct((P, D), f32),
        grid_spec=grid_spec,
        compiler_params=_cparams("arbitrary", "arbitrary"),
        name="moe_gmm",
    )(tile_expert, n_used, xs, wg, wu, wd)


def _combine_kernel(x_ref, gate_ref, y_ref, o_ref):
    gate = gate_ref[...]
    acc = x_ref[...]
    for k in range(TOP_K):
        acc = acc + y_ref[k] * gate[:, k:k + 1]
    o_ref[...] = acc


def _combine(xf, gate_tk, y_tok):
    N, D = xf.shape
    tt = min(1024, N)
    assert N % tt == 0
    return pl.pallas_call(
        _combine_kernel,
        out_shape=jax.ShapeDtypeStruct((N, D), f32),
        grid=(N // tt,),
        in_specs=[pl.BlockSpec((tt, D), lambda i: (i, 0)),
                  pl.BlockSpec((tt, TOP_K), lambda i: (i, 0)),
                  pl.BlockSpec((TOP_K, tt, D), lambda i: (0, i, 0))],
        out_specs=pl.BlockSpec((tt, D), lambda i: (i, 0)),
        compiler_params=_cparams("parallel"),
        name="moe_combine",
    )(xf, gate_tk, y_tok)


def _moe_layer(xf, g, router, wg, wu, wd, first_expert):
    N, D = xf.shape
    E = router.shape[1]
    tm = min(1024, N)
    hp, idx, gate = _router(xf, g, router)
    e_flat = idx.reshape(-1)
    onehot = (e_flat[:, None] == jnp.arange(E, dtype=i32)[None, :]).astype(i32)
    csum = jnp.cumsum(onehot, axis=0)
    counts = csum[-1]
    padded = ((counts + tm - 1) // tm) * tm
    ends = jnp.cumsum(padded)
    starts = ends - padded
    pos = (jnp.sum((csum + starts[None, :]) * onehot, axis=1) - 1).astype(i32)
    P = TOP_K * N + E * tm
    n_used = (ends[-1] // tm).astype(i32)
    tile_start = jnp.minimum(jnp.arange(P // tm, dtype=i32), n_used - 1) * tm
    tile_expert = jnp.sum((ends[None, :] <= tile_start[:, None]).astype(i32), axis=1)
    group_end = (starts + counts)[tile_expert]
    tile_rows = jnp.clip(group_end - tile_start, 0, tm).astype(i32)
    xs = _sc_scatter_rows(hp, pos.reshape(TOP_K, N), P, SC_GATHER_CHUNK)
    ys = _gmm(tile_expert + first_expert, jnp.concatenate([n_used.reshape(1), tile_rows]), xs, wg, wu, wd, tm)
    y_tok = _sc_gather_rows(ys, pos, SC_GATHER_CHUNK).reshape(TOP_K, N, D)
    return _combine(xf, gate.T, y_tok)


def _kv_kernel(x_ref, g_ref, w_ref, gk_ref, seg_ref, segt_ref, kc_ref, vc_ref, ks_ref, vs_ref, kw_ref, vw_ref):
    h = _rms(x_ref[...], g_ref[...]).astype(bf16)
    kv = _dot(h, w_ref[...])
    wd = kc_ref.shape[1]
    part = lambda p: kv[:, p * wd:(p + 1) * wd]
    seg, segt = seg_ref[...], segt_ref[...]
    kc_ref[...] = part(0)
    vc_ref[...] = part(1)
    G, DH = ks_ref.shape[1], ks_ref.shape[3]
    for k_ref, v_ref, pk, gain in ((ks_ref, vs_ref, 2, gk_ref[0:1, :]), (kw_ref, vw_ref, 4, gk_ref[1:2, :])):
        kn = _head_rms(part(pk), gain, seg, segt).astype(bf16)
        v = part(pk + 1)
        for g in range(G):
            k_ref[0, g] = kn[:, g * DH:(g + 1) * DH]
        for t in range(v_ref.shape[2]):
            vt = v[t * KEY_TILE:(t + 1) * KEY_TILE, :].T
            for g in range(G):
                v_ref[0, g, t] = vt[g * DH:(g + 1) * DH].astype(bf16)


def _kv_project(xf, B, S, norm_kv, w_kv, g_k):
    N, D = xf.shape
    G, DH = N_KV_HEADS, HEAD_DIM
    wd = G * DH
    tk = min(512, S)
    spb = S // tk
    tpk = tk // KEY_TILE
    assert S % tk == 0 and tk % KEY_TILE == 0 and w_kv.shape[1] == 2 * N_BRANCH * wd
    seg, segt = _seg_mats(wd)
    gk = jnp.stack([jnp.tile(g_k[1], G), jnp.tile(g_k[2], G)])
    row = lambda i: (i, 0)
    const = lambda i: (0, 0)
    k_shape = jax.ShapeDtypeStruct((B, G, S, DH), bf16)
    v_shape = jax.ShapeDtypeStruct((B, G, S // KEY_TILE, DH, KEY_TILE), bf16)
    k_spec = pl.BlockSpec((1, G, tk, DH), lambda i: (i // spb, 0, i % spb, 0))
    v_spec = pl.BlockSpec((1, G, tpk, DH, KEY_TILE), lambda i: (i // spb, 0, i % spb, 0, 0))
    return pl.pallas_call(
        _kv_kernel,
        out_shape=(jax.ShapeDtypeStruct((N, wd), f32), jax.ShapeDtypeStruct((N, wd), f32),
                   k_shape, v_shape, k_shape, v_shape),
        grid=(N // tk,),
        in_specs=[
            pl.BlockSpec((tk, D), row),
            pl.BlockSpec((1, D), const),
            pl.BlockSpec(w_kv.shape, const),
            pl.BlockSpec((2, wd), const),
            pl.BlockSpec((wd, NH_PAD), const),
            pl.BlockSpec((NH_PAD, wd), const),
        ],
        out_specs=(pl.BlockSpec((tk, wd), row), pl.BlockSpec((tk, wd), row), k_spec, v_spec, k_spec, v_spec),
        compiler_params=_cparams("parallel"),
        name="kv_project",
    )(xf, norm_kv.reshape(1, D), w_kv.astype(bf16), gk, seg, segt)


def _compress_kernel(c_ref, pos_ref, w1_ref, w2_ref, gk_ref, o_ref):
    kv = pl.program_id(0)
    c = c_ref[0, 0, 0]
    a = _dot((c + pos_ref[0, 0:1, :]).astype(bf16), w1_ref[0, 0])
    b = _dot((c + pos_ref[0, 1:2, :]).astype(bf16), w1_ref[0, 1])
    n = c.shape[0]
    hid = a + pltpu.roll(b, n - 1, axis=0)
    out = _dot(_silu(hid).astype(bf16), w2_ref[0])

    @pl.when(kv == 0)
    def _():
        o_ref[0, 0, 0] = _rms(out, gk_ref[...])

    @pl.when(kv != 0)
    def _():
        o_ref[0, 0, 0] = out


def _compress(kc, vc, B, S, cmp_pos, cmp_w1, cmp_w2, g_k0):
    G, DH = N_KV_HEADS, HEAD_DIM
    nch = S // CMP_STRIDE
    cw = CMP_STRIDE * DH

    def chunks(z):
        return z.reshape(B, nch, CMP_STRIDE, G, DH).transpose(0, 3, 1, 2, 4).reshape(B, G, nch, cw)

    c = jnp.stack([chunks(kc), chunks(vc)])
    pos = cmp_pos.reshape(2, 2, cw)
    w1 = cmp_w1.reshape(2, 2, cw, CMP_HIDDEN).astype(bf16)
    return pl.pallas_call(
        _compress_kernel,
        out_shape=jax.ShapeDtypeStruct((2, B, G, nch, DH), f32),
        grid=(2, B, G),
        in_specs=[
            pl.BlockSpec((1, 1, 1, nch, cw), lambda k, b, g: (k, b, g, 0, 0)),
            pl.BlockSpec((1, 2, cw), lambda k, b, g: (k, 0, 0)),
            pl.BlockSpec((1, 2, cw, CMP_HIDDEN), lambda k, b, g: (k, 0, 0, 0)),
            pl.BlockSpec((1, CMP_HIDDEN, DH), lambda k, b, g: (k, 0, 0)),
            pl.BlockSpec((1, DH), lambda k, b, g: (0, 0)),
        ],
        out_specs=pl.BlockSpec((1, 1, 1, nch, DH), lambda k, b, g: (k, b, g, 0, 0)),
        compiler_params=_cparams("arbitrary", "arbitrary", "arbitrary"),
        name="kv_compress",
    )(c, pos, w1, cmp_w2.astype(bf16), g_k0.reshape(1, DH))


def _build_shared(xf, B, S, norm_kv, w_kv, g_k, cmp_pos, cmp_w1, cmp_w2):
    kc, vc, ks, vst, kw, vwt = _kv_project(xf, B, S, norm_kv, w_kv, g_k)
    cmp = _compress(kc, vc, B, S, cmp_pos, cmp_w1, cmp_w2, g_k[0]).astype(bf16)
    kcm = cmp[0]
    vct = cmp[1].transpose(0, 1, 3, 2)
    return kcm, vct, ks, vst, kw, vwt


def _qg_kernel(x_ref, g_ref, wq_ref, wgate_ref, gq_ref, seg_ref, segt_ref, q_ref, gate_ref):
    h = _rms(x_ref[...], g_ref[...]).astype(bf16)
    q = _head_rms(_dot(h, wq_ref[...]), gq_ref[...], seg_ref[...], segt_ref[...])
    q = q * (HEAD_DIM ** -0.5 * LOG2E)
    gate_ref[...] = jax.nn.sigmoid(_dot(h, wgate_ref[...]))
    nt, G, DH, RT = q_ref.shape
    T = q.shape[0] // nt
    r = RT // T
    for t in range(nt):
        for g in range(G):
            qt = q[t * T:(t + 1) * T, g * r * DH:(g + 1) * r * DH].T
            q_ref[t, g] = jnp.concatenate([qt[k * DH:(k + 1) * DH] for k in range(r)], axis=1).astype(bf16)


def _qg_project(xf, g, w_qg, g_q):
    N, D = xf.shape
    HD = D
    ng = w_qg.shape[1] - HD
    tq = min(512, N)
    G, T = N_KV_HEADS, Q_BLOCK
    RT = HD // (G * HEAD_DIM) * T
    assert N % tq == 0 and tq % T == 0
    seg, segt = _seg_mats(HD)
    row = lambda i: (i, 0)
    const = lambda i: (0, 0)
    return pl.pallas_call(
        _qg_kernel,
        out_shape=(jax.ShapeDtypeStruct((N // T, G, HEAD_DIM, RT), bf16), jax.ShapeDtypeStruct((N, ng), f32)),
        grid=(N // tq,),
        in_specs=[
            pl.BlockSpec((tq, D), row),
            pl.BlockSpec((1, D), const),
            pl.BlockSpec((D, HD), const),
            pl.BlockSpec((D, ng), const),
            pl.BlockSpec((1, HD), const),
            pl.BlockSpec((HD, NH_PAD), const),
            pl.BlockSpec((NH_PAD, HD), const),
        ],
        out_specs=(pl.BlockSpec((tq // T, G, HEAD_DIM, RT), lambda i: (i, 0, 0, 0)), pl.BlockSpec((tq, ng), row)),
        compiler_params=_cparams("parallel"),
        name="qg_project",
    )(xf, g.reshape(1, D), w_qg[:, :HD].astype(bf16), w_qg[:, HD:].astype(bf16),
      jnp.tile(g_q, HD // HEAD_DIM).reshape(1, HD), seg, segt)


def _fold8(x, op):
    return op(x.reshape(x.shape[0] // 8, 8, x.shape[1]), axis=0)


def _nsa_kernel(q_ref, gt_ref, kc_ref, vct_ref, ks_ref, vst_ref, kw_ref, vwt_ref, cbn_ref, cstep_ref, tb_ref, c2st_ref,
                o_ref, cb_scr, pen_ref, penf_ref, s_scr, m8_scr, l8_scr, acc_scr, *, r, nsb, n_sel):
    qb = pl.program_id(1)
    t0 = qb * Q_BLOCK
    T = Q_BLOCK
    RT = r * T
    G = q_ref.shape[1]
    nch = kc_ref.shape[2]
    nkt = s_scr.shape[1] // KEY_TILE - SLC_UNROLL

    def tile_rows(kt):
        return pl.ds(pl.multiple_of(kt * KEY_TILE, KEY_TILE), KEY_TILE)

    def add_block_pen(s, ref, g, kt):
        return jnp.concatenate([s[:SLC_LEN] + ref[g, pl.ds(2 * kt, 1), :],
                                s[SLC_LEN:] + ref[g, pl.ds(2 * kt + 1, 1), :]], axis=0)

    for g in range(G):
        qT = q_ref[0, g]

        cb_scr[g] = cstep_ref[g, pl.ds(pl.multiple_of(nch - (qb + 1) * CMP_PER_Q, 8), CMP_PAD + nch), :]
        cb_scr[g, pl.ds(pl.multiple_of(qb * CMP_PER_Q, 8), CMP_NEAR), :] = cbn_ref[g]
        s = _dot(kc_ref[0, g], qT) + cb_scr[g, CMP_PAD:CMP_PAD + nch, :]
        p = jnp.where(s > 0.5 * NEG, jnp.exp2(s - jnp.max(s, axis=0, keepdims=True)), 0.0)
        p = p * (1.0 / jnp.maximum(jnp.sum(p, axis=0, keepdims=True), TINY))
        o_cmp = _dot(vct_ref[0, g], p.astype(bf16))

        psum = p[:, 0:T]
        for k in range(1, r):
            psum = psum + p[:, k * T:(k + 1) * T]
        hi = psum.astype(bf16)
        lo = (psum - hi.astype(f32)).astype(bf16)
        imp = _dot(c2st_ref[...], hi) + _dot(c2st_ref[...], lo)
        jb = lax.broadcasted_iota(i32, (nsb, T), 0)
        blk_q = jnp.right_shift(t0 + lax.broadcasted_iota(i32, (nsb, T), 1), SLC_LEN.bit_length() - 1)
        forced = (jb == 0) | (jb == blk_q) | (jb == blk_q - 1)
        score = jnp.where(forced, FORCED_SCORE, jnp.where(jb <= blk_q, imp, NEG))
        pen = jnp.full((nsb, T), NEG, f32)
        for _ in range(n_sel):
            mx = jnp.max(score, axis=0, keepdims=True)
            first = jnp.min(jnp.where(score == mx, jb, nsb), axis=0, keepdims=True)
            hit = jb == first
            pen = jnp.where(hit, 0.0, pen)
            score = jnp.where(hit, -jnp.inf, score)
        pen = jnp.concatenate([pen] * r, axis=1)
        pen_ref[g] = pen
        penf_ref[g] = pen + cstep_ref[g, 0:1, :]

        win = []
        for d in range(WINDOW // KEY_TILE, -1, -1):
            kt = qb - d
            ktc = jnp.maximum(kt, 0)
            tab = tb_ref[g, jnp.where(kt >= 0, d, TB_NONE)]
            win.append((_dot(kw_ref[0, g, tile_rows(ktc), :], qT) + tab, vwt_ref[0, g, ktc]))
        m8 = _fold8(win[0][0], jnp.max)
        for s_d, _ in win[1:]:
            m8 = jnp.maximum(m8, _fold8(s_d, jnp.max))
        m = jnp.max(m8, axis=0, keepdims=True)
        l8 = jnp.zeros((8, RT), f32)
        acc = jnp.zeros((HEAD_DIM, RT), f32)
        for s_d, v_d in win:
            p_d = jnp.exp2(s_d - m)
            l8 = l8 + _fold8(p_d, jnp.sum)
            acc = acc + _dot(v_d, p_d.astype(bf16))
        o_win = acc * (1.0 / jnp.maximum(jnp.sum(l8, axis=0, keepdims=True), TINY))

        gt = gt_ref[0, g]
        o_ref[0, g] = gt[0:1] * o_cmp + gt[2:3] * o_win

        ktp = jnp.maximum(qb - 1, 0)
        s_prev = add_block_pen(_dot(ks_ref[0, g, tile_rows(ktp), :], qT)
                               + tb_ref[g, jnp.where(qb >= 1, TB_PREV, TB_NONE)], pen_ref, g, ktp)
        s_scr[g, tile_rows(ktp), :] = s_prev
        s_own = add_block_pen(_dot(ks_ref[0, g, tile_rows(qb), :], qT) + tb_ref[g, TB_OWN], pen_ref, g, qb)
        s_scr[g, tile_rows(qb), :] = s_own
        m8_scr[g] = jnp.maximum(_fold8(s_prev, jnp.max), _fold8(s_own, jnp.max))
        l8_scr[g] = jnp.zeros((8, RT), f32)
        acc_scr[g] = jnp.zeros((HEAD_DIM, RT), f32)

    def pass_a(i, c):
        for g in range(G):
            qT = q_ref[0, g]
            m8 = m8_scr[g]
            for u in range(SLC_UNROLL):
                kt = SLC_UNROLL * i + u
                live = kt < qb - 1
                ktc = jnp.minimum(kt, nkt - 1)
                s = add_block_pen(_dot(ks_ref[0, g, tile_rows(ktc), :], qT) + jnp.where(live, 0.0, NEG),
                                  penf_ref, g, ktc)
                s_scr[g, tile_rows(jnp.where(live, kt, nkt + u)), :] = s
                m8 = jnp.maximum(m8, _fold8(s, jnp.max))
            m8_scr[g] = m8
        return c

    lax.fori_loop(0, (jnp.maximum(qb - 1, 0) + SLC_UNROLL - 1) // SLC_UNROLL, pass_a, 0)

    def pass_b(i, c):
        for g in range(G):
            m = jnp.max(m8_scr[g], axis=0, keepdims=True)
            l8 = l8_scr[g]
            acc = acc_scr[g]
            for u in range(SLC_UNROLL):
                kt = SLC_UNROLL * i + u
                ktc = jnp.minimum(kt, qb)
                p_u = jnp.exp2(s_scr[g, tile_rows(ktc), :] - (m + jnp.where(kt <= qb, 0.0, -NEG)))
                l8 = l8 + _fold8(p_u, jnp.sum)
                acc = acc + _dot(vst_ref[0, g, ktc], p_u.astype(bf16))
            l8_scr[g] = l8
            acc_scr[g] = acc
        return c

    lax.fori_loop(0, qb // SLC_UNROLL + 1, pass_b, 0)

    for g in range(G):
        o_slc = acc_scr[g] * (1.0 / jnp.maximum(jnp.sum(l8_scr[g], axis=0, keepdims=True), TINY))
        o_ref[0, g] = o_ref[0, g] + gt_ref[0, g][1:2] * o_slc


def _bias_tables(rel_bias, S):
    G = N_KV_HEADS
    H = rel_bias.shape[1]
    r = H // G
    T = KEY_TILE
    n = jnp.arange(S + 2 * T, dtype=i32)
    max_exact = N_BUCKETS // 2
    nf = jnp.maximum(n, 1).astype(f32)
    large = max_exact + (jnp.log(nf / max_exact) / math.log(MAX_DISTANCE / max_exact)
                         * (N_BUCKETS - max_exact)).astype(i32)
    bucket = jnp.where(n < max_exact, n, jnp.minimum(large, N_BUCKETS - 1))
    bias1d = rel_bias.astype(f32)[bucket] * LOG2E

    def per_group(tab):
        lead = tab.shape[:-2]
        k = len(lead)
        t = jnp.moveaxis(tab, -1, 0).reshape((G, r) + lead + (T,))
        return jnp.moveaxis(t, 1, k + 1).reshape((G,) + lead + (r * T,))

    def masked(dist, ok):
        return jnp.where(jnp.asarray(ok)[..., None], bias1d[np.maximum(dist, 0)], NEG)

    kj, qi = np.arange(T)[:, None], np.arange(T)[None, :]
    dist = np.stack([d * T + qi - kj for d in range(3)])
    ok = np.stack([dist[0] >= 0, np.ones((T, T), bool), dist[2] < WINDOW])
    tb = jnp.concatenate([masked(dist, ok), jnp.full((1, T, T, H), NEG, f32)])
    cend = (np.arange(CMP_NEAR)[:, None] - CMP_PAD) * CMP_STRIDE + CMP_LEN - 1
    dcn = np.arange(T)[None, :] - cend
    rows = CMP_PAD + S // CMP_STRIDE
    far = jnp.broadcast_to(bias1d[-1][None, None, :], (rows, T, H))
    cstep = jnp.concatenate([far, jnp.full((rows, T, H), NEG, f32)])
    return per_group(tb), per_group(masked(dcn, dcn >= 0)), per_group(cstep)


def _nsa_attention(q, gates_t, shared, tables, B, S):
    kcm, vct, ks, vst, kw, vwt = shared
    tb, cbn, cstep = tables
    G, DH, T = N_KV_HEADS, HEAD_DIM, Q_BLOCK
    RT = q.shape[3]
    r = RT // T
    nqb = S // T
    nch = S // CMP_STRIDE
    nsb = S // SLC_LEN
    nkt = S // KEY_TILE
    cmp_start = np.arange(nch) * CMP_STRIDE
    slc_start = np.arange(nsb) * SLC_LEN
    overlap = np.clip(np.minimum(cmp_start[:, None] + CMP_LEN, slc_start[None, :] + SLC_LEN)
                      - np.maximum(cmp_start[:, None], slc_start[None, :]), 0, None) / CMP_LEN
    overlap[nch - 1] = 0.0
    c2st = jnp.asarray(overlap.T, bf16)
    qmap = lambda b, i: (b * nqb + i, 0, 0, 0)
    bat = lambda b, i: (b, 0, 0, 0)
    bat5 = lambda b, i: (b, 0, 0, 0, 0)
    once = pl.Buffered(1)
    return pl.pallas_call(
        functools.partial(_nsa_kernel, r=r, nsb=nsb, n_sel=min(N_SEL, nsb)),
        out_shape=jax.ShapeDtypeStruct(q.shape, f32),
        grid=(B, nqb),
        in_specs=[
            pl.BlockSpec((1, G, DH, RT), qmap),
            pl.BlockSpec((1, G, N_BRANCH, RT), qmap),
            pl.BlockSpec((1, G, nch, DH), bat),
            pl.BlockSpec((1, G, DH, nch), bat),
            pl.BlockSpec((1, G, S, DH), bat),
            pl.BlockSpec((1, G, nkt, DH, KEY_TILE), bat5),
            pl.BlockSpec((1, G, S, DH), bat),
            pl.BlockSpec((1, G, nkt, DH, KEY_TILE), bat5),
            pl.BlockSpec(cbn.shape, lambda b, i: (0, 0, 0), pipeline_mode=once),
            pl.BlockSpec(cstep.shape, lambda b, i: (0, 0, 0), pipeline_mode=once),
            pl.BlockSpec(tb.shape, lambda b, i: (0, 0, 0, 0), pipeline_mode=once),
            pl.BlockSpec((nsb, nch), lambda b, i: (0, 0), pipeline_mode=once),
        ],
        out_specs=pl.BlockSpec((1, G, DH, RT), qmap),
        scratch_shapes=[pltpu.VMEM((G, CMP_PAD + nch, RT), f32), pltpu.VMEM((G, nsb, RT), f32),
                        pltpu.VMEM((G, nsb, RT), f32), pltpu.VMEM((G, (nkt + SLC_UNROLL) * KEY_TILE, RT), f32),
                        pltpu.VMEM((G, 8, RT), f32), pltpu.VMEM((G, 8, RT), f32), pltpu.VMEM((G, DH, RT), f32)],
        compiler_params=_cparams("parallel", "arbitrary"),
        name="nsa_attention",
    )(q, gates_t, kcm, vct, ks, vst, kw, vwt, cbn, cstep, tb, c2st)


def _oproj_kernel(x_ref, a_ref, w_ref, o_ref):
    nt, G, DH, RT = a_ref.shape
    T = x_ref.shape[0] // nt
    r = RT // T
    rows = []
    for t in range(nt):
        cols = []
        for g in range(G):
            a = a_ref[t, g]
            cols.append(jnp.concatenate([a[:, k * T:(k + 1) * T] for k in range(r)], axis=0).T)
        rows.append(jnp.concatenate(cols, axis=1))
    attn = jnp.concatenate(rows, axis=0)
    o_ref[...] = x_ref[...] + _dot(attn.astype(bf16), w_ref[...])


def _out_project(xf, attn, w_o):
    N, D = xf.shape
    nt_all, G, DH, RT = attn.shape
    T = N // nt_all
    to = min(512, N)
    assert N % to == 0 and to % T == 0
    row = lambda i: (i, 0)
    return pl.pallas_call(
        _oproj_kernel,
        out_shape=jax.ShapeDtypeStruct((N, D), f32),
        grid=(N // to,),
        in_specs=[pl.BlockSpec((to, D), row), pl.BlockSpec((to // T, G, DH, RT), lambda i: (i, 0, 0, 0)),
                  pl.BlockSpec(w_o.shape, lambda i: (0, 0))],
        out_specs=pl.BlockSpec((to, D), row),
        compiler_params=_cparams("parallel"),
        name="out_project",
    )(xf, attn, w_o.astype(bf16))


def _nsa_layer(xf, B, S, g, w_qg, g_q, w_o, shared, tables):
    N = xf.shape[0]
    G, T = N_KV_HEADS, Q_BLOCK
    q, gates = _qg_project(xf, g, w_qg, g_q)
    r = gates.shape[1] // (G * N_BRANCH)
    gates_t = gates.reshape(N // T, T, G, r, N_BRANCH).transpose(0, 2, 4, 3, 1).reshape(N // T, G, N_BRANCH, r * T)
    attn = _nsa_attention(q, gates_t, shared, tables, B, S)
    return _out_project(xf, attn, w_o)


def kernel(x, rel_bias, norm_mix, norm_ffn, pool_w, pool_scale, norm_kv, w_kv, g_k, cmp_pos, cmp_w1, cmp_w2,
           w_qg, g_q, w_o, ffn_wg, ffn_wu, ffn_wd, router, moe_wg, moe_wu, moe_wd):
    B, S, D = x.shape
    depth = norm_mix.shape[0]
    n_a = depth // 2
    assert S % Q_BLOCK == 0
    xf = x.reshape(B * S, D)
    shared = None
    tables = _bias_tables(rel_bias, S)
    n_exp = moe_wg.shape[1]
    moe_w = [w.reshape((-1,) + w.shape[2:]).astype(bf16) for w in (moe_wg, moe_wu, moe_wd)]
    for layer in range(depth):
        if layer < n_a:
            xf = _pool_layer(xf.reshape(B, S, D), norm_mix[layer], pool_w[layer], pool_scale[layer]).reshape(B * S, D)
        else:
            j = layer - n_a
            xf = _nsa_layer(xf, B, S, norm_mix[layer], w_qg[j], g_q[j], w_o[j], shared, tables)
        i = layer // 2
        if layer % 2 == 0:
            xf = _ffn_layer(xf, norm_ffn[layer], ffn_wg[i], ffn_wu[i], ffn_wd[i])
        else:
            xf = _moe_layer(xf, norm_ffn[layer], router[i], *moe_w, first_expert=i * n_exp)
        if layer == n_a - 1:
            shared = _build_shared(xf, B, S, norm_kv, w_kv, g_k, cmp_pos, cmp_w1, cmp_w2)
    return xf.reshape(B, S, D)
```

```python
import functools
import math

import numpy as np
import jax
import jax.numpy as jnp
from jax import lax
from jax.experimental import pallas as pl
from jax.experimental.pallas import tpu as pltpu
from jax.experimental.pallas import tpu_sc as plsc

f32 = jnp.float32
bf16 = jnp.bfloat16
i32 = jnp.int32
u32 = jnp.uint32

POOL_WINDOWS = (2, 4, 8, 16)
HEAD_DIM = 64
N_KV_HEADS = 4
N_BRANCH = 3
CMP_LEN = 32
CMP_STRIDE = 16
CMP_HIDDEN = 4 * HEAD_DIM
SLC_LEN = 64
N_SEL = 4
WINDOW = 256
Q_BLOCK = 128
FORCED_SCORE = 1.0e4
N_BUCKETS = 32
MAX_DISTANCE = 128
TOP_K = 2
EPS = 1e-6
NEG = -1e30
TINY = 1e-30
LOG2E = math.log2(math.e)
TB_OWN, TB_PREV, TB_WIN2, TB_NONE = range(4)

KEY_TILE = Q_BLOCK
SLC_UNROLL = 4
POOL_HALO = 16
NH_PAD = 16
V7X_VMEM_LIMIT = 56 * 1024 * 1024
V7X_SC_CORES, V7X_SC_SUBCORES = 2, 16
SC_MAX_INDEX_VECTOR = 128
SC_GATHER_CHUNK = 32

CMP_PER_Q = Q_BLOCK // CMP_STRIDE
CMP_PAD = 2 * CMP_PER_Q
CMP_NEAR = 3 * CMP_PER_Q

assert CMP_LEN == 2 * CMP_STRIDE and KEY_TILE == 2 * SLC_LEN and WINDOW == 2 * KEY_TILE
assert max(POOL_WINDOWS) <= POOL_HALO
assert 2 * KEY_TILE - (Q_BLOCK - 1) >= MAX_DISTANCE
assert (CMP_PAD + 1) * CMP_STRIDE - (CMP_LEN - 1) >= MAX_DISTANCE and CMP_PER_Q % 8 == 0


def _cparams(*sem):
    return pltpu.CompilerParams(dimension_semantics=sem, vmem_limit_bytes=V7X_VMEM_LIMIT)


def _rms(xf, g):
    ms = jnp.mean(xf * xf, axis=-1, keepdims=True)
    return (xf * lax.rsqrt(ms + EPS)) * g


def _dot(a, b):
    return jnp.dot(a, b, preferred_element_type=f32)


def _dot_hilo(a, b):
    hi = a.astype(bf16)
    lo = (a - hi.astype(f32)).astype(bf16)
    return _dot(hi, b) + _dot(lo, b)


def _head_rms(z, gvec, seg, segt):
    ssq = _dot_hilo(z * z, seg)
    inv = lax.rsqrt(ssq * (1.0 / HEAD_DIM) + EPS)
    return (z * _dot_hilo(inv, segt)) * gvec


def _silu(a):
    return a * jax.nn.sigmoid(a)


def _seg_mats(width):
    heads = width // HEAD_DIM
    seg = np.zeros((width, NH_PAD), np.float32)
    seg[np.arange(width), np.arange(width) // HEAD_DIM] = 1.0
    assert heads <= NH_PAD
    return jnp.asarray(seg, bf16), jnp.asarray(seg.T, bf16)


def _pool_kernel(x_ref, halo_ref, g_ref, w_ref, scale_ref, o_ref, *, tp, cg):
    i = pl.program_id(1)
    x = x_ref[0]
    xh = jnp.concatenate([halo_ref[0], x], axis=0)
    h = _rms(xh, g_ref[...])
    row = lax.broadcasted_iota(i32, (tp + POOL_HALO, 1), 0)
    t_abs = i * tp + row - POOL_HALO
    h = jnp.where(t_abs >= 0, h, 0.0)
    outs = []
    for gi, w in enumerate(POOL_WINDOWS):
        hg = h[:, gi * cg:(gi + 1) * cg]
        s = hg
        sh = 1
        while sh < w:
            s = s + pltpu.roll(s, sh, axis=0)
            sh *= 2
        cnt = jnp.clip(t_abs + 1, 1, w).astype(f32)
        diff = (s / cnt - hg)[POOL_HALO:]
        outs.append(_dot(diff.astype(bf16), w_ref[gi]))
    y = jnp.concatenate(outs, axis=1)
    o_ref[0] = x + y * scale_ref[...]


def _pool_layer(x3, g, w_grp, scale):
    B, S, D = x3.shape
    tp = min(512, S)
    cg = D // len(POOL_WINDOWS)
    assert S % tp == 0 and tp % POOL_HALO == 0 and all(w & (w - 1) == 0 for w in POOL_WINDOWS)
    hb = tp // POOL_HALO
    return pl.pallas_call(
        functools.partial(_pool_kernel, tp=tp, cg=cg),
        out_shape=jax.ShapeDtypeStruct((B, S, D), f32),
        grid=(B, S // tp),
        in_specs=[
            pl.BlockSpec((1, tp, D), lambda b, i: (b, i, 0)),
            pl.BlockSpec((1, POOL_HALO, D), lambda b, i: (b, jnp.maximum(i * hb - 1, 0), 0)),
            pl.BlockSpec((1, D), lambda b, i: (0, 0)),
            pl.BlockSpec((len(POOL_WINDOWS), cg, cg), lambda b, i: (0, 0, 0)),
            pl.BlockSpec((1, D), lambda b, i: (0, 0)),
        ],
        out_specs=pl.BlockSpec((1, tp, D), lambda b, i: (b, i, 0)),
        compiler_params=_cparams("parallel", "arbitrary"),
        name="pool_layer",
    )(x3, x3, g.reshape(1, D), w_grp.astype(bf16), scale.reshape(1, D))


def _ffn_kernel(x_ref, g_ref, wg_ref, wu_ref, wd_ref, o_ref, h_ref, acc_ref):
    j = pl.program_id(1)

    @pl.when(j == 0)
    def _():
        x = x_ref[...]
        h_ref[...] = _rms(x, g_ref[...]).astype(bf16)
        acc_ref[...] = x

    h = h_ref[...]
    act = _silu(_dot(h, wg_ref[...])) * _dot(h, wu_ref[...])
    acc_ref[...] += _dot(act.astype(bf16), wd_ref[...])

    @pl.when(j == pl.num_programs(1) - 1)
    def _():
        o_ref[...] = acc_ref[...]


def _ffn_layer(xf, g, wg, wu, wd):
    N, D = xf.shape
    F = wg.shape[1]
    tm = min(1024, N)
    tf = 512
    assert N % tm == 0 and F % tf == 0
    return pl.pallas_call(
        _ffn_kernel,
        out_shape=jax.ShapeDtypeStruct((N, D), f32),
        grid=(N // tm, F // tf),
        in_specs=[
            pl.BlockSpec((tm, D), lambda i, j: (i, 0)),
            pl.BlockSpec((1, D), lambda i, j: (0, 0)),
            pl.BlockSpec((D, tf), lambda i, j: (0, j)),
            pl.BlockSpec((D, tf), lambda i, j: (0, j)),
            pl.BlockSpec((tf, D), lambda i, j: (j, 0)),
        ],
        out_specs=pl.BlockSpec((tm, D), lambda i, j: (i, 0)),
        scratch_shapes=[pltpu.VMEM((tm, D), bf16), pltpu.VMEM((tm, D), f32)],
        compiler_params=_cparams("parallel", "arbitrary"),
        name="ffn_dense",
    )(xf, g.reshape(1, D), wg.astype(bf16), wu.astype(bf16), wd.astype(bf16))


def _router_kernel(x_ref, g_ref, rt_ref, hp_ref, idx_ref, gate_ref):
    h = _rms(x_ref[...], g_ref[...])
    half = h.shape[1] // 2
    bits = pltpu.bitcast(h.astype(bf16).astype(f32), u32)
    hp_ref[...] = pltpu.bitcast((bits[:, :half] & jnp.uint32(0xFFFF0000)) | (bits[:, half:] >> 16), i32)
    logits = lax.dot_general(rt_ref[...], h, (((1,), (1,)), ((), ())),
                             precision=lax.Precision.HIGHEST, preferred_element_type=f32)
    ne = logits.shape[0]
    row = lax.broadcasted_iota(i32, logits.shape, 0)
    m1 = jnp.max(logits, axis=0, keepdims=True)
    i1 = jnp.min(jnp.where(logits == m1, row, ne), axis=0, keepdims=True)
    rest = jnp.where(row == i1, -jnp.inf, logits)
    m2 = jnp.max(rest, axis=0, keepdims=True)
    i2 = jnp.min(jnp.where(rest == m2, row, ne), axis=0, keepdims=True)
    e2 = jnp.exp(m2 - m1)
    den = 1.0 + e2
    idx_ref[...] = jnp.concatenate([i1, i2], axis=0)
    gate_ref[...] = jnp.concatenate([1.0 / den, e2 / den], axis=0)


def _router(xf, g, router):
    N, D = xf.shape
    E = router.shape[1]
    tr = min(1024, N)
    assert N % tr == 0 and TOP_K == 2
    return pl.pallas_call(
        _router_kernel,
        out_shape=(jax.ShapeDtypeStruct((N, D // 2), i32),
                   jax.ShapeDtypeStruct((TOP_K, N), i32),
                   jax.ShapeDtypeStruct((TOP_K, N), f32)),
        grid=(N // tr,),
        in_specs=[
            pl.BlockSpec((tr, D), lambda i: (i, 0)),
            pl.BlockSpec((1, D), lambda i: (0, 0)),
            pl.BlockSpec((E, D), lambda i: (0, 0)),
        ],
        out_specs=(pl.BlockSpec((tr, D // 2), lambda i: (i, 0)),
                   pl.BlockSpec((TOP_K, tr), lambda i: (0, i)),
                   pl.BlockSpec((TOP_K, tr), lambda i: (0, i))),
        compiler_params=_cparams("parallel"),
        name="moe_router",
    )(xf, g.reshape(1, D), router.T)


def _sc_gather_rows(table, idx, chunk):
    B = idx.shape[0]
    D = table.shape[1]
    workers = V7X_SC_CORES * V7X_SC_SUBCORES
    per_w = B // workers
    cpw = per_w // chunk
    assert B % (8 * workers) == 0 and per_w % (2 * chunk) == 0 and cpw % 8 == 0
    assert chunk % 8 == 0 and chunk <= SC_MAX_INDEX_VECTOR
    mesh = plsc.VectorSubcoreMesh(core_axis_name="c", subcore_axis_name="s")

    @functools.partial(
        pl.kernel, mesh=mesh, out_type=jax.ShapeDtypeStruct((B, D), table.dtype),
        scratch_types=[pltpu.VMEM((cpw, chunk), i32), pltpu.VMEM((2, chunk, D), table.dtype),
                       pltpu.SemaphoreType.DMA, pltpu.SemaphoreType.DMA])
    def gather(table_hbm, idx_hbm, out_hbm, idx_v, rows_v, sem0, sem1):
        wid = lax.axis_index("s") * V7X_SC_CORES + lax.axis_index("c")
        pltpu.sync_copy(idx_hbm.at[pl.ds(pl.multiple_of(wid * cpw, 8), cpw)], idx_v)
        sems = (sem0, sem1)

        def fetch(c, b):
            return pltpu.make_async_copy(table_hbm.at[idx_v.at[c]], rows_v.at[b], sems[b])

        fetch(0, 0).start()

        @pl.loop(0, cpw, step=2)
        def _(c):
            for b in range(2):
                cur = c + b
                fetch(cur, b).wait()

                @pl.when(cur + 1 < cpw)
                def _():
                    fetch(cur + 1, 1 - b).start()

                pltpu.sync_copy(rows_v.at[b], out_hbm.at[pl.ds(pl.multiple_of(wid * per_w + cur * chunk, 8), chunk)])

    return gather(table, idx.reshape(B // chunk, chunk))


def _sc_scatter_rows(src, slots, n_slots, chunk):
    K, N = slots.shape
    D = src.shape[1]
    workers = V7X_SC_CORES * V7X_SC_SUBCORES
    per_w = N // workers
    cpw = per_w // chunk
    assert N % (8 * workers) == 0 and per_w % (2 * chunk) == 0 and cpw % 8 == 0
    assert chunk % 8 == 0 and chunk <= SC_MAX_INDEX_VECTOR
    mesh = plsc.VectorSubcoreMesh(core_axis_name="c", subcore_axis_name="s")

    @functools.partial(
        pl.kernel, mesh=mesh, out_type=jax.ShapeDtypeStruct((n_slots, D), src.dtype),
        scratch_types=[pltpu.VMEM((K, cpw, chunk), i32), pltpu.VMEM((2, chunk, D), src.dtype),
                       pltpu.SemaphoreType.DMA, pltpu.SemaphoreType.DMA, pltpu.SemaphoreType.DMA,
                       pltpu.SemaphoreType.DMA])
    def scatter(src_hbm, slots_hbm, out_hbm, slot_v, rows_v, lsem0, lsem1, ssem0, ssem1):
        wid = lax.axis_index("s") * V7X_SC_CORES + lax.axis_index("c")
        for k in range(K):
            pltpu.sync_copy(slots_hbm.at[k, pl.ds(pl.multiple_of(wid * cpw, 8), cpw)], slot_v.at[k])
        lsems, ssems = (lsem0, lsem1), (ssem0, ssem1)

        def load(c, b):
            return pltpu.make_async_copy(
                src_hbm.at[pl.ds(pl.multiple_of(wid * per_w + c * chunk, 8), chunk)], rows_v.at[b], lsems[b])

        def store(c, b, k):
            return pltpu.make_async_copy(rows_v.at[b], out_hbm.at[slot_v.at[k].at[c]], ssems[b])

        load(0, 0).start()

        @pl.loop(0, cpw, step=2)
        def _(c):
            for b in range(2):
                cur = c + b
                load(cur, b).wait()
                for k in range(K):
                    store(cur, b, k).start()

                @pl.when(cur >= 1)
                def _():
                    for k in range(K):
                        store(cur - 1, 1 - b, k).wait()

                @pl.when(cur + 1 < cpw)
                def _():
                    load(cur + 1, 1 - b).start()

        for k in range(K):
            store(cpw - 1, 1, k).wait()

    return scatter(src, slots.reshape(K, N // chunk, chunk))


def _gmm_kernel(te_ref, nu_ref, xs_ref, wg_ref, wu_ref, wd_ref, o_ref, h_ref, acc_ref):
    i = pl.program_id(0)
    j = pl.program_id(1)

    @pl.when(i < nu_ref[0])
    def _():
        @pl.when(j == 0)
        def _():
            w = pltpu.bitcast(xs_ref[...], u32)
            left = pltpu.bitcast(w & jnp.uint32(0xFFFF0000), f32)
            right = pltpu.bitcast(w << 16, f32)
            h = jnp.concatenate([left, right], axis=1)
            filled = lax.broadcasted_iota(i32, (h.shape[0], 1), 0) < nu_ref[1 + i]
            h_ref[...] = jnp.where(filled, h, 0.0).astype(bf16)
            acc_ref[...] = jnp.zeros_like(acc_ref)

        h = h_ref[...]
        act = _silu(_dot(h, wg_ref[0])) * _dot(h, wu_ref[0])
        acc_ref[...] += _dot(act.astype(bf16), wd_ref[0])

        @pl.when(j == pl.num_programs(1) - 1)
        def _():
            o_ref[...] = acc_ref[...]

    @pl.when((i >= nu_ref[0]) & (j == 0))
    def _():
        o_ref[...] = jnp.zeros_like(o_ref)


def _gmm(tile_expert, n_used, xs, wg, wu, wd, tm):
    P, half = xs.shape
    D = 2 * half
    F = wg.shape[2]
    tf = 512
    assert P % tm == 0 and F % tf == 0

    nf = F // tf

    def row_map(i, j, te, nu):
        return (jnp.minimum(i, nu[0] - 1), 0)

    def ff(i, j, nu):
        return jnp.where(i < nu[0], j, nf - 1)

    grid_spec = pltpu.PrefetchScalarGridSpec(
        num_scalar_prefetch=2,
        grid=(P // tm, nf),
        in_specs=[
            pl.BlockSpec((tm, half), row_map),
            pl.BlockSpec((1, D, tf), lambda i, j, te, nu: (te[i], 0, ff(i, j, nu))),
            pl.BlockSpec((1, D, tf), lambda i, j, te, nu: (te[i], 0, ff(i, j, nu))),
            pl.BlockSpec((1, tf, D), lambda i, j, te, nu: (te[i], ff(i, j, nu), 0)),
        ],
        out_specs=pl.BlockSpec((tm, D), lambda i, j, te, nu: (i, 0)),
        scratch_shapes=[pltpu.VMEM((tm, D), bf16), pltpu.VMEM((tm, D), f32)],
    )
    return pl.pallas_call(
        _gmm_kernel,
        out_shape=jax.ShapeDtypeStruct((P, D), f32),
        grid_spec=grid_spec,
        compiler_params=_cparams("arbitrary", "arbitrary"),
        name="moe_gmm",
    )(tile_expert, n_used, xs, wg, wu, wd)


def _combine_kernel(x_ref, gate_ref, y_ref, o_ref):
    gate = gate_ref[...]
    acc = x_ref[...]
    for k in range(TOP_K):
        acc = acc + y_ref[k] * gate[:, k:k + 1]
    o_ref[...] = acc


def _combine(xf, gate_tk, y_tok):
    N, D = xf.shape
    tt = min(1024, N)
    assert N % tt == 0
    return pl.pallas_call(
        _combine_kernel,
        out_shape=jax.ShapeDtypeStruct((N, D), f32),
        grid=(N // tt,),
        in_specs=[pl.BlockSpec((tt, D), lambda i: (i, 0)),
                  pl.BlockSpec((tt, TOP_K), lambda i: (i, 0)),
                  pl.BlockSpec((TOP_K, tt, D), lambda i: (0, i, 0))],
        out_specs=pl.BlockSpec((tt, D), lambda i: (i, 0)),
        compiler_params=_cparams("parallel"),
        name="moe_combine",
    )(xf, gate_tk, y_tok)


def _moe_layer(xf, g, router, wg, wu, wd, first_expert):
    N, D = xf.shape
    E = router.shape[1]
    tm = min(1024, N)
    hp, idx, gate = _router(xf, g, router)
    e_flat = idx.reshape(-1)
    onehot = (e_flat[:, None] == jnp.arange(E, dtype=i32)[None, :]).astype(i32)
    csum = jnp.cumsum(onehot, axis=0)
    counts = csum[-1]
    padded = ((counts + tm - 1) // tm) * tm
    ends = jnp.cumsum(padded)
    starts = ends - padded
    pos = (jnp.sum((csum + starts[None, :]) * onehot, axis=1) - 1).astype(i32)
    P = TOP_K * N + E * tm
    n_used = (ends[-1] // tm).astype(i32)
    tile_start = jnp.minimum(jnp.arange(P // tm, dtype=i32), n_used - 1) * tm
    tile_expert = jnp.sum((ends[None, :] <= tile_start[:, None]).astype(i32), axis=1)
    group_end = (starts + counts)[tile_expert]
    tile_rows = jnp.clip(group_end - tile_start, 0, tm).astype(i32)
    xs = _sc_scatter_rows(hp, pos.reshape(TOP_K, N), P, SC_GATHER_CHUNK)
    ys = _gmm(tile_expert + first_expert, jnp.concatenate([n_used.reshape(1), tile_rows]), xs, wg, wu, wd, tm)
    y_tok = _sc_gather_rows(ys, pos, SC_GATHER_CHUNK).reshape(TOP_K, N, D)
    return _combine(xf, gate.T, y_tok)


def _kv_kernel(x_ref, g_ref, w_ref, gk_ref, seg_ref, segt_ref, kc_ref, vc_ref, ks_ref, vs_ref, kw_ref, vw_ref):
    h = _rms(x_ref[...], g_ref[...]).astype(bf16)
    kv = _dot(h, w_ref[...])
    G, DH = ks_ref.shape[1], ks_ref.shape[3]
    wd = G * DH
    part = lambda p: kv[:, p * wd:(p + 1) * wd]
    seg, segt = seg_ref[...], segt_ref[...]
    for c_ref, pc in ((kc_ref, 0), (vc_ref, 1)):
        z = part(pc)
        for g in range(G):
            c_ref[0, g] = z[:, g * DH:(g + 1) * DH]
    for k_ref, v_ref, pk, gain in ((ks_ref, vs_ref, 2, gk_ref[0:1, :]), (kw_ref, vw_ref, 4, gk_ref[1:2, :])):
        kn = _head_rms(part(pk), gain, seg, segt).astype(bf16)
        v = part(pk + 1)
        for g in range(G):
            k_ref[0, g] = kn[:, g * DH:(g + 1) * DH]
        for t in range(v_ref.shape[2]):
            vt = v[t * KEY_TILE:(t + 1) * KEY_TILE, :].T
            for g in range(G):
                v_ref[0, g, t] = vt[g * DH:(g + 1) * DH].astype(bf16)


def _kv_project(xf, B, S, norm_kv, w_kv, g_k):
    N, D = xf.shape
    G, DH = N_KV_HEADS, HEAD_DIM
    wd = G * DH
    tk = min(512, S)
    spb = S // tk
    tpk = tk // KEY_TILE
    assert S % tk == 0 and tk % KEY_TILE == 0 and w_kv.shape[1] == 2 * N_BRANCH * wd
    seg, segt = _seg_mats(wd)
    gk = jnp.stack([jnp.tile(g_k[1], G), jnp.tile(g_k[2], G)])
    row = lambda i: (i, 0)
    const = lambda i: (0, 0)
    k_shape = jax.ShapeDtypeStruct((B, G, S, DH), bf16)
    v_shape = jax.ShapeDtypeStruct((B, G, S // KEY_TILE, DH, KEY_TILE), bf16)
    k_spec = pl.BlockSpec((1, G, tk, DH), lambda i: (i // spb, 0, i % spb, 0))
    v_spec = pl.BlockSpec((1, G, tpk, DH, KEY_TILE), lambda i: (i // spb, 0, i % spb, 0, 0))
    return pl.pallas_call(
        _kv_kernel,
        out_shape=(jax.ShapeDtypeStruct((B, G, S, DH), f32), jax.ShapeDtypeStruct((B, G, S, DH), f32),
                   k_shape, v_shape, k_shape, v_shape),
        grid=(N // tk,),
        in_specs=[
            pl.BlockSpec((tk, D), row),
            pl.BlockSpec((1, D), const),
            pl.BlockSpec(w_kv.shape, const),
            pl.BlockSpec((2, wd), const),
            pl.BlockSpec((wd, NH_PAD), const),
            pl.BlockSpec((NH_PAD, wd), const),
        ],
        out_specs=(k_spec, k_spec, k_spec, v_spec, k_spec, v_spec),
        compiler_params=_cparams("parallel"),
        name="kv_project",
    )(xf, norm_kv.reshape(1, D), w_kv.astype(bf16), gk, seg, segt)


def _compress_kernel(c_ref, pos_ref, w1_ref, w2_ref, gk_ref, o_ref):
    kv = pl.program_id(0)
    c = c_ref[0, 0, 0]
    a = _dot((c + pos_ref[0, 0:1, :]).astype(bf16), w1_ref[0, 0])
    b = _dot((c + pos_ref[0, 1:2, :]).astype(bf16), w1_ref[0, 1])
    n = c.shape[0]
    hid = a + pltpu.roll(b, n - 1, axis=0)
    out = _dot(_silu(hid).astype(bf16), w2_ref[0])

    @pl.when(kv == 0)
    def _():
        o_ref[0, 0, 0] = _rms(out, gk_ref[...])

    @pl.when(kv != 0)
    def _():
        o_ref[0, 0, 0] = out


def _compress(kc, vc, B, S, cmp_pos, cmp_w1, cmp_w2, g_k0):
    G, DH = N_KV_HEADS, HEAD_DIM
    nch = S // CMP_STRIDE
    cw = CMP_STRIDE * DH

    c = jnp.stack([kc.reshape(B, G, nch, cw), vc.reshape(B, G, nch, cw)])
    pos = cmp_pos.reshape(2, 2, cw)
    w1 = cmp_w1.reshape(2, 2, cw, CMP_HIDDEN).astype(bf16)
    return pl.pallas_call(
        _compress_kernel,
        out_shape=jax.ShapeDtypeStruct((2, B, G, nch, DH), f32),
        grid=(2, B, G),
        in_specs=[
            pl.BlockSpec((1, 1, 1, nch, cw), lambda k, b, g: (k, b, g, 0, 0)),
            pl.BlockSpec((1, 2, cw), lambda k, b, g: (k, 0, 0)),
            pl.BlockSpec((1, 2, cw, CMP_HIDDEN), lambda k, b, g: (k, 0, 0, 0)),
            pl.BlockSpec((1, CMP_HIDDEN, DH), lambda k, b, g: (k, 0, 0)),
            pl.BlockSpec((1, DH), lambda k, b, g: (0, 0)),
        ],
        out_specs=pl.BlockSpec((1, 1, 1, nch, DH), lambda k, b, g: (k, b, g, 0, 0)),
        compiler_params=_cparams("arbitrary", "arbitrary", "arbitrary"),
        name="kv_compress",
    )(c, pos, w1, cmp_w2.astype(bf16), g_k0.reshape(1, DH))


def _build_shared(xf, B, S, norm_kv, w_kv, g_k, cmp_pos, cmp_w1, cmp_w2):
    kc, vc, ks, vst, kw, vwt = _kv_project(xf, B, S, norm_kv, w_kv, g_k)
    cmp = _compress(kc, vc, B, S, cmp_pos, cmp_w1, cmp_w2, g_k[0]).astype(bf16)
    kcm = cmp[0]
    vct = cmp[1].transpose(0, 1, 3, 2)
    return kcm, vct, ks, vst, kw, vwt


def _qg_kernel(x_ref, g_ref, wq_ref, wgate_ref, gq_ref, seg_ref, segt_ref, q_ref, gate_ref):
    h = _rms(x_ref[...], g_ref[...]).astype(bf16)
    q = _head_rms(_dot(h, wq_ref[...]), gq_ref[...], seg_ref[...], segt_ref[...])
    q = q * (HEAD_DIM ** -0.5 * LOG2E)
    gate_ref[...] = jax.nn.sigmoid(_dot(h, wgate_ref[...]))
    nt, G, DH, RT = q_ref.shape
    T = q.shape[0] // nt
    r = RT // T
    for t in range(nt):
        for g in range(G):
            qt = q[t * T:(t + 1) * T, g * r * DH:(g + 1) * r * DH].T
            q_ref[t, g] = jnp.concatenate([qt[k * DH:(k + 1) * DH] for k in range(r)], axis=1).astype(bf16)


def _qg_project(xf, g, w_qg, g_q):
    N, D = xf.shape
    HD = D
    ng = w_qg.shape[1] - HD
    tq = min(512, N)
    G, T = N_KV_HEADS, Q_BLOCK
    RT = HD // (G * HEAD_DIM) * T
    assert N % tq == 0 and tq % T == 0
    seg, segt = _seg_mats(HD)
    row = lambda i: (i, 0)
    const = lambda i: (0, 0)
    return pl.pallas_call(
        _qg_kernel,
        out_shape=(jax.ShapeDtypeStruct((N // T, G, HEAD_DIM, RT), bf16), jax.ShapeDtypeStruct((N, ng), f32)),
        grid=(N // tq,),
        in_specs=[
            pl.BlockSpec((tq, D), row),
            pl.BlockSpec((1, D), const),
            pl.BlockSpec((D, HD), const),
            pl.BlockSpec((D, ng), const),
            pl.BlockSpec((1, HD), const),
            pl.BlockSpec((HD, NH_PAD), const),
            pl.BlockSpec((NH_PAD, HD), const),
        ],
        out_specs=(pl.BlockSpec((tq // T, G, HEAD_DIM, RT), lambda i: (i, 0, 0, 0)), pl.BlockSpec((tq, ng), row)),
        compiler_params=_cparams("parallel"),
        name="qg_project",
    )(xf, g.reshape(1, D), w_qg[:, :HD].astype(bf16), w_qg[:, HD:].astype(bf16),
      jnp.tile(g_q, HD // HEAD_DIM).reshape(1, HD), seg, segt)


def _fold8(x, op):
    return op(x.reshape(x.shape[0] // 8, 8, x.shape[1]), axis=0)


def _nsa_kernel(q_ref, gt_ref, kc_ref, vct_ref, ks_ref, vst_ref, kw_ref, vwt_ref, cbn_ref, cstep_ref, tb_ref, c2st_ref,
                o_ref, cb_scr, pen_ref, penf_ref, s_scr, m8_scr, l8_scr, acc_scr, part_scr, *, r, nsb, n_sel):
    qb = pl.program_id(1)
    t0 = qb * Q_BLOCK
    T = Q_BLOCK
    RT = r * T
    G = q_ref.shape[1]
    nch = kc_ref.shape[2]
    nkt = s_scr.shape[1] // KEY_TILE - SLC_UNROLL

    def tile_rows(kt):
        return pl.ds(pl.multiple_of(kt * KEY_TILE, KEY_TILE), KEY_TILE)

    def add_block_pen(s, ref, g, kt):
        return jnp.concatenate([s[:SLC_LEN] + ref[g, pl.ds(2 * kt, 1), :],
                                s[SLC_LEN:] + ref[g, pl.ds(2 * kt + 1, 1), :]], axis=0)

    for g in range(G):
        qT = q_ref[0, g]

        cb_scr[g] = cstep_ref[g, pl.ds(pl.multiple_of(nch - (qb + 1) * CMP_PER_Q, 8), CMP_PAD + nch), :]
        cb_scr[g, pl.ds(pl.multiple_of(qb * CMP_PER_Q, 8), CMP_NEAR), :] = cbn_ref[g]
        s = _dot(kc_ref[0, g], qT) + cb_scr[g, CMP_PAD:CMP_PAD + nch, :]
        p = jnp.where(s > 0.5 * NEG, jnp.exp2(s - jnp.max(s, axis=0, keepdims=True)), 0.0)
        p = p * (1.0 / jnp.maximum(jnp.sum(p, axis=0, keepdims=True), TINY))
        o_cmp = _dot(vct_ref[0, g], p.astype(bf16))

        psum = p[:, 0:T]
        for k in range(1, r):
            psum = psum + p[:, k * T:(k + 1) * T]
        hi = psum.astype(bf16)
        lo = (psum - hi.astype(f32)).astype(bf16)
        imp = _dot(c2st_ref[...], hi) + _dot(c2st_ref[...], lo)
        jb = lax.broadcasted_iota(i32, (nsb, T), 0)
        blk_q = jnp.right_shift(t0 + lax.broadcasted_iota(i32, (nsb, T), 1), SLC_LEN.bit_length() - 1)
        forced = (jb == 0) | (jb == blk_q) | (jb == blk_q - 1)
        score = jnp.where(forced, FORCED_SCORE, jnp.where(jb <= blk_q, imp, NEG))
        pen = jnp.full((nsb, T), NEG, f32)
        for _ in range(n_sel):
            mx = jnp.max(score, axis=0, keepdims=True)
            first = jnp.min(jnp.where(score == mx, jb, nsb), axis=0, keepdims=True)
            hit = jb == first
            pen = jnp.where(hit, 0.0, pen)
            score = jnp.where(hit, -jnp.inf, score)
        pen = jnp.concatenate([pen] * r, axis=1)
        pen_ref[g] = pen
        penf_ref[g] = pen + cstep_ref[g, 0:1, :]

        win = []
        for d in range(WINDOW // KEY_TILE, -1, -1):
            kt = qb - d
            ktc = jnp.maximum(kt, 0)
            tab = tb_ref[g, jnp.where(kt >= 0, d, TB_NONE)]
            win.append((_dot(kw_ref[0, g, tile_rows(ktc), :], qT) + tab, vwt_ref[0, g, ktc]))
        m8 = _fold8(win[0][0], jnp.max)
        for s_d, _ in win[1:]:
            m8 = jnp.maximum(m8, _fold8(s_d, jnp.max))
        m = jnp.max(m8, axis=0, keepdims=True)
        l8 = jnp.zeros((8, RT), f32)
        acc = jnp.zeros((HEAD_DIM, RT), f32)
        for s_d, v_d in win:
            p_d = jnp.exp2(s_d - m)
            l8 = l8 + _fold8(p_d, jnp.sum)
            acc = acc + _dot(v_d, p_d.astype(bf16))
        o_win = acc * (1.0 / jnp.maximum(jnp.sum(l8, axis=0, keepdims=True), TINY))

        gt = gt_ref[0, g]
        part_scr[g] = gt[0:1] * o_cmp + gt[2:3] * o_win

        ktp = jnp.maximum(qb - 1, 0)
        s_prev = add_block_pen(_dot(ks_ref[0, g, tile_rows(ktp), :], qT)
                               + tb_ref[g, jnp.where(qb >= 1, TB_PREV, TB_NONE)], pen_ref, g, ktp)
        s_scr[g, tile_rows(ktp), :] = s_prev
        s_own = add_block_pen(_dot(ks_ref[0, g, tile_rows(qb), :], qT) + tb_ref[g, TB_OWN], pen_ref, g, qb)
        s_scr[g, tile_rows(qb), :] = s_own
        m8_scr[g] = jnp.maximum(_fold8(s_prev, jnp.max), _fold8(s_own, jnp.max))
        l8_scr[g] = jnp.zeros((8, RT), f32)
        acc_scr[g] = jnp.zeros((HEAD_DIM, RT), f32)

    def pass_a(i, c):
        for g in range(G):
            qT = q_ref[0, g]
            m8 = m8_scr[g]
            for u in range(SLC_UNROLL):
                kt = SLC_UNROLL * i + u
                live = kt < qb - 1
                ktc = jnp.minimum(kt, nkt - 1)
                s = add_block_pen(_dot(ks_ref[0, g, tile_rows(ktc), :], qT) + jnp.where(live, 0.0, NEG),
                                  penf_ref, g, ktc)
                s_scr[g, tile_rows(jnp.where(live, kt, nkt + u)), :] = s
                m8 = jnp.maximum(m8, _fold8(s, jnp.max))
            m8_scr[g] = m8
        return c

    lax.fori_loop(0, (jnp.maximum(qb - 1, 0) + SLC_UNROLL - 1) // SLC_UNROLL, pass_a, 0)

    def pass_b(i, c):
        for g in range(G):
            m = jnp.max(m8_scr[g], axis=0, keepdims=True)
            l8 = l8_scr[g]
            acc = acc_scr[g]
            for u in range(SLC_UNROLL):
                kt = SLC_UNROLL * i + u
                ktc = jnp.minimum(kt, qb)
                p_u = jnp.exp2(s_scr[g, tile_rows(ktc), :] - (m + jnp.where(kt <= qb, 0.0, -NEG)))
                l8 = l8 + _fold8(p_u, jnp.sum)
                acc = acc + _dot(vst_ref[0, g, ktc], p_u.astype(bf16))
            l8_scr[g] = l8
            acc_scr[g] = acc
        return c

    lax.fori_loop(0, qb // SLC_UNROLL + 1, pass_b, 0)

    for g in range(G):
        o_slc = acc_scr[g] * (1.0 / jnp.maximum(jnp.sum(l8_scr[g], axis=0, keepdims=True), TINY))
        o_ref[0, g] = (part_scr[g] + gt_ref[0, g][1:2] * o_slc).astype(bf16)


def _bias_tables(rel_bias, S):
    G = N_KV_HEADS
    H = rel_bias.shape[1]
    r = H // G
    T = KEY_TILE
    n = jnp.arange(S + 2 * T, dtype=i32)
    max_exact = N_BUCKETS // 2
    nf = jnp.maximum(n, 1).astype(f32)
    large = max_exact + (jnp.log(nf / max_exact) / math.log(MAX_DISTANCE / max_exact)
                         * (N_BUCKETS - max_exact)).astype(i32)
    bucket = jnp.where(n < max_exact, n, jnp.minimum(large, N_BUCKETS - 1))
    bias1d = rel_bias.astype(f32)[bucket] * LOG2E

    def per_group(tab):
        lead = tab.shape[:-2]
        k = len(lead)
        t = jnp.moveaxis(tab, -1, 0).reshape((G, r) + lead + (T,))
        return jnp.moveaxis(t, 1, k + 1).reshape((G,) + lead + (r * T,))

    def masked(dist, ok):
        return jnp.where(jnp.asarray(ok)[..., None], bias1d[np.maximum(dist, 0)], NEG)

    def toeplitz(first):
        w = bias1d[np.maximum(first - (T - 1) + np.arange(2 * T), 0)]
        skew = jnp.broadcast_to(w[None], (T, 2 * T, H)).reshape(2 * T * T, H)[:T * (2 * T - 1)]
        return skew.reshape(T, 2 * T - 1, H)[:, T - 1:]

    kj, qi = np.arange(T)[:, None], np.arange(T)[None, :]
    ok = [qi - kj >= 0, np.ones((T, T), bool), 2 * T + qi - kj < WINDOW]
    tb = jnp.stack([jnp.where(jnp.asarray(ok[d])[..., None], toeplitz(d * T), NEG) for d in range(3)]
                   + [jnp.full((T, T, H), NEG, f32)])
    cend = (np.arange(CMP_NEAR)[:, None] - CMP_PAD) * CMP_STRIDE + CMP_LEN - 1
    dcn = np.arange(T)[None, :] - cend
    rows = CMP_PAD + S // CMP_STRIDE
    far = jnp.broadcast_to(bias1d[-1][None, None, :], (rows, T, H))
    cstep = jnp.concatenate([far, jnp.full((rows, T, H), NEG, f32)])
    return per_group(tb), per_group(masked(dcn, dcn >= 0)), per_group(cstep)


def _nsa_attention(q, gates_t, shared, tables, B, S):
    kcm, vct, ks, vst, kw, vwt = shared
    tb, cbn, cstep = tables
    G, DH, T = N_KV_HEADS, HEAD_DIM, Q_BLOCK
    RT = q.shape[3]
    r = RT // T
    nqb = S // T
    nch = S // CMP_STRIDE
    nsb = S // SLC_LEN
    nkt = S // KEY_TILE
    cmp_start = np.arange(nch) * CMP_STRIDE
    slc_start = np.arange(nsb) * SLC_LEN
    overlap = np.clip(np.minimum(cmp_start[:, None] + CMP_LEN, slc_start[None, :] + SLC_LEN)
                      - np.maximum(cmp_start[:, None], slc_start[None, :]), 0, None) / CMP_LEN
    overlap[nch - 1] = 0.0
    c2st = jnp.asarray(overlap.T, bf16)
    qmap = lambda b, i: (b * nqb + i, 0, 0, 0)
    bat = lambda b, i: (b, 0, 0, 0)
    bat5 = lambda b, i: (b, 0, 0, 0, 0)
    once = pl.Buffered(1)
    return pl.pallas_call(
        functools.partial(_nsa_kernel, r=r, nsb=nsb, n_sel=min(N_SEL, nsb)),
        out_shape=jax.ShapeDtypeStruct(q.shape, bf16),
        grid=(B, nqb),
        in_specs=[
            pl.BlockSpec((1, G, DH, RT), qmap),
            pl.BlockSpec((1, G, N_BRANCH, RT), qmap),
            pl.BlockSpec((1, G, nch, DH), bat),
            pl.BlockSpec((1, G, DH, nch), bat),
            pl.BlockSpec((1, G, S, DH), bat),
            pl.BlockSpec((1, G, nkt, DH, KEY_TILE), bat5),
            pl.BlockSpec((1, G, S, DH), bat),
            pl.BlockSpec((1, G, nkt, DH, KEY_TILE), bat5),
            pl.BlockSpec(cbn.shape, lambda b, i: (0, 0, 0), pipeline_mode=once),
            pl.BlockSpec(cstep.shape, lambda b, i: (0, 0, 0), pipeline_mode=once),
            pl.BlockSpec(tb.shape, lambda b, i: (0, 0, 0, 0), pipeline_mode=once),
            pl.BlockSpec((nsb, nch), lambda b, i: (0, 0), pipeline_mode=once),
        ],
        out_specs=pl.BlockSpec((1, G, DH, RT), qmap),
        scratch_shapes=[pltpu.VMEM((G, CMP_PAD + nch, RT), f32), pltpu.VMEM((G, nsb, RT), f32),
                        pltpu.VMEM((G, nsb, RT), f32), pltpu.VMEM((G, (nkt + SLC_UNROLL) * KEY_TILE, RT), f32),
                        pltpu.VMEM((G, 8, RT), f32), pltpu.VMEM((G, 8, RT), f32), pltpu.VMEM((G, DH, RT), f32),
                        pltpu.VMEM((G, DH, RT), f32)],
        compiler_params=_cparams("parallel", "arbitrary"),
        name="nsa_attention",
    )(q, gates_t, kcm, vct, ks, vst, kw, vwt, cbn, cstep, tb, c2st)


def _oproj_kernel(x_ref, a_ref, w_ref, o_ref):
    nt, G, DH, RT = a_ref.shape
    T = x_ref.shape[0] // nt
    r = RT // T
    rows = []
    for t in range(nt):
        cols = []
        for g in range(G):
            a = a_ref[t, g].astype(f32)
            cols.append(jnp.concatenate([a[:, k * T:(k + 1) * T] for k in range(r)], axis=0).T)
        rows.append(jnp.concatenate(cols, axis=1))
    attn = jnp.concatenate(rows, axis=0)
    o_ref[...] = x_ref[...] + _dot(attn.astype(bf16), w_ref[...])


def _out_project(xf, attn, w_o):
    N, D = xf.shape
    nt_all, G, DH, RT = attn.shape
    T = N // nt_all
    to = min(512, N)
    assert N % to == 0 and to % T == 0
    row = lambda i: (i, 0)
    return pl.pallas_call(
        _oproj_kernel,
        out_shape=jax.ShapeDtypeStruct((N, D), f32),
        grid=(N // to,),
        in_specs=[pl.BlockSpec((to, D), row), pl.BlockSpec((to // T, G, DH, RT), lambda i: (i, 0, 0, 0)),
                  pl.BlockSpec(w_o.shape, lambda i: (0, 0))],
        out_specs=pl.BlockSpec((to, D), row),
        compiler_params=_cparams("parallel"),
        name="out_project",
    )(xf, attn, w_o.astype(bf16))


def _nsa_layer(xf, B, S, g, w_qg, g_q, w_o, shared, tables):
    N = xf.shape[0]
    G, T = N_KV_HEADS, Q_BLOCK
    q, gates = _qg_project(xf, g, w_qg, g_q)
    r = gates.shape[1] // (G * N_BRANCH)
    gates_t = gates.reshape(N // T, T, G, r, N_BRANCH).transpose(0, 2, 4, 3, 1).reshape(N // T, G, N_BRANCH, r * T)
    attn = _nsa_attention(q, gates_t, shared, tables, B, S)
    return _out_project(xf, attn, w_o)


def kernel(x, rel_bias, norm_mix, norm_ffn, pool_w, pool_scale, norm_kv, w_kv, g_k, cmp_pos, cmp_w1, cmp_w2,
           w_qg, g_q, w_o, ffn_wg, ffn_wu, ffn_wd, router, moe_wg, moe_wu, moe_wd):
    B, S, D = x.shape
    depth = norm_mix.shape[0]
    n_a = depth // 2
    assert S % Q_BLOCK == 0
    xf = x.reshape(B * S, D)
    shared = None
    tables = _bias_tables(rel_bias, S)
    n_exp = moe_wg.shape[1]
    moe_w = [w.reshape((-1,) + w.shape[2:]).astype(bf16) for w in (moe_wg, moe_wu, moe_wd)]
    for layer in range(depth):
        if layer < n_a:
            xf = _pool_layer(xf.reshape(B, S, D), norm_mix[layer], pool_w[layer], pool_scale[layer]).reshape(B * S, D)
        else:
            j = layer - n_a
            xf = _nsa_layer(xf, B, S, norm_mix[layer], w_qg[j], g_q[j], w_o[j], shared, tables)
        i = layer // 2
        if layer % 2 == 0:
            xf = _ffn_layer(xf, norm_ffn[layer], ffn_wg[i], ffn_wu[i], ffn_wd[i])
        else:
            xf = _moe_layer(xf, norm_ffn[layer], router[i], *moe_w, first_expert=i * n_exp)
        if layer == n_a - 1:
            shared = _build_shared(xf, B, S, norm_kv, w_kv, g_k, cmp_pos, cmp_w1, cmp_w2)
    return xf.reshape(B, S, D)
```

```python
import functools
import math

import numpy as np
import jax
import jax.numpy as jnp
from jax import lax
from jax.experimental import pallas as pl
from jax.experimental.pallas import tpu as pltpu
from jax.experimental.pallas import tpu_sc as plsc

f32 = jnp.float32
bf16 = jnp.bfloat16
i32 = jnp.int32
u32 = jnp.uint32

POOL_WINDOWS = (2, 4, 8, 16)
HEAD_DIM = 64
N_KV_HEADS = 4
N_BRANCH = 3
CMP_LEN = 32
CMP_STRIDE = 16
CMP_HIDDEN = 4 * HEAD_DIM
SLC_LEN = 64
N_SEL = 4
WINDOW = 256
Q_BLOCK = 128
FORCED_SCORE = 1.0e4
N_BUCKETS = 32
MAX_DISTANCE = 128
TOP_K = 2
EPS = 1e-6
NEG = -1e30
TINY = 1e-30
LOG2E = math.log2(math.e)
TB_OWN, TB_PREV, TB_WIN2, TB_NONE = range(4)

KEY_TILE = Q_BLOCK
SLC_UNROLL = 4
NSA_SCRATCH_KINDS = 8
POOL_HALO = 16
NH_PAD = 16
V7X_VMEM_LIMIT = 56 * 1024 * 1024
V7X_SC_CORES, V7X_SC_SUBCORES = 2, 16
SC_MAX_INDEX_VECTOR = 128
SC_GATHER_CHUNK = 32
COMBINE_PARTS = 4

CMP_PER_Q = Q_BLOCK // CMP_STRIDE
CMP_PAD = 2 * CMP_PER_Q
CMP_NEAR = 3 * CMP_PER_Q

assert CMP_LEN == 2 * CMP_STRIDE and KEY_TILE == 2 * SLC_LEN and WINDOW == 2 * KEY_TILE
assert max(POOL_WINDOWS) <= POOL_HALO
assert 2 * KEY_TILE - (Q_BLOCK - 1) >= MAX_DISTANCE
assert (CMP_PAD + 1) * CMP_STRIDE - (CMP_LEN - 1) >= MAX_DISTANCE and CMP_PER_Q % 8 == 0


def _cparams(*sem):
    return pltpu.CompilerParams(dimension_semantics=sem, vmem_limit_bytes=V7X_VMEM_LIMIT)


def _rms(xf, g):
    ms = jnp.mean(xf * xf, axis=-1, keepdims=True)
    return (xf * lax.rsqrt(ms + EPS)) * g


def _dot(a, b):
    return jnp.dot(a, b, preferred_element_type=f32)


def _dot_hilo(a, b):
    hi = a.astype(bf16)
    lo = (a - hi.astype(f32)).astype(bf16)
    return _dot(hi, b) + _dot(lo, b)


def _head_rms(z, gvec, seg, segt):
    ssq = _dot_hilo(z * z, seg)
    inv = lax.rsqrt(ssq * (1.0 / HEAD_DIM) + EPS)
    return (z * _dot_hilo(inv, segt)) * gvec


def _silu(a):
    return a * jax.nn.sigmoid(a)


def _seg_mats(width):
    heads = width // HEAD_DIM
    seg = np.zeros((width, NH_PAD), np.float32)
    seg[np.arange(width), np.arange(width) // HEAD_DIM] = 1.0
    assert heads <= NH_PAD
    return jnp.asarray(seg, bf16), jnp.asarray(seg.T, bf16)


def _pool_kernel(x_ref, halo_ref, g_ref, w_ref, scale_ref, o_ref, *, tp, cg):
    i = pl.program_id(1)
    x = x_ref[0]
    xh = jnp.concatenate([halo_ref[0], x], axis=0)
    h = _rms(xh, g_ref[...])
    row = lax.broadcasted_iota(i32, (tp + POOL_HALO, 1), 0)
    t_abs = i * tp + row - POOL_HALO
    h = jnp.where(t_abs >= 0, h, 0.0)
    outs = []
    for gi, w in enumerate(POOL_WINDOWS):
        hg = h[:, gi * cg:(gi + 1) * cg]
        s = hg
        sh = 1
        while sh < w:
            s = s + pltpu.roll(s, sh, axis=0)
            sh *= 2
        cnt = jnp.clip(t_abs + 1, 1, w).astype(f32)
        diff = (s / cnt - hg)[POOL_HALO:]
        outs.append(_dot(diff.astype(bf16), w_ref[gi]))
    y = jnp.concatenate(outs, axis=1)
    o_ref[0] = x + y * scale_ref[...]


def _pool_layer(x3, g, w_grp, scale):
    B, S, D = x3.shape
    tp = min(512, S)
    cg = D // len(POOL_WINDOWS)
    assert S % tp == 0 and tp % POOL_HALO == 0 and all(w & (w - 1) == 0 for w in POOL_WINDOWS)
    hb = tp // POOL_HALO
    return pl.pallas_call(
        functools.partial(_pool_kernel, tp=tp, cg=cg),
        out_shape=jax.ShapeDtypeStruct((B, S, D), f32),
        grid=(B, S // tp),
        in_specs=[
            pl.BlockSpec((1, tp, D), lambda b, i: (b, i, 0)),
            pl.BlockSpec((1, POOL_HALO, D), lambda b, i: (b, jnp.maximum(i * hb - 1, 0), 0)),
            pl.BlockSpec((1, D), lambda b, i: (0, 0)),
            pl.BlockSpec((len(POOL_WINDOWS), cg, cg), lambda b, i: (0, 0, 0)),
            pl.BlockSpec((1, D), lambda b, i: (0, 0)),
        ],
        out_specs=pl.BlockSpec((1, tp, D), lambda b, i: (b, i, 0)),
        compiler_params=_cparams("parallel", "arbitrary"),
        name="pool_layer",
    )(x3, x3, g.reshape(1, D), w_grp.astype(bf16), scale.reshape(1, D))


def _ffn_kernel(x_ref, g_ref, wg_ref, wu_ref, wd_ref, o_ref, h_ref, acc_ref):
    j = pl.program_id(1)

    @pl.when(j == 0)
    def _():
        x = x_ref[...]
        h_ref[...] = _rms(x, g_ref[...]).astype(bf16)
        acc_ref[...] = x

    h = h_ref[...]
    act = _silu(_dot(h, wg_ref[...])) * _dot(h, wu_ref[...])
    acc_ref[...] += _dot(act.astype(bf16), wd_ref[...])

    @pl.when(j == pl.num_programs(1) - 1)
    def _():
        o_ref[...] = acc_ref[...]


def _ffn_layer(xf, g, wg, wu, wd):
    N, D = xf.shape
    F = wg.shape[1]
    tm = min(1024, N)
    tf = 512
    assert N % tm == 0 and F % tf == 0
    return pl.pallas_call(
        _ffn_kernel,
        out_shape=jax.ShapeDtypeStruct((N, D), f32),
        grid=(N // tm, F // tf),
        in_specs=[
            pl.BlockSpec((tm, D), lambda i, j: (i, 0)),
            pl.BlockSpec((1, D), lambda i, j: (0, 0)),
            pl.BlockSpec((D, tf), lambda i, j: (0, j)),
            pl.BlockSpec((D, tf), lambda i, j: (0, j)),
            pl.BlockSpec((tf, D), lambda i, j: (j, 0)),
        ],
        out_specs=pl.BlockSpec((tm, D), lambda i, j: (i, 0)),
        scratch_shapes=[pltpu.VMEM((tm, D), bf16), pltpu.VMEM((tm, D), f32)],
        compiler_params=_cparams("parallel", "arbitrary"),
        name="ffn_dense",
    )(xf, g.reshape(1, D), wg.astype(bf16), wu.astype(bf16), wd.astype(bf16))


def _router_kernel(x_ref, g_ref, rt_ref, hp_ref, idx_ref, gate_ref):
    h = _rms(x_ref[...], g_ref[...])
    half = h.shape[1] // 2
    bits = pltpu.bitcast(h.astype(bf16).astype(f32), u32)
    hp_ref[...] = pltpu.bitcast((bits[:, :half] & jnp.uint32(0xFFFF0000)) | (bits[:, half:] >> 16), i32)
    logits = lax.dot_general(rt_ref[...], h, (((1,), (1,)), ((), ())),
                             precision=lax.Precision.HIGHEST, preferred_element_type=f32)
    ne = logits.shape[0]
    row = lax.broadcasted_iota(i32, logits.shape, 0)
    m1 = jnp.max(logits, axis=0, keepdims=True)
    i1 = jnp.min(jnp.where(logits == m1, row, ne), axis=0, keepdims=True)
    rest = jnp.where(row == i1, -jnp.inf, logits)
    m2 = jnp.max(rest, axis=0, keepdims=True)
    i2 = jnp.min(jnp.where(rest == m2, row, ne), axis=0, keepdims=True)
    e2 = jnp.exp(m2 - m1)
    den = 1.0 + e2
    idx_ref[...] = jnp.concatenate([i1, i2], axis=0)
    gate_ref[...] = jnp.concatenate([1.0 / den, e2 / den], axis=0)


def _router(xf, g, router):
    N, D = xf.shape
    E = router.shape[1]
    tr = min(1024, N)
    assert N % tr == 0 and TOP_K == 2
    return pl.pallas_call(
        _router_kernel,
        out_shape=(jax.ShapeDtypeStruct((N, D // 2), i32),
                   jax.ShapeDtypeStruct((TOP_K, N), i32),
                   jax.ShapeDtypeStruct((TOP_K, N), f32)),
        grid=(N // tr,),
        in_specs=[
            pl.BlockSpec((tr, D), lambda i: (i, 0)),
            pl.BlockSpec((1, D), lambda i: (0, 0)),
            pl.BlockSpec((E, D), lambda i: (0, 0)),
        ],
        out_specs=(pl.BlockSpec((tr, D // 2), lambda i: (i, 0)),
                   pl.BlockSpec((TOP_K, tr), lambda i: (0, i)),
                   pl.BlockSpec((TOP_K, tr), lambda i: (0, i))),
        compiler_params=_cparams("parallel"),
        name="moe_router",
    )(xf, g.reshape(1, D), router.T)


def _sc_gather_rows(table, idx, chunk):
    B = idx.shape[0]
    D = table.shape[1]
    workers = V7X_SC_CORES * V7X_SC_SUBCORES
    per_w = B // workers
    cpw = per_w // chunk
    assert B % (8 * workers) == 0 and per_w % (2 * chunk) == 0 and cpw % 8 == 0
    assert chunk % 8 == 0 and chunk <= SC_MAX_INDEX_VECTOR
    mesh = plsc.VectorSubcoreMesh(core_axis_name="c", subcore_axis_name="s")

    @functools.partial(
        pl.kernel, mesh=mesh, out_type=jax.ShapeDtypeStruct((B, D), table.dtype),
        scratch_types=[pltpu.VMEM((cpw, chunk), i32), pltpu.VMEM((2, chunk, D), table.dtype),
                       pltpu.SemaphoreType.DMA, pltpu.SemaphoreType.DMA])
    def gather(table_hbm, idx_hbm, out_hbm, idx_v, rows_v, sem0, sem1):
        wid = lax.axis_index("s") * V7X_SC_CORES + lax.axis_index("c")
        pltpu.sync_copy(idx_hbm.at[pl.ds(pl.multiple_of(wid * cpw, 8), cpw)], idx_v)
        sems = (sem0, sem1)

        def fetch(c, b):
            return pltpu.make_async_copy(table_hbm.at[idx_v.at[c]], rows_v.at[b], sems[b])

        fetch(0, 0).start()

        @pl.loop(0, cpw, step=2)
        def _(c):
            for b in range(2):
                cur = c + b
                fetch(cur, b).wait()

                @pl.when(cur + 1 < cpw)
                def _():
                    fetch(cur + 1, 1 - b).start()

                pltpu.sync_copy(rows_v.at[b], out_hbm.at[pl.ds(pl.multiple_of(wid * per_w + cur * chunk, 8), chunk)])

    return gather(table, idx.reshape(B // chunk, chunk))


def _sc_scatter_rows(src, slots, n_slots, chunk):
    K, N = slots.shape
    D = src.shape[1]
    workers = V7X_SC_CORES * V7X_SC_SUBCORES
    per_w = N // workers
    cpw = per_w // chunk
    assert N % (8 * workers) == 0 and per_w % (2 * chunk) == 0 and cpw % 8 == 0
    assert chunk % 8 == 0 and chunk <= SC_MAX_INDEX_VECTOR
    mesh = plsc.VectorSubcoreMesh(core_axis_name="c", subcore_axis_name="s")

    @functools.partial(
        pl.kernel, mesh=mesh, out_type=jax.ShapeDtypeStruct((n_slots, D), src.dtype),
        scratch_types=[pltpu.VMEM((K, cpw, chunk), i32), pltpu.VMEM((2, chunk, D), src.dtype),
                       pltpu.SemaphoreType.DMA, pltpu.SemaphoreType.DMA, pltpu.SemaphoreType.DMA,
                       pltpu.SemaphoreType.DMA])
    def scatter(src_hbm, slots_hbm, out_hbm, slot_v, rows_v, lsem0, lsem1, ssem0, ssem1):
        wid = lax.axis_index("s") * V7X_SC_CORES + lax.axis_index("c")
        for k in range(K):
            pltpu.sync_copy(slots_hbm.at[k, pl.ds(pl.multiple_of(wid * cpw, 8), cpw)], slot_v.at[k])
        lsems, ssems = (lsem0, lsem1), (ssem0, ssem1)

        def load(c, b):
            return pltpu.make_async_copy(
                src_hbm.at[pl.ds(pl.multiple_of(wid * per_w + c * chunk, 8), chunk)], rows_v.at[b], lsems[b])

        def store(c, b, k):
            return pltpu.make_async_copy(rows_v.at[b], out_hbm.at[slot_v.at[k].at[c]], ssems[b])

        load(0, 0).start()

        @pl.loop(0, cpw, step=2)
        def _(c):
            for b in range(2):
                cur = c + b
                load(cur, b).wait()
                for k in range(K):
                    store(cur, b, k).start()

                @pl.when(cur >= 1)
                def _():
                    for k in range(K):
                        store(cur - 1, 1 - b, k).wait()

                @pl.when(cur + 1 < cpw)
                def _():
                    load(cur + 1, 1 - b).start()

        for k in range(K):
            store(cpw - 1, 1, k).wait()

    return scatter(src, slots.reshape(K, N // chunk, chunk))


def _gmm_kernel(te_ref, nu_ref, xs_ref, wg_ref, wu_ref, wd_ref, o_ref, h_ref, acc_ref):
    i = pl.program_id(0)
    j = pl.program_id(1)

    @pl.when(i < nu_ref[0])
    def _():
        @pl.when(j == 0)
        def _():
            w = pltpu.bitcast(xs_ref[...], u32)
            left = pltpu.bitcast(w & jnp.uint32(0xFFFF0000), f32)
            right = pltpu.bitcast(w << 16, f32)
            h = jnp.concatenate([left, right], axis=1)
            filled = lax.broadcasted_iota(i32, (h.shape[0], 1), 0) < nu_ref[1 + i]
            h_ref[...] = jnp.where(filled, h, 0.0).astype(bf16)
            acc_ref[...] = jnp.zeros_like(acc_ref)

        h = h_ref[...]
        act = _silu(_dot(h, wg_ref[0])) * _dot(h, wu_ref[0])
        acc_ref[...] += _dot(act.astype(bf16), wd_ref[0])

        @pl.when(j == pl.num_programs(1) - 1)
        def _():
            o_ref[...] = acc_ref[...]

    @pl.when((i >= nu_ref[0]) & (j == 0))
    def _():
        o_ref[...] = jnp.zeros_like(o_ref)


def _gmm(tile_expert, n_used, xs, wg, wu, wd, tm):
    P, half = xs.shape
    D = 2 * half
    F = wg.shape[2]
    tf = 512
    assert P % tm == 0 and F % tf == 0

    nf = F // tf

    def row_map(i, j, te, nu):
        return (jnp.minimum(i, nu[0] - 1), 0)

    def ff(i, j, nu):
        return jnp.where(i < nu[0], j, nf - 1)

    grid_spec = pltpu.PrefetchScalarGridSpec(
        num_scalar_prefetch=2,
        grid=(P // tm, nf),
        in_specs=[
            pl.BlockSpec((tm, half), row_map),
            pl.BlockSpec((1, D, tf), lambda i, j, te, nu: (te[i], 0, ff(i, j, nu))),
            pl.BlockSpec((1, D, tf), lambda i, j, te, nu: (te[i], 0, ff(i, j, nu))),
            pl.BlockSpec((1, tf, D), lambda i, j, te, nu: (te[i], ff(i, j, nu), 0)),
        ],
        out_specs=pl.BlockSpec((tm, D), lambda i, j, te, nu: (i, 0)),
        scratch_shapes=[pltpu.VMEM((tm, D), bf16), pltpu.VMEM((tm, D), f32)],
    )
    return pl.pallas_call(
        _gmm_kernel,
        out_shape=jax.ShapeDtypeStruct((P, D), f32),
        grid_spec=grid_spec,
        compiler_params=_cparams("arbitrary", "arbitrary"),
        name="moe_gmm",
    )(tile_expert, n_used, xs, wg, wu, wd)


def _combine_kernel(x_ref, gate_ref, y_ref, *rest):
    o_ref = rest[-1]
    gate = gate_ref[...]
    acc = x_ref[...]
    for k in range(TOP_K):
        acc = acc + y_ref[k] * gate[:, k:k + 1]
    o_ref[...] = acc


def _combine_part(xf, gate_tk, y_tok, out, part):
    N, D = xf.shape
    n = y_tok.shape[1]
    tt = min(1024, n)
    assert n % tt == 0
    off = part * (n // tt)
    tok = lambda i: (i + off, 0)
    in_specs = [pl.BlockSpec((tt, D), tok), pl.BlockSpec((tt, TOP_K), tok),
                pl.BlockSpec((TOP_K, tt, D), lambda i: (0, i, 0))]
    args = [xf, gate_tk, y_tok]
    if out is not None:
        in_specs.append(pl.BlockSpec(memory_space=pl.ANY))
        args.append(out)
    return pl.pallas_call(
        _combine_kernel,
        out_shape=jax.ShapeDtypeStruct((N, D), f32),
        grid=(n // tt,),
        in_specs=in_specs,
        out_specs=pl.BlockSpec((tt, D), tok),
        input_output_aliases={} if out is None else {3: 0},
        compiler_params=_cparams("parallel"),
        name="moe_combine",
    )(*args)


def _moe_layer(xf, g, router, wg, wu, wd, first_expert):
    N, D = xf.shape
    E = router.shape[1]
    tm = min(1024, N)
    hp, idx, gate = _router(xf, g, router)
    e_flat = idx.reshape(-1)
    onehot = (e_flat[:, None] == jnp.arange(E, dtype=i32)[None, :]).astype(i32)
    csum = jnp.cumsum(onehot, axis=0)
    counts = csum[-1]
    padded = ((counts + tm - 1) // tm) * tm
    ends = jnp.cumsum(padded)
    starts = ends - padded
    pos = (jnp.sum((csum + starts[None, :]) * onehot, axis=1) - 1).astype(i32)
    P = TOP_K * N + E * tm
    n_used = (ends[-1] // tm).astype(i32)
    tile_start = jnp.minimum(jnp.arange(P // tm, dtype=i32), n_used - 1) * tm
    tile_expert = jnp.sum((ends[None, :] <= tile_start[:, None]).astype(i32), axis=1)
    group_end = (starts + counts)[tile_expert]
    tile_rows = jnp.clip(group_end - tile_start, 0, tm).astype(i32)
    xs = _sc_scatter_rows(hp, pos.reshape(TOP_K, N), P, SC_GATHER_CHUNK)
    ys = _gmm(tile_expert + first_expert, jnp.concatenate([n_used.reshape(1), tile_rows]), xs, wg, wu, wd, tm)
    pos_k = pos.reshape(TOP_K, N)
    gate_tk = gate.T
    n = N // COMBINE_PARTS
    out = None
    for part in range(COMBINE_PARTS):
        rows = _sc_gather_rows(ys, pos_k[:, part * n:(part + 1) * n].reshape(-1), SC_GATHER_CHUNK)
        out = _combine_part(xf, gate_tk, rows.reshape(TOP_K, n, D), out, part)
    return out


def _kv_kernel(x_ref, g_ref, w_ref, gk_ref, seg_ref, segt_ref, kc_ref, vc_ref, ks_ref, vs_ref, kw_ref, vw_ref):
    h = _rms(x_ref[...], g_ref[...]).astype(bf16)
    kv = _dot(h, w_ref[...])
    G, DH = ks_ref.shape[1], ks_ref.shape[3]
    wd = G * DH
    part = lambda p: kv[:, p * wd:(p + 1) * wd]
    seg, segt = seg_ref[...], segt_ref[...]
    for c_ref, pc in ((kc_ref, 0), (vc_ref, 1)):
        z = part(pc)
        for g in range(G):
            c_ref[0, g] = z[:, g * DH:(g + 1) * DH]
    for k_ref, v_ref, pk, gain in ((ks_ref, vs_ref, 2, gk_ref[0:1, :]), (kw_ref, vw_ref, 4, gk_ref[1:2, :])):
        kn = _head_rms(part(pk), gain, seg, segt).astype(bf16)
        v = part(pk + 1)
        for g in range(G):
            k_ref[0, g] = kn[:, g * DH:(g + 1) * DH]
        for t in range(v_ref.shape[2]):
            vt = v[t * KEY_TILE:(t + 1) * KEY_TILE, :].T
            for g in range(G):
                v_ref[0, g, t] = vt[g * DH:(g + 1) * DH].astype(bf16)


def _kv_project(xf, B, S, norm_kv, w_kv, g_k):
    N, D = xf.shape
    G, DH = N_KV_HEADS, HEAD_DIM
    wd = G * DH
    tk = min(512, S)
    spb = S // tk
    tpk = tk // KEY_TILE
    assert S % tk == 0 and tk % KEY_TILE == 0 and w_kv.shape[1] == 2 * N_BRANCH * wd
    seg, segt = _seg_mats(wd)
    gk = jnp.stack([jnp.tile(g_k[1], G), jnp.tile(g_k[2], G)])
    row = lambda i: (i, 0)
    const = lambda i: (0, 0)
    k_shape = jax.ShapeDtypeStruct((B, G, S, DH), bf16)
    v_shape = jax.ShapeDtypeStruct((B, G, S // KEY_TILE, DH, KEY_TILE), bf16)
    k_spec = pl.BlockSpec((1, G, tk, DH), lambda i: (i // spb, 0, i % spb, 0))
    v_spec = pl.BlockSpec((1, G, tpk, DH, KEY_TILE), lambda i: (i // spb, 0, i % spb, 0, 0))
    return pl.pallas_call(
        _kv_kernel,
        out_shape=(jax.ShapeDtypeStruct((B, G, S, DH), f32), jax.ShapeDtypeStruct((B, G, S, DH), f32),
                   k_shape, v_shape, k_shape, v_shape),
        grid=(N // tk,),
        in_specs=[
            pl.BlockSpec((tk, D), row),
            pl.BlockSpec((1, D), const),
            pl.BlockSpec(w_kv.shape, const),
            pl.BlockSpec((2, wd), const),
            pl.BlockSpec((wd, NH_PAD), const),
            pl.BlockSpec((NH_PAD, wd), const),
        ],
        out_specs=(k_spec, k_spec, k_spec, v_spec, k_spec, v_spec),
        compiler_params=_cparams("parallel"),
        name="kv_project",
    )(xf, norm_kv.reshape(1, D), w_kv.astype(bf16), gk, seg, segt)


def _compress_kernel(c_ref, pos_ref, w1_ref, w2_ref, gk_ref, o_ref):
    kv = pl.program_id(0)
    c = c_ref[0, 0, 0]
    a = _dot((c + pos_ref[0, 0:1, :]).astype(bf16), w1_ref[0, 0])
    b = _dot((c + pos_ref[0, 1:2, :]).astype(bf16), w1_ref[0, 1])
    n = c.shape[0]
    hid = a + pltpu.roll(b, n - 1, axis=0)
    out = _dot(_silu(hid).astype(bf16), w2_ref[0])

    @pl.when(kv == 0)
    def _():
        o_ref[0, 0, 0] = _rms(out, gk_ref[...])

    @pl.when(kv != 0)
    def _():
        o_ref[0, 0, 0] = out


def _compress(kc, vc, B, S, cmp_pos, cmp_w1, cmp_w2, g_k0):
    G, DH = N_KV_HEADS, HEAD_DIM
    nch = S // CMP_STRIDE
    cw = CMP_STRIDE * DH

    c = jnp.stack([kc.reshape(B, G, nch, cw), vc.reshape(B, G, nch, cw)])
    pos = cmp_pos.reshape(2, 2, cw)
    w1 = cmp_w1.reshape(2, 2, cw, CMP_HIDDEN).astype(bf16)
    return pl.pallas_call(
        _compress_kernel,
        out_shape=jax.ShapeDtypeStruct((2, B, G, nch, DH), f32),
        grid=(2, B, G),
        in_specs=[
            pl.BlockSpec((1, 1, 1, nch, cw), lambda k, b, g: (k, b, g, 0, 0)),
            pl.BlockSpec((1, 2, cw), lambda k, b, g: (k, 0, 0)),
            pl.BlockSpec((1, 2, cw, CMP_HIDDEN), lambda k, b, g: (k, 0, 0, 0)),
            pl.BlockSpec((1, CMP_HIDDEN, DH), lambda k, b, g: (k, 0, 0)),
            pl.BlockSpec((1, DH), lambda k, b, g: (0, 0)),
        ],
        out_specs=pl.BlockSpec((1, 1, 1, nch, DH), lambda k, b, g: (k, b, g, 0, 0)),
        compiler_params=_cparams("arbitrary", "arbitrary", "arbitrary"),
        name="kv_compress",
    )(c, pos, w1, cmp_w2.astype(bf16), g_k0.reshape(1, DH))


def _build_shared(xf, B, S, norm_kv, w_kv, g_k, cmp_pos, cmp_w1, cmp_w2):
    kc, vc, ks, vst, kw, vwt = _kv_project(xf, B, S, norm_kv, w_kv, g_k)
    cmp = _compress(kc, vc, B, S, cmp_pos, cmp_w1, cmp_w2, g_k[0]).astype(bf16)
    kcm = cmp[0]
    vct = cmp[1].transpose(0, 1, 3, 2)
    return kcm, vct, ks, vst, kw, vwt


def _qg_kernel(x_ref, g_ref, wq_ref, wgate_ref, gq_ref, seg_ref, segt_ref, q_ref, gate_ref):
    h = _rms(x_ref[...], g_ref[...]).astype(bf16)
    q = _head_rms(_dot(h, wq_ref[...]), gq_ref[...], seg_ref[...], segt_ref[...])
    q = q * (HEAD_DIM ** -0.5 * LOG2E)
    gate_ref[...] = jax.nn.sigmoid(_dot(h, wgate_ref[...]))
    nt, G, DH, RT = q_ref.shape
    T = q.shape[0] // nt
    r = RT // T
    for t in range(nt):
        for g in range(G):
            qt = q[t * T:(t + 1) * T, g * r * DH:(g + 1) * r * DH].T
            q_ref[t, g] = jnp.concatenate([qt[k * DH:(k + 1) * DH] for k in range(r)], axis=1).astype(bf16)


def _qg_project(xf, g, w_qg, g_q):
    N, D = xf.shape
    HD = D
    ng = w_qg.shape[1] - HD
    tq = min(512, N)
    G, T = N_KV_HEADS, Q_BLOCK
    RT = HD // (G * HEAD_DIM) * T
    assert N % tq == 0 and tq % T == 0
    seg, segt = _seg_mats(HD)
    row = lambda i: (i, 0)
    const = lambda i: (0, 0)
    return pl.pallas_call(
        _qg_kernel,
        out_shape=(jax.ShapeDtypeStruct((N // T, G, HEAD_DIM, RT), bf16), jax.ShapeDtypeStruct((N, ng), f32)),
        grid=(N // tq,),
        in_specs=[
            pl.BlockSpec((tq, D), row),
            pl.BlockSpec((1, D), const),
            pl.BlockSpec((D, HD), const),
            pl.BlockSpec((D, ng), const),
            pl.BlockSpec((1, HD), const),
            pl.BlockSpec((HD, NH_PAD), const),
            pl.BlockSpec((NH_PAD, HD), const),
        ],
        out_specs=(pl.BlockSpec((tq // T, G, HEAD_DIM, RT), lambda i: (i, 0, 0, 0)), pl.BlockSpec((tq, ng), row)),
        compiler_params=_cparams("parallel"),
        name="qg_project",
    )(xf, g.reshape(1, D), w_qg[:, :HD].astype(bf16), w_qg[:, HD:].astype(bf16),
      jnp.tile(g_q, HD // HEAD_DIM).reshape(1, HD), seg, segt)


def _fold8(x, op):
    return op(x.reshape(x.shape[0] // 8, 8, x.shape[1]), axis=0)


def _nsa_kernel(q_ref, gt_ref, kc_ref, vct_ref, ks_ref, vst_ref, kw_ref, vwt_ref, cbn_ref, cstep_ref, tb_ref, c2st_ref,
                o_ref, *scratch, r, nsb, n_sel):
    qb = pl.program_id(1)
    t0 = qb * Q_BLOCK
    T = Q_BLOCK
    RT = r * T
    G = q_ref.shape[1]
    nch = kc_ref.shape[2]
    cb_scr, pen_ref, penf_ref, s_scr, m8_scr, l8_scr, acc_scr, part_scr = (
        scratch[k * G:(k + 1) * G] for k in range(NSA_SCRATCH_KINDS))
    nkt = s_scr[0].shape[0] // KEY_TILE - SLC_UNROLL

    def tile_rows(kt):
        return pl.ds(pl.multiple_of(kt * KEY_TILE, KEY_TILE), KEY_TILE)

    def add_block_pen(s, ref, kt):
        return jnp.concatenate([s[:SLC_LEN] + ref[pl.ds(2 * kt, 1), :], s[SLC_LEN:] + ref[pl.ds(2 * kt + 1, 1), :]],
                               axis=0)

    for g in range(G):
        qT = q_ref[0, g]

        cb_scr[g][...] = cstep_ref[g, pl.ds(pl.multiple_of(nch - (qb + 1) * CMP_PER_Q, 8), CMP_PAD + nch), :]
        cb_scr[g][pl.ds(pl.multiple_of(qb * CMP_PER_Q, 8), CMP_NEAR), :] = cbn_ref[g]
        s = _dot(kc_ref[0, g], qT) + cb_scr[g][CMP_PAD:CMP_PAD + nch, :]
        p = jnp.where(s > 0.5 * NEG, jnp.exp2(s - jnp.max(s, axis=0, keepdims=True)), 0.0)
        p = p * (1.0 / jnp.maximum(jnp.sum(p, axis=0, keepdims=True), TINY))
        o_cmp = _dot(vct_ref[0, g], p.astype(bf16))

        psum = p[:, 0:T]
        for k in range(1, r):
            psum = psum + p[:, k * T:(k + 1) * T]
        hi = psum.astype(bf16)
        lo = (psum - hi.astype(f32)).astype(bf16)
        imp = _dot(c2st_ref[...], hi) + _dot(c2st_ref[...], lo)
        jb = lax.broadcasted_iota(i32, (nsb, T), 0)
        blk_q = jnp.right_shift(t0 + lax.broadcasted_iota(i32, (nsb, T), 1), SLC_LEN.bit_length() - 1)
        forced = (jb == 0) | (jb == blk_q) | (jb == blk_q - 1)
        score = jnp.where(forced, FORCED_SCORE, jnp.where(jb <= blk_q, imp, NEG))
        pen = jnp.full((nsb, T), NEG, f32)
        for _ in range(n_sel):
            mx = jnp.max(score, axis=0, keepdims=True)
            first = jnp.min(jnp.where(score == mx, jb, nsb), axis=0, keepdims=True)
            hit = jb == first
            pen = jnp.where(hit, 0.0, pen)
            score = jnp.where(hit, -jnp.inf, score)
        pen = jnp.concatenate([pen] * r, axis=1)
        pen_ref[g][...] = pen
        penf_ref[g][...] = pen + cstep_ref[g, 0:1, :]

        win = []
        for d in range(WINDOW // KEY_TILE, -1, -1):
            kt = qb - d
            ktc = jnp.maximum(kt, 0)
            tab = tb_ref[g, jnp.where(kt >= 0, d, TB_NONE)]
            win.append((_dot(kw_ref[0, g, tile_rows(ktc), :], qT) + tab, vwt_ref[0, g, ktc]))
        m8 = _fold8(win[0][0], jnp.max)
        for s_d, _ in win[1:]:
            m8 = jnp.maximum(m8, _fold8(s_d, jnp.max))
        m = jnp.max(m8, axis=0, keepdims=True)
        l8 = jnp.zeros((8, RT), f32)
        acc = jnp.zeros((HEAD_DIM, RT), f32)
        for s_d, v_d in win:
            p_d = jnp.exp2(s_d - m)
            l8 = l8 + _fold8(p_d, jnp.sum)
            acc = acc + _dot(v_d, p_d.astype(bf16))
        o_win = acc * (1.0 / jnp.maximum(jnp.sum(l8, axis=0, keepdims=True), TINY))

        gt = gt_ref[0, g]
        part_scr[g][...] = gt[0:1] * o_cmp + gt[2:3] * o_win

        ktp = jnp.maximum(qb - 1, 0)
        s_prev = add_block_pen(_dot(ks_ref[0, g, tile_rows(ktp), :], qT)
                               + tb_ref[g, jnp.where(qb >= 1, TB_PREV, TB_NONE)], pen_ref[g], ktp)
        s_scr[g][tile_rows(ktp), :] = s_prev
        s_own = add_block_pen(_dot(ks_ref[0, g, tile_rows(qb), :], qT) + tb_ref[g, TB_OWN], pen_ref[g], qb)
        s_scr[g][tile_rows(qb), :] = s_own
        m8_scr[g][...] = jnp.maximum(_fold8(s_prev, jnp.max), _fold8(s_own, jnp.max))
        l8_scr[g][...] = jnp.zeros((8, RT), f32)
        acc_scr[g][...] = jnp.zeros((HEAD_DIM, RT), f32)

    def pass_a(i, c):
        for g in range(G):
            qT = q_ref[0, g]
            m8 = m8_scr[g][...]
            for u in range(SLC_UNROLL):
                kt = SLC_UNROLL * i + u
                live = kt < qb - 1
                ktc = jnp.minimum(kt, nkt - 1)
                s = add_block_pen(_dot(ks_ref[0, g, tile_rows(ktc), :], qT) + jnp.where(live, 0.0, NEG),
                                  penf_ref[g], ktc)
                s_scr[g][tile_rows(jnp.where(live, kt, nkt + u)), :] = s
                m8 = jnp.maximum(m8, _fold8(s, jnp.max))
            m8_scr[g][...] = m8
        return c

    lax.fori_loop(0, (jnp.maximum(qb - 1, 0) + SLC_UNROLL - 1) // SLC_UNROLL, pass_a, 0)

    def pass_b(i, c):
        for g in range(G):
            m = jnp.max(m8_scr[g][...], axis=0, keepdims=True)
            l8 = l8_scr[g][...]
            acc = acc_scr[g][...]
            for u in range(SLC_UNROLL):
                kt = SLC_UNROLL * i + u
                ktc = jnp.minimum(kt, qb)
                p_u = jnp.exp2(s_scr[g][tile_rows(ktc), :] - (m + jnp.where(kt <= qb, 0.0, -NEG)))
                l8 = l8 + _fold8(p_u, jnp.sum)
                acc = acc + _dot(vst_ref[0, g, ktc], p_u.astype(bf16))
            l8_scr[g][...] = l8
            acc_scr[g][...] = acc
        return c

    lax.fori_loop(0, qb // SLC_UNROLL + 1, pass_b, 0)

    for g in range(G):
        o_slc = acc_scr[g][...] * (1.0 / jnp.maximum(jnp.sum(l8_scr[g][...], axis=0, keepdims=True), TINY))
        o_ref[0, g] = (part_scr[g][...] + gt_ref[0, g][1:2] * o_slc).astype(bf16)


def _bias_tables(rel_bias, S):
    G = N_KV_HEADS
    H = rel_bias.shape[1]
    r = H // G
    T = KEY_TILE
    n = jnp.arange(S + 2 * T, dtype=i32)
    max_exact = N_BUCKETS // 2
    nf = jnp.maximum(n, 1).astype(f32)
    large = max_exact + (jnp.log(nf / max_exact) / math.log(MAX_DISTANCE / max_exact)
                         * (N_BUCKETS - max_exact)).astype(i32)
    bucket = jnp.where(n < max_exact, n, jnp.minimum(large, N_BUCKETS - 1))
    bias1d = rel_bias.astype(f32)[bucket] * LOG2E

    def per_group(tab):
        lead = tab.shape[:-2]
        k = len(lead)
        t = jnp.moveaxis(tab, -1, 0).reshape((G, r) + lead + (T,))
        return jnp.moveaxis(t, 1, k + 1).reshape((G,) + lead + (r * T,))

    def masked(dist, ok):
        return jnp.where(jnp.asarray(ok)[..., None], bias1d[np.maximum(dist, 0)], NEG)

    def toeplitz(first):
        w = bias1d[np.maximum(first - (T - 1) + np.arange(2 * T), 0)]
        skew = jnp.broadcast_to(w[None], (T, 2 * T, H)).reshape(2 * T * T, H)[:T * (2 * T - 1)]
        return skew.reshape(T, 2 * T - 1, H)[:, T - 1:]

    kj, qi = np.arange(T)[:, None], np.arange(T)[None, :]
    ok = [qi - kj >= 0, np.ones((T, T), bool), 2 * T + qi - kj < WINDOW]
    tb = jnp.stack([jnp.where(jnp.asarray(ok[d])[..., None], toeplitz(d * T), NEG) for d in range(3)]
                   + [jnp.full((T, T, H), NEG, f32)])
    cend = (np.arange(CMP_NEAR)[:, None] - CMP_PAD) * CMP_STRIDE + CMP_LEN - 1
    dcn = np.arange(T)[None, :] - cend
    rows = CMP_PAD + S // CMP_STRIDE
    far = jnp.broadcast_to(bias1d[-1][None, None, :], (rows, T, H))
    cstep = jnp.concatenate([far, jnp.full((rows, T, H), NEG, f32)])
    return per_group(tb), per_group(masked(dcn, dcn >= 0)), per_group(cstep)


def _nsa_attention(q, gates_t, shared, tables, B, S):
    kcm, vct, ks, vst, kw, vwt = shared
    tb, cbn, cstep = tables
    G, DH, T = N_KV_HEADS, HEAD_DIM, Q_BLOCK
    RT = q.shape[3]
    r = RT // T
    nqb = S // T
    nch = S // CMP_STRIDE
    nsb = S // SLC_LEN
    nkt = S // KEY_TILE
    cmp_start = np.arange(nch) * CMP_STRIDE
    slc_start = np.arange(nsb) * SLC_LEN
    overlap = np.clip(np.minimum(cmp_start[:, None] + CMP_LEN, slc_start[None, :] + SLC_LEN)
                      - np.maximum(cmp_start[:, None], slc_start[None, :]), 0, None) / CMP_LEN
    overlap[nch - 1] = 0.0
    c2st = jnp.asarray(overlap.T, bf16)
    qmap = lambda b, i: (b * nqb + i, 0, 0, 0)
    bat = lambda b, i: (b, 0, 0, 0)
    bat5 = lambda b, i: (b, 0, 0, 0, 0)
    once = pl.Buffered(1)
    scratch_rows = (CMP_PAD + nch, nsb, nsb, (nkt + SLC_UNROLL) * KEY_TILE, 8, 8, DH, DH)
    assert len(scratch_rows) == NSA_SCRATCH_KINDS
    return pl.pallas_call(
        functools.partial(_nsa_kernel, r=r, nsb=nsb, n_sel=min(N_SEL, nsb)),
        out_shape=jax.ShapeDtypeStruct(q.shape, bf16),
        grid=(B, nqb),
        in_specs=[
            pl.BlockSpec((1, G, DH, RT), qmap),
            pl.BlockSpec((1, G, N_BRANCH, RT), qmap),
            pl.BlockSpec((1, G, nch, DH), bat),
            pl.BlockSpec((1, G, DH, nch), bat),
            pl.BlockSpec((1, G, S, DH), bat),
            pl.BlockSpec((1, G, nkt, DH, KEY_TILE), bat5),
            pl.BlockSpec((1, G, S, DH), bat),
            pl.BlockSpec((1, G, nkt, DH, KEY_TILE), bat5),
            pl.BlockSpec(cbn.shape, lambda b, i: (0, 0, 0), pipeline_mode=once),
            pl.BlockSpec(cstep.shape, lambda b, i: (0, 0, 0), pipeline_mode=once),
            pl.BlockSpec(tb.shape, lambda b, i: (0, 0, 0, 0), pipeline_mode=once),
            pl.BlockSpec((nsb, nch), lambda b, i: (0, 0), pipeline_mode=once),
        ],
        out_specs=pl.BlockSpec((1, G, DH, RT), qmap),
        scratch_shapes=[pltpu.VMEM((rows, RT), f32) for rows in scratch_rows for _ in range(G)],
        compiler_params=_cparams("parallel", "arbitrary"),
        name="nsa_attention",
    )(q, gates_t, kcm, vct, ks, vst, kw, vwt, cbn, cstep, tb, c2st)


def _oproj_kernel(x_ref, a_ref, w_ref, o_ref):
    nt, G, DH, RT = a_ref.shape
    T = x_ref.shape[0] // nt
    r = RT // T
    rows = []
    for t in range(nt):
        cols = []
        for g in range(G):
            a = a_ref[t, g].astype(f32)
            cols.append(jnp.concatenate([a[:, k * T:(k + 1) * T] for k in range(r)], axis=0).T)
        rows.append(jnp.concatenate(cols, axis=1))
    attn = jnp.concatenate(rows, axis=0)
    o_ref[...] = x_ref[...] + _dot(attn.astype(bf16), w_ref[...])


def _out_project(xf, attn, w_o):
    N, D = xf.shape
    nt_all, G, DH, RT = attn.shape
    T = N // nt_all
    to = min(512, N)
    assert N % to == 0 and to % T == 0
    row = lambda i: (i, 0)
    return pl.pallas_call(
        _oproj_kernel,
        out_shape=jax.ShapeDtypeStruct((N, D), f32),
        grid=(N // to,),
        in_specs=[pl.BlockSpec((to, D), row), pl.BlockSpec((to // T, G, DH, RT), lambda i: (i, 0, 0, 0)),
                  pl.BlockSpec(w_o.shape, lambda i: (0, 0))],
        out_specs=pl.BlockSpec((to, D), row),
        compiler_params=_cparams("parallel"),
        name="out_project",
    )(xf, attn, w_o.astype(bf16))


def _nsa_layer(xf, B, S, g, w_qg, g_q, w_o, shared, tables):
    N = xf.shape[0]
    G, T = N_KV_HEADS, Q_BLOCK
    q, gates = _qg_project(xf, g, w_qg, g_q)
    r = gates.shape[1] // (G * N_BRANCH)
    gates_t = gates.reshape(N // T, T, G, r, N_BRANCH).transpose(0, 2, 4, 3, 1).reshape(N // T, G, N_BRANCH, r * T)
    attn = _nsa_attention(q, gates_t, shared, tables, B, S)
    return _out_project(xf, attn, w_o)


def kernel(x, rel_bias, norm_mix, norm_ffn, pool_w, pool_scale, norm_kv, w_kv, g_k, cmp_pos, cmp_w1, cmp_w2,
           w_qg, g_q, w_o, ffn_wg, ffn_wu, ffn_wd, router, moe_wg, moe_wu, moe_wd):
    B, S, D = x.shape
    depth = norm_mix.shape[0]
    n_a = depth // 2
    assert S % Q_BLOCK == 0
    xf = x.reshape(B * S, D)
    shared = None
    tables = _bias_tables(rel_bias, S)
    n_exp = moe_wg.shape[1]
    moe_w = [w.reshape((-1,) + w.shape[2:]).astype(bf16) for w in (moe_wg, moe_wu, moe_wd)]
    for layer in range(depth):
        if layer < n_a:
            xf = _pool_layer(xf.reshape(B, S, D), norm_mix[layer], pool_w[layer], pool_scale[layer]).reshape(B * S, D)
        else:
            j = layer - n_a
            xf = _nsa_layer(xf, B, S, norm_mix[layer], w_qg[j], g_q[j], w_o[j], shared, tables)
        i = layer // 2
        if layer % 2 == 0:
            xf = _ffn_layer(xf, norm_ffn[layer], ffn_wg[i], ffn_wu[i], ffn_wd[i])
        else:
            xf = _moe_layer(xf, norm_ffn[layer], router[i], *moe_w, first_expert=i * n_exp)
        if layer == n_a - 1:
            shared = _build_shared(xf, B, S, norm_kv, w_kv, g_k, cmp_pos, cmp_w1, cmp_w2)
    return xf.reshape(B, S, D)
```

```python
import functools
import math

import numpy as np
import jax
import jax.numpy as jnp
from jax import lax
from jax.experimental import pallas as pl
from jax.experimental.pallas import tpu as pltpu
from jax.experimental.pallas import tpu_sc as plsc

f32 = jnp.float32
bf16 = jnp.bfloat16
i32 = jnp.int32
u32 = jnp.uint32

POOL_WINDOWS = (2, 4, 8, 16)
HEAD_DIM = 64
N_KV_HEADS = 4
N_BRANCH = 3
CMP_LEN = 32
CMP_STRIDE = 16
CMP_HIDDEN = 4 * HEAD_DIM
SLC_LEN = 64
N_SEL = 4
WINDOW = 256
Q_BLOCK = 128
FORCED_SCORE = 1.0e4
N_BUCKETS = 32
MAX_DISTANCE = 128
TOP_K = 2
EPS = 1e-6
NEG = -1e30
TINY = 1e-30
LOG2E = math.log2(math.e)
TB_OWN, TB_PREV, TB_WIN2, TB_NONE = range(4)

KEY_TILE = Q_BLOCK
SLC_UNROLL = 4
POOL_HALO = 16
NH_PAD = 16
V7X_VMEM_LIMIT = 56 * 1024 * 1024
V7X_SC_CORES, V7X_SC_SUBCORES = 2, 16
SC_MAX_INDEX_VECTOR = 128
SC_GATHER_CHUNK = 32

CMP_PER_Q = Q_BLOCK // CMP_STRIDE
CMP_PAD = 2 * CMP_PER_Q
CMP_NEAR = 3 * CMP_PER_Q

assert CMP_LEN == 2 * CMP_STRIDE and KEY_TILE == 2 * SLC_LEN and WINDOW == 2 * KEY_TILE
assert max(POOL_WINDOWS) <= POOL_HALO
assert 2 * KEY_TILE - (Q_BLOCK - 1) >= MAX_DISTANCE
assert (CMP_PAD + 1) * CMP_STRIDE - (CMP_LEN - 1) >= MAX_DISTANCE and CMP_PER_Q % 8 == 0


def _cparams(*sem):
    return pltpu.CompilerParams(dimension_semantics=sem, vmem_limit_bytes=V7X_VMEM_LIMIT)


def _rms(xf, g):
    ms = jnp.mean(xf * xf, axis=-1, keepdims=True)
    return (xf * lax.rsqrt(ms + EPS)) * g


def _dot(a, b):
    return jnp.dot(a, b, preferred_element_type=f32)


def _dot_hilo(a, b):
    hi = a.astype(bf16)
    lo = (a - hi.astype(f32)).astype(bf16)
    return _dot(hi, b) + _dot(lo, b)


def _head_rms(z, gvec, seg, segt):
    ssq = _dot_hilo(z * z, seg)
    inv = lax.rsqrt(ssq * (1.0 / HEAD_DIM) + EPS)
    return (z * _dot_hilo(inv, segt)) * gvec


def _silu(a):
    return a * jax.nn.sigmoid(a)


def _seg_mats(width):
    heads = width // HEAD_DIM
    seg = np.zeros((width, NH_PAD), np.float32)
    seg[np.arange(width), np.arange(width) // HEAD_DIM] = 1.0
    assert heads <= NH_PAD
    return jnp.asarray(seg, bf16), jnp.asarray(seg.T, bf16)


def _pool_kernel(x_ref, halo_ref, g_ref, w_ref, scale_ref, o_ref, *, tp, cg):
    i = pl.program_id(1)
    x = x_ref[0]
    xh = jnp.concatenate([halo_ref[0], x], axis=0)
    h = _rms(xh, g_ref[...])
    row = lax.broadcasted_iota(i32, (tp + POOL_HALO, 1), 0)
    t_abs = i * tp + row - POOL_HALO
    h = jnp.where(t_abs >= 0, h, 0.0)
    outs = []
    for gi, w in enumerate(POOL_WINDOWS):
        hg = h[:, gi * cg:(gi + 1) * cg]
        s = hg
        sh = 1
        while sh < w:
            s = s + pltpu.roll(s, sh, axis=0)
            sh *= 2
        cnt = jnp.clip(t_abs + 1, 1, w).astype(f32)
        diff = (s / cnt - hg)[POOL_HALO:]
        outs.append(_dot(diff.astype(bf16), w_ref[gi]))
    y = jnp.concatenate(outs, axis=1)
    o_ref[0] = x + y * scale_ref[...]


def _pool_layer(x3, g, w_grp, scale):
    B, S, D = x3.shape
    tp = min(512, S)
    cg = D // len(POOL_WINDOWS)
    assert S % tp == 0 and tp % POOL_HALO == 0 and all(w & (w - 1) == 0 for w in POOL_WINDOWS)
    hb = tp // POOL_HALO
    return pl.pallas_call(
        functools.partial(_pool_kernel, tp=tp, cg=cg),
        out_shape=jax.ShapeDtypeStruct((B, S, D), f32),
        grid=(B, S // tp),
        in_specs=[
            pl.BlockSpec((1, tp, D), lambda b, i: (b, i, 0)),
            pl.BlockSpec((1, POOL_HALO, D), lambda b, i: (b, jnp.maximum(i * hb - 1, 0), 0)),
            pl.BlockSpec((1, D), lambda b, i: (0, 0)),
            pl.BlockSpec((len(POOL_WINDOWS), cg, cg), lambda b, i: (0, 0, 0)),
            pl.BlockSpec((1, D), lambda b, i: (0, 0)),
        ],
        out_specs=pl.BlockSpec((1, tp, D), lambda b, i: (b, i, 0)),
        compiler_params=_cparams("parallel", "arbitrary"),
        name="pool_layer",
    )(x3, x3, g.reshape(1, D), w_grp.astype(bf16), scale.reshape(1, D))


def _ffn_kernel(x_ref, g_ref, wg_ref, wu_ref, wd_ref, o_ref, h_ref, acc_ref):
    j = pl.program_id(1)

    @pl.when(j == 0)
    def _():
        x = x_ref[...]
        h_ref[...] = _rms(x, g_ref[...]).astype(bf16)
        acc_ref[...] = x

    h = h_ref[...]
    act = _silu(_dot(h, wg_ref[...])) * _dot(h, wu_ref[...])
    acc_ref[...] += _dot(act.astype(bf16), wd_ref[...])

    @pl.when(j == pl.num_programs(1) - 1)
    def _():
        o_ref[...] = acc_ref[...]


def _ffn_layer(xf, g, wg, wu, wd):
    N, D = xf.shape
    F = wg.shape[1]
    tm = min(1024, N)
    tf = 512
    assert N % tm == 0 and F % tf == 0
    return pl.pallas_call(
        _ffn_kernel,
        out_shape=jax.ShapeDtypeStruct((N, D), f32),
        grid=(N // tm, F // tf),
        in_specs=[
            pl.BlockSpec((tm, D), lambda i, j: (i, 0)),
            pl.BlockSpec((1, D), lambda i, j: (0, 0)),
            pl.BlockSpec((D, tf), lambda i, j: (0, j)),
            pl.BlockSpec((D, tf), lambda i, j: (0, j)),
            pl.BlockSpec((tf, D), lambda i, j: (j, 0)),
        ],
        out_specs=pl.BlockSpec((tm, D), lambda i, j: (i, 0)),
        scratch_shapes=[pltpu.VMEM((tm, D), bf16), pltpu.VMEM((tm, D), f32)],
        compiler_params=_cparams("parallel", "arbitrary"),
        name="ffn_dense",
    )(xf, g.reshape(1, D), wg.astype(bf16), wu.astype(bf16), wd.astype(bf16))


def _router_kernel(x_ref, g_ref, rt_ref, hp_ref, idx_ref, gate_ref):
    h = _rms(x_ref[...], g_ref[...])
    half = h.shape[1] // 2
    bits = pltpu.bitcast(h.astype(bf16).astype(f32), u32)
    hp_ref[...] = pltpu.bitcast((bits[:, :half] & jnp.uint32(0xFFFF0000)) | (bits[:, half:] >> 16), i32)
    logits = lax.dot_general(rt_ref[...], h, (((1,), (1,)), ((), ())),
                             precision=lax.Precision.HIGHEST, preferred_element_type=f32)
    ne = logits.shape[0]
    row = lax.broadcasted_iota(i32, logits.shape, 0)
    m1 = jnp.max(logits, axis=0, keepdims=True)
    i1 = jnp.min(jnp.where(logits == m1, row, ne), axis=0, keepdims=True)
    rest = jnp.where(row == i1, -jnp.inf, logits)
    m2 = jnp.max(rest, axis=0, keepdims=True)
    i2 = jnp.min(jnp.where(rest == m2, row, ne), axis=0, keepdims=True)
    e2 = jnp.exp(m2 - m1)
    den = 1.0 + e2
    idx_ref[...] = jnp.concatenate([i1, i2], axis=0)
    gate_ref[...] = jnp.concatenate([1.0 / den, e2 / den], axis=0)


def _router(xf, g, router):
    N, D = xf.shape
    E = router.shape[1]
    tr = min(1024, N)
    assert N % tr == 0 and TOP_K == 2
    return pl.pallas_call(
        _router_kernel,
        out_shape=(jax.ShapeDtypeStruct((N, D // 2), i32),
                   jax.ShapeDtypeStruct((TOP_K, N), i32),
                   jax.ShapeDtypeStruct((TOP_K, N), f32)),
        grid=(N // tr,),
        in_specs=[
            pl.BlockSpec((tr, D), lambda i: (i, 0)),
            pl.BlockSpec((1, D), lambda i: (0, 0)),
            pl.BlockSpec((E, D), lambda i: (0, 0)),
        ],
        out_specs=(pl.BlockSpec((tr, D // 2), lambda i: (i, 0)),
                   pl.BlockSpec((TOP_K, tr), lambda i: (0, i)),
                   pl.BlockSpec((TOP_K, tr), lambda i: (0, i))),
        compiler_params=_cparams("parallel"),
        name="moe_router",
    )(xf, g.reshape(1, D), router.T)


def _sc_gather_rows(table, idx, chunk):
    B = idx.shape[0]
    D = table.shape[1]
    workers = V7X_SC_CORES * V7X_SC_SUBCORES
    per_w = B // workers
    cpw = per_w // chunk
    assert B % (8 * workers) == 0 and per_w % (2 * chunk) == 0 and cpw % 8 == 0
    assert chunk % 8 == 0 and chunk <= SC_MAX_INDEX_VECTOR
    mesh = plsc.VectorSubcoreMesh(core_axis_name="c", subcore_axis_name="s")

    @functools.partial(
        pl.kernel, mesh=mesh, out_type=jax.ShapeDtypeStruct((B, D), table.dtype),
        scratch_types=[pltpu.VMEM((cpw, chunk), i32), pltpu.VMEM((2, chunk, D), table.dtype),
                       pltpu.SemaphoreType.DMA, pltpu.SemaphoreType.DMA])
    def gather(table_hbm, idx_hbm, out_hbm, idx_v, rows_v, sem0, sem1):
        wid = lax.axis_index("s") * V7X_SC_CORES + lax.axis_index("c")
        pltpu.sync_copy(idx_hbm.at[pl.ds(pl.multiple_of(wid * cpw, 8), cpw)], idx_v)
        sems = (sem0, sem1)

        def fetch(c, b):
            return pltpu.make_async_copy(table_hbm.at[idx_v.at[c]], rows_v.at[b], sems[b])

        fetch(0, 0).start()

        @pl.loop(0, cpw, step=2)
        def _(c):
            for b in range(2):
                cur = c + b
                fetch(cur, b).wait()

                @pl.when(cur + 1 < cpw)
                def _():
                    fetch(cur + 1, 1 - b).start()

                pltpu.sync_copy(rows_v.at[b], out_hbm.at[pl.ds(pl.multiple_of(wid * per_w + cur * chunk, 8), chunk)])

    return gather(table, idx.reshape(B // chunk, chunk))


def _sc_scatter_rows(src, slots, n_slots, chunk):
    K, N = slots.shape
    D = src.shape[1]
    workers = V7X_SC_CORES * V7X_SC_SUBCORES
    per_w = N // workers
    cpw = per_w // chunk
    assert N % (8 * workers) == 0 and per_w % (2 * chunk) == 0 and cpw % 8 == 0
    assert chunk % 8 == 0 and chunk <= SC_MAX_INDEX_VECTOR
    mesh = plsc.VectorSubcoreMesh(core_axis_name="c", subcore_axis_name="s")

    @functools.partial(
        pl.kernel, mesh=mesh, out_type=jax.ShapeDtypeStruct((n_slots, D), src.dtype),
        scratch_types=[pltpu.VMEM((K, cpw, chunk), i32), pltpu.VMEM((2, chunk, D), src.dtype),
                       pltpu.SemaphoreType.DMA, pltpu.SemaphoreType.DMA, pltpu.SemaphoreType.DMA,
                       pltpu.SemaphoreType.DMA])
    def scatter(src_hbm, slots_hbm, out_hbm, slot_v, rows_v, lsem0, lsem1, ssem0, ssem1):
        wid = lax.axis_index("s") * V7X_SC_CORES + lax.axis_index("c")
        for k in range(K):
            pltpu.sync_copy(slots_hbm.at[k, pl.ds(pl.multiple_of(wid * cpw, 8), cpw)], slot_v.at[k])
        lsems, ssems = (lsem0, lsem1), (ssem0, ssem1)

        def load(c, b):
            return pltpu.make_async_copy(
                src_hbm.at[pl.ds(pl.multiple_of(wid * per_w + c * chunk, 8), chunk)], rows_v.at[b], lsems[b])

        def store(c, b, k):
            return pltpu.make_async_copy(rows_v.at[b], out_hbm.at[slot_v.at[k].at[c]], ssems[b])

        load(0, 0).start()

        @pl.loop(0, cpw, step=2)
        def _(c):
            for b in range(2):
                cur = c + b
                load(cur, b).wait()
                for k in range(K):
                    store(cur, b, k).start()

                @pl.when(cur >= 1)
                def _():
                    for k in range(K):
                        store(cur - 1, 1 - b, k).wait()

                @pl.when(cur + 1 < cpw)
                def _():
                    load(cur + 1, 1 - b).start()

        for k in range(K):
            store(cpw - 1, 1, k).wait()

    return scatter(src, slots.reshape(K, N // chunk, chunk))


def _gmm_kernel(te_ref, nu_ref, xs_ref, wg_ref, wu_ref, wd_ref, o_ref, h_ref, acc_ref):
    i = pl.program_id(0)
    j = pl.program_id(1)

    @pl.when(i < nu_ref[0])
    def _():
        @pl.when(j == 0)
        def _():
            w = pltpu.bitcast(xs_ref[...], u32)
            left = pltpu.bitcast(w & jnp.uint32(0xFFFF0000), f32)
            right = pltpu.bitcast(w << 16, f32)
            h = jnp.concatenate([left, right], axis=1)
            filled = lax.broadcasted_iota(i32, (h.shape[0], 1), 0) < nu_ref[1 + i]
            h_ref[...] = jnp.where(filled, h, 0.0).astype(bf16)
            acc_ref[...] = jnp.zeros_like(acc_ref)

        h = h_ref[...]
        act = _silu(_dot(h, wg_ref[0])) * _dot(h, wu_ref[0])
        acc_ref[...] += _dot(act.astype(bf16), wd_ref[0])

        @pl.when(j == pl.num_programs(1) - 1)
        def _():
            o_ref[...] = acc_ref[...]

    @pl.when((i >= nu_ref[0]) & (j == 0))
    def _():
        o_ref[...] = jnp.zeros_like(o_ref)


def _gmm(tile_expert, n_used, xs, wg, wu, wd, tm):
    P, half = xs.shape
    D = 2 * half
    F = wg.shape[2]
    tf = 512
    assert P % tm == 0 and F % tf == 0

    nf = F // tf

    def row_map(i, j, te, nu):
        return (jnp.minimum(i, nu[0] - 1), 0)

    def ff(i, j, nu):
        return jnp.where(i < nu[0], j, nf - 1)

    grid_spec = pltpu.PrefetchScalarGridSpec(
        num_scalar_prefetch=2,
        grid=(P // tm, nf),
        in_specs=[
            pl.BlockSpec((tm, half), row_map),
            pl.BlockSpec((1, D, tf), lambda i, j, te, nu: (te[i], 0, ff(i, j, nu))),
            pl.BlockSpec((1, D, tf), lambda i, j, te, nu: (te[i], 0, ff(i, j, nu))),
            pl.BlockSpec((1, tf, D), lambda i, j, te, nu: (te[i], ff(i, j, nu), 0)),
        ],
        out_specs=pl.BlockSpec((tm, D), lambda i, j, te, nu: (i, 0)),
        scratch_shapes=[pltpu.VMEM((tm, D), bf16), pltpu.VMEM((tm, D), f32)],
    )
    return pl.pallas_call(
        _gmm_kernel,
        out_shape=jax.ShapeDtypeStruct((P, D), f32),
        grid_spec=grid_spec,
        compiler_params=_cparams("arbitrary", "arbitrary"),
        name="moe_gmm",
    )(tile_expert, n_used, xs, wg, wu, wd)


def _combine_kernel(x_ref, gate_ref, y_ref, o_ref):
    gate = gate_ref[...]
    acc = x_ref[...]
    for k in range(TOP_K):
        acc = acc + y_ref[k] * gate[:, k:k + 1]
    o_ref[...] = acc


def _combine(xf, gate_tk, y_tok):
    N, D = xf.shape
    tt = min(1024, N)
    assert N % tt == 0
    return pl.pallas_call(
        _combine_kernel,
        out_shape=jax.ShapeDtypeStruct((N, D), f32),
        grid=(N // tt,),
        in_specs=[pl.BlockSpec((tt, D), lambda i: (i, 0)),
                  pl.BlockSpec((tt, TOP_K), lambda i: (i, 0)),
                  pl.BlockSpec((TOP_K, tt, D), lambda i: (0, i, 0))],
        out_specs=pl.BlockSpec((tt, D), lambda i: (i, 0)),
        compiler_params=_cparams("parallel"),
        name="moe_combine",
    )(xf, gate_tk, y_tok)


def _moe_layer(xf, g, router, wg, wu, wd, first_expert):
    N, D = xf.shape
    E = router.shape[1]
    tm = min(1024, N)
    hp, idx, gate = _router(xf, g, router)
    e_flat = idx.reshape(-1)
    onehot = (e_flat[:, None] == jnp.arange(E, dtype=i32)[None, :]).astype(i32)
    csum = jnp.cumsum(onehot, axis=0)
    counts = csum[-1]
    padded = ((counts + tm - 1) // tm) * tm
    ends = jnp.cumsum(padded)
    starts = ends - padded
    pos = (jnp.sum((csum + starts[None, :]) * onehot, axis=1) - 1).astype(i32)
    P = TOP_K * N + E * tm
    n_used = (ends[-1] // tm).astype(i32)
    tile_start = jnp.minimum(jnp.arange(P // tm, dtype=i32), n_used - 1) * tm
    tile_expert = jnp.sum((ends[None, :] <= tile_start[:, None]).astype(i32), axis=1)
    group_end = (starts + counts)[tile_expert]
    tile_rows = jnp.clip(group_end - tile_start, 0, tm).astype(i32)
    xs = _sc_scatter_rows(hp, pos.reshape(TOP_K, N), P, SC_GATHER_CHUNK)
    ys = _gmm(tile_expert + first_expert, jnp.concatenate([n_used.reshape(1), tile_rows]), xs, wg, wu, wd, tm)
    y_tok = _sc_gather_rows(ys, pos, SC_GATHER_CHUNK).reshape(TOP_K, N, D)
    return _combine(xf, gate.T, y_tok)


def _kv_kernel(x_ref, g_ref, w_ref, gk_ref, seg_ref, segt_ref, kc_ref, vc_ref, ks_ref, vs_ref, kw_ref, vw_ref):
    h = _rms(x_ref[...], g_ref[...]).astype(bf16)
    kv = _dot(h, w_ref[...])
    G, DH = ks_ref.shape[1], ks_ref.shape[3]
    wd = G * DH
    part = lambda p: kv[:, p * wd:(p + 1) * wd]
    seg, segt = seg_ref[...], segt_ref[...]
    for c_ref, pc in ((kc_ref, 0), (vc_ref, 1)):
        z = part(pc)
        for g in range(G):
            c_ref[0, g] = z[:, g * DH:(g + 1) * DH]
    for k_ref, v_ref, pk, gain in ((ks_ref, vs_ref, 2, gk_ref[0:1, :]), (kw_ref, vw_ref, 4, gk_ref[1:2, :])):
        kn = _head_rms(part(pk), gain, seg, segt).astype(bf16)
        v = part(pk + 1)
        for g in range(G):
            k_ref[0, g] = kn[:, g * DH:(g + 1) * DH]
        for t in range(v_ref.shape[2]):
            vt = v[t * KEY_TILE:(t + 1) * KEY_TILE, :].T
            for g in range(G):
                v_ref[0, g, t] = vt[g * DH:(g + 1) * DH].astype(bf16)


def _kv_project(xf, B, S, norm_kv, w_kv, g_k):
    N, D = xf.shape
    G, DH = N_KV_HEADS, HEAD_DIM
    wd = G * DH
    tk = min(512, S)
    spb = S // tk
    tpk = tk // KEY_TILE
    assert S % tk == 0 and tk % KEY_TILE == 0 and w_kv.shape[1] == 2 * N_BRANCH * wd
    seg, segt = _seg_mats(wd)
    gk = jnp.stack([jnp.tile(g_k[1], G), jnp.tile(g_k[2], G)])
    row = lambda i: (i, 0)
    const = lambda i: (0, 0)
    k_shape = jax.ShapeDtypeStruct((B, G, S, DH), bf16)
    v_shape = jax.ShapeDtypeStruct((B, G, S // KEY_TILE, DH, KEY_TILE), bf16)
    k_spec = pl.BlockSpec((1, G, tk, DH), lambda i: (i // spb, 0, i % spb, 0))
    v_spec = pl.BlockSpec((1, G, tpk, DH, KEY_TILE), lambda i: (i // spb, 0, i % spb, 0, 0))
    return pl.pallas_call(
        _kv_kernel,
        out_shape=(jax.ShapeDtypeStruct((B, G, S, DH), f32), jax.ShapeDtypeStruct((B, G, S, DH), f32),
                   k_shape, v_shape, k_shape, v_shape),
        grid=(N // tk,),
        in_specs=[
            pl.BlockSpec((tk, D), row),
            pl.BlockSpec((1, D), const),
            pl.BlockSpec(w_kv.shape, const),
            pl.BlockSpec((2, wd), const),
            pl.BlockSpec((wd, NH_PAD), const),
            pl.BlockSpec((NH_PAD, wd), const),
        ],
        out_specs=(k_spec, k_spec, k_spec, v_spec, k_spec, v_spec),
        compiler_params=_cparams("parallel"),
        name="kv_project",
    )(xf, norm_kv.reshape(1, D), w_kv.astype(bf16), gk, seg, segt)


def _compress_kernel(c_ref, pos_ref, w1_ref, w2_ref, gk_ref, o_ref):
    kv = pl.program_id(0)
    c = c_ref[0, 0, 0]
    a = _dot((c + pos_ref[0, 0:1, :]).astype(bf16), w1_ref[0, 0])
    b = _dot((c + pos_ref[0, 1:2, :]).astype(bf16), w1_ref[0, 1])
    n = c.shape[0]
    hid = a + pltpu.roll(b, n - 1, axis=0)
    out = _dot(_silu(hid).astype(bf16), w2_ref[0])

    @pl.when(kv == 0)
    def _():
        o_ref[0, 0, 0] = _rms(out, gk_ref[...])

    @pl.when(kv != 0)
    def _():
        o_ref[0, 0, 0] = out


def _compress(kc, vc, B, S, cmp_pos, cmp_w1, cmp_w2, g_k0):
    G, DH = N_KV_HEADS, HEAD_DIM
    nch = S // CMP_STRIDE
    cw = CMP_STRIDE * DH

    c = jnp.stack([kc.reshape(B, G, nch, cw), vc.reshape(B, G, nch, cw)])
    pos = cmp_pos.reshape(2, 2, cw)
    w1 = cmp_w1.reshape(2, 2, cw, CMP_HIDDEN).astype(bf16)
    return pl.pallas_call(
        _compress_kernel,
        out_shape=jax.ShapeDtypeStruct((2, B, G, nch, DH), f32),
        grid=(2, B, G),
        in_specs=[
            pl.BlockSpec((1, 1, 1, nch, cw), lambda k, b, g: (k, b, g, 0, 0)),
            pl.BlockSpec((1, 2, cw), lambda k, b, g: (k, 0, 0)),
            pl.BlockSpec((1, 2, cw, CMP_HIDDEN), lambda k, b, g: (k, 0, 0, 0)),
            pl.BlockSpec((1, CMP_HIDDEN, DH), lambda k, b, g: (k, 0, 0)),
            pl.BlockSpec((1, DH), lambda k, b, g: (0, 0)),
        ],
        out_specs=pl.BlockSpec((1, 1, 1, nch, DH), lambda k, b, g: (k, b, g, 0, 0)),
        compiler_params=_cparams("arbitrary", "arbitrary", "arbitrary"),
        name="kv_compress",
    )(c, pos, w1, cmp_w2.astype(bf16), g_k0.reshape(1, DH))


def _build_shared(xf, B, S, norm_kv, w_kv, g_k, cmp_pos, cmp_w1, cmp_w2):
    kc, vc, ks, vst, kw, vwt = _kv_project(xf, B, S, norm_kv, w_kv, g_k)
    cmp = _compress(kc, vc, B, S, cmp_pos, cmp_w1, cmp_w2, g_k[0]).astype(bf16)
    kcm = cmp[0]
    vct = cmp[1].transpose(0, 1, 3, 2)
    return kcm, vct, ks, vst, kw, vwt


def _qg_kernel(x_ref, g_ref, wq_ref, wgate_ref, gq_ref, seg_ref, segt_ref, q_ref, gate_ref):
    h = _rms(x_ref[...], g_ref[...]).astype(bf16)
    q = _head_rms(_dot(h, wq_ref[...]), gq_ref[...], seg_ref[...], segt_ref[...])
    q = q * (HEAD_DIM ** -0.5 * LOG2E)
    gate_ref[...] = jax.nn.sigmoid(_dot(h, wgate_ref[...]))
    nt, G, DH, RT = q_ref.shape
    T = q.shape[0] // nt
    r = RT // T
    for t in range(nt):
        for g in range(G):
            qt = q[t * T:(t + 1) * T, g * r * DH:(g + 1) * r * DH].T
            q_ref[t, g] = jnp.concatenate([qt[k * DH:(k + 1) * DH] for k in range(r)], axis=1).astype(bf16)


def _qg_project(xf, g, w_qg, g_q):
    N, D = xf.shape
    HD = D
    ng = w_qg.shape[1] - HD
    tq = min(512, N)
    G, T = N_KV_HEADS, Q_BLOCK
    RT = HD // (G * HEAD_DIM) * T
    assert N % tq == 0 and tq % T == 0
    seg, segt = _seg_mats(HD)
    row = lambda i: (i, 0)
    const = lambda i: (0, 0)
    return pl.pallas_call(
        _qg_kernel,
        out_shape=(jax.ShapeDtypeStruct((N // T, G, HEAD_DIM, RT), bf16), jax.ShapeDtypeStruct((N, ng), f32)),
        grid=(N // tq,),
        in_specs=[
            pl.BlockSpec((tq, D), row),
            pl.BlockSpec((1, D), const),
            pl.BlockSpec((D, HD), const),
            pl.BlockSpec((D, ng), const),
            pl.BlockSpec((1, HD), const),
            pl.BlockSpec((HD, NH_PAD), const),
            pl.BlockSpec((NH_PAD, HD), const),
        ],
        out_specs=(pl.BlockSpec((tq // T, G, HEAD_DIM, RT), lambda i: (i, 0, 0, 0)), pl.BlockSpec((tq, ng), row)),
        compiler_params=_cparams("parallel"),
        name="qg_project",
    )(xf, g.reshape(1, D), w_qg[:, :HD].astype(bf16), w_qg[:, HD:].astype(bf16),
      jnp.tile(g_q, HD // HEAD_DIM).reshape(1, HD), seg, segt)


def _fold8(x, op):
    return op(x.reshape(x.shape[0] // 8, 8, x.shape[1]), axis=0)


def _nsa_kernel(q_ref, gt_ref, kc_ref, vct_ref, ks_ref, vst_ref, kw_ref, vwt_ref, cbn_ref, cstep_ref, tb_ref, c2st_ref,
                o_ref, cb_scr, pen_ref, penf_ref, s_scr, m8_scr, l8_scr, acc_scr, part_scr, *, r, nsb, n_sel):
    qb = pl.program_id(1)
    t0 = qb * Q_BLOCK
    T = Q_BLOCK
    RT = r * T
    G = q_ref.shape[1]
    nch = kc_ref.shape[2]
    nkt = s_scr.shape[1] // KEY_TILE - SLC_UNROLL

    def tile_rows(kt):
        return pl.ds(pl.multiple_of(kt * KEY_TILE, KEY_TILE), KEY_TILE)

    def add_block_pen(s, ref, g, kt):
        return jnp.concatenate([s[:SLC_LEN] + ref[g, pl.ds(2 * kt, 1), :],
                                s[SLC_LEN:] + ref[g, pl.ds(2 * kt + 1, 1), :]], axis=0)

    def group_phases(g):
        qT = q_ref[0, g]

        cb_scr[g] = cstep_ref[g, pl.ds(pl.multiple_of(nch - (qb + 1) * CMP_PER_Q, 8), CMP_PAD + nch), :]
        cb_scr[g, pl.ds(pl.multiple_of(qb * CMP_PER_Q, 8), CMP_NEAR), :] = cbn_ref[g]
        s = _dot(kc_ref[0, g], qT) + cb_scr[g, CMP_PAD:CMP_PAD + nch, :]
        yield
        p = jnp.where(s > 0.5 * NEG, jnp.exp2(s - jnp.max(s, axis=0, keepdims=True)), 0.0)
        p = p * (1.0 / jnp.maximum(jnp.sum(p, axis=0, keepdims=True), TINY))
        o_cmp = _dot(vct_ref[0, g], p.astype(bf16))
        yield

        psum = p[:, 0:T]
        for k in range(1, r):
            psum = psum + p[:, k * T:(k + 1) * T]
        hi = psum.astype(bf16)
        lo = (psum - hi.astype(f32)).astype(bf16)
        imp = _dot(c2st_ref[...], hi) + _dot(c2st_ref[...], lo)
        jb = lax.broadcasted_iota(i32, (nsb, T), 0)
        blk_q = jnp.right_shift(t0 + lax.broadcasted_iota(i32, (nsb, T), 1), SLC_LEN.bit_length() - 1)
        forced = (jb == 0) | (jb == blk_q) | (jb == blk_q - 1)
        score = jnp.where(forced, FORCED_SCORE, jnp.where(jb <= blk_q, imp, NEG))
        pen = jnp.full((nsb, T), NEG, f32)
        for _ in range(n_sel):
            mx = jnp.max(score, axis=0, keepdims=True)
            first = jnp.min(jnp.where(score == mx, jb, nsb), axis=0, keepdims=True)
            hit = jb == first
            pen = jnp.where(hit, 0.0, pen)
            score = jnp.where(hit, -jnp.inf, score)
        pen = jnp.concatenate([pen] * r, axis=1)
        pen_ref[g] = pen
        penf_ref[g] = pen + cstep_ref[g, 0:1, :]
        yield

        win = []
        for d in range(WINDOW // KEY_TILE, -1, -1):
            kt = qb - d
            ktc = jnp.maximum(kt, 0)
            tab = tb_ref[g, jnp.where(kt >= 0, d, TB_NONE)]
            win.append((_dot(kw_ref[0, g, tile_rows(ktc), :], qT) + tab, vwt_ref[0, g, ktc]))
        yield
        m8 = _fold8(win[0][0], jnp.max)
        for s_d, _ in win[1:]:
            m8 = jnp.maximum(m8, _fold8(s_d, jnp.max))
        m = jnp.max(m8, axis=0, keepdims=True)
        l8 = jnp.zeros((8, RT), f32)
        acc = jnp.zeros((HEAD_DIM, RT), f32)
        for s_d, v_d in win:
            p_d = jnp.exp2(s_d - m)
            l8 = l8 + _fold8(p_d, jnp.sum)
            acc = acc + _dot(v_d, p_d.astype(bf16))
        o_win = acc * (1.0 / jnp.maximum(jnp.sum(l8, axis=0, keepdims=True), TINY))

        gt = gt_ref[0, g]
        part_scr[g] = gt[0:1] * o_cmp + gt[2:3] * o_win
        yield

        ktp = jnp.maximum(qb - 1, 0)
        s_prev = add_block_pen(_dot(ks_ref[0, g, tile_rows(ktp), :], qT)
                               + tb_ref[g, jnp.where(qb >= 1, TB_PREV, TB_NONE)], pen_ref, g, ktp)
        s_scr[g, tile_rows(ktp), :] = s_prev
        s_own = add_block_pen(_dot(ks_ref[0, g, tile_rows(qb), :], qT) + tb_ref[g, TB_OWN], pen_ref, g, qb)
        s_scr[g, tile_rows(qb), :] = s_own
        m8_scr[g] = jnp.maximum(_fold8(s_prev, jnp.max), _fold8(s_own, jnp.max))
        l8_scr[g] = jnp.zeros((8, RT), f32)
        acc_scr[g] = jnp.zeros((HEAD_DIM, RT), f32)

    groups = [group_phases(g) for g in range(G)]
    while groups:
        groups = [phases for phases in groups if next(phases, False) is None]

    def pass_a(i, c):
        for g in range(G):
            qT = q_ref[0, g]
            m8 = m8_scr[g]
            for u in range(SLC_UNROLL):
                kt = SLC_UNROLL * i + u
                live = kt < qb - 1
                ktc = jnp.minimum(kt, nkt - 1)
                s = add_block_pen(_dot(ks_ref[0, g, tile_rows(ktc), :], qT) + jnp.where(live, 0.0, NEG),
                                  penf_ref, g, ktc)
                s_scr[g, tile_rows(jnp.where(live, kt, nkt + u)), :] = s
                m8 = jnp.maximum(m8, _fold8(s, jnp.max))
            m8_scr[g] = m8
        return c

    lax.fori_loop(0, (jnp.maximum(qb - 1, 0) + SLC_UNROLL - 1) // SLC_UNROLL, pass_a, 0)

    def pass_b(i, c):
        for g in range(G):
            m = jnp.max(m8_scr[g], axis=0, keepdims=True)
            l8 = l8_scr[g]
            acc = acc_scr[g]
            for u in range(SLC_UNROLL):
                kt = SLC_UNROLL * i + u
                ktc = jnp.minimum(kt, qb)
                p_u = jnp.exp2(s_scr[g, tile_rows(ktc), :] - (m + jnp.where(kt <= qb, 0.0, -NEG)))
                l8 = l8 + _fold8(p_u, jnp.sum)
                acc = acc + _dot(vst_ref[0, g, ktc], p_u.astype(bf16))
            l8_scr[g] = l8
            acc_scr[g] = acc
        return c

    lax.fori_loop(0, qb // SLC_UNROLL + 1, pass_b, 0)

    for g in range(G):
        o_slc = acc_scr[g] * (1.0 / jnp.maximum(jnp.sum(l8_scr[g], axis=0, keepdims=True), TINY))
        o_ref[0, g] = (part_scr[g] + gt_ref[0, g][1:2] * o_slc).astype(bf16)


def _bias_tables(rel_bias, S):
    G = N_KV_HEADS
    H = rel_bias.shape[1]
    r = H // G
    T = KEY_TILE
    n = jnp.arange(S + 2 * T, dtype=i32)
    max_exact = N_BUCKETS // 2
    nf = jnp.maximum(n, 1).astype(f32)
    large = max_exact + (jnp.log(nf / max_exact) / math.log(MAX_DISTANCE / max_exact)
                         * (N_BUCKETS - max_exact)).astype(i32)
    bucket = jnp.where(n < max_exact, n, jnp.minimum(large, N_BUCKETS - 1))
    bias1d = rel_bias.astype(f32)[bucket] * LOG2E

    def per_group(tab):
        lead = tab.shape[:-2]
        k = len(lead)
        t = jnp.moveaxis(tab, -1, 0).reshape((G, r) + lead + (T,))
        return jnp.moveaxis(t, 1, k + 1).reshape((G,) + lead + (r * T,))

    def masked(dist, ok):
        return jnp.where(jnp.asarray(ok)[..., None], bias1d[np.maximum(dist, 0)], NEG)

    def toeplitz(first):
        w = bias1d[np.maximum(first - (T - 1) + np.arange(2 * T), 0)]
        skew = jnp.broadcast_to(w[None], (T, 2 * T, H)).reshape(2 * T * T, H)[:T * (2 * T - 1)]
        return skew.reshape(T, 2 * T - 1, H)[:, T - 1:]

    kj, qi = np.arange(T)[:, None], np.arange(T)[None, :]
    ok = [qi - kj >= 0, np.ones((T, T), bool), 2 * T + qi - kj < WINDOW]
    tb = jnp.stack([jnp.where(jnp.asarray(ok[d])[..., None], toeplitz(d * T), NEG) for d in range(3)]
                   + [jnp.full((T, T, H), NEG, f32)])
    cend = (np.arange(CMP_NEAR)[:, None] - CMP_PAD) * CMP_STRIDE + CMP_LEN - 1
    dcn = np.arange(T)[None, :] - cend
    rows = CMP_PAD + S // CMP_STRIDE
    far = jnp.broadcast_to(bias1d[-1][None, None, :], (rows, T, H))
    cstep = jnp.concatenate([far, jnp.full((rows, T, H), NEG, f32)])
    return per_group(tb), per_group(masked(dcn, dcn >= 0)), per_group(cstep)


def _nsa_attention(q, gates_t, shared, tables, B, S):
    kcm, vct, ks, vst, kw, vwt = shared
    tb, cbn, cstep = tables
    G, DH, T = N_KV_HEADS, HEAD_DIM, Q_BLOCK
    RT = q.shape[3]
    r = RT // T
    nqb = S // T
    nch = S // CMP_STRIDE
    nsb = S // SLC_LEN
    nkt = S // KEY_TILE
    cmp_start = np.arange(nch) * CMP_STRIDE
    slc_start = np.arange(nsb) * SLC_LEN
    overlap = np.clip(np.minimum(cmp_start[:, None] + CMP_LEN, slc_start[None, :] + SLC_LEN)
                      - np.maximum(cmp_start[:, None], slc_start[None, :]), 0, None) / CMP_LEN
    overlap[nch - 1] = 0.0
    c2st = jnp.asarray(overlap.T, bf16)
    qmap = lambda b, i: (b * nqb + i, 0, 0, 0)
    bat = lambda b, i: (b, 0, 0, 0)
    bat5 = lambda b, i: (b, 0, 0, 0, 0)
    once = pl.Buffered(1)
    return pl.pallas_call(
        functools.partial(_nsa_kernel, r=r, nsb=nsb, n_sel=min(N_SEL, nsb)),
        out_shape=jax.ShapeDtypeStruct(q.shape, bf16),
        grid=(B, nqb),
        in_specs=[
            pl.BlockSpec((1, G, DH, RT), qmap),
            pl.BlockSpec((1, G, N_BRANCH, RT), qmap),
            pl.BlockSpec((1, G, nch, DH), bat),
            pl.BlockSpec((1, G, DH, nch), bat),
            pl.BlockSpec((1, G, S, DH), bat),
            pl.BlockSpec((1, G, nkt, DH, KEY_TILE), bat5),
            pl.BlockSpec((1, G, S, DH), bat),
            pl.BlockSpec((1, G, nkt, DH, KEY_TILE), bat5),
            pl.BlockSpec(cbn.shape, lambda b, i: (0, 0, 0), pipeline_mode=once),
            pl.BlockSpec(cstep.shape, lambda b, i: (0, 0, 0), pipeline_mode=once),
            pl.BlockSpec(tb.shape, lambda b, i: (0, 0, 0, 0), pipeline_mode=once),
            pl.BlockSpec((nsb, nch), lambda b, i: (0, 0), pipeline_mode=once),
        ],
        out_specs=pl.BlockSpec((1, G, DH, RT), qmap),
        scratch_shapes=[pltpu.VMEM((G, CMP_PAD + nch, RT), f32), pltpu.VMEM((G, nsb, RT), f32),
                        pltpu.VMEM((G, nsb, RT), f32), pltpu.VMEM((G, (nkt + SLC_UNROLL) * KEY_TILE, RT), f32),
                        pltpu.VMEM((G, 8, RT), f32), pltpu.VMEM((G, 8, RT), f32), pltpu.VMEM((G, DH, RT), f32),
                        pltpu.VMEM((G, DH, RT), f32)],
        compiler_params=_cparams("parallel", "arbitrary"),
        name="nsa_attention",
    )(q, gates_t, kcm, vct, ks, vst, kw, vwt, cbn, cstep, tb, c2st)


def _oproj_kernel(x_ref, a_ref, w_ref, o_ref):
    nt, G, DH, RT = a_ref.shape
    T = x_ref.shape[0] // nt
    r = RT // T
    rows = []
    for t in range(nt):
        cols = []
        for g in range(G):
            a = a_ref[t, g].astype(f32)
            cols.append(jnp.concatenate([a[:, k * T:(k + 1) * T] for k in range(r)], axis=0).T)
        rows.append(jnp.concatenate(cols, axis=1))
    attn = jnp.concatenate(rows, axis=0)
    o_ref[...] = x_ref[...] + _dot(attn.astype(bf16), w_ref[...])


def _out_project(xf, attn, w_o):
    N, D = xf.shape
    nt_all, G, DH, RT = attn.shape
    T = N // nt_all
    to = min(512, N)
    assert N % to == 0 and to % T == 0
    row = lambda i: (i, 0)
    return pl.pallas_call(
        _oproj_kernel,
        out_shape=jax.ShapeDtypeStruct((N, D), f32),
        grid=(N // to,),
        in_specs=[pl.BlockSpec((to, D), row), pl.BlockSpec((to // T, G, DH, RT), lambda i: (i, 0, 0, 0)),
                  pl.BlockSpec(w_o.shape, lambda i: (0, 0))],
        out_specs=pl.BlockSpec((to, D), row),
        compiler_params=_cparams("parallel"),
        name="out_project",
    )(xf, attn, w_o.astype(bf16))


def _nsa_layer(xf, B, S, g, w_qg, g_q, w_o, shared, tables):
    N = xf.shape[0]
    G, T = N_KV_HEADS, Q_BLOCK
    q, gates = _qg_project(xf, g, w_qg, g_q)
    r = gates.shape[1] // (G * N_BRANCH)
    gates_t = gates.reshape(N // T, T, G, r, N_BRANCH).transpose(0, 2, 4, 3, 1).reshape(N // T, G, N_BRANCH, r * T)
    attn = _nsa_attention(q, gates_t, shared, tables, B, S)
    return _out_project(xf, attn, w_o)


def kernel(x, rel_bias, norm_mix, norm_ffn, pool_w, pool_scale, norm_kv, w_kv, g_k, cmp_pos, cmp_w1, cmp_w2,
           w_qg, g_q, w_o, ffn_wg, ffn_wu, ffn_wd, router, moe_wg, moe_wu, moe_wd):
    B, S, D = x.shape
    depth = norm_mix.shape[0]
    n_a = depth // 2
    assert S % Q_BLOCK == 0
    xf = x.reshape(B * S, D)
    shared = None
    tables = _bias_tables(rel_bias, S)
    n_exp = moe_wg.shape[1]
    moe_w = [w.reshape((-1,) + w.shape[2:]).astype(bf16) for w in (moe_wg, moe_wu, moe_wd)]
    for layer in range(depth):
        if layer < n_a:
            xf = _pool_layer(xf.reshape(B, S, D), norm_mix[layer], pool_w[layer], pool_scale[layer]).reshape(B * S, D)
        else:
            j = layer - n_a
            xf = _nsa_layer(xf, B, S, norm_mix[layer], w_qg[j], g_q[j], w_o[j], shared, tables)
        i = layer // 2
        if layer % 2 == 0:
            xf = _ffn_layer(xf, norm_ffn[layer], ffn_wg[i], ffn_wu[i], ffn_wd[i])
        else:
            xf = _moe_layer(xf, norm_ffn[layer], router[i], *moe_w, first_expert=i * n_exp)
        if layer == n_a - 1:
            shared = _build_shared(xf, B, S, norm_kv, w_kv, g_k, cmp_pos, cmp_w1, cmp_w2)
    return xf.reshape(B, S, D)
```

```python
import functools
import math

import numpy as np
import jax
import jax.numpy as jnp
from jax import lax
from jax.experimental import pallas as pl
from jax.experimental.pallas import tpu as pltpu
from jax.experimental.pallas import tpu_sc as plsc

f32 = jnp.float32
bf16 = jnp.bfloat16
i32 = jnp.int32
u32 = jnp.uint32

POOL_WINDOWS = (2, 4, 8, 16)
HEAD_DIM = 64
N_KV_HEADS = 4
N_BRANCH = 3
CMP_LEN = 32
CMP_STRIDE = 16
CMP_HIDDEN = 4 * HEAD_DIM
SLC_LEN = 64
N_SEL = 4
WINDOW = 256
Q_BLOCK = 128
FORCED_SCORE = 1.0e4
N_BUCKETS = 32
MAX_DISTANCE = 128
TOP_K = 2
EPS = 1e-6
NEG = -1e30
TINY = 1e-30
LOG2E = math.log2(math.e)
TB_OWN, TB_PREV, TB_WIN2, TB_NONE = range(4)

KEY_TILE = Q_BLOCK
SLC_UNROLL = 4
POOL_HALO = 16
NH_PAD = 16
V7X_VMEM_LIMIT = 56 * 1024 * 1024
V7X_SC_CORES, V7X_SC_SUBCORES = 2, 16
SC_MAX_INDEX_VECTOR = 128
SC_GATHER_CHUNK = 32

CMP_PER_Q = Q_BLOCK // CMP_STRIDE
CMP_PAD = 2 * CMP_PER_Q
CMP_NEAR = 3 * CMP_PER_Q

assert CMP_LEN == 2 * CMP_STRIDE and KEY_TILE == 2 * SLC_LEN and WINDOW == 2 * KEY_TILE
assert max(POOL_WINDOWS) <= POOL_HALO
assert 2 * KEY_TILE - (Q_BLOCK - 1) >= MAX_DISTANCE
assert (CMP_PAD + 1) * CMP_STRIDE - (CMP_LEN - 1) >= MAX_DISTANCE and CMP_PER_Q % 8 == 0


def _cparams(*sem):
    return pltpu.CompilerParams(dimension_semantics=sem, vmem_limit_bytes=V7X_VMEM_LIMIT)


def _rms(xf, g):
    ms = jnp.mean(xf * xf, axis=-1, keepdims=True)
    return (xf * lax.rsqrt(ms + EPS)) * g


def _dot(a, b):
    return jnp.dot(a, b, preferred_element_type=f32)


def _dot_hilo(a, b):
    hi = a.astype(bf16)
    lo = (a - hi.astype(f32)).astype(bf16)
    return _dot(hi, b) + _dot(lo, b)


def _head_rms(z, gvec, seg, segt):
    ssq = _dot_hilo(z * z, seg)
    inv = lax.rsqrt(ssq * (1.0 / HEAD_DIM) + EPS)
    return (z * _dot_hilo(inv, segt)) * gvec


def _silu(a):
    return a * jax.nn.sigmoid(a)


def _seg_mats(width):
    heads = width // HEAD_DIM
    seg = np.zeros((width, NH_PAD), np.float32)
    seg[np.arange(width), np.arange(width) // HEAD_DIM] = 1.0
    assert heads <= NH_PAD
    return jnp.asarray(seg, bf16), jnp.asarray(seg.T, bf16)


def _pool_kernel(x_ref, halo_ref, g_ref, w_ref, scale_ref, o_ref, *, tp, cg):
    i = pl.program_id(1)
    x = x_ref[0]
    xh = jnp.concatenate([halo_ref[0], x], axis=0)
    h = _rms(xh, g_ref[...])
    row = lax.broadcasted_iota(i32, (tp + POOL_HALO, 1), 0)
    t_abs = i * tp + row - POOL_HALO
    h = jnp.where(t_abs >= 0, h, 0.0)
    outs = []
    for gi, w in enumerate(POOL_WINDOWS):
        hg = h[:, gi * cg:(gi + 1) * cg]
        s = hg
        sh = 1
        while sh < w:
            s = s + pltpu.roll(s, sh, axis=0)
            sh *= 2
        cnt = jnp.clip(t_abs + 1, 1, w).astype(f32)
        diff = (s / cnt - hg)[POOL_HALO:]
        outs.append(_dot(diff.astype(bf16), w_ref[gi]))
    y = jnp.concatenate(outs, axis=1)
    o_ref[0] = x + y * scale_ref[...]


def _pool_layer(x3, g, w_grp, scale):
    B, S, D = x3.shape
    tp = min(512, S)
    cg = D // len(POOL_WINDOWS)
    assert S % tp == 0 and tp % POOL_HALO == 0 and all(w & (w - 1) == 0 for w in POOL_WINDOWS)
    hb = tp // POOL_HALO
    return pl.pallas_call(
        functools.partial(_pool_kernel, tp=tp, cg=cg),
        out_shape=jax.ShapeDtypeStruct((B, S, D), f32),
        grid=(B, S // tp),
        in_specs=[
            pl.BlockSpec((1, tp, D), lambda b, i: (b, i, 0)),
            pl.BlockSpec((1, POOL_HALO, D), lambda b, i: (b, jnp.maximum(i * hb - 1, 0), 0)),
            pl.BlockSpec((1, D), lambda b, i: (0, 0)),
            pl.BlockSpec((len(POOL_WINDOWS), cg, cg), lambda b, i: (0, 0, 0)),
            pl.BlockSpec((1, D), lambda b, i: (0, 0)),
        ],
        out_specs=pl.BlockSpec((1, tp, D), lambda b, i: (b, i, 0)),
        compiler_params=_cparams("parallel", "arbitrary"),
        name="pool_layer",
    )(x3, x3, g.reshape(1, D), w_grp.astype(bf16), scale.reshape(1, D))


def _ffn_kernel(x_ref, g_ref, wg_ref, wu_ref, wd_ref, o_ref, h_ref, acc_ref):
    j = pl.program_id(1)

    @pl.when(j == 0)
    def _():
        x = x_ref[...]
        h_ref[...] = _rms(x, g_ref[...]).astype(bf16)
        acc_ref[...] = x

    h = h_ref[...]
    act = _silu(_dot(h, wg_ref[...])) * _dot(h, wu_ref[...])
    acc_ref[...] += _dot(act.astype(bf16), wd_ref[...])

    @pl.when(j == pl.num_programs(1) - 1)
    def _():
        o_ref[...] = acc_ref[...]


def _ffn_layer(xf, g, wg, wu, wd):
    N, D = xf.shape
    F = wg.shape[1]
    tm = min(1024, N)
    tf = 512
    assert N % tm == 0 and F % tf == 0
    return pl.pallas_call(
        _ffn_kernel,
        out_shape=jax.ShapeDtypeStruct((N, D), f32),
        grid=(N // tm, F // tf),
        in_specs=[
            pl.BlockSpec((tm, D), lambda i, j: (i, 0)),
            pl.BlockSpec((1, D), lambda i, j: (0, 0)),
            pl.BlockSpec((D, tf), lambda i, j: (0, j)),
            pl.BlockSpec((D, tf), lambda i, j: (0, j)),
            pl.BlockSpec((tf, D), lambda i, j: (j, 0)),
        ],
        out_specs=pl.BlockSpec((tm, D), lambda i, j: (i, 0)),
        scratch_shapes=[pltpu.VMEM((tm, D), bf16), pltpu.VMEM((tm, D), f32)],
        compiler_params=_cparams("parallel", "arbitrary"),
        name="ffn_dense",
    )(xf, g.reshape(1, D), wg.astype(bf16), wu.astype(bf16), wd.astype(bf16))


def _router_kernel(x_ref, g_ref, rt_ref, hp_ref, idx_ref, gate_ref):
    h = _rms(x_ref[...], g_ref[...])
    half = h.shape[1] // 2
    h_hi = h.astype(bf16)
    bits = pltpu.bitcast(h_hi.astype(f32), u32)
    hp_ref[...] = pltpu.bitcast((bits[:, :half] & jnp.uint32(0xFFFF0000)) | (bits[:, half:] >> 16), i32)
    h_lo = (h - h_hi.astype(f32)).astype(bf16)
    rt = rt_ref[...]
    rt_hi = rt.astype(bf16)
    rt_lo = (rt - rt_hi.astype(f32)).astype(bf16)
    nt_dot = lambda a, b: lax.dot_general(a, b, (((1,), (1,)), ((), ())), preferred_element_type=f32)
    logits = nt_dot(rt_hi, h_hi) + nt_dot(rt_hi, h_lo) + nt_dot(rt_lo, h_hi)
    ne = logits.shape[0]
    row = lax.broadcasted_iota(i32, logits.shape, 0)
    m1 = jnp.max(logits, axis=0, keepdims=True)
    i1 = jnp.min(jnp.where(logits == m1, row, ne), axis=0, keepdims=True)
    rest = jnp.where(row == i1, -jnp.inf, logits)
    m2 = jnp.max(rest, axis=0, keepdims=True)
    i2 = jnp.min(jnp.where(rest == m2, row, ne), axis=0, keepdims=True)
    e2 = jnp.exp(m2 - m1)
    den = 1.0 + e2
    idx_ref[...] = jnp.concatenate([i1, i2], axis=0)
    gate_ref[...] = jnp.concatenate([1.0 / den, e2 / den], axis=0)


def _router(xf, g, router):
    N, D = xf.shape
    E = router.shape[1]
    tr = min(1024, N)
    assert N % tr == 0 and TOP_K == 2
    return pl.pallas_call(
        _router_kernel,
        out_shape=(jax.ShapeDtypeStruct((N, D // 2), i32),
                   jax.ShapeDtypeStruct((TOP_K, N), i32),
                   jax.ShapeDtypeStruct((TOP_K, N), f32)),
        grid=(N // tr,),
        in_specs=[
            pl.BlockSpec((tr, D), lambda i: (i, 0)),
            pl.BlockSpec((1, D), lambda i: (0, 0)),
            pl.BlockSpec((E, D), lambda i: (0, 0)),
        ],
        out_specs=(pl.BlockSpec((tr, D // 2), lambda i: (i, 0)),
                   pl.BlockSpec((TOP_K, tr), lambda i: (0, i)),
                   pl.BlockSpec((TOP_K, tr), lambda i: (0, i))),
        compiler_params=_cparams("parallel"),
        name="moe_router",
    )(xf, g.reshape(1, D), router.T)


def _sc_gather_rows(table, idx, chunk):
    B = idx.shape[0]
    D = table.shape[1]
    workers = V7X_SC_CORES * V7X_SC_SUBCORES
    per_w = B // workers
    cpw = per_w // chunk
    assert B % (8 * workers) == 0 and per_w % (2 * chunk) == 0 and cpw % 8 == 0
    assert chunk % 8 == 0 and chunk <= SC_MAX_INDEX_VECTOR
    mesh = plsc.VectorSubcoreMesh(core_axis_name="c", subcore_axis_name="s")

    @functools.partial(
        pl.kernel, mesh=mesh, out_type=jax.ShapeDtypeStruct((B, D), table.dtype),
        scratch_types=[pltpu.VMEM((cpw, chunk), i32), pltpu.VMEM((2, chunk, D), table.dtype),
                       pltpu.SemaphoreType.DMA, pltpu.SemaphoreType.DMA])
    def gather(table_hbm, idx_hbm, out_hbm, idx_v, rows_v, sem0, sem1):
        wid = lax.axis_index("s") * V7X_SC_CORES + lax.axis_index("c")
        pltpu.sync_copy(idx_hbm.at[pl.ds(pl.multiple_of(wid * cpw, 8), cpw)], idx_v)
        sems = (sem0, sem1)

        def fetch(c, b):
            return pltpu.make_async_copy(table_hbm.at[idx_v.at[c]], rows_v.at[b], sems[b])

        fetch(0, 0).start()

        @pl.loop(0, cpw, step=2)
        def _(c):
            for b in range(2):
                cur = c + b
                fetch(cur, b).wait()

                @pl.when(cur + 1 < cpw)
                def _():
                    fetch(cur + 1, 1 - b).start()

                pltpu.sync_copy(rows_v.at[b], out_hbm.at[pl.ds(pl.multiple_of(wid * per_w + cur * chunk, 8), chunk)])

    return gather(table, idx.reshape(B // chunk, chunk))


def _sc_scatter_rows(src, slots, n_slots, chunk):
    K, N = slots.shape
    D = src.shape[1]
    workers = V7X_SC_CORES * V7X_SC_SUBCORES
    per_w = N // workers
    cpw = per_w // chunk
    assert N % (8 * workers) == 0 and per_w % (2 * chunk) == 0 and cpw % 8 == 0
    assert chunk % 8 == 0 and chunk <= SC_MAX_INDEX_VECTOR
    mesh = plsc.VectorSubcoreMesh(core_axis_name="c", subcore_axis_name="s")

    @functools.partial(
        pl.kernel, mesh=mesh, out_type=jax.ShapeDtypeStruct((n_slots, D), src.dtype),
        scratch_types=[pltpu.VMEM((K, cpw, chunk), i32), pltpu.VMEM((2, chunk, D), src.dtype),
                       pltpu.SemaphoreType.DMA, pltpu.SemaphoreType.DMA, pltpu.SemaphoreType.DMA,
                       pltpu.SemaphoreType.DMA])
    def scatter(src_hbm, slots_hbm, out_hbm, slot_v, rows_v, lsem0, lsem1, ssem0, ssem1):
        wid = lax.axis_index("s") * V7X_SC_CORES + lax.axis_index("c")
        for k in range(K):
            pltpu.sync_copy(slots_hbm.at[k, pl.ds(pl.multiple_of(wid * cpw, 8), cpw)], slot_v.at[k])
        lsems, ssems = (lsem0, lsem1), (ssem0, ssem1)

        def load(c, b):
            return pltpu.make_async_copy(
                src_hbm.at[pl.ds(pl.multiple_of(wid * per_w + c * chunk, 8), chunk)], rows_v.at[b], lsems[b])

        def store(c, b, k):
            return pltpu.make_async_copy(rows_v.at[b], out_hbm.at[slot_v.at[k].at[c]], ssems[b])

        load(0, 0).start()

        @pl.loop(0, cpw, step=2)
        def _(c):
            for b in range(2):
                cur = c + b
                load(cur, b).wait()
                for k in range(K):
                    store(cur, b, k).start()

                @pl.when(cur >= 1)
                def _():
                    for k in range(K):
                        store(cur - 1, 1 - b, k).wait()

                @pl.when(cur + 1 < cpw)
                def _():
                    load(cur + 1, 1 - b).start()

        for k in range(K):
            store(cpw - 1, 1, k).wait()

    return scatter(src, slots.reshape(K, N // chunk, chunk))


def _gmm_kernel(te_ref, nu_ref, xs_ref, wg_ref, wu_ref, wd_ref, o_ref, h_ref, acc_ref):
    i = pl.program_id(0)
    j = pl.program_id(1)

    @pl.when(i < nu_ref[0])
    def _():
        @pl.when(j == 0)
        def _():
            w = pltpu.bitcast(xs_ref[...], u32)
            left = pltpu.bitcast(w & jnp.uint32(0xFFFF0000), f32)
            right = pltpu.bitcast(w << 16, f32)
            h = jnp.concatenate([left, right], axis=1)
            filled = lax.broadcasted_iota(i32, (h.shape[0], 1), 0) < nu_ref[1 + i]
            h_ref[...] = jnp.where(filled, h, 0.0).astype(bf16)
            acc_ref[...] = jnp.zeros_like(acc_ref)

        h = h_ref[...]
        act = _silu(_dot(h, wg_ref[0])) * _dot(h, wu_ref[0])
        acc_ref[...] += _dot(act.astype(bf16), wd_ref[0])

        @pl.when(j == pl.num_programs(1) - 1)
        def _():
            o_ref[...] = acc_ref[...]

    @pl.when((i >= nu_ref[0]) & (j == 0))
    def _():
        o_ref[...] = jnp.zeros_like(o_ref)


def _gmm(tile_expert, n_used, xs, wg, wu, wd, tm):
    P, half = xs.shape
    D = 2 * half
    F = wg.shape[2]
    tf = 512
    assert P % tm == 0 and F % tf == 0

    nf = F // tf

    def row_map(i, j, te, nu):
        return (jnp.minimum(i, nu[0] - 1), 0)

    def ff(i, j, nu):
        return jnp.where(i < nu[0], j, nf - 1)

    grid_spec = pltpu.PrefetchScalarGridSpec(
        num_scalar_prefetch=2,
        grid=(P // tm, nf),
        in_specs=[
            pl.BlockSpec((tm, half), row_map),
            pl.BlockSpec((1, D, tf), lambda i, j, te, nu: (te[i], 0, ff(i, j, nu))),
            pl.BlockSpec((1, D, tf), lambda i, j, te, nu: (te[i], 0, ff(i, j, nu))),
            pl.BlockSpec((1, tf, D), lambda i, j, te, nu: (te[i], ff(i, j, nu), 0)),
        ],
        out_specs=pl.BlockSpec((tm, D), lambda i, j, te, nu: (i, 0)),
        scratch_shapes=[pltpu.VMEM((tm, D), bf16), pltpu.VMEM((tm, D), f32)],
    )
    return pl.pallas_call(
        _gmm_kernel,
        out_shape=jax.ShapeDtypeStruct((P, D), f32),
        grid_spec=grid_spec,
        compiler_params=_cparams("arbitrary", "arbitrary"),
        name="moe_gmm",
    )(tile_expert, n_used, xs, wg, wu, wd)


def _combine_kernel(x_ref, gate_ref, y_ref, o_ref):
    gate = gate_ref[...]
    acc = x_ref[...]
    for k in range(TOP_K):
        acc = acc + y_ref[k] * gate[:, k:k + 1]
    o_ref[...] = acc


def _combine(xf, gate_tk, y_tok):
    N, D = xf.shape
    tt = min(1024, N)
    assert N % tt == 0
    return pl.pallas_call(
        _combine_kernel,
        out_shape=jax.ShapeDtypeStruct((N, D), f32),
        grid=(N // tt,),
        in_specs=[pl.BlockSpec((tt, D), lambda i: (i, 0)),
                  pl.BlockSpec((tt, TOP_K), lambda i: (i, 0)),
                  pl.BlockSpec((TOP_K, tt, D), lambda i: (0, i, 0))],
        out_specs=pl.BlockSpec((tt, D), lambda i: (i, 0)),
        compiler_params=_cparams("parallel"),
        name="moe_combine",
    )(xf, gate_tk, y_tok)


def _moe_layer(xf, g, router, wg, wu, wd, first_expert):
    N, D = xf.shape
    E = router.shape[1]
    tm = min(1024, N)
    hp, idx, gate = _router(xf, g, router)
    e_flat = idx.reshape(-1)
    onehot = (e_flat[:, None] == jnp.arange(E, dtype=i32)[None, :]).astype(i32)
    csum = jnp.cumsum(onehot, axis=0)
    counts = csum[-1]
    padded = ((counts + tm - 1) // tm) * tm
    ends = jnp.cumsum(padded)
    starts = ends - padded
    pos = (jnp.sum((csum + starts[None, :]) * onehot, axis=1) - 1).astype(i32)
    P = TOP_K * N + E * tm
    n_used = (ends[-1] // tm).astype(i32)
    tile_start = jnp.minimum(jnp.arange(P // tm, dtype=i32), n_used - 1) * tm
    tile_expert = jnp.sum((ends[None, :] <= tile_start[:, None]).astype(i32), axis=1)
    group_end = (starts + counts)[tile_expert]
    tile_rows = jnp.clip(group_end - tile_start, 0, tm).astype(i32)
    xs = _sc_scatter_rows(hp, pos.reshape(TOP_K, N), P, SC_GATHER_CHUNK)
    ys = _gmm(tile_expert + first_expert, jnp.concatenate([n_used.reshape(1), tile_rows]), xs, wg, wu, wd, tm)
    y_tok = _sc_gather_rows(ys, pos, SC_GATHER_CHUNK).reshape(TOP_K, N, D)
    return _combine(xf, gate.T, y_tok)


def _kv_kernel(x_ref, g_ref, w_ref, gk_ref, seg_ref, segt_ref, kc_ref, vc_ref, ks_ref, vs_ref, kw_ref, vw_ref):
    h = _rms(x_ref[...], g_ref[...]).astype(bf16)
    kv = _dot(h, w_ref[...])
    G, DH = ks_ref.shape[1], ks_ref.shape[3]
    wd = G * DH
    part = lambda p: kv[:, p * wd:(p + 1) * wd]
    seg, segt = seg_ref[...], segt_ref[...]
    for c_ref, pc in ((kc_ref, 0), (vc_ref, 1)):
        z = part(pc)
        for g in range(G):
            c_ref[0, g] = z[:, g * DH:(g + 1) * DH]
    for k_ref, v_ref, pk, gain in ((ks_ref, vs_ref, 2, gk_ref[0:1, :]), (kw_ref, vw_ref, 4, gk_ref[1:2, :])):
        kn = _head_rms(part(pk), gain, seg, segt).astype(bf16)
        v = part(pk + 1)
        for g in range(G):
            k_ref[0, g] = kn[:, g * DH:(g + 1) * DH]
        for t in range(v_ref.shape[2]):
            vt = v[t * KEY_TILE:(t + 1) * KEY_TILE, :].T
            for g in range(G):
                v_ref[0, g, t] = vt[g * DH:(g + 1) * DH].astype(bf16)


def _kv_project(xf, B, S, norm_kv, w_kv, g_k):
    N, D = xf.shape
    G, DH = N_KV_HEADS, HEAD_DIM
    wd = G * DH
    tk = min(512, S)
    spb = S // tk
    tpk = tk // KEY_TILE
    assert S % tk == 0 and tk % KEY_TILE == 0 and w_kv.shape[1] == 2 * N_BRANCH * wd
    seg, segt = _seg_mats(wd)
    gk = jnp.stack([jnp.tile(g_k[1], G), jnp.tile(g_k[2], G)])
    row = lambda i: (i, 0)
    const = lambda i: (0, 0)
    k_shape = jax.ShapeDtypeStruct((B, G, S, DH), bf16)
    v_shape = jax.ShapeDtypeStruct((B, G, S // KEY_TILE, DH, KEY_TILE), bf16)
    k_spec = pl.BlockSpec((1, G, tk, DH), lambda i: (i // spb, 0, i % spb, 0))
    v_spec = pl.BlockSpec((1, G, tpk, DH, KEY_TILE), lambda i: (i // spb, 0, i % spb, 0, 0))
    return pl.pallas_call(
        _kv_kernel,
        out_shape=(jax.ShapeDtypeStruct((B, G, S, DH), f32), jax.ShapeDtypeStruct((B, G, S, DH), f32),
                   k_shape, v_shape, k_shape, v_shape),
        grid=(N // tk,),
        in_specs=[
            pl.BlockSpec((tk, D), row),
            pl.BlockSpec((1, D), const),
            pl.BlockSpec(w_kv.shape, const),
            pl.BlockSpec((2, wd), const),
            pl.BlockSpec((wd, NH_PAD), const),
            pl.BlockSpec((NH_PAD, wd), const),
        ],
        out_specs=(k_spec, k_spec, k_spec, v_spec, k_spec, v_spec),
        compiler_params=_cparams("parallel"),
        name="kv_project",
    )(xf, norm_kv.reshape(1, D), w_kv.astype(bf16), gk, seg, segt)


def _compress_kernel(c_ref, pos_ref, w1_ref, w2_ref, gk_ref, o_ref):
    kv = pl.program_id(0)
    c = c_ref[0, 0, 0]
    a = _dot((c + pos_ref[0, 0:1, :]).astype(bf16), w1_ref[0, 0])
    b = _dot((c + pos_ref[0, 1:2, :]).astype(bf16), w1_ref[0, 1])
    n = c.shape[0]
    hid = a + pltpu.roll(b, n - 1, axis=0)
    out = _dot(_silu(hid).astype(bf16), w2_ref[0])

    @pl.when(kv == 0)
    def _():
        o_ref[0, 0, 0] = _rms(out, gk_ref[...])

    @pl.when(kv != 0)
    def _():
        o_ref[0, 0, 0] = out


def _compress(kc, vc, B, S, cmp_pos, cmp_w1, cmp_w2, g_k0):
    G, DH = N_KV_HEADS, HEAD_DIM
    nch = S // CMP_STRIDE
    cw = CMP_STRIDE * DH

    c = jnp.stack([kc.reshape(B, G, nch, cw), vc.reshape(B, G, nch, cw)])
    pos = cmp_pos.reshape(2, 2, cw)
    w1 = cmp_w1.reshape(2, 2, cw, CMP_HIDDEN).astype(bf16)
    return pl.pallas_call(
        _compress_kernel,
        out_shape=jax.ShapeDtypeStruct((2, B, G, nch, DH), f32),
        grid=(2, B, G),
        in_specs=[
            pl.BlockSpec((1, 1, 1, nch, cw), lambda k, b, g: (k, b, g, 0, 0)),
            pl.BlockSpec((1, 2, cw), lambda k, b, g: (k, 0, 0)),
            pl.BlockSpec((1, 2, cw, CMP_HIDDEN), lambda k, b, g: (k, 0, 0, 0)),
            pl.BlockSpec((1, CMP_HIDDEN, DH), lambda k, b, g: (k, 0, 0)),
            pl.BlockSpec((1, DH), lambda k, b, g: (0, 0)),
        ],
        out_specs=pl.BlockSpec((1, 1, 1, nch, DH), lambda k, b, g: (k, b, g, 0, 0)),
        compiler_params=_cparams("arbitrary", "arbitrary", "arbitrary"),
        name="kv_compress",
    )(c, pos, w1, cmp_w2.astype(bf16), g_k0.reshape(1, DH))


def _build_shared(xf, B, S, norm_kv, w_kv, g_k, cmp_pos, cmp_w1, cmp_w2):
    kc, vc, ks, vst, kw, vwt = _kv_project(xf, B, S, norm_kv, w_kv, g_k)
    cmp = _compress(kc, vc, B, S, cmp_pos, cmp_w1, cmp_w2, g_k[0]).astype(bf16)
    kcm = cmp[0]
    vct = cmp[1].transpose(0, 1, 3, 2)
    return kcm, vct, ks, vst, kw, vwt


def _qg_kernel(x_ref, g_ref, wq_ref, wgate_ref, gq_ref, seg_ref, segt_ref, q_ref, gate_ref):
    h = _rms(x_ref[...], g_ref[...]).astype(bf16)
    q_raw = _dot(h, wq_ref[...])
    gate_ref[...] = jax.nn.sigmoid(_dot(h, wgate_ref[...]))
    nt, G, DH, RT = q_ref.shape
    T = q_raw.shape[0] // nt
    r = RT // T

    def tile_phases(t):
        q = _head_rms(q_raw[t * T:(t + 1) * T], gq_ref[...], seg_ref[...], segt_ref[...])
        q = q * (HEAD_DIM ** -0.5 * LOG2E)
        yield
        for g in range(G):
            qt = q[:, g * r * DH:(g + 1) * r * DH].T
            q_ref[t, g] = jnp.concatenate([qt[k * DH:(k + 1) * DH] for k in range(r)], axis=1).astype(bf16)
            yield

    tiles = [tile_phases(t) for t in range(nt)]
    while tiles:
        tiles = [phases for phases in tiles if next(phases, False) is None]


def _qg_project(xf, g, w_qg, g_q):
    N, D = xf.shape
    HD = D
    ng = w_qg.shape[1] - HD
    tq = min(512, N)
    G, T = N_KV_HEADS, Q_BLOCK
    RT = HD // (G * HEAD_DIM) * T
    assert N % tq == 0 and tq % T == 0
    seg, segt = _seg_mats(HD)
    row = lambda i: (i, 0)
    const = lambda i: (0, 0)
    return pl.pallas_call(
        _qg_kernel,
        out_shape=(jax.ShapeDtypeStruct((N // T, G, HEAD_DIM, RT), bf16), jax.ShapeDtypeStruct((N, ng), f32)),
        grid=(N // tq,),
        in_specs=[
            pl.BlockSpec((tq, D), row),
            pl.BlockSpec((1, D), const),
            pl.BlockSpec((D, HD), const),
            pl.BlockSpec((D, ng), const),
            pl.BlockSpec((1, HD), const),
            pl.BlockSpec((HD, NH_PAD), const),
            pl.BlockSpec((NH_PAD, HD), const),
        ],
        out_specs=(pl.BlockSpec((tq // T, G, HEAD_DIM, RT), lambda i: (i, 0, 0, 0)), pl.BlockSpec((tq, ng), row)),
        compiler_params=_cparams("parallel"),
        name="qg_project",
    )(xf, g.reshape(1, D), w_qg[:, :HD].astype(bf16), w_qg[:, HD:].astype(bf16),
      jnp.tile(g_q, HD // HEAD_DIM).reshape(1, HD), seg, segt)


def _fold8(x, op):
    return op(x.reshape(x.shape[0] // 8, 8, x.shape[1]), axis=0)


def _nsa_kernel(q_ref, gt_ref, kc_ref, vct_ref, ks_ref, vst_ref, kw_ref, vwt_ref, cbn_ref, cstep_ref, tb_ref, c2st_ref,
                o_ref, cb_scr, pen_ref, penf_ref, s_scr, m8_scr, l8_scr, acc_scr, part_scr, *, r, nsb, n_sel):
    qb = pl.program_id(1)
    t0 = qb * Q_BLOCK
    T = Q_BLOCK
    RT = r * T
    G = q_ref.shape[1]
    nch = kc_ref.shape[2]
    nkt = s_scr.shape[1] // KEY_TILE - SLC_UNROLL

    def tile_rows(kt):
        return pl.ds(pl.multiple_of(kt * KEY_TILE, KEY_TILE), KEY_TILE)

    def add_block_pen(s, ref, g, kt):
        return jnp.concatenate([s[:SLC_LEN] + ref[g, pl.ds(2 * kt, 1), :],
                                s[SLC_LEN:] + ref[g, pl.ds(2 * kt + 1, 1), :]], axis=0)

    def group_phases(g):
        qT = q_ref[0, g]

        cb_scr[g] = cstep_ref[g, pl.ds(pl.multiple_of(nch - (qb + 1) * CMP_PER_Q, 8), CMP_PAD + nch), :]
        cb_scr[g, pl.ds(pl.multiple_of(qb * CMP_PER_Q, 8), CMP_NEAR), :] = cbn_ref[g]
        s = _dot(kc_ref[0, g], qT) + cb_scr[g, CMP_PAD:CMP_PAD + nch, :]
        yield
        p = jnp.where(s > 0.5 * NEG, jnp.exp2(s - jnp.max(s, axis=0, keepdims=True)), 0.0)
        p = p * (1.0 / jnp.maximum(jnp.sum(p, axis=0, keepdims=True), TINY))
        o_cmp = _dot(vct_ref[0, g], p.astype(bf16))
        yield

        psum = p[:, 0:T]
        for k in range(1, r):
            psum = psum + p[:, k * T:(k + 1) * T]
        hi = psum.astype(bf16)
        lo = (psum - hi.astype(f32)).astype(bf16)
        imp = _dot(c2st_ref[...], hi) + _dot(c2st_ref[...], lo)
        jb = lax.broadcasted_iota(i32, (nsb, T), 0)
        blk_q = jnp.right_shift(t0 + lax.broadcasted_iota(i32, (nsb, T), 1), SLC_LEN.bit_length() - 1)
        forced = (jb == 0) | (jb == blk_q) | (jb == blk_q - 1)
        score = jnp.where(forced, FORCED_SCORE, jnp.where(jb <= blk_q, imp, NEG))
        pen = jnp.full((nsb, T), NEG, f32)
        for _ in range(n_sel):
            mx = jnp.max(score, axis=0, keepdims=True)
            first = jnp.min(jnp.where(score == mx, jb, nsb), axis=0, keepdims=True)
            hit = jb == first
            pen = jnp.where(hit, 0.0, pen)
            score = jnp.where(hit, -jnp.inf, score)
        pen = jnp.concatenate([pen] * r, axis=1)
        pen_ref[g] = pen
        penf_ref[g] = pen + cstep_ref[g, 0:1, :]
        yield

        win = []
        for d in range(WINDOW // KEY_TILE, -1, -1):
            kt = qb - d
            ktc = jnp.maximum(kt, 0)
            tab = tb_ref[g, jnp.where(kt >= 0, d, TB_NONE)]
            win.append((_dot(kw_ref[0, g, tile_rows(ktc), :], qT) + tab, vwt_ref[0, g, ktc]))
        yield
        m8 = _fold8(win[0][0], jnp.max)
        for s_d, _ in win[1:]:
            m8 = jnp.maximum(m8, _fold8(s_d, jnp.max))
        m = jnp.max(m8, axis=0, keepdims=True)
        l8 = jnp.zeros((8, RT), f32)
        acc = jnp.zeros((HEAD_DIM, RT), f32)
        for s_d, v_d in win:
            p_d = jnp.exp2(s_d - m)
            l8 = l8 + _fold8(p_d, jnp.sum)
            acc = acc + _dot(v_d, p_d.astype(bf16))
        o_win = acc * (1.0 / jnp.maximum(jnp.sum(l8, axis=0, keepdims=True), TINY))

        gt = gt_ref[0, g]
        part_scr[g] = gt[0:1] * o_cmp + gt[2:3] * o_win
        yield

        ktp = jnp.maximum(qb - 1, 0)
        s_prev = add_block_pen(_dot(ks_ref[0, g, tile_rows(ktp), :], qT)
                               + tb_ref[g, jnp.where(qb >= 1, TB_PREV, TB_NONE)], pen_ref, g, ktp)
        s_scr[g, tile_rows(ktp), :] = s_prev
        s_own = add_block_pen(_dot(ks_ref[0, g, tile_rows(qb), :], qT) + tb_ref[g, TB_OWN], pen_ref, g, qb)
        s_scr[g, tile_rows(qb), :] = s_own
        m8_scr[g] = jnp.maximum(_fold8(s_prev, jnp.max), _fold8(s_own, jnp.max))
        l8_scr[g] = jnp.zeros((8, RT), f32)
        acc_scr[g] = jnp.zeros((HEAD_DIM, RT), f32)

    groups = [group_phases(g) for g in range(G)]
    while groups:
        groups = [phases for phases in groups if next(phases, False) is None]

    def pass_a(i, c):
        for g in range(G):
            qT = q_ref[0, g]
            m8 = m8_scr[g]
            for u in range(SLC_UNROLL):
                kt = SLC_UNROLL * i + u
                live = kt < qb - 1
                ktc = jnp.minimum(kt, nkt - 1)
                s = add_block_pen(_dot(ks_ref[0, g, tile_rows(ktc), :], qT) + jnp.where(live, 0.0, NEG),
                                  penf_ref, g, ktc)
                s_scr[g, tile_rows(jnp.where(live, kt, nkt + u)), :] = s
                m8 = jnp.maximum(m8, _fold8(s, jnp.max))
            m8_scr[g] = m8
        return c

    lax.fori_loop(0, (jnp.maximum(qb - 1, 0) + SLC_UNROLL - 1) // SLC_UNROLL, pass_a, 0)

    def pass_b(i, c):
        for g in range(G):
            m = jnp.max(m8_scr[g], axis=0, keepdims=True)
            l8 = l8_scr[g]
            acc = acc_scr[g]
            for u in range(SLC_UNROLL):
                kt = SLC_UNROLL * i + u
                ktc = jnp.minimum(kt, qb)
                p_u = jnp.exp2(s_scr[g, tile_rows(ktc), :] - (m + jnp.where(kt <= qb, 0.0, -NEG)))
                l8 = l8 + _fold8(p_u, jnp.sum)
                acc = acc + _dot(vst_ref[0, g, ktc], p_u.astype(bf16))
            l8_scr[g] = l8
            acc_scr[g] = acc
        return c

    lax.fori_loop(0, qb // SLC_UNROLL + 1, pass_b, 0)

    for g in range(G):
        o_slc = acc_scr[g] * (1.0 / jnp.maximum(jnp.sum(l8_scr[g], axis=0, keepdims=True), TINY))
        o_ref[0, g] = (part_scr[g] + gt_ref[0, g][1:2] * o_slc).astype(bf16)


def _bias_tables(rel_bias, S):
    G = N_KV_HEADS
    H = rel_bias.shape[1]
    r = H // G
    T = KEY_TILE
    n = jnp.arange(S + 2 * T, dtype=i32)
    max_exact = N_BUCKETS // 2
    nf = jnp.maximum(n, 1).astype(f32)
    large = max_exact + (jnp.log(nf / max_exact) / math.log(MAX_DISTANCE / max_exact)
                         * (N_BUCKETS - max_exact)).astype(i32)
    bucket = jnp.where(n < max_exact, n, jnp.minimum(large, N_BUCKETS - 1))
    bias1d = rel_bias.astype(f32)[bucket] * LOG2E

    def per_group(tab):
        lead = tab.shape[:-2]
        k = len(lead)
        t = jnp.moveaxis(tab, -1, 0).reshape((G, r) + lead + (T,))
        return jnp.moveaxis(t, 1, k + 1).reshape((G,) + lead + (r * T,))

    def masked(dist, ok):
        return jnp.where(jnp.asarray(ok)[..., None], bias1d[np.maximum(dist, 0)], NEG)

    def toeplitz(first):
        w = bias1d[np.maximum(first - (T - 1) + np.arange(2 * T), 0)]
        skew = jnp.broadcast_to(w[None], (T, 2 * T, H)).reshape(2 * T * T, H)[:T * (2 * T - 1)]
        return skew.reshape(T, 2 * T - 1, H)[:, T - 1:]

    kj, qi = np.arange(T)[:, None], np.arange(T)[None, :]
    ok = [qi - kj >= 0, np.ones((T, T), bool), 2 * T + qi - kj < WINDOW]
    tb = jnp.stack([jnp.where(jnp.asarray(ok[d])[..., None], toeplitz(d * T), NEG) for d in range(3)]
                   + [jnp.full((T, T, H), NEG, f32)])
    cend = (np.arange(CMP_NEAR)[:, None] - CMP_PAD) * CMP_STRIDE + CMP_LEN - 1
    dcn = np.arange(T)[None, :] - cend
    rows = CMP_PAD + S // CMP_STRIDE
    far = jnp.broadcast_to(bias1d[-1][None, None, :], (rows, T, H))
    cstep = jnp.concatenate([far, jnp.full((rows, T, H), NEG, f32)])
    return per_group(tb), per_group(masked(dcn, dcn >= 0)), per_group(cstep)


def _nsa_attention(q, gates_t, shared, tables, B, S):
    kcm, vct, ks, vst, kw, vwt = shared
    tb, cbn, cstep = tables
    G, DH, T = N_KV_HEADS, HEAD_DIM, Q_BLOCK
    RT = q.shape[3]
    r = RT // T
    nqb = S // T
    nch = S // CMP_STRIDE
    nsb = S // SLC_LEN
    nkt = S // KEY_TILE
    cmp_start = np.arange(nch) * CMP_STRIDE
    slc_start = np.arange(nsb) * SLC_LEN
    overlap = np.clip(np.minimum(cmp_start[:, None] + CMP_LEN, slc_start[None, :] + SLC_LEN)
                      - np.maximum(cmp_start[:, None], slc_start[None, :]), 0, None) / CMP_LEN
    overlap[nch - 1] = 0.0
    c2st = jnp.asarray(overlap.T, bf16)
    qmap = lambda b, i: (b * nqb + i, 0, 0, 0)
    bat = lambda b, i: (b, 0, 0, 0)
    bat5 = lambda b, i: (b, 0, 0, 0, 0)
    once = pl.Buffered(1)
    return pl.pallas_call(
        functools.partial(_nsa_kernel, r=r, nsb=nsb, n_sel=min(N_SEL, nsb)),
        out_shape=jax.ShapeDtypeStruct(q.shape, bf16),
        grid=(B, nqb),
        in_specs=[
            pl.BlockSpec((1, G, DH, RT), qmap),
            pl.BlockSpec((1, G, N_BRANCH, RT), qmap),
            pl.BlockSpec((1, G, nch, DH), bat),
            pl.BlockSpec((1, G, DH, nch), bat),
            pl.BlockSpec((1, G, S, DH), bat),
            pl.BlockSpec((1, G, nkt, DH, KEY_TILE), bat5),
            pl.BlockSpec((1, G, S, DH), bat),
            pl.BlockSpec((1, G, nkt, DH, KEY_TILE), bat5),
            pl.BlockSpec(cbn.shape, lambda b, i: (0, 0, 0), pipeline_mode=once),
            pl.BlockSpec(cstep.shape, lambda b, i: (0, 0, 0), pipeline_mode=once),
            pl.BlockSpec(tb.shape, lambda b, i: (0, 0, 0, 0), pipeline_mode=once),
            pl.BlockSpec((nsb, nch), lambda b, i: (0, 0), pipeline_mode=once),
        ],
        out_specs=pl.BlockSpec((1, G, DH, RT), qmap),
        scratch_shapes=[pltpu.VMEM((G, CMP_PAD + nch, RT), f32), pltpu.VMEM((G, nsb, RT), f32),
                        pltpu.VMEM((G, nsb, RT), f32), pltpu.VMEM((G, (nkt + SLC_UNROLL) * KEY_TILE, RT), f32),
                        pltpu.VMEM((G, 8, RT), f32), pltpu.VMEM((G, 8, RT), f32), pltpu.VMEM((G, DH, RT), f32),
                        pltpu.VMEM((G, DH, RT), f32)],
        compiler_params=_cparams("parallel", "arbitrary"),
        name="nsa_attention",
    )(q, gates_t, kcm, vct, ks, vst, kw, vwt, cbn, cstep, tb, c2st)


def _oproj_kernel(x_ref, a_ref, w_ref, o_ref):
    nt, G, DH, RT = a_ref.shape
    T = x_ref.shape[0] // nt
    r = RT // T
    rows = []
    for t in range(nt):
        cols = []
        for g in range(G):
            a = a_ref[t, g].astype(f32)
            cols.append(jnp.concatenate([a[:, k * T:(k + 1) * T] for k in range(r)], axis=0).T)
        rows.append(jnp.concatenate(cols, axis=1))
    attn = jnp.concatenate(rows, axis=0)
    o_ref[...] = x_ref[...] + _dot(attn.astype(bf16), w_ref[...])


def _out_project(xf, attn, w_o):
    N, D = xf.shape
    nt_all, G, DH, RT = attn.shape
    T = N // nt_all
    to = min(512, N)
    assert N % to == 0 and to % T == 0
    row = lambda i: (i, 0)
    return pl.pallas_call(
        _oproj_kernel,
        out_shape=jax.ShapeDtypeStruct((N, D), f32),
        grid=(N // to,),
        in_specs=[pl.BlockSpec((to, D), row), pl.BlockSpec((to // T, G, DH, RT), lambda i: (i, 0, 0, 0)),
                  pl.BlockSpec(w_o.shape, lambda i: (0, 0))],
        out_specs=pl.BlockSpec((to, D), row),
        compiler_params=_cparams("parallel"),
        name="out_project",
    )(xf, attn, w_o.astype(bf16))


def _nsa_layer(xf, B, S, g, w_qg, g_q, w_o, shared, tables):
    N = xf.shape[0]
    G, T = N_KV_HEADS, Q_BLOCK
    q, gates = _qg_project(xf, g, w_qg, g_q)
    r = gates.shape[1] // (G * N_BRANCH)
    gates_t = gates.reshape(N // T, T, G, r, N_BRANCH).transpose(0, 2, 4, 3, 1).reshape(N // T, G, N_BRANCH, r * T)
    attn = _nsa_attention(q, gates_t, shared, tables, B, S)
    return _out_project(xf, attn, w_o)


def kernel(x, rel_bias, norm_mix, norm_ffn, pool_w, pool_scale, norm_kv, w_kv, g_k, cmp_pos, cmp_w1, cmp_w2,
           w_qg, g_q, w_o, ffn_wg, ffn_wu, ffn_wd, router, moe_wg, moe_wu, moe_wd):
    B, S, D = x.shape
    depth = norm_mix.shape[0]
    n_a = depth // 2
    assert S % Q_BLOCK == 0
    xf = x.reshape(B * S, D)
    shared = None
    tables = _bias_tables(rel_bias, S)
    n_exp = moe_wg.shape[1]
    moe_w = [w.reshape((-1,) + w.shape[2:]).astype(bf16) for w in (moe_wg, moe_wu, moe_wd)]
    for layer in range(depth):
        if layer < n_a:
            xf = _pool_layer(xf.reshape(B, S, D), norm_mix[layer], pool_w[layer], pool_scale[layer]).reshape(B * S, D)
        else:
            j = layer - n_a
            xf = _nsa_layer(xf, B, S, norm_mix[layer], w_qg[j], g_q[j], w_o[j], shared, tables)
        i = layer // 2
        if layer % 2 == 0:
            xf = _ffn_layer(xf, norm_ffn[layer], ffn_wg[i], ffn_wu[i], ffn_wd[i])
        else:
            xf = _moe_layer(xf, norm_ffn[layer], router[i], *moe_w, first_expert=i * n_exp)
        if layer == n_a - 1:
            shared = _build_shared(xf, B, S, norm_kv, w_kv, g_k, cmp_pos, cmp_w1, cmp_w2)
    return xf.reshape(B, S, D)
```

```python
import functools
import math

import numpy as np
import jax
import jax.numpy as jnp
from jax import lax
from jax.experimental import pallas as pl
from jax.experimental.pallas import tpu as pltpu
from jax.experimental.pallas import tpu_sc as plsc

f32 = jnp.float32
bf16 = jnp.bfloat16
i32 = jnp.int32
u32 = jnp.uint32

POOL_WINDOWS = (2, 4, 8, 16)
HEAD_DIM = 64
N_KV_HEADS = 4
N_BRANCH = 3
CMP_LEN = 32
CMP_STRIDE = 16
CMP_HIDDEN = 4 * HEAD_DIM
SLC_LEN = 64
N_SEL = 4
WINDOW = 256
Q_BLOCK = 128
FORCED_SCORE = 1.0e4
N_BUCKETS = 32
MAX_DISTANCE = 128
TOP_K = 2
EPS = 1e-6
NEG = -1e30
TINY = 1e-30
LOG2E = math.log2(math.e)
TB_OWN, TB_PREV, TB_WIN2, TB_NONE = range(4)

KEY_TILE = Q_BLOCK
SLC_UNROLL = 4
POOL_HALO = 16
NH_PAD = 16
V7X_VMEM_LIMIT = 56 * 1024 * 1024
V7X_SC_CORES, V7X_SC_SUBCORES = 2, 16
SC_MAX_INDEX_VECTOR = 128
SC_GATHER_CHUNK = 32

CMP_PER_Q = Q_BLOCK // CMP_STRIDE
CMP_PAD = 2 * CMP_PER_Q
CMP_NEAR = 3 * CMP_PER_Q

assert CMP_LEN == 2 * CMP_STRIDE and KEY_TILE == 2 * SLC_LEN and WINDOW == 2 * KEY_TILE
assert max(POOL_WINDOWS) <= POOL_HALO
assert 2 * KEY_TILE - (Q_BLOCK - 1) >= MAX_DISTANCE
assert (CMP_PAD + 1) * CMP_STRIDE - (CMP_LEN - 1) >= MAX_DISTANCE and CMP_PER_Q % 8 == 0


def _cparams(*sem):
    return pltpu.CompilerParams(dimension_semantics=sem, vmem_limit_bytes=V7X_VMEM_LIMIT)


def _rms(xf, g):
    ms = jnp.mean(xf * xf, axis=-1, keepdims=True)
    return (xf * lax.rsqrt(ms + EPS)) * g


def _dot(a, b):
    return jnp.dot(a, b, preferred_element_type=f32)


def _dot_hilo(a, b):
    hi = a.astype(bf16)
    lo = (a - hi.astype(f32)).astype(bf16)
    return _dot(hi, b) + _dot(lo, b)


def _head_rms(z, gvec, seg, segt):
    ssq = _dot_hilo(z * z, seg)
    inv = lax.rsqrt(ssq * (1.0 / HEAD_DIM) + EPS)
    return (z * _dot_hilo(inv, segt)) * gvec


def _silu(a):
    return a * jax.nn.sigmoid(a)


def _seg_mats(width):
    heads = width // HEAD_DIM
    seg = np.zeros((width, NH_PAD), np.float32)
    seg[np.arange(width), np.arange(width) // HEAD_DIM] = 1.0
    assert heads <= NH_PAD
    return jnp.asarray(seg, bf16), jnp.asarray(seg.T, bf16)


def _pool_kernel(x_ref, halo_ref, g_ref, w_ref, scale_ref, o_ref, *, tp, cg):
    i = pl.program_id(1)
    x = x_ref[0]
    xh = jnp.concatenate([halo_ref[0], x], axis=0)
    h = _rms(xh, g_ref[...])
    row = lax.broadcasted_iota(i32, (tp + POOL_HALO, 1), 0)
    t_abs = i * tp + row - POOL_HALO
    h = jnp.where(t_abs >= 0, h, 0.0)
    outs = []
    for gi, w in enumerate(POOL_WINDOWS):
        hg = h[:, gi * cg:(gi + 1) * cg]
        s = hg
        sh = 1
        while sh < w:
            s = s + pltpu.roll(s, sh, axis=0)
            sh *= 2
        cnt = jnp.clip(t_abs + 1, 1, w).astype(f32)
        diff = (s / cnt - hg)[POOL_HALO:]
        outs.append(_dot(diff.astype(bf16), w_ref[gi]))
    y = jnp.concatenate(outs, axis=1)
    o_ref[0] = x + y * scale_ref[...]


def _pool_layer(x3, g, w_grp, scale):
    B, S, D = x3.shape
    tp = min(512, S)
    cg = D // len(POOL_WINDOWS)
    assert S % tp == 0 and tp % POOL_HALO == 0 and all(w & (w - 1) == 0 for w in POOL_WINDOWS)
    hb = tp // POOL_HALO
    return pl.pallas_call(
        functools.partial(_pool_kernel, tp=tp, cg=cg),
        out_shape=jax.ShapeDtypeStruct((B, S, D), f32),
        grid=(B, S // tp),
        in_specs=[
            pl.BlockSpec((1, tp, D), lambda b, i: (b, i, 0)),
            pl.BlockSpec((1, POOL_HALO, D), lambda b, i: (b, jnp.maximum(i * hb - 1, 0), 0)),
            pl.BlockSpec((1, D), lambda b, i: (0, 0)),
            pl.BlockSpec((len(POOL_WINDOWS), cg, cg), lambda b, i: (0, 0, 0)),
            pl.BlockSpec((1, D), lambda b, i: (0, 0)),
        ],
        out_specs=pl.BlockSpec((1, tp, D), lambda b, i: (b, i, 0)),
        compiler_params=_cparams("parallel", "arbitrary"),
        name="pool_layer",
    )(x3, x3, g.reshape(1, D), w_grp.astype(bf16), scale.reshape(1, D))


def _ffn_kernel(x_ref, g_ref, wg_ref, wu_ref, wd_ref, o_ref, h_ref):
    j = pl.program_id(1)

    @pl.when(j == 0)
    def _():
        x = x_ref[...]
        h_ref[...] = _rms(x, g_ref[...]).astype(bf16)
        o_ref[...] = x

    h = h_ref[...]
    act = _silu(_dot(h, wg_ref[...])) * _dot(h, wu_ref[...])
    o_ref[...] += _dot(act.astype(bf16), wd_ref[...])


def _ffn_layer(xf, g, wg, wu, wd):
    N, D = xf.shape
    F = wg.shape[1]
    tm = min(1024, N)
    tf = 512
    assert N % tm == 0 and F % tf == 0
    return pl.pallas_call(
        _ffn_kernel,
        out_shape=jax.ShapeDtypeStruct((N, D), f32),
        grid=(N // tm, F // tf),
        in_specs=[
            pl.BlockSpec((tm, D), lambda i, j: (i, 0)),
            pl.BlockSpec((1, D), lambda i, j: (0, 0)),
            pl.BlockSpec((D, tf), lambda i, j: (0, j)),
            pl.BlockSpec((D, tf), lambda i, j: (0, j)),
            pl.BlockSpec((tf, D), lambda i, j: (j, 0)),
        ],
        out_specs=pl.BlockSpec((tm, D), lambda i, j: (i, 0)),
        scratch_shapes=[pltpu.VMEM((tm, D), bf16)],
        compiler_params=_cparams("parallel", "arbitrary"),
        name="ffn_dense",
    )(xf, g.reshape(1, D), wg.astype(bf16), wu.astype(bf16), wd.astype(bf16))


def _router_kernel(x_ref, g_ref, rt_ref, hp_ref, idx_ref, gate_ref):
    h = _rms(x_ref[...], g_ref[...])
    half = h.shape[1] // 2
    h_hi = h.astype(bf16)
    bits = pltpu.bitcast(h_hi.astype(f32), u32)
    hp_ref[...] = pltpu.bitcast((bits[:, :half] & jnp.uint32(0xFFFF0000)) | (bits[:, half:] >> 16), i32)
    h_lo = (h - h_hi.astype(f32)).astype(bf16)
    rt = rt_ref[...]
    rt_hi = rt.astype(bf16)
    rt_lo = (rt - rt_hi.astype(f32)).astype(bf16)
    nt_dot = lambda a, b: lax.dot_general(a, b, (((1,), (1,)), ((), ())), preferred_element_type=f32)
    logits = nt_dot(rt_hi, h_hi) + nt_dot(rt_hi, h_lo) + nt_dot(rt_lo, h_hi)
    ne = logits.shape[0]
    row = lax.broadcasted_iota(i32, logits.shape, 0)
    m1 = jnp.max(logits, axis=0, keepdims=True)
    i1 = jnp.min(jnp.where(logits == m1, row, ne), axis=0, keepdims=True)
    rest = jnp.where(row == i1, -jnp.inf, logits)
    m2 = jnp.max(rest, axis=0, keepdims=True)
    i2 = jnp.min(jnp.where(rest == m2, row, ne), axis=0, keepdims=True)
    e2 = jnp.exp(m2 - m1)
    den = 1.0 + e2
    idx_ref[...] = jnp.concatenate([i1, i2], axis=0)
    gate_ref[...] = jnp.concatenate([1.0 / den, e2 / den], axis=0)


def _router(xf, g, router):
    N, D = xf.shape
    E = router.shape[1]
    tr = min(1024, N)
    assert N % tr == 0 and TOP_K == 2
    return pl.pallas_call(
        _router_kernel,
        out_shape=(jax.ShapeDtypeStruct((N, D // 2), i32),
                   jax.ShapeDtypeStruct((TOP_K, N), i32),
                   jax.ShapeDtypeStruct((TOP_K, N), f32)),
        grid=(N // tr,),
        in_specs=[
            pl.BlockSpec((tr, D), lambda i: (i, 0)),
            pl.BlockSpec((1, D), lambda i: (0, 0)),
            pl.BlockSpec((E, D), lambda i: (0, 0)),
        ],
        out_specs=(pl.BlockSpec((tr, D // 2), lambda i: (i, 0)),
                   pl.BlockSpec((TOP_K, tr), lambda i: (0, i)),
                   pl.BlockSpec((TOP_K, tr), lambda i: (0, i))),
        compiler_params=_cparams("parallel"),
        name="moe_router",
    )(xf, g.reshape(1, D), router.T)


def _sc_gather_rows(table, idx, chunk):
    B = idx.shape[0]
    D = table.shape[1]
    workers = V7X_SC_CORES * V7X_SC_SUBCORES
    per_w = B // workers
    cpw = per_w // chunk
    assert B % (8 * workers) == 0 and per_w % (2 * chunk) == 0 and cpw % 8 == 0
    assert chunk % 8 == 0 and chunk <= SC_MAX_INDEX_VECTOR
    mesh = plsc.VectorSubcoreMesh(core_axis_name="c", subcore_axis_name="s")

    @functools.partial(
        pl.kernel, mesh=mesh, out_type=jax.ShapeDtypeStruct((B, D), table.dtype),
        scratch_types=[pltpu.VMEM((cpw, chunk), i32), pltpu.VMEM((2, chunk, D), table.dtype),
                       pltpu.SemaphoreType.DMA, pltpu.SemaphoreType.DMA])
    def gather(table_hbm, idx_hbm, out_hbm, idx_v, rows_v, sem0, sem1):
        wid = lax.axis_index("s") * V7X_SC_CORES + lax.axis_index("c")
        pltpu.sync_copy(idx_hbm.at[pl.ds(pl.multiple_of(wid * cpw, 8), cpw)], idx_v)
        sems = (sem0, sem1)

        def fetch(c, b):
            return pltpu.make_async_copy(table_hbm.at[idx_v.at[c]], rows_v.at[b], sems[b])

        fetch(0, 0).start()

        @pl.loop(0, cpw, step=2)
        def _(c):
            for b in range(2):
                cur = c + b
                fetch(cur, b).wait()

                @pl.when(cur + 1 < cpw)
                def _():
                    fetch(cur + 1, 1 - b).start()

                pltpu.sync_copy(rows_v.at[b], out_hbm.at[pl.ds(pl.multiple_of(wid * per_w + cur * chunk, 8), chunk)])

    return gather(table, idx.reshape(B // chunk, chunk))


def _sc_scatter_rows(src, slots, n_slots, chunk):
    K, N = slots.shape
    D = src.shape[1]
    workers = V7X_SC_CORES * V7X_SC_SUBCORES
    per_w = N // workers
    cpw = per_w // chunk
    assert N % (8 * workers) == 0 and per_w % (2 * chunk) == 0 and cpw % 8 == 0
    assert chunk % 8 == 0 and chunk <= SC_MAX_INDEX_VECTOR
    mesh = plsc.VectorSubcoreMesh(core_axis_name="c", subcore_axis_name="s")

    @functools.partial(
        pl.kernel, mesh=mesh, out_type=jax.ShapeDtypeStruct((n_slots, D), src.dtype),
        scratch_types=[pltpu.VMEM((K, cpw, chunk), i32), pltpu.VMEM((2, chunk, D), src.dtype),
                       pltpu.SemaphoreType.DMA, pltpu.SemaphoreType.DMA, pltpu.SemaphoreType.DMA,
                       pltpu.SemaphoreType.DMA])
    def scatter(src_hbm, slots_hbm, out_hbm, slot_v, rows_v, lsem0, lsem1, ssem0, ssem1):
        wid = lax.axis_index("s") * V7X_SC_CORES + lax.axis_index("c")
        for k in range(K):
            pltpu.sync_copy(slots_hbm.at[k, pl.ds(pl.multiple_of(wid * cpw, 8), cpw)], slot_v.at[k])
        lsems, ssems = (lsem0, lsem1), (ssem0, ssem1)

        def load(c, b):
            return pltpu.make_async_copy(
                src_hbm.at[pl.ds(pl.multiple_of(wid * per_w + c * chunk, 8), chunk)], rows_v.at[b], lsems[b])

        def store(c, b, k):
            return pltpu.make_async_copy(rows_v.at[b], out_hbm.at[slot_v.at[k].at[c]], ssems[b])

        load(0, 0).start()

        @pl.loop(0, cpw, step=2)
        def _(c):
            for b in range(2):
                cur = c + b
                load(cur, b).wait()
                for k in range(K):
                    store(cur, b, k).start()

                @pl.when(cur >= 1)
                def _():
                    for k in range(K):
                        store(cur - 1, 1 - b, k).wait()

                @pl.when(cur + 1 < cpw)
                def _():
                    load(cur + 1, 1 - b).start()

        for k in range(K):
            store(cpw - 1, 1, k).wait()

    return scatter(src, slots.reshape(K, N // chunk, chunk))


def _gmm_kernel(te_ref, nu_ref, xs_ref, wg_ref, wu_ref, wd_ref, o_ref, h_ref):
    i = pl.program_id(0)
    j = pl.program_id(1)

    @pl.when(i < nu_ref[0])
    def _():
        @pl.when(j == 0)
        def _():
            w = pltpu.bitcast(xs_ref[...], u32)
            left = pltpu.bitcast(w & jnp.uint32(0xFFFF0000), f32)
            right = pltpu.bitcast(w << 16, f32)
            h = jnp.concatenate([left, right], axis=1)
            filled = lax.broadcasted_iota(i32, (h.shape[0], 1), 0) < nu_ref[1 + i]
            h_ref[...] = jnp.where(filled, h, 0.0).astype(bf16)
            o_ref[...] = jnp.zeros_like(o_ref)

        h = h_ref[...]
        act = _silu(_dot(h, wg_ref[0])) * _dot(h, wu_ref[0])
        o_ref[...] += _dot(act.astype(bf16), wd_ref[0])

    @pl.when((i >= nu_ref[0]) & (j == 0))
    def _():
        o_ref[...] = jnp.zeros_like(o_ref)


def _gmm(tile_expert, n_used, xs, wg, wu, wd, tm):
    P, half = xs.shape
    D = 2 * half
    F = wg.shape[2]
    tf = 512
    assert P % tm == 0 and F % tf == 0

    nf = F // tf

    def row_map(i, j, te, nu):
        return (jnp.minimum(i, nu[0] - 1), 0)

    def ff(i, j, nu):
        return jnp.where(i < nu[0], j, nf - 1)

    grid_spec = pltpu.PrefetchScalarGridSpec(
        num_scalar_prefetch=2,
        grid=(P // tm, nf),
        in_specs=[
            pl.BlockSpec((tm, half), row_map),
            pl.BlockSpec((1, D, tf), lambda i, j, te, nu: (te[i], 0, ff(i, j, nu))),
            pl.BlockSpec((1, D, tf), lambda i, j, te, nu: (te[i], 0, ff(i, j, nu))),
            pl.BlockSpec((1, tf, D), lambda i, j, te, nu: (te[i], ff(i, j, nu), 0)),
        ],
        out_specs=pl.BlockSpec((tm, D), lambda i, j, te, nu: (i, 0)),
        scratch_shapes=[pltpu.VMEM((tm, D), bf16)],
    )
    return pl.pallas_call(
        _gmm_kernel,
        out_shape=jax.ShapeDtypeStruct((P, D), f32),
        grid_spec=grid_spec,
        compiler_params=_cparams("arbitrary", "arbitrary"),
        name="moe_gmm",
    )(tile_expert, n_used, xs, wg, wu, wd)


def _combine_kernel(x_ref, gate_ref, y_ref, o_ref):
    gate = gate_ref[...]
    acc = x_ref[...]
    for k in range(TOP_K):
        acc = acc + y_ref[k] * gate[:, k:k + 1]
    o_ref[...] = acc


def _combine(xf, gate_tk, y_tok):
    N, D = xf.shape
    tt = min(1024, N)
    assert N % tt == 0
    return pl.pallas_call(
        _combine_kernel,
        out_shape=jax.ShapeDtypeStruct((N, D), f32),
        grid=(N // tt,),
        in_specs=[pl.BlockSpec((tt, D), lambda i: (i, 0)),
                  pl.BlockSpec((tt, TOP_K), lambda i: (i, 0)),
                  pl.BlockSpec((TOP_K, tt, D), lambda i: (0, i, 0))],
        out_specs=pl.BlockSpec((tt, D), lambda i: (i, 0)),
        compiler_params=_cparams("parallel"),
        name="moe_combine",
    )(xf, gate_tk, y_tok)


def _moe_layer(xf, g, router, wg, wu, wd, first_expert):
    N, D = xf.shape
    E = router.shape[1]
    tm = min(1024, N)
    hp, idx, gate = _router(xf, g, router)
    e_flat = idx.reshape(-1)
    onehot = (e_flat[:, None] == jnp.arange(E, dtype=i32)[None, :]).astype(i32)
    csum = jnp.cumsum(onehot, axis=0)
    counts = csum[-1]
    padded = ((counts + tm - 1) // tm) * tm
    ends = jnp.cumsum(padded)
    starts = ends - padded
    pos = (jnp.sum((csum + starts[None, :]) * onehot, axis=1) - 1).astype(i32)
    P = TOP_K * N + E * tm
    n_used = (ends[-1] // tm).astype(i32)
    tile_start = jnp.minimum(jnp.arange(P // tm, dtype=i32), n_used - 1) * tm
    tile_expert = jnp.sum((ends[None, :] <= tile_start[:, None]).astype(i32), axis=1)
    group_end = (starts + counts)[tile_expert]
    tile_rows = jnp.clip(group_end - tile_start, 0, tm).astype(i32)
    xs = _sc_scatter_rows(hp, pos.reshape(TOP_K, N), P, SC_GATHER_CHUNK)
    ys = _gmm(tile_expert + first_expert, jnp.concatenate([n_used.reshape(1), tile_rows]), xs, wg, wu, wd, tm)
    y_tok = _sc_gather_rows(ys, pos, SC_GATHER_CHUNK).reshape(TOP_K, N, D)
    return _combine(xf, gate.T, y_tok)


def _kv_kernel(x_ref, g_ref, w_ref, gk_ref, seg_ref, segt_ref, kc_ref, vc_ref, ks_ref, vs_ref, kw_ref, vw_ref):
    h = _rms(x_ref[...], g_ref[...]).astype(bf16)
    kv = _dot(h, w_ref[...])
    G, DH = ks_ref.shape[1], ks_ref.shape[3]
    wd = G * DH
    part = lambda p: kv[:, p * wd:(p + 1) * wd]
    seg, segt = seg_ref[...], segt_ref[...]
    for c_ref, pc in ((kc_ref, 0), (vc_ref, 1)):
        z = part(pc)
        for g in range(G):
            c_ref[0, g] = z[:, g * DH:(g + 1) * DH]
    for k_ref, v_ref, pk, gain in ((ks_ref, vs_ref, 2, gk_ref[0:1, :]), (kw_ref, vw_ref, 4, gk_ref[1:2, :])):
        kn = _head_rms(part(pk), gain, seg, segt).astype(bf16)
        v = part(pk + 1)
        for g in range(G):
            k_ref[0, g] = kn[:, g * DH:(g + 1) * DH]
        for t in range(v_ref.shape[2]):
            vt = v[t * KEY_TILE:(t + 1) * KEY_TILE, :].T
            for g in range(G):
                v_ref[0, g, t] = vt[g * DH:(g + 1) * DH].astype(bf16)


def _kv_project(xf, B, S, norm_kv, w_kv, g_k):
    N, D = xf.shape
    G, DH = N_KV_HEADS, HEAD_DIM
    wd = G * DH
    tk = min(512, S)
    spb = S // tk
    tpk = tk // KEY_TILE
    assert S % tk == 0 and tk % KEY_TILE == 0 and w_kv.shape[1] == 2 * N_BRANCH * wd
    seg, segt = _seg_mats(wd)
    gk = jnp.stack([jnp.tile(g_k[1], G), jnp.tile(g_k[2], G)])
    row = lambda i: (i, 0)
    const = lambda i: (0, 0)
    k_shape = jax.ShapeDtypeStruct((B, G, S, DH), bf16)
    v_shape = jax.ShapeDtypeStruct((B, G, S // KEY_TILE, DH, KEY_TILE), bf16)
    k_spec = pl.BlockSpec((1, G, tk, DH), lambda i: (i // spb, 0, i % spb, 0))
    v_spec = pl.BlockSpec((1, G, tpk, DH, KEY_TILE), lambda i: (i // spb, 0, i % spb, 0, 0))
    return pl.pallas_call(
        _kv_kernel,
        out_shape=(jax.ShapeDtypeStruct((B, G, S, DH), f32), jax.ShapeDtypeStruct((B, G, S, DH), f32),
                   k_shape, v_shape, k_shape, v_shape),
        grid=(N // tk,),
        in_specs=[
            pl.BlockSpec((tk, D), row),
            pl.BlockSpec((1, D), const),
            pl.BlockSpec(w_kv.shape, const),
            pl.BlockSpec((2, wd), const),
            pl.BlockSpec((wd, NH_PAD), const),
            pl.BlockSpec((NH_PAD, wd), const),
        ],
        out_specs=(k_spec, k_spec, k_spec, v_spec, k_spec, v_spec),
        compiler_params=_cparams("parallel"),
        name="kv_project",
    )(xf, norm_kv.reshape(1, D), w_kv.astype(bf16), gk, seg, segt)


def _compress_kernel(c_ref, pos_ref, w1_ref, w2_ref, gk_ref, o_ref):
    kv = pl.program_id(0)
    c = c_ref[0, 0, 0]
    a = _dot((c + pos_ref[0, 0:1, :]).astype(bf16), w1_ref[0, 0])
    b = _dot((c + pos_ref[0, 1:2, :]).astype(bf16), w1_ref[0, 1])
    n = c.shape[0]
    hid = a + pltpu.roll(b, n - 1, axis=0)
    out = _dot(_silu(hid).astype(bf16), w2_ref[0])

    @pl.when(kv == 0)
    def _():
        o_ref[0, 0, 0] = _rms(out, gk_ref[...])

    @pl.when(kv != 0)
    def _():
        o_ref[0, 0, 0] = out


def _compress(kc, vc, B, S, cmp_pos, cmp_w1, cmp_w2, g_k0):
    G, DH = N_KV_HEADS, HEAD_DIM
    nch = S // CMP_STRIDE
    cw = CMP_STRIDE * DH

    c = jnp.stack([kc.reshape(B, G, nch, cw), vc.reshape(B, G, nch, cw)])
    pos = cmp_pos.reshape(2, 2, cw)
    w1 = cmp_w1.reshape(2, 2, cw, CMP_HIDDEN).astype(bf16)
    return pl.pallas_call(
        _compress_kernel,
        out_shape=jax.ShapeDtypeStruct((2, B, G, nch, DH), f32),
        grid=(2, B, G),
        in_specs=[
            pl.BlockSpec((1, 1, 1, nch, cw), lambda k, b, g: (k, b, g, 0, 0)),
            pl.BlockSpec((1, 2, cw), lambda k, b, g: (k, 0, 0)),
            pl.BlockSpec((1, 2, cw, CMP_HIDDEN), lambda k, b, g: (k, 0, 0, 0)),
            pl.BlockSpec((1, CMP_HIDDEN, DH), lambda k, b, g: (k, 0, 0)),
            pl.BlockSpec((1, DH), lambda k, b, g: (0, 0)),
        ],
        out_specs=pl.BlockSpec((1, 1, 1, nch, DH), lambda k, b, g: (k, b, g, 0, 0)),
        compiler_params=_cparams("arbitrary", "arbitrary", "arbitrary"),
        name="kv_compress",
    )(c, pos, w1, cmp_w2.astype(bf16), g_k0.reshape(1, DH))


def _build_shared(xf, B, S, norm_kv, w_kv, g_k, cmp_pos, cmp_w1, cmp_w2):
    kc, vc, ks, vst, kw, vwt = _kv_project(xf, B, S, norm_kv, w_kv, g_k)
    cmp = _compress(kc, vc, B, S, cmp_pos, cmp_w1, cmp_w2, g_k[0]).astype(bf16)
    kcm = cmp[0]
    vct = cmp[1].transpose(0, 1, 3, 2)
    return kcm, vct, ks, vst, kw, vwt


def _qg_kernel(x_ref, g_ref, wq_ref, wgate_ref, gq_ref, seg_ref, segt_ref, q_ref, gate_ref):
    h = _rms(x_ref[...], g_ref[...]).astype(bf16)
    q_raw = _dot(h, wq_ref[...])
    gate_ref[...] = jax.nn.sigmoid(_dot(h, wgate_ref[...]))
    nt, G, DH, RT = q_ref.shape
    T = q_raw.shape[0] // nt
    r = RT // T

    def tile_phases(t):
        q = _head_rms(q_raw[t * T:(t + 1) * T], gq_ref[...], seg_ref[...], segt_ref[...])
        q = q * (HEAD_DIM ** -0.5 * LOG2E)
        yield
        for g in range(G):
            qt = q[:, g * r * DH:(g + 1) * r * DH].T
            q_ref[t, g] = jnp.concatenate([qt[k * DH:(k + 1) * DH] for k in range(r)], axis=1).astype(bf16)
            yield

    tiles = [tile_phases(t) for t in range(nt)]
    while tiles:
        tiles = [phases for phases in tiles if next(phases, False) is None]


def _qg_project(xf, g, w_qg, g_q):
    N, D = xf.shape
    HD = D
    ng = w_qg.shape[1] - HD
    tq = min(512, N)
    G, T = N_KV_HEADS, Q_BLOCK
    RT = HD // (G * HEAD_DIM) * T
    assert N % tq == 0 and tq % T == 0
    seg, segt = _seg_mats(HD)
    row = lambda i: (i, 0)
    const = lambda i: (0, 0)
    return pl.pallas_call(
        _qg_kernel,
        out_shape=(jax.ShapeDtypeStruct((N // T, G, HEAD_DIM, RT), bf16), jax.ShapeDtypeStruct((N, ng), f32)),
        grid=(N // tq,),
        in_specs=[
            pl.BlockSpec((tq, D), row),
            pl.BlockSpec((1, D), const),
            pl.BlockSpec((D, HD), const),
            pl.BlockSpec((D, ng), const),
            pl.BlockSpec((1, HD), const),
            pl.BlockSpec((HD, NH_PAD), const),
            pl.BlockSpec((NH_PAD, HD), const),
        ],
        out_specs=(pl.BlockSpec((tq // T, G, HEAD_DIM, RT), lambda i: (i, 0, 0, 0)), pl.BlockSpec((tq, ng), row)),
        compiler_params=_cparams("parallel"),
        name="qg_project",
    )(xf, g.reshape(1, D), w_qg[:, :HD].astype(bf16), w_qg[:, HD:].astype(bf16),
      jnp.tile(g_q, HD // HEAD_DIM).reshape(1, HD), seg, segt)


def _fold8(x, op):
    return op(x.reshape(x.shape[0] // 8, 8, x.shape[1]), axis=0)


def _nsa_kernel(q_ref, gt_ref, kc_ref, vct_ref, ks_ref, vst_ref, kw_ref, vwt_ref, cbn_ref, cstep_ref, tb_ref, c2st_ref,
                o_ref, cb_scr, pen_ref, penf_ref, s_scr, m8_scr, l8_scr, acc_scr, part_scr, *, r, nsb, n_sel):
    qb = pl.program_id(1)
    t0 = qb * Q_BLOCK
    T = Q_BLOCK
    RT = r * T
    G = q_ref.shape[1]
    nch = kc_ref.shape[2]
    nkt = s_scr.shape[1] // KEY_TILE - SLC_UNROLL

    def tile_rows(kt):
        return pl.ds(pl.multiple_of(kt * KEY_TILE, KEY_TILE), KEY_TILE)

    def add_block_pen(s, ref, g, kt):
        return jnp.concatenate([s[:SLC_LEN] + ref[g, pl.ds(2 * kt, 1), :],
                                s[SLC_LEN:] + ref[g, pl.ds(2 * kt + 1, 1), :]], axis=0)

    def group_phases(g):
        qT = q_ref[0, g]

        cb_scr[g] = cstep_ref[g, pl.ds(pl.multiple_of(nch - (qb + 1) * CMP_PER_Q, 8), CMP_PAD + nch), :]
        cb_scr[g, pl.ds(pl.multiple_of(qb * CMP_PER_Q, 8), CMP_NEAR), :] = cbn_ref[g]
        s = _dot(kc_ref[0, g], qT) + cb_scr[g, CMP_PAD:CMP_PAD + nch, :]
        yield
        p = jnp.where(s > 0.5 * NEG, jnp.exp2(s - jnp.max(s, axis=0, keepdims=True)), 0.0)
        p = p * (1.0 / jnp.maximum(jnp.sum(p, axis=0, keepdims=True), TINY))
        o_cmp = _dot(vct_ref[0, g], p.astype(bf16))
        yield

        psum = p[:, 0:T]
        for k in range(1, r):
            psum = psum + p[:, k * T:(k + 1) * T]
        hi = psum.astype(bf16)
        lo = (psum - hi.astype(f32)).astype(bf16)
        imp = _dot(c2st_ref[...], hi) + _dot(c2st_ref[...], lo)
        jb = lax.broadcasted_iota(i32, (nsb, T), 0)
        blk_q = jnp.right_shift(t0 + lax.broadcasted_iota(i32, (nsb, T), 1), SLC_LEN.bit_length() - 1)
        forced = (jb == 0) | (jb == blk_q) | (jb == blk_q - 1)
        score = jnp.where(forced, FORCED_SCORE, jnp.where(jb <= blk_q, imp, NEG))
        pen = jnp.full((nsb, T), NEG, f32)
        for _ in range(n_sel):
            mx = jnp.max(score, axis=0, keepdims=True)
            first = jnp.min(jnp.where(score == mx, jb, nsb), axis=0, keepdims=True)
            hit = jb == first
            pen = jnp.where(hit, 0.0, pen)
            score = jnp.where(hit, -jnp.inf, score)
        pen = jnp.concatenate([pen] * r, axis=1)
        pen_ref[g] = pen
        penf_ref[g] = pen + cstep_ref[g, 0:1, :]
        yield

        win = []
        for d in range(WINDOW // KEY_TILE, -1, -1):
            kt = qb - d
            ktc = jnp.maximum(kt, 0)
            tab = tb_ref[g, jnp.where(kt >= 0, d, TB_NONE)]
            win.append((_dot(kw_ref[0, g, tile_rows(ktc), :], qT) + tab, vwt_ref[0, g, ktc]))
        yield
        m8 = _fold8(win[0][0], jnp.max)
        for s_d, _ in win[1:]:
            m8 = jnp.maximum(m8, _fold8(s_d, jnp.max))
        m = jnp.max(m8, axis=0, keepdims=True)
        l8 = jnp.zeros((8, RT), f32)
        acc = jnp.zeros((HEAD_DIM, RT), f32)
        for s_d, v_d in win:
            p_d = jnp.exp2(s_d - m)
            l8 = l8 + _fold8(p_d, jnp.sum)
            acc = acc + _dot(v_d, p_d.astype(bf16))
        o_win = acc * (1.0 / jnp.maximum(jnp.sum(l8, axis=0, keepdims=True), TINY))

        gt = gt_ref[0, g]
        part_scr[g] = gt[0:1] * o_cmp + gt[2:3] * o_win
        yield

        ktp = jnp.maximum(qb - 1, 0)
        s_prev = add_block_pen(_dot(ks_ref[0, g, tile_rows(ktp), :], qT)
                               + tb_ref[g, jnp.where(qb >= 1, TB_PREV, TB_NONE)], pen_ref, g, ktp)
        s_scr[g, tile_rows(ktp), :] = s_prev
        s_own = add_block_pen(_dot(ks_ref[0, g, tile_rows(qb), :], qT) + tb_ref[g, TB_OWN], pen_ref, g, qb)
        s_scr[g, tile_rows(qb), :] = s_own
        m8_scr[g] = jnp.maximum(_fold8(s_prev, jnp.max), _fold8(s_own, jnp.max))
        l8_scr[g] = jnp.zeros((8, RT), f32)
        acc_scr[g] = jnp.zeros((HEAD_DIM, RT), f32)

    groups = [group_phases(g) for g in range(G)]
    while groups:
        groups = [phases for phases in groups if next(phases, False) is None]

    def pass_a(i, c):
        for g in range(G):
            qT = q_ref[0, g]
            m8 = m8_scr[g]
            for u in range(SLC_UNROLL):
                kt = SLC_UNROLL * i + u
                live = kt < qb - 1
                ktc = jnp.minimum(kt, nkt - 1)
                s = add_block_pen(_dot(ks_ref[0, g, tile_rows(ktc), :], qT) + jnp.where(live, 0.0, NEG),
                                  penf_ref, g, ktc)
                s_scr[g, tile_rows(jnp.where(live, kt, nkt + u)), :] = s
                m8 = jnp.maximum(m8, _fold8(s, jnp.max))
            m8_scr[g] = m8
        return c

    lax.fori_loop(0, (jnp.maximum(qb - 1, 0) + SLC_UNROLL - 1) // SLC_UNROLL, pass_a, 0)

    def pass_b(i, c):
        for g in range(G):
            m = jnp.max(m8_scr[g], axis=0, keepdims=True)
            l8 = l8_scr[g]
            acc = acc_scr[g]
            for u in range(SLC_UNROLL):
                kt = SLC_UNROLL * i + u
                ktc = jnp.minimum(kt, qb)
                p_u = jnp.exp2(s_scr[g, tile_rows(ktc), :] - (m + jnp.where(kt <= qb, 0.0, -NEG)))
                l8 = l8 + _fold8(p_u, jnp.sum)
                acc = acc + _dot(vst_ref[0, g, ktc], p_u.astype(bf16))
            l8_scr[g] = l8
            acc_scr[g] = acc
        return c

    lax.fori_loop(0, qb // SLC_UNROLL + 1, pass_b, 0)

    for g in range(G):
        o_slc = acc_scr[g] * (1.0 / jnp.maximum(jnp.sum(l8_scr[g], axis=0, keepdims=True), TINY))
        o_ref[0, g] = (part_scr[g] + gt_ref[0, g][1:2] * o_slc).astype(bf16)


def _bias_tables(rel_bias, S):
    G = N_KV_HEADS
    H = rel_bias.shape[1]
    r = H // G
    T = KEY_TILE
    n = jnp.arange(S + 2 * T, dtype=i32)
    max_exact = N_BUCKETS // 2
    nf = jnp.maximum(n, 1).astype(f32)
    large = max_exact + (jnp.log(nf / max_exact) / math.log(MAX_DISTANCE / max_exact)
                         * (N_BUCKETS - max_exact)).astype(i32)
    bucket = jnp.where(n < max_exact, n, jnp.minimum(large, N_BUCKETS - 1))
    bias1d = rel_bias.astype(f32)[bucket] * LOG2E

    def per_group(tab):
        lead = tab.shape[:-2]
        k = len(lead)
        t = jnp.moveaxis(tab, -1, 0).reshape((G, r) + lead + (T,))
        return jnp.moveaxis(t, 1, k + 1).reshape((G,) + lead + (r * T,))

    def masked(dist, ok):
        return jnp.where(jnp.asarray(ok)[..., None], bias1d[np.maximum(dist, 0)], NEG)

    def toeplitz(first):
        w = bias1d[np.maximum(first - (T - 1) + np.arange(2 * T), 0)]
        skew = jnp.broadcast_to(w[None], (T, 2 * T, H)).reshape(2 * T * T, H)[:T * (2 * T - 1)]
        return skew.reshape(T, 2 * T - 1, H)[:, T - 1:]

    kj, qi = np.arange(T)[:, None], np.arange(T)[None, :]
    ok = [qi - kj >= 0, np.ones((T, T), bool), 2 * T + qi - kj < WINDOW]
    tb = jnp.stack([jnp.where(jnp.asarray(ok[d])[..., None], toeplitz(d * T), NEG) for d in range(3)]
                   + [jnp.full((T, T, H), NEG, f32)])
    cend = (np.arange(CMP_NEAR)[:, None] - CMP_PAD) * CMP_STRIDE + CMP_LEN - 1
    dcn = np.arange(T)[None, :] - cend
    rows = CMP_PAD + S // CMP_STRIDE
    far = jnp.broadcast_to(bias1d[-1][None, None, :], (rows, T, H))
    cstep = jnp.concatenate([far, jnp.full((rows, T, H), NEG, f32)])
    return per_group(tb), per_group(masked(dcn, dcn >= 0)), per_group(cstep)


def _nsa_attention(q, gates_t, shared, tables, B, S):
    kcm, vct, ks, vst, kw, vwt = shared
    tb, cbn, cstep = tables
    G, DH, T = N_KV_HEADS, HEAD_DIM, Q_BLOCK
    RT = q.shape[3]
    r = RT // T
    nqb = S // T
    nch = S // CMP_STRIDE
    nsb = S // SLC_LEN
    nkt = S // KEY_TILE
    cmp_start = np.arange(nch) * CMP_STRIDE
    slc_start = np.arange(nsb) * SLC_LEN
    overlap = np.clip(np.minimum(cmp_start[:, None] + CMP_LEN, slc_start[None, :] + SLC_LEN)
                      - np.maximum(cmp_start[:, None], slc_start[None, :]), 0, None) / CMP_LEN
    overlap[nch - 1] = 0.0
    c2st = jnp.asarray(overlap.T, bf16)
    qmap = lambda b, i: (b * nqb + i, 0, 0, 0)
    bat = lambda b, i: (b, 0, 0, 0)
    bat5 = lambda b, i: (b, 0, 0, 0, 0)
    once = pl.Buffered(1)
    return pl.pallas_call(
        functools.partial(_nsa_kernel, r=r, nsb=nsb, n_sel=min(N_SEL, nsb)),
        out_shape=jax.ShapeDtypeStruct(q.shape, bf16),
        grid=(B, nqb),
        in_specs=[
            pl.BlockSpec((1, G, DH, RT), qmap),
            pl.BlockSpec((1, G, N_BRANCH, RT), qmap),
            pl.BlockSpec((1, G, nch, DH), bat),
            pl.BlockSpec((1, G, DH, nch), bat),
            pl.BlockSpec((1, G, S, DH), bat),
            pl.BlockSpec((1, G, nkt, DH, KEY_TILE), bat5),
            pl.BlockSpec((1, G, S, DH), bat),
            pl.BlockSpec((1, G, nkt, DH, KEY_TILE), bat5),
            pl.BlockSpec(cbn.shape, lambda b, i: (0, 0, 0), pipeline_mode=once),
            pl.BlockSpec(cstep.shape, lambda b, i: (0, 0, 0), pipeline_mode=once),
            pl.BlockSpec(tb.shape, lambda b, i: (0, 0, 0, 0), pipeline_mode=once),
            pl.BlockSpec((nsb, nch), lambda b, i: (0, 0), pipeline_mode=once),
        ],
        out_specs=pl.BlockSpec((1, G, DH, RT), qmap),
        scratch_shapes=[pltpu.VMEM((G, CMP_PAD + nch, RT), f32), pltpu.VMEM((G, nsb, RT), f32),
                        pltpu.VMEM((G, nsb, RT), f32), pltpu.VMEM((G, (nkt + SLC_UNROLL) * KEY_TILE, RT), f32),
                        pltpu.VMEM((G, 8, RT), f32), pltpu.VMEM((G, 8, RT), f32), pltpu.VMEM((G, DH, RT), f32),
                        pltpu.VMEM((G, DH, RT), f32)],
        compiler_params=_cparams("parallel", "arbitrary"),
        name="nsa_attention",
    )(q, gates_t, kcm, vct, ks, vst, kw, vwt, cbn, cstep, tb, c2st)


def _oproj_kernel(x_ref, a_ref, w_ref, o_ref):
    nt, G, DH, RT = a_ref.shape
    T = x_ref.shape[0] // nt
    r = RT // T
    rows = []
    for t in range(nt):
        cols = []
        for g in range(G):
            a = a_ref[t, g].astype(f32)
            cols.append(jnp.concatenate([a[:, k * T:(k + 1) * T] for k in range(r)], axis=0).T)
        rows.append(jnp.concatenate(cols, axis=1))
    attn = jnp.concatenate(rows, axis=0)
    o_ref[...] = x_ref[...] + _dot(attn.astype(bf16), w_ref[...])


def _out_project(xf, attn, w_o):
    N, D = xf.shape
    nt_all, G, DH, RT = attn.shape
    T = N // nt_all
    to = min(512, N)
    assert N % to == 0 and to % T == 0
    row = lambda i: (i, 0)
    return pl.pallas_call(
        _oproj_kernel,
        out_shape=jax.ShapeDtypeStruct((N, D), f32),
        grid=(N // to,),
        in_specs=[pl.BlockSpec((to, D), row), pl.BlockSpec((to // T, G, DH, RT), lambda i: (i, 0, 0, 0)),
                  pl.BlockSpec(w_o.shape, lambda i: (0, 0))],
        out_specs=pl.BlockSpec((to, D), row),
        compiler_params=_cparams("parallel"),
        name="out_project",
    )(xf, attn, w_o.astype(bf16))


def _nsa_layer(xf, B, S, g, w_qg, g_q, w_o, shared, tables):
    N = xf.shape[0]
    G, T = N_KV_HEADS, Q_BLOCK
    q, gates = _qg_project(xf, g, w_qg, g_q)
    r = gates.shape[1] // (G * N_BRANCH)
    gates_t = gates.reshape(N // T, T, G, r, N_BRANCH).transpose(0, 2, 4, 3, 1).reshape(N // T, G, N_BRANCH, r * T)
    attn = _nsa_attention(q, gates_t, shared, tables, B, S)
    return _out_project(xf, attn, w_o)


def kernel(x, rel_bias, norm_mix, norm_ffn, pool_w, pool_scale, norm_kv, w_kv, g_k, cmp_pos, cmp_w1, cmp_w2,
           w_qg, g_q, w_o, ffn_wg, ffn_wu, ffn_wd, router, moe_wg, moe_wu, moe_wd):
    B, S, D = x.shape
    depth = norm_mix.shape[0]
    n_a = depth // 2
    assert S % Q_BLOCK == 0
    xf = x.reshape(B * S, D)
    shared = None
    tables = _bias_tables(rel_bias, S)
    n_exp = moe_wg.shape[1]
    moe_w = [w.reshape((-1,) + w.shape[2:]).astype(bf16) for w in (moe_wg, moe_wu, moe_wd)]
    for layer in range(depth):
        if layer < n_a:
            xf = _pool_layer(xf.reshape(B, S, D), norm_mix[layer], pool_w[layer], pool_scale[layer]).reshape(B * S, D)
        else:
            j = layer - n_a
            xf = _nsa_layer(xf, B, S, norm_mix[layer], w_qg[j], g_q[j], w_o[j], shared, tables)
        i = layer // 2
        if layer % 2 == 0:
            xf = _ffn_layer(xf, norm_ffn[layer], ffn_wg[i], ffn_wu[i], ffn_wd[i])
        else:
            xf = _moe_layer(xf, norm_ffn[layer], router[i], *moe_w, first_expert=i * n_exp)
        if layer == n_a - 1:
            shared = _build_shared(xf, B, S, norm_kv, w_kv, g_k, cmp_pos, cmp_w1, cmp_w2)
    return xf.reshape(B, S, D)
```

```python
import functools
import math

import numpy as np
import jax
import jax.numpy as jnp
from jax import lax
from jax.experimental import pallas as pl
from jax.experimental.pallas import tpu as pltpu
from jax.experimental.pallas import tpu_sc as plsc

f32 = jnp.float32
bf16 = jnp.bfloat16
i32 = jnp.int32
u32 = jnp.uint32

POOL_WINDOWS = (2, 4, 8, 16)
HEAD_DIM = 64
N_KV_HEADS = 4
N_BRANCH = 3
CMP_LEN = 32
CMP_STRIDE = 16
CMP_HIDDEN = 4 * HEAD_DIM
SLC_LEN = 64
N_SEL = 4
WINDOW = 256
Q_BLOCK = 128
FORCED_SCORE = 1.0e4
N_BUCKETS = 32
MAX_DISTANCE = 128
TOP_K = 2
EPS = 1e-6
NEG = -1e30
TINY = 1e-30
LOG2E = math.log2(math.e)
TB_OWN, TB_PREV, TB_WIN2, TB_NONE = range(4)

KEY_TILE = Q_BLOCK
SLC_UNROLL = 4
POOL_HALO = 16
NH_PAD = 16
V7X_VMEM_LIMIT = 56 * 1024 * 1024
V7X_SC_CORES, V7X_SC_SUBCORES = 2, 16
SC_MAX_INDEX_VECTOR = 128
SC_GATHER_CHUNK = 32

CMP_PER_Q = Q_BLOCK // CMP_STRIDE
CMP_PAD = 2 * CMP_PER_Q
CMP_NEAR = 3 * CMP_PER_Q

assert CMP_LEN == 2 * CMP_STRIDE and KEY_TILE == 2 * SLC_LEN and WINDOW == 2 * KEY_TILE
assert max(POOL_WINDOWS) <= POOL_HALO
assert 2 * KEY_TILE - (Q_BLOCK - 1) >= MAX_DISTANCE
assert (CMP_PAD + 1) * CMP_STRIDE - (CMP_LEN - 1) >= MAX_DISTANCE and CMP_PER_Q % 8 == 0


def _cparams(*sem):
    return pltpu.CompilerParams(dimension_semantics=sem, vmem_limit_bytes=V7X_VMEM_LIMIT)


def _rms(xf, g):
    ms = jnp.mean(xf * xf, axis=-1, keepdims=True)
    return (xf * lax.rsqrt(ms + EPS)) * g


def _dot(a, b):
    return jnp.dot(a, b, preferred_element_type=f32)


def _dot_hilo(a, b):
    hi = a.astype(bf16)
    lo = (a - hi.astype(f32)).astype(bf16)
    return _dot(hi, b) + _dot(lo, b)


def _head_rms(z, gvec, seg, segt):
    ssq = _dot_hilo(z * z, seg)
    inv = lax.rsqrt(ssq * (1.0 / HEAD_DIM) + EPS)
    return (z * _dot_hilo(inv, segt)) * gvec


def _silu(a):
    return a * jax.nn.sigmoid(a)


def _seg_mats(width):
    heads = width // HEAD_DIM
    seg = np.zeros((width, NH_PAD), np.float32)
    seg[np.arange(width), np.arange(width) // HEAD_DIM] = 1.0
    assert heads <= NH_PAD
    return jnp.asarray(seg, bf16), jnp.asarray(seg.T, bf16)


def _pool_kernel(x_ref, halo_ref, g_ref, w_ref, scale_ref, o_ref, *, tp, cg):
    i = pl.program_id(1)
    x = x_ref[0]
    xh = jnp.concatenate([halo_ref[0], x], axis=0)
    h = _rms(xh, g_ref[...])
    row = lax.broadcasted_iota(i32, (tp + POOL_HALO, 1), 0)
    t_abs = i * tp + row - POOL_HALO
    h = jnp.where(t_abs >= 0, h, 0.0)
    outs = []
    for gi, w in enumerate(POOL_WINDOWS):
        hg = h[:, gi * cg:(gi + 1) * cg]
        s = hg
        sh = 1
        while sh < w:
            s = s + pltpu.roll(s, sh, axis=0)
            sh *= 2
        cnt = jnp.clip(t_abs + 1, 1, w).astype(f32)
        diff = (s / cnt - hg)[POOL_HALO:]
        outs.append(_dot(diff.astype(bf16), w_ref[gi]))
    y = jnp.concatenate(outs, axis=1)
    o_ref[0] = x + y * scale_ref[...]


def _pool_layer(x3, g, w_grp, scale):
    B, S, D = x3.shape
    tp = min(512, S)
    cg = D // len(POOL_WINDOWS)
    assert S % tp == 0 and tp % POOL_HALO == 0 and all(w & (w - 1) == 0 for w in POOL_WINDOWS)
    hb = tp // POOL_HALO
    return pl.pallas_call(
        functools.partial(_pool_kernel, tp=tp, cg=cg),
        out_shape=jax.ShapeDtypeStruct((B, S, D), f32),
        grid=(B, S // tp),
        in_specs=[
            pl.BlockSpec((1, tp, D), lambda b, i: (b, i, 0)),
            pl.BlockSpec((1, POOL_HALO, D), lambda b, i: (b, jnp.maximum(i * hb - 1, 0), 0)),
            pl.BlockSpec((1, D), lambda b, i: (0, 0)),
            pl.BlockSpec((len(POOL_WINDOWS), cg, cg), lambda b, i: (0, 0, 0)),
            pl.BlockSpec((1, D), lambda b, i: (0, 0)),
        ],
        out_specs=pl.BlockSpec((1, tp, D), lambda b, i: (b, i, 0)),
        compiler_params=_cparams("parallel", "arbitrary"),
        name="pool_layer",
    )(x3, x3, g.reshape(1, D), w_grp.astype(bf16), scale.reshape(1, D))


def _ffn_kernel(x_ref, g_ref, wg_ref, wu_ref, wd_ref, o_ref, h_ref, acc_ref):
    j = pl.program_id(1)

    @pl.when(j == 0)
    def _():
        x = x_ref[...]
        h_ref[...] = _rms(x, g_ref[...]).astype(bf16)
        acc_ref[...] = x

    h = h_ref[...]
    act = _silu(_dot(h, wg_ref[...])) * _dot(h, wu_ref[...])
    acc_ref[...] += _dot(act.astype(bf16), wd_ref[...])

    @pl.when(j == pl.num_programs(1) - 1)
    def _():
        o_ref[...] = acc_ref[...]


def _ffn_layer(xf, g, wg, wu, wd):
    N, D = xf.shape
    F = wg.shape[1]
    tm = min(1024, N)
    tf = 512
    assert N % tm == 0 and F % tf == 0
    return pl.pallas_call(
        _ffn_kernel,
        out_shape=jax.ShapeDtypeStruct((N, D), f32),
        grid=(N // tm, F // tf),
        in_specs=[
            pl.BlockSpec((tm, D), lambda i, j: (i, 0)),
            pl.BlockSpec((1, D), lambda i, j: (0, 0)),
            pl.BlockSpec((D, tf), lambda i, j: (0, j)),
            pl.BlockSpec((D, tf), lambda i, j: (0, j)),
            pl.BlockSpec((tf, D), lambda i, j: (j, 0)),
        ],
        out_specs=pl.BlockSpec((tm, D), lambda i, j: (i, 0)),
        scratch_shapes=[pltpu.VMEM((tm, D), bf16), pltpu.VMEM((tm, D), f32)],
        compiler_params=_cparams("parallel", "arbitrary"),
        name="ffn_dense",
    )(xf, g.reshape(1, D), wg.astype(bf16), wu.astype(bf16), wd.astype(bf16))


def _router_kernel(x_ref, g_ref, rt_ref, hp_ref, idx_ref, gate_ref):
    h = _rms(x_ref[...], g_ref[...])
    half = h.shape[1] // 2
    h_hi = h.astype(bf16)
    bits = pltpu.bitcast(h_hi.astype(f32), u32)
    hp_ref[...] = pltpu.bitcast((bits[:, :half] & jnp.uint32(0xFFFF0000)) | (bits[:, half:] >> 16), i32)
    h_lo = (h - h_hi.astype(f32)).astype(bf16)
    rt = rt_ref[...]
    rt_hi = rt.astype(bf16)
    rt_lo = (rt - rt_hi.astype(f32)).astype(bf16)
    nt_dot = lambda a, b: lax.dot_general(a, b, (((1,), (1,)), ((), ())), preferred_element_type=f32)
    logits = nt_dot(rt_hi, h_hi) + nt_dot(rt_hi, h_lo) + nt_dot(rt_lo, h_hi)
    ne = logits.shape[0]
    row = lax.broadcasted_iota(i32, logits.shape, 0)
    m1 = jnp.max(logits, axis=0, keepdims=True)
    i1 = jnp.min(jnp.where(logits == m1, row, ne), axis=0, keepdims=True)
    rest = jnp.where(row == i1, -jnp.inf, logits)
    m2 = jnp.max(rest, axis=0, keepdims=True)
    i2 = jnp.min(jnp.where(rest == m2, row, ne), axis=0, keepdims=True)
    e2 = jnp.exp(m2 - m1)
    den = 1.0 + e2
    idx_ref[...] = jnp.concatenate([i1, i2], axis=0)
    gate_ref[...] = jnp.concatenate([1.0 / den, e2 / den], axis=0)


def _router(xf, g, router):
    N, D = xf.shape
    E = router.shape[1]
    tr = min(1024, N)
    assert N % tr == 0 and TOP_K == 2
    return pl.pallas_call(
        _router_kernel,
        out_shape=(jax.ShapeDtypeStruct((N, D // 2), i32),
                   jax.ShapeDtypeStruct((TOP_K, N), i32),
                   jax.ShapeDtypeStruct((TOP_K, N), f32)),
        grid=(N // tr,),
        in_specs=[
            pl.BlockSpec((tr, D), lambda i: (i, 0)),
            pl.BlockSpec((1, D), lambda i: (0, 0)),
            pl.BlockSpec((E, D), lambda i: (0, 0)),
        ],
        out_specs=(pl.BlockSpec((tr, D // 2), lambda i: (i, 0)),
                   pl.BlockSpec((TOP_K, tr), lambda i: (0, i)),
                   pl.BlockSpec((TOP_K, tr), lambda i: (0, i))),
        compiler_params=_cparams("parallel"),
        name="moe_router",
    )(xf, g.reshape(1, D), router.T)


def _sc_gather_rows(table, idx, chunk):
    B = idx.shape[0]
    D = table.shape[1]
    workers = V7X_SC_CORES * V7X_SC_SUBCORES
    per_w = B // workers
    cpw = per_w // chunk
    assert B % (8 * workers) == 0 and per_w % (2 * chunk) == 0 and cpw % 8 == 0
    assert chunk % 8 == 0 and chunk <= SC_MAX_INDEX_VECTOR
    mesh = plsc.VectorSubcoreMesh(core_axis_name="c", subcore_axis_name="s")

    @functools.partial(
        pl.kernel, mesh=mesh, out_type=jax.ShapeDtypeStruct((B, D), table.dtype),
        scratch_types=[pltpu.VMEM((cpw, chunk), i32), pltpu.VMEM((2, chunk, D), table.dtype),
                       pltpu.SemaphoreType.DMA, pltpu.SemaphoreType.DMA])
    def gather(table_hbm, idx_hbm, out_hbm, idx_v, rows_v, sem0, sem1):
        wid = lax.axis_index("s") * V7X_SC_CORES + lax.axis_index("c")
        pltpu.sync_copy(idx_hbm.at[pl.ds(pl.multiple_of(wid * cpw, 8), cpw)], idx_v)
        sems = (sem0, sem1)

        def fetch(c, b):
            return pltpu.make_async_copy(table_hbm.at[idx_v.at[c]], rows_v.at[b], sems[b])

        fetch(0, 0).start()

        @pl.loop(0, cpw, step=2)
        def _(c):
            for b in range(2):
                cur = c + b
                fetch(cur, b).wait()

                @pl.when(cur + 1 < cpw)
                def _():
                    fetch(cur + 1, 1 - b).start()

                pltpu.sync_copy(rows_v.at[b], out_hbm.at[pl.ds(pl.multiple_of(wid * per_w + cur * chunk, 8), chunk)])

    return gather(table, idx.reshape(B // chunk, chunk))


def _sc_scatter_rows(src, slots, n_slots, chunk):
    K, N = slots.shape
    D = src.shape[1]
    workers = V7X_SC_CORES * V7X_SC_SUBCORES
    per_w = N // workers
    cpw = per_w // chunk
    assert N % (8 * workers) == 0 and per_w % (2 * chunk) == 0 and cpw % 8 == 0
    assert chunk % 8 == 0 and chunk <= SC_MAX_INDEX_VECTOR
    mesh = plsc.VectorSubcoreMesh(core_axis_name="c", subcore_axis_name="s")

    @functools.partial(
        pl.kernel, mesh=mesh, out_type=jax.ShapeDtypeStruct((n_slots, D), src.dtype),
        scratch_types=[pltpu.VMEM((K, cpw, chunk), i32), pltpu.VMEM((2, chunk, D), src.dtype),
                       pltpu.SemaphoreType.DMA, pltpu.SemaphoreType.DMA, pltpu.SemaphoreType.DMA,
                       pltpu.SemaphoreType.DMA])
    def scatter(src_hbm, slots_hbm, out_hbm, slot_v, rows_v, lsem0, lsem1, ssem0, ssem1):
        wid = lax.axis_index("s") * V7X_SC_CORES + lax.axis_index("c")
        for k in range(K):
            pltpu.sync_copy(slots_hbm.at[k, pl.ds(pl.multiple_of(wid * cpw, 8), cpw)], slot_v.at[k])
        lsems, ssems = (lsem0, lsem1), (ssem0, ssem1)

        def load(c, b):
            return pltpu.make_async_copy(
                src_hbm.at[pl.ds(pl.multiple_of(wid * per_w + c * chunk, 8), chunk)], rows_v.at[b], lsems[b])

        def store(c, b, k):
            return pltpu.make_async_copy(rows_v.at[b], out_hbm.at[slot_v.at[k].at[c]], ssems[b])

        load(0, 0).start()

        @pl.loop(0, cpw, step=2)
        def _(c):
            for b in range(2):
                cur = c + b
                load(cur, b).wait()
                for k in range(K):
                    store(cur, b, k).start()

                @pl.when(cur >= 1)
                def _():
                    for k in range(K):
                        store(cur - 1, 1 - b, k).wait()

                @pl.when(cur + 1 < cpw)
                def _():
                    load(cur + 1, 1 - b).start()

        for k in range(K):
            store(cpw - 1, 1, k).wait()

    return scatter(src, slots.reshape(K, N // chunk, chunk))


def _gmm_kernel(te_ref, nu_ref, xs_ref, wg_ref, wu_ref, wd_ref, o_ref, h_ref, acc_ref):
    i = pl.program_id(0)
    j = pl.program_id(1)

    @pl.when(i < nu_ref[0])
    def _():
        @pl.when(j == 0)
        def _():
            w = pltpu.bitcast(xs_ref[...], u32)
            left = pltpu.bitcast(w & jnp.uint32(0xFFFF0000), f32)
            right = pltpu.bitcast(w << 16, f32)
            h = jnp.concatenate([left, right], axis=1)
            filled = lax.broadcasted_iota(i32, (h.shape[0], 1), 0) < nu_ref[1 + i]
            h_ref[...] = jnp.where(filled, h, 0.0).astype(bf16)
            acc_ref[...] = jnp.zeros_like(acc_ref)

        h = h_ref[...]
        act = _silu(_dot(h, wg_ref[0].astype(bf16))) * _dot(h, wu_ref[0].astype(bf16))
        acc_ref[...] += _dot(act.astype(bf16), wd_ref[0].astype(bf16))

        @pl.when(j == pl.num_programs(1) - 1)
        def _():
            o_ref[...] = acc_ref[...]

    @pl.when((i >= nu_ref[0]) & (j == 0))
    def _():
        o_ref[...] = jnp.zeros_like(o_ref)


def _gmm(tile_expert, n_used, xs, wg, wu, wd, tm):
    P, half = xs.shape
    D = 2 * half
    F = wg.shape[2]
    tf = 512
    assert P % tm == 0 and F % tf == 0

    nf = F // tf

    def row_map(i, j, te, nu):
        return (jnp.minimum(i, nu[0] - 1), 0)

    def ff(i, j, nu):
        return jnp.where(i < nu[0], j, nf - 1)

    grid_spec = pltpu.PrefetchScalarGridSpec(
        num_scalar_prefetch=2,
        grid=(P // tm, nf),
        in_specs=[
            pl.BlockSpec((tm, half), row_map),
            pl.BlockSpec((1, D, tf), lambda i, j, te, nu: (te[i], 0, ff(i, j, nu))),
            pl.BlockSpec((1, D, tf), lambda i, j, te, nu: (te[i], 0, ff(i, j, nu))),
            pl.BlockSpec((1, tf, D), lambda i, j, te, nu: (te[i], ff(i, j, nu), 0)),
        ],
        out_specs=pl.BlockSpec((tm, D), lambda i, j, te, nu: (i, 0)),
        scratch_shapes=[pltpu.VMEM((tm, D), bf16), pltpu.VMEM((tm, D), f32)],
    )
    return pl.pallas_call(
        _gmm_kernel,
        out_shape=jax.ShapeDtypeStruct((P, D), f32),
        grid_spec=grid_spec,
        compiler_params=_cparams("arbitrary", "arbitrary"),
        name="moe_gmm",
    )(tile_expert, n_used, xs, wg, wu, wd)


def _combine_kernel(x_ref, gate_ref, y_ref, o_ref):
    gate = gate_ref[...]
    acc = x_ref[...]
    for k in range(TOP_K):
        acc = acc + y_ref[k] * gate[:, k:k + 1]
    o_ref[...] = acc


def _combine(xf, gate_tk, y_tok):
    N, D = xf.shape
    tt = min(1024, N)
    assert N % tt == 0
    return pl.pallas_call(
        _combine_kernel,
        out_shape=jax.ShapeDtypeStruct((N, D), f32),
        grid=(N // tt,),
        in_specs=[pl.BlockSpec((tt, D), lambda i: (i, 0)),
                  pl.BlockSpec((tt, TOP_K), lambda i: (i, 0)),
                  pl.BlockSpec((TOP_K, tt, D), lambda i: (0, i, 0))],
        out_specs=pl.BlockSpec((tt, D), lambda i: (i, 0)),
        compiler_params=_cparams("parallel"),
        name="moe_combine",
    )(xf, gate_tk, y_tok)


def _moe_layer(xf, g, router, wg, wu, wd, first_expert):
    N, D = xf.shape
    E = router.shape[1]
    tm = min(1024, N)
    hp, idx, gate = _router(xf, g, router)
    e_flat = idx.reshape(-1)
    onehot = (e_flat[:, None] == jnp.arange(E, dtype=i32)[None, :]).astype(i32)
    csum = jnp.cumsum(onehot, axis=0)
    counts = csum[-1]
    padded = ((counts + tm - 1) // tm) * tm
    ends = jnp.cumsum(padded)
    starts = ends - padded
    pos = (jnp.sum((csum + starts[None, :]) * onehot, axis=1) - 1).astype(i32)
    P = TOP_K * N + E * tm
    n_used = (ends[-1] // tm).astype(i32)
    tile_start = jnp.minimum(jnp.arange(P // tm, dtype=i32), n_used - 1) * tm
    tile_expert = jnp.sum((ends[None, :] <= tile_start[:, None]).astype(i32), axis=1)
    group_end = (starts + counts)[tile_expert]
    tile_rows = jnp.clip(group_end - tile_start, 0, tm).astype(i32)
    xs = _sc_scatter_rows(hp, pos.reshape(TOP_K, N), P, SC_GATHER_CHUNK)
    ys = _gmm(tile_expert + first_expert, jnp.concatenate([n_used.reshape(1), tile_rows]), xs, wg, wu, wd, tm)
    y_tok = _sc_gather_rows(ys, pos, SC_GATHER_CHUNK).reshape(TOP_K, N, D)
    return _combine(xf, gate.T, y_tok)


def _kv_kernel(x_ref, g_ref, w_ref, gk_ref, seg_ref, segt_ref, kc_ref, vc_ref, ks_ref, vs_ref, kw_ref, vw_ref):
    h = _rms(x_ref[...], g_ref[...]).astype(bf16)
    kv = _dot(h, w_ref[...])
    G, DH = ks_ref.shape[1], ks_ref.shape[3]
    wd = G * DH
    part = lambda p: kv[:, p * wd:(p + 1) * wd]
    seg, segt = seg_ref[...], segt_ref[...]
    for c_ref, pc in ((kc_ref, 0), (vc_ref, 1)):
        z = part(pc)
        for g in range(G):
            c_ref[0, g] = z[:, g * DH:(g + 1) * DH]
    for k_ref, v_ref, pk, gain in ((ks_ref, vs_ref, 2, gk_ref[0:1, :]), (kw_ref, vw_ref, 4, gk_ref[1:2, :])):
        kn = _head_rms(part(pk), gain, seg, segt).astype(bf16)
        v = part(pk + 1)
        for g in range(G):
            k_ref[0, g] = kn[:, g * DH:(g + 1) * DH]
        for t in range(v_ref.shape[2]):
            vt = v[t * KEY_TILE:(t + 1) * KEY_TILE, :].T
            for g in range(G):
                v_ref[0, g, t] = vt[g * DH:(g + 1) * DH].astype(bf16)


def _kv_project(xf, B, S, norm_kv, w_kv, g_k):
    N, D = xf.shape
    G, DH = N_KV_HEADS, HEAD_DIM
    wd = G * DH
    tk = min(512, S)
    spb = S // tk
    tpk = tk // KEY_TILE
    assert S % tk == 0 and tk % KEY_TILE == 0 and w_kv.shape[1] == 2 * N_BRANCH * wd
    seg, segt = _seg_mats(wd)
    gk = jnp.stack([jnp.tile(g_k[1], G), jnp.tile(g_k[2], G)])
    row = lambda i: (i, 0)
    const = lambda i: (0, 0)
    k_shape = jax.ShapeDtypeStruct((B, G, S, DH), bf16)
    v_shape = jax.ShapeDtypeStruct((B, G, S // KEY_TILE, DH, KEY_TILE), bf16)
    k_spec = pl.BlockSpec((1, G, tk, DH), lambda i: (i // spb, 0, i % spb, 0))
    v_spec = pl.BlockSpec((1, G, tpk, DH, KEY_TILE), lambda i: (i // spb, 0, i % spb, 0, 0))
    return pl.pallas_call(
        _kv_kernel,
        out_shape=(jax.ShapeDtypeStruct((B, G, S, DH), f32), jax.ShapeDtypeStruct((B, G, S, DH), f32),
                   k_shape, v_shape, k_shape, v_shape),
        grid=(N // tk,),
        in_specs=[
            pl.BlockSpec((tk, D), row),
            pl.BlockSpec((1, D), const),
            pl.BlockSpec(w_kv.shape, const),
            pl.BlockSpec((2, wd), const),
            pl.BlockSpec((wd, NH_PAD), const),
            pl.BlockSpec((NH_PAD, wd), const),
        ],
        out_specs=(k_spec, k_spec, k_spec, v_spec, k_spec, v_spec),
        compiler_params=_cparams("parallel"),
        name="kv_project",
    )(xf, norm_kv.reshape(1, D), w_kv.astype(bf16), gk, seg, segt)


def _compress_kernel(c_ref, pos_ref, w1_ref, w2_ref, gk_ref, o_ref):
    kv = pl.program_id(0)
    c = c_ref[0, 0, 0]
    a = _dot((c + pos_ref[0, 0:1, :]).astype(bf16), w1_ref[0, 0])
    b = _dot((c + pos_ref[0, 1:2, :]).astype(bf16), w1_ref[0, 1])
    n = c.shape[0]
    hid = a + pltpu.roll(b, n - 1, axis=0)
    out = _dot(_silu(hid).astype(bf16), w2_ref[0])

    @pl.when(kv == 0)
    def _():
        o_ref[0, 0, 0] = _rms(out, gk_ref[...])

    @pl.when(kv != 0)
    def _():
        o_ref[0, 0, 0] = out


def _compress(kc, vc, B, S, cmp_pos, cmp_w1, cmp_w2, g_k0):
    G, DH = N_KV_HEADS, HEAD_DIM
    nch = S // CMP_STRIDE
    cw = CMP_STRIDE * DH

    c = jnp.stack([kc.reshape(B, G, nch, cw), vc.reshape(B, G, nch, cw)])
    pos = cmp_pos.reshape(2, 2, cw)
    w1 = cmp_w1.reshape(2, 2, cw, CMP_HIDDEN).astype(bf16)
    return pl.pallas_call(
        _compress_kernel,
        out_shape=jax.ShapeDtypeStruct((2, B, G, nch, DH), f32),
        grid=(2, B, G),
        in_specs=[
            pl.BlockSpec((1, 1, 1, nch, cw), lambda k, b, g: (k, b, g, 0, 0)),
            pl.BlockSpec((1, 2, cw), lambda k, b, g: (k, 0, 0)),
            pl.BlockSpec((1, 2, cw, CMP_HIDDEN), lambda k, b, g: (k, 0, 0, 0)),
            pl.BlockSpec((1, CMP_HIDDEN, DH), lambda k, b, g: (k, 0, 0)),
            pl.BlockSpec((1, DH), lambda k, b, g: (0, 0)),
        ],
        out_specs=pl.BlockSpec((1, 1, 1, nch, DH), lambda k, b, g: (k, b, g, 0, 0)),
        compiler_params=_cparams("arbitrary", "arbitrary", "arbitrary"),
        name="kv_compress",
    )(c, pos, w1, cmp_w2.astype(bf16), g_k0.reshape(1, DH))


def _build_shared(xf, B, S, norm_kv, w_kv, g_k, cmp_pos, cmp_w1, cmp_w2):
    kc, vc, ks, vst, kw, vwt = _kv_project(xf, B, S, norm_kv, w_kv, g_k)
    cmp = _compress(kc, vc, B, S, cmp_pos, cmp_w1, cmp_w2, g_k[0]).astype(bf16)
    kcm = cmp[0]
    vct = cmp[1].transpose(0, 1, 3, 2)
    return kcm, vct, ks, vst, kw, vwt


def _qg_kernel(x_ref, g_ref, wq_ref, wgate_ref, gq_ref, seg_ref, segt_ref, q_ref, gate_ref):
    h = _rms(x_ref[...], g_ref[...]).astype(bf16)
    q_raw = _dot(h, wq_ref[...])
    gate_ref[...] = jax.nn.sigmoid(_dot(h, wgate_ref[...]))
    nt, G, DH, RT = q_ref.shape
    T = q_raw.shape[0] // nt
    r = RT // T

    def tile_phases(t):
        q = _head_rms(q_raw[t * T:(t + 1) * T], gq_ref[...], seg_ref[...], segt_ref[...])
        q = q * (HEAD_DIM ** -0.5 * LOG2E)
        yield
        for g in range(G):
            qt = q[:, g * r * DH:(g + 1) * r * DH].T
            q_ref[t, g] = jnp.concatenate([qt[k * DH:(k + 1) * DH] for k in range(r)], axis=1).astype(bf16)
            yield

    tiles = [tile_phases(t) for t in range(nt)]
    while tiles:
        tiles = [phases for phases in tiles if next(phases, False) is None]


def _qg_project(xf, g, w_qg, g_q):
    N, D = xf.shape
    HD = D
    ng = w_qg.shape[1] - HD
    tq = min(512, N)
    G, T = N_KV_HEADS, Q_BLOCK
    RT = HD // (G * HEAD_DIM) * T
    assert N % tq == 0 and tq % T == 0
    seg, segt = _seg_mats(HD)
    row = lambda i: (i, 0)
    const = lambda i: (0, 0)
    return pl.pallas_call(
        _qg_kernel,
        out_shape=(jax.ShapeDtypeStruct((N // T, G, HEAD_DIM, RT), bf16), jax.ShapeDtypeStruct((N, ng), f32)),
        grid=(N // tq,),
        in_specs=[
            pl.BlockSpec((tq, D), row),
            pl.BlockSpec((1, D), const),
            pl.BlockSpec((D, HD), const),
            pl.BlockSpec((D, ng), const),
            pl.BlockSpec((1, HD), const),
            pl.BlockSpec((HD, NH_PAD), const),
            pl.BlockSpec((NH_PAD, HD), const),
        ],
        out_specs=(pl.BlockSpec((tq // T, G, HEAD_DIM, RT), lambda i: (i, 0, 0, 0)), pl.BlockSpec((tq, ng), row)),
        compiler_params=_cparams("parallel"),
        name="qg_project",
    )(xf, g.reshape(1, D), w_qg[:, :HD].astype(bf16), w_qg[:, HD:].astype(bf16),
      jnp.tile(g_q, HD // HEAD_DIM).reshape(1, HD), seg, segt)


def _fold8(x, op):
    return op(x.reshape(x.shape[0] // 8, 8, x.shape[1]), axis=0)


def _nsa_kernel(q_ref, gt_ref, kc_ref, vct_ref, ks_ref, vst_ref, kw_ref, vwt_ref, cbn_ref, cstep_ref, tb_ref, c2st_ref,
                o_ref, cb_scr, pen_ref, penf_ref, s_scr, m8_scr, l8_scr, acc_scr, part_scr, *, r, nsb, n_sel):
    qb = pl.program_id(1)
    t0 = qb * Q_BLOCK
    T = Q_BLOCK
    RT = r * T
    G = q_ref.shape[1]
    nch = kc_ref.shape[2]
    nkt = s_scr.shape[1] // KEY_TILE - SLC_UNROLL

    def tile_rows(kt):
        return pl.ds(pl.multiple_of(kt * KEY_TILE, KEY_TILE), KEY_TILE)

    def add_block_pen(s, ref, g, kt):
        return jnp.concatenate([s[:SLC_LEN] + ref[g, pl.ds(2 * kt, 1), :],
                                s[SLC_LEN:] + ref[g, pl.ds(2 * kt + 1, 1), :]], axis=0)

    def group_phases(g):
        qT = q_ref[0, g]

        cb_scr[g] = cstep_ref[g, pl.ds(pl.multiple_of(nch - (qb + 1) * CMP_PER_Q, 8), CMP_PAD + nch), :]
        cb_scr[g, pl.ds(pl.multiple_of(qb * CMP_PER_Q, 8), CMP_NEAR), :] = cbn_ref[g]
        s = _dot(kc_ref[0, g], qT) + cb_scr[g, CMP_PAD:CMP_PAD + nch, :]
        yield
        p = jnp.where(s > 0.5 * NEG, jnp.exp2(s - jnp.max(s, axis=0, keepdims=True)), 0.0)
        p = p * (1.0 / jnp.maximum(jnp.sum(p, axis=0, keepdims=True), TINY))
        o_cmp = _dot(vct_ref[0, g], p.astype(bf16))
        yield

        psum = p[:, 0:T]
        for k in range(1, r):
            psum = psum + p[:, k * T:(k + 1) * T]
        hi = psum.astype(bf16)
        lo = (psum - hi.astype(f32)).astype(bf16)
        imp = _dot(c2st_ref[...], hi) + _dot(c2st_ref[...], lo)
        jb = lax.broadcasted_iota(i32, (nsb, T), 0)
        blk_q = jnp.right_shift(t0 + lax.broadcasted_iota(i32, (nsb, T), 1), SLC_LEN.bit_length() - 1)
        forced = (jb == 0) | (jb == blk_q) | (jb == blk_q - 1)
        score = jnp.where(forced, FORCED_SCORE, jnp.where(jb <= blk_q, imp, NEG))
        pen = jnp.full((nsb, T), NEG, f32)
        for _ in range(n_sel):
            mx = jnp.max(score, axis=0, keepdims=True)
            first = jnp.min(jnp.where(score == mx, jb, nsb), axis=0, keepdims=True)
            hit = jb == first
            pen = jnp.where(hit, 0.0, pen)
            score = jnp.where(hit, -jnp.inf, score)
        pen = jnp.concatenate([pen] * r, axis=1)
        pen_ref[g] = pen
        penf_ref[g] = pen + cstep_ref[g, 0:1, :]
        yield

        win = []
        for d in range(WINDOW // KEY_TILE, -1, -1):
            kt = qb - d
            ktc = jnp.maximum(kt, 0)
            tab = tb_ref[g, jnp.where(kt >= 0, d, TB_NONE)]
            win.append((_dot(kw_ref[0, g, tile_rows(ktc), :], qT) + tab, vwt_ref[0, g, ktc]))
        yield
        m8 = _fold8(win[0][0], jnp.max)
        for s_d, _ in win[1:]:
            m8 = jnp.maximum(m8, _fold8(s_d, jnp.max))
        m = jnp.max(m8, axis=0, keepdims=True)
        l8 = jnp.zeros((8, RT), f32)
        acc = jnp.zeros((HEAD_DIM, RT), f32)
        for s_d, v_d in win:
            p_d = jnp.exp2(s_d - m)
            l8 = l8 + _fold8(p_d, jnp.sum)
            acc = acc + _dot(v_d, p_d.astype(bf16))
        o_win = acc * (1.0 / jnp.maximum(jnp.sum(l8, axis=0, keepdims=True), TINY))

        gt = gt_ref[0, g]
        part_scr[g] = gt[0:1] * o_cmp + gt[2:3] * o_win
        yield

        ktp = jnp.maximum(qb - 1, 0)
        s_prev = add_block_pen(_dot(ks_ref[0, g, tile_rows(ktp), :], qT)
                               + tb_ref[g, jnp.where(qb >= 1, TB_PREV, TB_NONE)], pen_ref, g, ktp)
        s_scr[g, tile_rows(ktp), :] = s_prev
        s_own = add_block_pen(_dot(ks_ref[0, g, tile_rows(qb), :], qT) + tb_ref[g, TB_OWN], pen_ref, g, qb)
        s_scr[g, tile_rows(qb), :] = s_own
        m8_scr[g] = jnp.maximum(_fold8(s_prev, jnp.max), _fold8(s_own, jnp.max))
        l8_scr[g] = jnp.zeros((8, RT), f32)
        acc_scr[g] = jnp.zeros((HEAD_DIM, RT), f32)

    groups = [group_phases(g) for g in range(G)]
    while groups:
        groups = [phases for phases in groups if next(phases, False) is None]

    def pass_a(i, c):
        for g in range(G):
            qT = q_ref[0, g]
            m8 = m8_scr[g]
            for u in range(SLC_UNROLL):
                kt = SLC_UNROLL * i + u
                live = kt < qb - 1
                ktc = jnp.minimum(kt, nkt - 1)
                s = add_block_pen(_dot(ks_ref[0, g, tile_rows(ktc), :], qT) + jnp.where(live, 0.0, NEG),
                                  penf_ref, g, ktc)
                s_scr[g, tile_rows(jnp.where(live, kt, nkt + u)), :] = s
                m8 = jnp.maximum(m8, _fold8(s, jnp.max))
            m8_scr[g] = m8
        return c

    lax.fori_loop(0, (jnp.maximum(qb - 1, 0) + SLC_UNROLL - 1) // SLC_UNROLL, pass_a, 0)

    def pass_b(i, c):
        for g in range(G):
            m = jnp.max(m8_scr[g], axis=0, keepdims=True)
            l8 = l8_scr[g]
            acc = acc_scr[g]
            for u in range(SLC_UNROLL):
                kt = SLC_UNROLL * i + u
                ktc = jnp.minimum(kt, qb)
                p_u = jnp.exp2(s_scr[g, tile_rows(ktc), :] - (m + jnp.where(kt <= qb, 0.0, -NEG)))
                l8 = l8 + _fold8(p_u, jnp.sum)
                acc = acc + _dot(vst_ref[0, g, ktc], p_u.astype(bf16))
            l8_scr[g] = l8
            acc_scr[g] = acc
        return c

    lax.fori_loop(0, qb // SLC_UNROLL + 1, pass_b, 0)

    for g in range(G):
        o_slc = acc_scr[g] * (1.0 / jnp.maximum(jnp.sum(l8_scr[g], axis=0, keepdims=True), TINY))
        o_ref[0, g] = (part_scr[g] + gt_ref[0, g][1:2] * o_slc).astype(bf16)


def _bias_tables(rel_bias, S):
    G = N_KV_HEADS
    H = rel_bias.shape[1]
    r = H // G
    T = KEY_TILE
    n = jnp.arange(S + 2 * T, dtype=i32)
    max_exact = N_BUCKETS // 2
    nf = jnp.maximum(n, 1).astype(f32)
    large = max_exact + (jnp.log(nf / max_exact) / math.log(MAX_DISTANCE / max_exact)
                         * (N_BUCKETS - max_exact)).astype(i32)
    bucket = jnp.where(n < max_exact, n, jnp.minimum(large, N_BUCKETS - 1))
    bias1d = rel_bias.astype(f32)[bucket] * LOG2E

    def per_group(tab):
        lead = tab.shape[:-2]
        k = len(lead)
        t = jnp.moveaxis(tab, -1, 0).reshape((G, r) + lead + (T,))
        return jnp.moveaxis(t, 1, k + 1).reshape((G,) + lead + (r * T,))

    def masked(dist, ok):
        return jnp.where(jnp.asarray(ok)[..., None], bias1d[np.maximum(dist, 0)], NEG)

    def toeplitz(first):
        w = bias1d[np.maximum(first - (T - 1) + np.arange(2 * T), 0)]
        skew = jnp.broadcast_to(w[None], (T, 2 * T, H)).reshape(2 * T * T, H)[:T * (2 * T - 1)]
        return skew.reshape(T, 2 * T - 1, H)[:, T - 1:]

    kj, qi = np.arange(T)[:, None], np.arange(T)[None, :]
    ok = [qi - kj >= 0, np.ones((T, T), bool), 2 * T + qi - kj < WINDOW]
    tb = jnp.stack([jnp.where(jnp.asarray(ok[d])[..., None], toeplitz(d * T), NEG) for d in range(3)]
                   + [jnp.full((T, T, H), NEG, f32)])
    cend = (np.arange(CMP_NEAR)[:, None] - CMP_PAD) * CMP_STRIDE + CMP_LEN - 1
    dcn = np.arange(T)[None, :] - cend
    rows = CMP_PAD + S // CMP_STRIDE
    far = jnp.broadcast_to(bias1d[-1][None, None, :], (rows, T, H))
    cstep = jnp.concatenate([far, jnp.full((rows, T, H), NEG, f32)])
    return per_group(tb), per_group(masked(dcn, dcn >= 0)), per_group(cstep)


def _nsa_attention(q, gates_t, shared, tables, B, S):
    kcm, vct, ks, vst, kw, vwt = shared
    tb, cbn, cstep = tables
    G, DH, T = N_KV_HEADS, HEAD_DIM, Q_BLOCK
    RT = q.shape[3]
    r = RT // T
    nqb = S // T
    nch = S // CMP_STRIDE
    nsb = S // SLC_LEN
    nkt = S // KEY_TILE
    cmp_start = np.arange(nch) * CMP_STRIDE
    slc_start = np.arange(nsb) * SLC_LEN
    overlap = np.clip(np.minimum(cmp_start[:, None] + CMP_LEN, slc_start[None, :] + SLC_LEN)
                      - np.maximum(cmp_start[:, None], slc_start[None, :]), 0, None) / CMP_LEN
    overlap[nch - 1] = 0.0
    c2st = jnp.asarray(overlap.T, bf16)
    qmap = lambda b, i: (b * nqb + i, 0, 0, 0)
    bat = lambda b, i: (b, 0, 0, 0)
    bat5 = lambda b, i: (b, 0, 0, 0, 0)
    once = pl.Buffered(1)
    return pl.pallas_call(
        functools.partial(_nsa_kernel, r=r, nsb=nsb, n_sel=min(N_SEL, nsb)),
        out_shape=jax.ShapeDtypeStruct(q.shape, bf16),
        grid=(B, nqb),
        in_specs=[
            pl.BlockSpec((1, G, DH, RT), qmap),
            pl.BlockSpec((1, G, N_BRANCH, RT), qmap),
            pl.BlockSpec((1, G, nch, DH), bat),
            pl.BlockSpec((1, G, DH, nch), bat),
            pl.BlockSpec((1, G, S, DH), bat),
            pl.BlockSpec((1, G, nkt, DH, KEY_TILE), bat5),
            pl.BlockSpec((1, G, S, DH), bat),
            pl.BlockSpec((1, G, nkt, DH, KEY_TILE), bat5),
            pl.BlockSpec(cbn.shape, lambda b, i: (0, 0, 0), pipeline_mode=once),
            pl.BlockSpec(cstep.shape, lambda b, i: (0, 0, 0), pipeline_mode=once),
            pl.BlockSpec(tb.shape, lambda b, i: (0, 0, 0, 0), pipeline_mode=once),
            pl.BlockSpec((nsb, nch), lambda b, i: (0, 0), pipeline_mode=once),
        ],
        out_specs=pl.BlockSpec((1, G, DH, RT), qmap),
        scratch_shapes=[pltpu.VMEM((G, CMP_PAD + nch, RT), f32), pltpu.VMEM((G, nsb, RT), f32),
                        pltpu.VMEM((G, nsb, RT), f32), pltpu.VMEM((G, (nkt + SLC_UNROLL) * KEY_TILE, RT), f32),
                        pltpu.VMEM((G, 8, RT), f32), pltpu.VMEM((G, 8, RT), f32), pltpu.VMEM((G, DH, RT), f32),
                        pltpu.VMEM((G, DH, RT), f32)],
        compiler_params=_cparams("parallel", "arbitrary"),
        name="nsa_attention",
    )(q, gates_t, kcm, vct, ks, vst, kw, vwt, cbn, cstep, tb, c2st)


def _oproj_kernel(x_ref, a_ref, w_ref, o_ref):
    nt, G, DH, RT = a_ref.shape
    T = x_ref.shape[0] // nt
    r = RT // T
    rows = []
    for t in range(nt):
        cols = []
        for g in range(G):
            a = a_ref[t, g].astype(f32)
            cols.append(jnp.concatenate([a[:, k * T:(k + 1) * T] for k in range(r)], axis=0).T)
        rows.append(jnp.concatenate(cols, axis=1))
    attn = jnp.concatenate(rows, axis=0)
    o_ref[...] = x_ref[...] + _dot(attn.astype(bf16), w_ref[...])


def _out_project(xf, attn, w_o):
    N, D = xf.shape
    nt_all, G, DH, RT = attn.shape
    T = N // nt_all
    to = min(512, N)
    assert N % to == 0 and to % T == 0
    row = lambda i: (i, 0)
    return pl.pallas_call(
        _oproj_kernel,
        out_shape=jax.ShapeDtypeStruct((N, D), f32),
        grid=(N // to,),
        in_specs=[pl.BlockSpec((to, D), row), pl.BlockSpec((to // T, G, DH, RT), lambda i: (i, 0, 0, 0)),
                  pl.BlockSpec(w_o.shape, lambda i: (0, 0))],
        out_specs=pl.BlockSpec((to, D), row),
        compiler_params=_cparams("parallel"),
        name="out_project",
    )(xf, attn, w_o.astype(bf16))


def _nsa_layer(xf, B, S, g, w_qg, g_q, w_o, shared, tables):
    N = xf.shape[0]
    G, T = N_KV_HEADS, Q_BLOCK
    q, gates = _qg_project(xf, g, w_qg, g_q)
    r = gates.shape[1] // (G * N_BRANCH)
    gates_t = gates.reshape(N // T, T, G, r, N_BRANCH).transpose(0, 2, 4, 3, 1).reshape(N // T, G, N_BRANCH, r * T)
    attn = _nsa_attention(q, gates_t, shared, tables, B, S)
    return _out_project(xf, attn, w_o)


def kernel(x, rel_bias, norm_mix, norm_ffn, pool_w, pool_scale, norm_kv, w_kv, g_k, cmp_pos, cmp_w1, cmp_w2,
           w_qg, g_q, w_o, ffn_wg, ffn_wu, ffn_wd, router, moe_wg, moe_wu, moe_wd):
    B, S, D = x.shape
    depth = norm_mix.shape[0]
    n_a = depth // 2
    assert S % Q_BLOCK == 0
    xf = x.reshape(B * S, D)
    shared = None
    tables = _bias_tables(rel_bias, S)
    n_exp = moe_wg.shape[1]
    moe_w = [w.reshape((-1,) + w.shape[2:]) for w in (moe_wg, moe_wu, moe_wd)]
    for layer in range(depth):
        if layer < n_a:
            xf = _pool_layer(xf.reshape(B, S, D), norm_mix[layer], pool_w[layer], pool_scale[layer]).reshape(B * S, D)
        else:
            j = layer - n_a
            xf = _nsa_layer(xf, B, S, norm_mix[layer], w_qg[j], g_q[j], w_o[j], shared, tables)
        i = layer // 2
        if layer % 2 == 0:
            xf = _ffn_layer(xf, norm_ffn[layer], ffn_wg[i], ffn_wu[i], ffn_wd[i])
        else:
            xf = _moe_layer(xf, norm_ffn[layer], router[i], *moe_w, first_expert=i * n_exp)
        if layer == n_a - 1:
            shared = _build_shared(xf, B, S, norm_kv, w_kv, g_k, cmp_pos, cmp_w1, cmp_w2)
    return xf.reshape(B, S, D)
```

```python
import functools
import math

import numpy as np
import jax
import jax.numpy as jnp
from jax import lax
from jax.experimental import pallas as pl
from jax.experimental.pallas import tpu as pltpu
from jax.experimental.pallas import tpu_sc as plsc

f32 = jnp.float32
bf16 = jnp.bfloat16
i32 = jnp.int32
u32 = jnp.uint32

POOL_WINDOWS = (2, 4, 8, 16)
HEAD_DIM = 64
N_KV_HEADS = 4
N_BRANCH = 3
CMP_LEN = 32
CMP_STRIDE = 16
CMP_HIDDEN = 4 * HEAD_DIM
SLC_LEN = 64
N_SEL = 4
WINDOW = 256
Q_BLOCK = 128
FORCED_SCORE = 1.0e4
N_BUCKETS = 32
MAX_DISTANCE = 128
TOP_K = 2
EPS = 1e-6
NEG = -1e30
TINY = 1e-30
LOG2E = math.log2(math.e)
TB_OWN, TB_PREV, TB_WIN2, TB_NONE = range(4)

KEY_TILE = Q_BLOCK
SLC_UNROLL = 4
POOL_HALO = 16
NH_PAD = 16
V7X_VMEM_LIMIT = 56 * 1024 * 1024
V7X_SC_CORES, V7X_SC_SUBCORES = 2, 16
SC_MAX_INDEX_VECTOR = 128
SC_GATHER_CHUNK = 32

CMP_PER_Q = Q_BLOCK // CMP_STRIDE
CMP_PAD = 2 * CMP_PER_Q
CMP_NEAR = 3 * CMP_PER_Q

assert CMP_LEN == 2 * CMP_STRIDE and KEY_TILE == 2 * SLC_LEN and WINDOW == 2 * KEY_TILE
assert max(POOL_WINDOWS) <= POOL_HALO
assert 2 * KEY_TILE - (Q_BLOCK - 1) >= MAX_DISTANCE
assert (CMP_PAD + 1) * CMP_STRIDE - (CMP_LEN - 1) >= MAX_DISTANCE and CMP_PER_Q % 8 == 0


def _cparams(*sem):
    return pltpu.CompilerParams(dimension_semantics=sem, vmem_limit_bytes=V7X_VMEM_LIMIT)


def _rms(xf, g):
    ms = jnp.mean(xf * xf, axis=-1, keepdims=True)
    return (xf * lax.rsqrt(ms + EPS)) * g


def _dot(a, b):
    return jnp.dot(a, b, preferred_element_type=f32)


def _dot_hilo(a, b):
    hi = a.astype(bf16)
    lo = (a - hi.astype(f32)).astype(bf16)
    return _dot(hi, b) + _dot(lo, b)


def _head_rms(z, gvec, seg, segt):
    ssq = _dot_hilo(z * z, seg)
    inv = lax.rsqrt(ssq * (1.0 / HEAD_DIM) + EPS)
    return (z * _dot_hilo(inv, segt)) * gvec


def _silu(a):
    return a * jax.nn.sigmoid(a)


def _seg_mats(width):
    heads = width // HEAD_DIM
    seg = np.zeros((width, NH_PAD), np.float32)
    seg[np.arange(width), np.arange(width) // HEAD_DIM] = 1.0
    assert heads <= NH_PAD
    return jnp.asarray(seg, bf16), jnp.asarray(seg.T, bf16)


def _pool_kernel(x_ref, halo_ref, g_ref, w_ref, scale_ref, o_ref, *, tp, cg):
    i = pl.program_id(1)
    x = x_ref[0]
    xh = jnp.concatenate([halo_ref[0], x], axis=0)
    h = _rms(xh, g_ref[...])
    row = lax.broadcasted_iota(i32, (tp + POOL_HALO, 1), 0)
    t_abs = i * tp + row - POOL_HALO
    h = jnp.where(t_abs >= 0, h, 0.0)
    outs = []
    for gi, w in enumerate(POOL_WINDOWS):
        hg = h[:, gi * cg:(gi + 1) * cg]
        s = hg
        sh = 1
        while sh < w:
            s = s + pltpu.roll(s, sh, axis=0)
            sh *= 2
        cnt = jnp.clip(t_abs + 1, 1, w).astype(f32)
        diff = (s / cnt - hg)[POOL_HALO:]
        outs.append(_dot(diff.astype(bf16), w_ref[gi]))
    y = jnp.concatenate(outs, axis=1)
    o_ref[0] = x + y * scale_ref[...]


def _pool_layer(x3, g, w_grp, scale):
    B, S, D = x3.shape
    tp = min(512, S)
    cg = D // len(POOL_WINDOWS)
    assert S % tp == 0 and tp % POOL_HALO == 0 and all(w & (w - 1) == 0 for w in POOL_WINDOWS)
    hb = tp // POOL_HALO
    return pl.pallas_call(
        functools.partial(_pool_kernel, tp=tp, cg=cg),
        out_shape=jax.ShapeDtypeStruct((B, S, D), f32),
        grid=(B, S // tp),
        in_specs=[
            pl.BlockSpec((1, tp, D), lambda b, i: (b, i, 0)),
            pl.BlockSpec((1, POOL_HALO, D), lambda b, i: (b, jnp.maximum(i * hb - 1, 0), 0)),
            pl.BlockSpec((1, D), lambda b, i: (0, 0)),
            pl.BlockSpec((len(POOL_WINDOWS), cg, cg), lambda b, i: (0, 0, 0)),
            pl.BlockSpec((1, D), lambda b, i: (0, 0)),
        ],
        out_specs=pl.BlockSpec((1, tp, D), lambda b, i: (b, i, 0)),
        compiler_params=_cparams("parallel", "arbitrary"),
        name="pool_layer",
    )(x3, x3, g.reshape(1, D), w_grp.astype(bf16), scale.reshape(1, D))


def _ffn_kernel(x_ref, g_ref, wg_ref, wu_ref, wd_ref, o_ref, h_ref, acc_ref):
    j = pl.program_id(1)

    @pl.when(j == 0)
    def _():
        x = x_ref[...]
        h_ref[...] = _rms(x, g_ref[...]).astype(bf16)
        acc_ref[...] = x

    h = h_ref[...]
    act = _silu(_dot(h, wg_ref[...])) * _dot(h, wu_ref[...])
    acc_ref[...] += _dot(act.astype(bf16), wd_ref[...])

    @pl.when(j == pl.num_programs(1) - 1)
    def _():
        o_ref[...] = acc_ref[...]


def _ffn_layer(xf, g, wg, wu, wd):
    N, D = xf.shape
    F = wg.shape[1]
    tm = min(1024, N)
    tf = 512
    assert N % tm == 0 and F % tf == 0
    return pl.pallas_call(
        _ffn_kernel,
        out_shape=jax.ShapeDtypeStruct((N, D), f32),
        grid=(N // tm, F // tf),
        in_specs=[
            pl.BlockSpec((tm, D), lambda i, j: (i, 0)),
            pl.BlockSpec((1, D), lambda i, j: (0, 0)),
            pl.BlockSpec((D, tf), lambda i, j: (0, j)),
            pl.BlockSpec((D, tf), lambda i, j: (0, j)),
            pl.BlockSpec((tf, D), lambda i, j: (j, 0)),
        ],
        out_specs=pl.BlockSpec((tm, D), lambda i, j: (i, 0)),
        scratch_shapes=[pltpu.VMEM((tm, D), bf16), pltpu.VMEM((tm, D), f32)],
        compiler_params=_cparams("parallel", "arbitrary"),
        name="ffn_dense",
    )(xf, g.reshape(1, D), wg.astype(bf16), wu.astype(bf16), wd.astype(bf16))


def _router_kernel(x_ref, g_ref, rt_ref, hp_ref, idx_ref, gate_ref):
    h = _rms(x_ref[...], g_ref[...])
    half = h.shape[1] // 2
    h_hi = h.astype(bf16)
    bits = pltpu.bitcast(h_hi.astype(f32), u32)
    hp_ref[...] = pltpu.bitcast((bits[:, :half] & jnp.uint32(0xFFFF0000)) | (bits[:, half:] >> 16), i32)
    h_lo = (h - h_hi.astype(f32)).astype(bf16)
    rt = rt_ref[...]
    rt_hi = rt.astype(bf16)
    rt_lo = (rt - rt_hi.astype(f32)).astype(bf16)
    nt_dot = lambda a, b: lax.dot_general(a, b, (((1,), (1,)), ((), ())), preferred_element_type=f32)
    logits = nt_dot(rt_hi, h_hi) + nt_dot(rt_hi, h_lo) + nt_dot(rt_lo, h_hi)
    ne = logits.shape[0]
    row = lax.broadcasted_iota(i32, logits.shape, 0)
    m1 = jnp.max(logits, axis=0, keepdims=True)
    i1 = jnp.min(jnp.where(logits == m1, row, ne), axis=0, keepdims=True)
    rest = jnp.where(row == i1, -jnp.inf, logits)
    m2 = jnp.max(rest, axis=0, keepdims=True)
    i2 = jnp.min(jnp.where(rest == m2, row, ne), axis=0, keepdims=True)
    e2 = jnp.exp(m2 - m1)
    den = 1.0 + e2
    idx_ref[...] = jnp.concatenate([i1, i2], axis=0)
    gate_ref[...] = jnp.concatenate([1.0 / den, e2 / den], axis=0)


def _router(xf, g, router):
    N, D = xf.shape
    E = router.shape[1]
    tr = min(1024, N)
    assert N % tr == 0 and TOP_K == 2
    return pl.pallas_call(
        _router_kernel,
        out_shape=(jax.ShapeDtypeStruct((N, D // 2), i32),
                   jax.ShapeDtypeStruct((TOP_K, N), i32),
                   jax.ShapeDtypeStruct((TOP_K, N), f32)),
        grid=(N // tr,),
        in_specs=[
            pl.BlockSpec((tr, D), lambda i: (i, 0)),
            pl.BlockSpec((1, D), lambda i: (0, 0)),
            pl.BlockSpec((E, D), lambda i: (0, 0)),
        ],
        out_specs=(pl.BlockSpec((tr, D // 2), lambda i: (i, 0)),
                   pl.BlockSpec((TOP_K, tr), lambda i: (0, i)),
                   pl.BlockSpec((TOP_K, tr), lambda i: (0, i))),
        compiler_params=_cparams("parallel"),
        name="moe_router",
    )(xf, g.reshape(1, D), router.T)


def _sc_gather_rows(table, idx, chunk):
    B = idx.shape[0]
    D = table.shape[1]
    workers = V7X_SC_CORES * V7X_SC_SUBCORES
    per_w = B // workers
    cpw = per_w // chunk
    assert B % (8 * workers) == 0 and per_w % (2 * chunk) == 0 and cpw % 8 == 0
    assert chunk % 8 == 0 and chunk <= SC_MAX_INDEX_VECTOR
    mesh = plsc.VectorSubcoreMesh(core_axis_name="c", subcore_axis_name="s")

    @functools.partial(
        pl.kernel, mesh=mesh, out_type=jax.ShapeDtypeStruct((B, D), table.dtype),
        scratch_types=[pltpu.VMEM((cpw, chunk), i32), pltpu.VMEM((2, chunk, D), table.dtype),
                       pltpu.SemaphoreType.DMA, pltpu.SemaphoreType.DMA])
    def gather(table_hbm, idx_hbm, out_hbm, idx_v, rows_v, sem0, sem1):
        wid = lax.axis_index("s") * V7X_SC_CORES + lax.axis_index("c")
        pltpu.sync_copy(idx_hbm.at[pl.ds(pl.multiple_of(wid * cpw, 8), cpw)], idx_v)
        sems = (sem0, sem1)

        def fetch(c, b):
            return pltpu.make_async_copy(table_hbm.at[idx_v.at[c]], rows_v.at[b], sems[b])

        fetch(0, 0).start()

        @pl.loop(0, cpw, step=2)
        def _(c):
            for b in range(2):
                cur = c + b
                fetch(cur, b).wait()

                @pl.when(cur + 1 < cpw)
                def _():
                    fetch(cur + 1, 1 - b).start()

                pltpu.sync_copy(rows_v.at[b], out_hbm.at[pl.ds(pl.multiple_of(wid * per_w + cur * chunk, 8), chunk)])

    return gather(table, idx.reshape(B // chunk, chunk))


def _sc_scatter_rows(src, slots, n_slots, chunk):
    K, N = slots.shape
    D = src.shape[1]
    workers = V7X_SC_CORES * V7X_SC_SUBCORES
    per_w = N // workers
    cpw = per_w // chunk
    assert N % (8 * workers) == 0 and per_w % (2 * chunk) == 0 and cpw % 8 == 0
    assert chunk % 8 == 0 and chunk <= SC_MAX_INDEX_VECTOR
    mesh = plsc.VectorSubcoreMesh(core_axis_name="c", subcore_axis_name="s")

    @functools.partial(
        pl.kernel, mesh=mesh, out_type=jax.ShapeDtypeStruct((n_slots, D), src.dtype),
        scratch_types=[pltpu.VMEM((K, cpw, chunk), i32), pltpu.VMEM((2, chunk, D), src.dtype),
                       pltpu.SemaphoreType.DMA, pltpu.SemaphoreType.DMA, pltpu.SemaphoreType.DMA,
                       pltpu.SemaphoreType.DMA])
    def scatter(src_hbm, slots_hbm, out_hbm, slot_v, rows_v, lsem0, lsem1, ssem0, ssem1):
        wid = lax.axis_index("s") * V7X_SC_CORES + lax.axis_index("c")
        for k in range(K):
            pltpu.sync_copy(slots_hbm.at[k, pl.ds(pl.multiple_of(wid * cpw, 8), cpw)], slot_v.at[k])
        lsems, ssems = (lsem0, lsem1), (ssem0, ssem1)

        def load(c, b):
            return pltpu.make_async_copy(
                src_hbm.at[pl.ds(pl.multiple_of(wid * per_w + c * chunk, 8), chunk)], rows_v.at[b], lsems[b])

        def store(c, b, k):
            return pltpu.make_async_copy(rows_v.at[b], out_hbm.at[slot_v.at[k].at[c]], ssems[b])

        load(0, 0).start()

        @pl.loop(0, cpw, step=2)
        def _(c):
            for b in range(2):
                cur = c + b
                load(cur, b).wait()
                for k in range(K):
                    store(cur, b, k).start()

                @pl.when(cur >= 1)
                def _():
                    for k in range(K):
                        store(cur - 1, 1 - b, k).wait()

                @pl.when(cur + 1 < cpw)
                def _():
                    load(cur + 1, 1 - b).start()

        for k in range(K):
            store(cpw - 1, 1, k).wait()

    return scatter(src, slots.reshape(K, N // chunk, chunk))


def _gmm_kernel(te_ref, nu_ref, xs_ref, wg_ref, wu_ref, wd_ref, o_ref, h_ref, acc_ref):
    i = pl.program_id(0)
    j = pl.program_id(1)

    @pl.when(i < nu_ref[0])
    def _():
        @pl.when(j == 0)
        def _():
            w = pltpu.bitcast(xs_ref[...], u32)
            left = pltpu.bitcast(w & jnp.uint32(0xFFFF0000), f32)
            right = pltpu.bitcast(w << 16, f32)
            h = jnp.concatenate([left, right], axis=1)
            filled = lax.broadcasted_iota(i32, (h.shape[0], 1), 0) < nu_ref[1 + i]
            h_ref[...] = jnp.where(filled, h, 0.0).astype(bf16)
            acc_ref[...] = jnp.zeros_like(acc_ref)

        h = h_ref[...]
        act = _silu(_dot(h, wg_ref[0].astype(bf16))) * _dot(h, wu_ref[0].astype(bf16))
        acc_ref[...] += _dot(act.astype(bf16), wd_ref[0].astype(bf16))

        @pl.when(j == pl.num_programs(1) - 1)
        def _():
            o_ref[...] = acc_ref[...]

    @pl.when((i >= nu_ref[0]) & (j == 0))
    def _():
        o_ref[...] = jnp.zeros_like(o_ref)


def _gmm(tile_expert, n_used, xs, wg, wu, wd, tm):
    P, half = xs.shape
    D = 2 * half
    F = wg.shape[2]
    tf = 512
    assert P % tm == 0 and F % tf == 0

    nf = F // tf

    def row_map(i, j, te, nu):
        return (jnp.minimum(i, nu[0] - 1), 0)

    def ff(i, j, nu):
        return jnp.where(i < nu[0], j, nf - 1)

    grid_spec = pltpu.PrefetchScalarGridSpec(
        num_scalar_prefetch=2,
        grid=(P // tm, nf),
        in_specs=[
            pl.BlockSpec((tm, half), row_map),
            pl.BlockSpec((1, D, tf), lambda i, j, te, nu: (te[i], 0, ff(i, j, nu))),
            pl.BlockSpec((1, D, tf), lambda i, j, te, nu: (te[i], 0, ff(i, j, nu))),
            pl.BlockSpec((1, tf, D), lambda i, j, te, nu: (te[i], ff(i, j, nu), 0)),
        ],
        out_specs=pl.BlockSpec((tm, D), lambda i, j, te, nu: (i, 0)),
        scratch_shapes=[pltpu.VMEM((tm, D), bf16), pltpu.VMEM((tm, D), f32)],
    )
    return pl.pallas_call(
        _gmm_kernel,
        out_shape=jax.ShapeDtypeStruct((P, D), f32),
        grid_spec=grid_spec,
        compiler_params=_cparams("arbitrary", "arbitrary"),
        name="moe_gmm",
    )(tile_expert, n_used, xs, wg, wu, wd)


def _combine_kernel(x_ref, gate_ref, y_ref, o_ref):
    gate = gate_ref[...]
    acc = x_ref[...]
    for k in range(TOP_K):
        acc = acc + y_ref[k] * gate[:, k:k + 1]
    o_ref[...] = acc


def _combine(xf, gate_tk, y_tok):
    N, D = xf.shape
    tt = min(1024, N)
    assert N % tt == 0
    return pl.pallas_call(
        _combine_kernel,
        out_shape=jax.ShapeDtypeStruct((N, D), f32),
        grid=(N // tt,),
        in_specs=[pl.BlockSpec((tt, D), lambda i: (i, 0)),
                  pl.BlockSpec((tt, TOP_K), lambda i: (i, 0)),
                  pl.BlockSpec((TOP_K, tt, D), lambda i: (0, i, 0))],
        out_specs=pl.BlockSpec((tt, D), lambda i: (i, 0)),
        compiler_params=_cparams("parallel"),
        name="moe_combine",
    )(xf, gate_tk, y_tok)


def _moe_layer(xf, g, router, wg, wu, wd, first_expert):
    N, D = xf.shape
    E = router.shape[1]
    tm = min(1024, N)
    hp, idx, gate = _router(xf, g, router)
    e_flat = idx.reshape(-1)
    onehot = (e_flat[:, None] == jnp.arange(E, dtype=i32)[None, :]).astype(i32)
    csum = jnp.cumsum(onehot, axis=0)
    counts = csum[-1]
    padded = ((counts + tm - 1) // tm) * tm
    ends = jnp.cumsum(padded)
    starts = ends - padded
    pos = (jnp.sum((csum + starts[None, :]) * onehot, axis=1) - 1).astype(i32)
    P = TOP_K * N + E * tm
    n_used = (ends[-1] // tm).astype(i32)
    tile_start = jnp.minimum(jnp.arange(P // tm, dtype=i32), n_used - 1) * tm
    tile_expert = jnp.sum((ends[None, :] <= tile_start[:, None]).astype(i32), axis=1)
    group_end = (starts + counts)[tile_expert]
    tile_rows = jnp.clip(group_end - tile_start, 0, tm).astype(i32)
    xs = _sc_scatter_rows(hp, pos.reshape(TOP_K, N), P, SC_GATHER_CHUNK)
    ys = _gmm(tile_expert + first_expert, jnp.concatenate([n_used.reshape(1), tile_rows]), xs, wg, wu, wd, tm)
    y_tok = _sc_gather_rows(ys, pos, SC_GATHER_CHUNK).reshape(TOP_K, N, D)
    return _combine(xf, gate.T, y_tok)


def _kv_kernel(x_ref, g_ref, w_ref, gk_ref, seg_ref, segt_ref, kc_ref, vc_ref, ks_ref, vs_ref, kw_ref, vw_ref,
               stage_ref):
    h = _rms(x_ref[...], g_ref[...]).astype(bf16)
    kv = _dot(h, w_ref[...])
    G, DH = ks_ref.shape[1], ks_ref.shape[3]
    wd = G * DH
    part = lambda p: kv[:, p * wd:(p + 1) * wd]
    seg, segt = seg_ref[...], segt_ref[...]
    nrow = stage_ref.shape[0] // CMP_STRIDE
    for c_ref, pc in ((kc_ref, 0), (vc_ref, 1)):
        z = part(pc)
        for g in range(G):
            stage_ref[...] = z[:, g * DH:(g + 1) * DH]
            c_ref[0, g] = jnp.concatenate(
                [stage_ref[pl.ds(l, nrow, stride=CMP_STRIDE), :] for l in range(CMP_STRIDE)], axis=1)
    for k_ref, v_ref, pk, gain in ((ks_ref, vs_ref, 2, gk_ref[0:1, :]), (kw_ref, vw_ref, 4, gk_ref[1:2, :])):
        kn = _head_rms(part(pk), gain, seg, segt).astype(bf16)
        v = part(pk + 1)
        for g in range(G):
            k_ref[0, g] = kn[:, g * DH:(g + 1) * DH]
        for t in range(v_ref.shape[2]):
            vt = v[t * KEY_TILE:(t + 1) * KEY_TILE, :].T
            for g in range(G):
                v_ref[0, g, t] = vt[g * DH:(g + 1) * DH].astype(bf16)


def _kv_project(xf, B, S, norm_kv, w_kv, g_k):
    N, D = xf.shape
    G, DH = N_KV_HEADS, HEAD_DIM
    wd = G * DH
    tk = min(512, S)
    spb = S // tk
    tpk = tk // KEY_TILE
    assert S % tk == 0 and tk % KEY_TILE == 0 and w_kv.shape[1] == 2 * N_BRANCH * wd
    seg, segt = _seg_mats(wd)
    gk = jnp.stack([jnp.tile(g_k[1], G), jnp.tile(g_k[2], G)])
    row = lambda i: (i, 0)
    const = lambda i: (0, 0)
    k_shape = jax.ShapeDtypeStruct((B, G, S, DH), bf16)
    v_shape = jax.ShapeDtypeStruct((B, G, S // KEY_TILE, DH, KEY_TILE), bf16)
    k_spec = pl.BlockSpec((1, G, tk, DH), lambda i: (i // spb, 0, i % spb, 0))
    c_shape = jax.ShapeDtypeStruct((B, G, S // CMP_STRIDE, CMP_STRIDE * DH), f32)
    c_spec = pl.BlockSpec((1, G, tk // CMP_STRIDE, CMP_STRIDE * DH), lambda i: (i // spb, 0, i % spb, 0))
    v_spec = pl.BlockSpec((1, G, tpk, DH, KEY_TILE), lambda i: (i // spb, 0, i % spb, 0, 0))
    return pl.pallas_call(
        _kv_kernel,
        out_shape=(c_shape, c_shape,
                   k_shape, v_shape, k_shape, v_shape),
        grid=(N // tk,),
        in_specs=[
            pl.BlockSpec((tk, D), row),
            pl.BlockSpec((1, D), const),
            pl.BlockSpec(w_kv.shape, const),
            pl.BlockSpec((2, wd), const),
            pl.BlockSpec((wd, NH_PAD), const),
            pl.BlockSpec((NH_PAD, wd), const),
        ],
        out_specs=(c_spec, c_spec, k_spec, v_spec, k_spec, v_spec),
        scratch_shapes=[pltpu.VMEM((tk, DH), f32)],
        compiler_params=_cparams("parallel"),
        name="kv_project",
    )(xf, norm_kv.reshape(1, D), w_kv.astype(bf16), gk, seg, segt)


def _compress_kernel(c_ref, pos_ref, w1_ref, w2_ref, gk_ref, o_ref):
    kv = pl.program_id(0)
    c = c_ref[0, 0, 0]
    a = _dot((c + pos_ref[0, 0:1, :]).astype(bf16), w1_ref[0, 0])
    b = _dot((c + pos_ref[0, 1:2, :]).astype(bf16), w1_ref[0, 1])
    n = c.shape[0]
    hid = a + pltpu.roll(b, n - 1, axis=0)
    out = _dot(_silu(hid).astype(bf16), w2_ref[0])

    @pl.when(kv == 0)
    def _():
        o_ref[0, 0, 0] = _rms(out, gk_ref[...])

    @pl.when(kv != 0)
    def _():
        o_ref[0, 0, 0] = out


def _compress(kc, vc, B, S, cmp_pos, cmp_w1, cmp_w2, g_k0):
    G, DH = N_KV_HEADS, HEAD_DIM
    nch = S // CMP_STRIDE
    cw = CMP_STRIDE * DH

    c = jnp.stack([kc, vc])
    pos = cmp_pos.reshape(2, 2, cw)
    w1 = cmp_w1.reshape(2, 2, cw, CMP_HIDDEN).astype(bf16)
    return pl.pallas_call(
        _compress_kernel,
        out_shape=jax.ShapeDtypeStruct((2, B, G, nch, DH), f32),
        grid=(2, B, G),
        in_specs=[
            pl.BlockSpec((1, 1, 1, nch, cw), lambda k, b, g: (k, b, g, 0, 0)),
            pl.BlockSpec((1, 2, cw), lambda k, b, g: (k, 0, 0)),
            pl.BlockSpec((1, 2, cw, CMP_HIDDEN), lambda k, b, g: (k, 0, 0, 0)),
            pl.BlockSpec((1, CMP_HIDDEN, DH), lambda k, b, g: (k, 0, 0)),
            pl.BlockSpec((1, DH), lambda k, b, g: (0, 0)),
        ],
        out_specs=pl.BlockSpec((1, 1, 1, nch, DH), lambda k, b, g: (k, b, g, 0, 0)),
        compiler_params=_cparams("arbitrary", "arbitrary", "arbitrary"),
        name="kv_compress",
    )(c, pos, w1, cmp_w2.astype(bf16), g_k0.reshape(1, DH))


def _build_shared(xf, B, S, norm_kv, w_kv, g_k, cmp_pos, cmp_w1, cmp_w2):
    kc, vc, ks, vst, kw, vwt = _kv_project(xf, B, S, norm_kv, w_kv, g_k)
    cmp = _compress(kc, vc, B, S, cmp_pos, cmp_w1, cmp_w2, g_k[0]).astype(bf16)
    kcm = cmp[0]
    vct = cmp[1].transpose(0, 1, 3, 2)
    return kcm, vct, ks, vst, kw, vwt


def _qg_kernel(x_ref, g_ref, wq_ref, wgate_ref, gq_ref, seg_ref, segt_ref, q_ref, gate_ref):
    h = _rms(x_ref[...], g_ref[...]).astype(bf16)
    q_raw = _dot(h, wq_ref[...])
    gate_ref[...] = jax.nn.sigmoid(_dot(h, wgate_ref[...]))
    nt, G, DH, RT = q_ref.shape
    T = q_raw.shape[0] // nt
    r = RT // T

    def tile_phases(t):
        q = _head_rms(q_raw[t * T:(t + 1) * T], gq_ref[...], seg_ref[...], segt_ref[...])
        q = q * (HEAD_DIM ** -0.5 * LOG2E)
        yield
        for g in range(G):
            qt = q[:, g * r * DH:(g + 1) * r * DH].T
            q_ref[t, g] = jnp.concatenate([qt[k * DH:(k + 1) * DH] for k in range(r)], axis=1).astype(bf16)
            yield

    tiles = [tile_phases(t) for t in range(nt)]
    while tiles:
        tiles = [phases for phases in tiles if next(phases, False) is None]


def _qg_project(xf, g, w_qg, g_q):
    N, D = xf.shape
    HD = D
    ng = w_qg.shape[1] - HD
    tq = min(512, N)
    G, T = N_KV_HEADS, Q_BLOCK
    RT = HD // (G * HEAD_DIM) * T
    assert N % tq == 0 and tq % T == 0
    seg, segt = _seg_mats(HD)
    row = lambda i: (i, 0)
    const = lambda i: (0, 0)
    return pl.pallas_call(
        _qg_kernel,
        out_shape=(jax.ShapeDtypeStruct((N // T, G, HEAD_DIM, RT), bf16), jax.ShapeDtypeStruct((N, ng), f32)),
        grid=(N // tq,),
        in_specs=[
            pl.BlockSpec((tq, D), row),
            pl.BlockSpec((1, D), const),
            pl.BlockSpec((D, HD), const),
            pl.BlockSpec((D, ng), const),
            pl.BlockSpec((1, HD), const),
            pl.BlockSpec((HD, NH_PAD), const),
            pl.BlockSpec((NH_PAD, HD), const),
        ],
        out_specs=(pl.BlockSpec((tq // T, G, HEAD_DIM, RT), lambda i: (i, 0, 0, 0)), pl.BlockSpec((tq, ng), row)),
        compiler_params=_cparams("parallel"),
        name="qg_project",
    )(xf, g.reshape(1, D), w_qg[:, :HD].astype(bf16), w_qg[:, HD:].astype(bf16),
      jnp.tile(g_q, HD // HEAD_DIM).reshape(1, HD), seg, segt)


def _fold8(x, op):
    return op(x.reshape(x.shape[0] // 8, 8, x.shape[1]), axis=0)


def _nsa_kernel(q_ref, gt_ref, kc_ref, vct_ref, ks_ref, vst_ref, kw_ref, vwt_ref, cbn_ref, cstep_ref, tb_ref, c2st_ref,
                o_ref, cb_scr, pen_ref, penf_ref, s_scr, m8_scr, l8_scr, acc_scr, part_scr, *, r, nsb, n_sel):
    qb = pl.program_id(1)
    t0 = qb * Q_BLOCK
    T = Q_BLOCK
    RT = r * T
    G = q_ref.shape[1]
    nch = kc_ref.shape[2]
    nkt = s_scr.shape[1] // KEY_TILE - SLC_UNROLL

    def tile_rows(kt):
        return pl.ds(pl.multiple_of(kt * KEY_TILE, KEY_TILE), KEY_TILE)

    def add_block_pen(s, ref, g, kt):
        return jnp.concatenate([s[:SLC_LEN] + ref[g, pl.ds(2 * kt, 1), :],
                                s[SLC_LEN:] + ref[g, pl.ds(2 * kt + 1, 1), :]], axis=0)

    def group_phases(g):
        qT = q_ref[0, g]

        cb_scr[g] = cstep_ref[g, pl.ds(pl.multiple_of(nch - (qb + 1) * CMP_PER_Q, 8), CMP_PAD + nch), :]
        cb_scr[g, pl.ds(pl.multiple_of(qb * CMP_PER_Q, 8), CMP_NEAR), :] = cbn_ref[g]
        s = _dot(kc_ref[0, g], qT) + cb_scr[g, CMP_PAD:CMP_PAD + nch, :]
        yield
        p = jnp.where(s > 0.5 * NEG, jnp.exp2(s - jnp.max(s, axis=0, keepdims=True)), 0.0)
        p = p * (1.0 / jnp.maximum(jnp.sum(p, axis=0, keepdims=True), TINY))
        o_cmp = _dot(vct_ref[0, g], p.astype(bf16))
        yield

        psum = p[:, 0:T]
        for k in range(1, r):
            psum = psum + p[:, k * T:(k + 1) * T]
        hi = psum.astype(bf16)
        lo = (psum - hi.astype(f32)).astype(bf16)
        imp = _dot(c2st_ref[...], hi) + _dot(c2st_ref[...], lo)
        jb = lax.broadcasted_iota(i32, (nsb, T), 0)
        blk_q = jnp.right_shift(t0 + lax.broadcasted_iota(i32, (nsb, T), 1), SLC_LEN.bit_length() - 1)
        forced = (jb == 0) | (jb == blk_q) | (jb == blk_q - 1)
        score = jnp.where(forced, FORCED_SCORE, jnp.where(jb <= blk_q, imp, NEG))
        pen = jnp.full((nsb, T), NEG, f32)
        for _ in range(n_sel):
            mx = jnp.max(score, axis=0, keepdims=True)
            first = jnp.min(jnp.where(score == mx, jb, nsb), axis=0, keepdims=True)
            hit = jb == first
            pen = jnp.where(hit, 0.0, pen)
            score = jnp.where(hit, -jnp.inf, score)
        pen = jnp.concatenate([pen] * r, axis=1)
        pen_ref[g] = pen
        penf_ref[g] = pen + cstep_ref[g, 0:1, :]
        yield

        win = []
        for d in range(WINDOW // KEY_TILE, -1, -1):
            kt = qb - d
            ktc = jnp.maximum(kt, 0)
            tab = tb_ref[g, jnp.where(kt >= 0, d, TB_NONE)]
            win.append((_dot(kw_ref[0, g, tile_rows(ktc), :], qT) + tab, vwt_ref[0, g, ktc]))
        yield
        m8 = _fold8(win[0][0], jnp.max)
        for s_d, _ in win[1:]:
            m8 = jnp.maximum(m8, _fold8(s_d, jnp.max))
        m = jnp.max(m8, axis=0, keepdims=True)
        l8 = jnp.zeros((8, RT), f32)
        acc = jnp.zeros((HEAD_DIM, RT), f32)
        for s_d, v_d in win:
            p_d = jnp.exp2(s_d - m)
            l8 = l8 + _fold8(p_d, jnp.sum)
            acc = acc + _dot(v_d, p_d.astype(bf16))
        o_win = acc * (1.0 / jnp.maximum(jnp.sum(l8, axis=0, keepdims=True), TINY))

        gt = gt_ref[0, g]
        part_scr[g] = gt[0:1] * o_cmp + gt[2:3] * o_win
        yield

        ktp = jnp.maximum(qb - 1, 0)
        s_prev = add_block_pen(_dot(ks_ref[0, g, tile_rows(ktp), :], qT)
                               + tb_ref[g, jnp.where(qb >= 1, TB_PREV, TB_NONE)], pen_ref, g, ktp)
        s_scr[g, tile_rows(ktp), :] = s_prev
        s_own = add_block_pen(_dot(ks_ref[0, g, tile_rows(qb), :], qT) + tb_ref[g, TB_OWN], pen_ref, g, qb)
        s_scr[g, tile_rows(qb), :] = s_own
        m8_scr[g] = jnp.maximum(_fold8(s_prev, jnp.max), _fold8(s_own, jnp.max))
        l8_scr[g] = jnp.zeros((8, RT), f32)
        acc_scr[g] = jnp.zeros((HEAD_DIM, RT), f32)

    groups = [group_phases(g) for g in range(G)]
    while groups:
        groups = [phases for phases in groups if next(phases, False) is None]

    def pass_a(i, c):
        for g in range(G):
            qT = q_ref[0, g]
            m8 = m8_scr[g]
            for u in range(SLC_UNROLL):
                kt = SLC_UNROLL * i + u
                live = kt < qb - 1
                ktc = jnp.minimum(kt, nkt - 1)
                s = add_block_pen(_dot(ks_ref[0, g, tile_rows(ktc), :], qT) + jnp.where(live, 0.0, NEG),
                                  penf_ref, g, ktc)
                s_scr[g, tile_rows(jnp.where(live, kt, nkt + u)), :] = s
                m8 = jnp.maximum(m8, _fold8(s, jnp.max))
            m8_scr[g] = m8
        return c

    lax.fori_loop(0, (jnp.maximum(qb - 1, 0) + SLC_UNROLL - 1) // SLC_UNROLL, pass_a, 0)

    def pass_b(i, c):
        for g in range(G):
            m = jnp.max(m8_scr[g], axis=0, keepdims=True)
            l8 = l8_scr[g]
            acc = acc_scr[g]
            for u in range(SLC_UNROLL):
                kt = SLC_UNROLL * i + u
                ktc = jnp.minimum(kt, qb)
                p_u = jnp.exp2(s_scr[g, tile_rows(ktc), :] - (m + jnp.where(kt <= qb, 0.0, -NEG)))
                l8 = l8 + _fold8(p_u, jnp.sum)
                acc = acc + _dot(vst_ref[0, g, ktc], p_u.astype(bf16))
            l8_scr[g] = l8
            acc_scr[g] = acc
        return c

    lax.fori_loop(0, qb // SLC_UNROLL + 1, pass_b, 0)

    for g in range(G):
        o_slc = acc_scr[g] * (1.0 / jnp.maximum(jnp.sum(l8_scr[g], axis=0, keepdims=True), TINY))
        o_ref[0, g] = (part_scr[g] + gt_ref[0, g][1:2] * o_slc).astype(bf16)


def _bias_tables(rel_bias, S):
    G = N_KV_HEADS
    H = rel_bias.shape[1]
    r = H // G
    T = KEY_TILE
    n = jnp.arange(S + 2 * T, dtype=i32)
    max_exact = N_BUCKETS // 2
    nf = jnp.maximum(n, 1).astype(f32)
    large = max_exact + (jnp.log(nf / max_exact) / math.log(MAX_DISTANCE / max_exact)
                         * (N_BUCKETS - max_exact)).astype(i32)
    bucket = jnp.where(n < max_exact, n, jnp.minimum(large, N_BUCKETS - 1))
    bias1d = rel_bias.astype(f32)[bucket] * LOG2E

    def per_group(tab):
        lead = tab.shape[:-2]
        k = len(lead)
        t = jnp.moveaxis(tab, -1, 0).reshape((G, r) + lead + (T,))
        return jnp.moveaxis(t, 1, k + 1).reshape((G,) + lead + (r * T,))

    def masked(dist, ok):
        return jnp.where(jnp.asarray(ok)[..., None], bias1d[np.maximum(dist, 0)], NEG)

    def toeplitz(first):
        w = bias1d[np.maximum(first - (T - 1) + np.arange(2 * T), 0)]
        skew = jnp.broadcast_to(w[None], (T, 2 * T, H)).reshape(2 * T * T, H)[:T * (2 * T - 1)]
        return skew.reshape(T, 2 * T - 1, H)[:, T - 1:]

    kj, qi = np.arange(T)[:, None], np.arange(T)[None, :]
    ok = [qi - kj >= 0, np.ones((T, T), bool), 2 * T + qi - kj < WINDOW]
    tb = jnp.stack([jnp.where(jnp.asarray(ok[d])[..., None], toeplitz(d * T), NEG) for d in range(3)]
                   + [jnp.full((T, T, H), NEG, f32)])
    cend = (np.arange(CMP_NEAR)[:, None] - CMP_PAD) * CMP_STRIDE + CMP_LEN - 1
    dcn = np.arange(T)[None, :] - cend
    rows = CMP_PAD + S // CMP_STRIDE
    far = jnp.broadcast_to(bias1d[-1][None, None, :], (rows, T, H))
    cstep = jnp.concatenate([far, jnp.full((rows, T, H), NEG, f32)])
    return per_group(tb), per_group(masked(dcn, dcn >= 0)), per_group(cstep)


def _nsa_attention(q, gates_t, shared, tables, B, S):
    kcm, vct, ks, vst, kw, vwt = shared
    tb, cbn, cstep = tables
    G, DH, T = N_KV_HEADS, HEAD_DIM, Q_BLOCK
    RT = q.shape[3]
    r = RT // T
    nqb = S // T
    nch = S // CMP_STRIDE
    nsb = S // SLC_LEN
    nkt = S // KEY_TILE
    cmp_start = np.arange(nch) * CMP_STRIDE
    slc_start = np.arange(nsb) * SLC_LEN
    overlap = np.clip(np.minimum(cmp_start[:, None] + CMP_LEN, slc_start[None, :] + SLC_LEN)
                      - np.maximum(cmp_start[:, None], slc_start[None, :]), 0, None) / CMP_LEN
    overlap[nch - 1] = 0.0
    c2st = jnp.asarray(overlap.T, bf16)
    qmap = lambda b, i: (b * nqb + i, 0, 0, 0)
    bat = lambda b, i: (b, 0, 0, 0)
    bat5 = lambda b, i: (b, 0, 0, 0, 0)
    once = pl.Buffered(1)
    return pl.pallas_call(
        functools.partial(_nsa_kernel, r=r, nsb=nsb, n_sel=min(N_SEL, nsb)),
        out_shape=jax.ShapeDtypeStruct(q.shape, bf16),
        grid=(B, nqb),
        in_specs=[
            pl.BlockSpec((1, G, DH, RT), qmap),
            pl.BlockSpec((1, G, N_BRANCH, RT), qmap),
            pl.BlockSpec((1, G, nch, DH), bat),
            pl.BlockSpec((1, G, DH, nch), bat),
            pl.BlockSpec((1, G, S, DH), bat),
            pl.BlockSpec((1, G, nkt, DH, KEY_TILE), bat5),
            pl.BlockSpec((1, G, S, DH), bat),
            pl.BlockSpec((1, G, nkt, DH, KEY_TILE), bat5),
            pl.BlockSpec(cbn.shape, lambda b, i: (0, 0, 0), pipeline_mode=once),
            pl.BlockSpec(cstep.shape, lambda b, i: (0, 0, 0), pipeline_mode=once),
            pl.BlockSpec(tb.shape, lambda b, i: (0, 0, 0, 0), pipeline_mode=once),
            pl.BlockSpec((nsb, nch), lambda b, i: (0, 0), pipeline_mode=once),
        ],
        out_specs=pl.BlockSpec((1, G, DH, RT), qmap),
        scratch_shapes=[pltpu.VMEM((G, CMP_PAD + nch, RT), f32), pltpu.VMEM((G, nsb, RT), f32),
                        pltpu.VMEM((G, nsb, RT), f32), pltpu.VMEM((G, (nkt + SLC_UNROLL) * KEY_TILE, RT), f32),
                        pltpu.VMEM((G, 8, RT), f32), pltpu.VMEM((G, 8, RT), f32), pltpu.VMEM((G, DH, RT), f32),
                        pltpu.VMEM((G, DH, RT), f32)],
        compiler_params=_cparams("parallel", "arbitrary"),
        name="nsa_attention",
    )(q, gates_t, kcm, vct, ks, vst, kw, vwt, cbn, cstep, tb, c2st)


def _oproj_kernel(x_ref, a_ref, w_ref, o_ref):
    nt, G, DH, RT = a_ref.shape
    T = x_ref.shape[0] // nt
    r = RT // T
    rows = []
    for t in range(nt):
        cols = []
        for g in range(G):
            a = a_ref[t, g].astype(f32)
            cols.append(jnp.concatenate([a[:, k * T:(k + 1) * T] for k in range(r)], axis=0).T)
        rows.append(jnp.concatenate(cols, axis=1))
    attn = jnp.concatenate(rows, axis=0)
    o_ref[...] = x_ref[...] + _dot(attn.astype(bf16), w_ref[...])


def _out_project(xf, attn, w_o):
    N, D = xf.shape
    nt_all, G, DH, RT = attn.shape
    T = N // nt_all
    to = min(512, N)
    assert N % to == 0 and to % T == 0
    row = lambda i: (i, 0)
    return pl.pallas_call(
        _oproj_kernel,
        out_shape=jax.ShapeDtypeStruct((N, D), f32),
        grid=(N // to,),
        in_specs=[pl.BlockSpec((to, D), row), pl.BlockSpec((to // T, G, DH, RT), lambda i: (i, 0, 0, 0)),
                  pl.BlockSpec(w_o.shape, lambda i: (0, 0))],
        out_specs=pl.BlockSpec((to, D), row),
        compiler_params=_cparams("parallel"),
        name="out_project",
    )(xf, attn, w_o.astype(bf16))


def _nsa_layer(xf, B, S, g, w_qg, g_q, w_o, shared, tables):
    N = xf.shape[0]
    G, T = N_KV_HEADS, Q_BLOCK
    q, gates = _qg_project(xf, g, w_qg, g_q)
    r = gates.shape[1] // (G * N_BRANCH)
    gates_t = gates.reshape(N // T, T, G, r, N_BRANCH).transpose(0, 2, 4, 3, 1).reshape(N // T, G, N_BRANCH, r * T)
    attn = _nsa_attention(q, gates_t, shared, tables, B, S)
    return _out_project(xf, attn, w_o)


def kernel(x, rel_bias, norm_mix, norm_ffn, pool_w, pool_scale, norm_kv, w_kv, g_k, cmp_pos, cmp_w1, cmp_w2,
           w_qg, g_q, w_o, ffn_wg, ffn_wu, ffn_wd, router, moe_wg, moe_wu, moe_wd):
    B, S, D = x.shape
    depth = norm_mix.shape[0]
    n_a = depth // 2
    assert S % Q_BLOCK == 0
    xf = x.reshape(B * S, D)
    shared = None
    tables = _bias_tables(rel_bias, S)
    n_exp = moe_wg.shape[1]
    moe_w = [w.reshape((-1,) + w.shape[2:]) for w in (moe_wg, moe_wu, moe_wd)]
    for layer in range(depth):
        if layer < n_a:
            xf = _pool_layer(xf.reshape(B, S, D), norm_mix[layer], pool_w[layer], pool_scale[layer]).reshape(B * S, D)
        else:
            j = layer - n_a
            xf = _nsa_layer(xf, B, S, norm_mix[layer], w_qg[j], g_q[j], w_o[j], shared, tables)
        i = layer // 2
        if layer % 2 == 0:
            xf = _ffn_layer(xf, norm_ffn[layer], ffn_wg[i], ffn_wu[i], ffn_wd[i])
        else:
            xf = _moe_layer(xf, norm_ffn[layer], router[i], *moe_w, first_expert=i * n_exp)
        if layer == n_a - 1:
            shared = _build_shared(xf, B, S, norm_kv, w_kv, g_k, cmp_pos, cmp_w1, cmp_w2)
    return xf.reshape(B, S, D)
```

```python
import functools
import math

import numpy as np
import jax
import jax.numpy as jnp
from jax import lax
from jax.experimental import pallas as pl
from jax.experimental.pallas import tpu as pltpu
from jax.experimental.pallas import tpu_sc as plsc

f32 = jnp.float32
bf16 = jnp.bfloat16
i32 = jnp.int32
u32 = jnp.uint32

POOL_WINDOWS = (2, 4, 8, 16)
HEAD_DIM = 64
N_KV_HEADS = 4
N_BRANCH = 3
CMP_LEN = 32
CMP_STRIDE = 16
CMP_HIDDEN = 4 * HEAD_DIM
SLC_LEN = 64
N_SEL = 4
WINDOW = 256
Q_BLOCK = 128
FORCED_SCORE = 1.0e4
N_BUCKETS = 32
MAX_DISTANCE = 128
TOP_K = 2
EPS = 1e-6
NEG = -1e30
TINY = 1e-30
LOG2E = math.log2(math.e)
TB_OWN, TB_PREV, TB_WIN2, TB_NONE = range(4)

KEY_TILE = Q_BLOCK
SLC_UNROLL = 4
POOL_HALO = 16
NH_PAD = 16
V7X_VMEM_LIMIT = 56 * 1024 * 1024
V7X_SC_CORES, V7X_SC_SUBCORES = 2, 16
SC_MAX_INDEX_VECTOR = 128
SC_GATHER_CHUNK = 32

CMP_PER_Q = Q_BLOCK // CMP_STRIDE
CMP_PAD = 2 * CMP_PER_Q
CMP_NEAR = 3 * CMP_PER_Q

assert CMP_LEN == 2 * CMP_STRIDE and KEY_TILE == 2 * SLC_LEN and WINDOW == 2 * KEY_TILE
assert max(POOL_WINDOWS) <= POOL_HALO
assert 2 * KEY_TILE - (Q_BLOCK - 1) >= MAX_DISTANCE
assert (CMP_PAD + 1) * CMP_STRIDE - (CMP_LEN - 1) >= MAX_DISTANCE and CMP_PER_Q % 8 == 0


def _cparams(*sem):
    return pltpu.CompilerParams(dimension_semantics=sem, vmem_limit_bytes=V7X_VMEM_LIMIT)


def _rms(xf, g):
    ms = jnp.mean(xf * xf, axis=-1, keepdims=True)
    return (xf * lax.rsqrt(ms + EPS)) * g


def _dot(a, b):
    return jnp.dot(a, b, preferred_element_type=f32)


def _dot_hilo(a, b):
    hi = a.astype(bf16)
    lo = (a - hi.astype(f32)).astype(bf16)
    return _dot(hi, b) + _dot(lo, b)


def _head_rms(z, gvec, seg, segt):
    ssq = _dot_hilo(z * z, seg)
    inv = lax.rsqrt(ssq * (1.0 / HEAD_DIM) + EPS)
    return (z * _dot_hilo(inv, segt)) * gvec


def _silu(a):
    return a * jax.nn.sigmoid(a)


def _seg_mats(width):
    heads = width // HEAD_DIM
    seg = np.zeros((width, NH_PAD), np.float32)
    seg[np.arange(width), np.arange(width) // HEAD_DIM] = 1.0
    assert heads <= NH_PAD
    return jnp.asarray(seg, bf16), jnp.asarray(seg.T, bf16)


def _pool_kernel(x_ref, halo_ref, g_ref, w_ref, scale_ref, o_ref, *, tp, cg):
    i = pl.program_id(1)
    x = x_ref[0]
    xh = jnp.concatenate([halo_ref[0], x], axis=0)
    h = _rms(xh, g_ref[...])
    row = lax.broadcasted_iota(i32, (tp + POOL_HALO, 1), 0)
    t_abs = i * tp + row - POOL_HALO
    h = jnp.where(t_abs >= 0, h, 0.0)
    outs = []
    for gi, w in enumerate(POOL_WINDOWS):
        hg = h[:, gi * cg:(gi + 1) * cg]
        s = hg
        sh = 1
        while sh < w:
            s = s + pltpu.roll(s, sh, axis=0)
            sh *= 2
        cnt = jnp.clip(t_abs + 1, 1, w).astype(f32)
        diff = (s / cnt - hg)[POOL_HALO:]
        outs.append(_dot(diff.astype(bf16), w_ref[gi]))
    y = jnp.concatenate(outs, axis=1)
    o_ref[0] = x + y * scale_ref[...]


def _pool_layer(x3, g, w_grp, scale):
    B, S, D = x3.shape
    tp = min(512, S)
    cg = D // len(POOL_WINDOWS)
    assert S % tp == 0 and tp % POOL_HALO == 0 and all(w & (w - 1) == 0 for w in POOL_WINDOWS)
    hb = tp // POOL_HALO
    return pl.pallas_call(
        functools.partial(_pool_kernel, tp=tp, cg=cg),
        out_shape=jax.ShapeDtypeStruct((B, S, D), f32),
        grid=(B, S // tp),
        in_specs=[
            pl.BlockSpec((1, tp, D), lambda b, i: (b, i, 0)),
            pl.BlockSpec((1, POOL_HALO, D), lambda b, i: (b, jnp.maximum(i * hb - 1, 0), 0)),
            pl.BlockSpec((1, D), lambda b, i: (0, 0)),
            pl.BlockSpec((len(POOL_WINDOWS), cg, cg), lambda b, i: (0, 0, 0)),
            pl.BlockSpec((1, D), lambda b, i: (0, 0)),
        ],
        out_specs=pl.BlockSpec((1, tp, D), lambda b, i: (b, i, 0)),
        compiler_params=_cparams("parallel", "arbitrary"),
        name="pool_layer",
    )(x3, x3, g.reshape(1, D), w_grp.astype(bf16), scale.reshape(1, D))


def _ffn_kernel(x_ref, g_ref, wg_ref, wu_ref, wd_ref, o_ref, h_ref, acc_ref):
    j = pl.program_id(1)

    @pl.when(j == 0)
    def _():
        x = x_ref[...]
        h_ref[...] = _rms(x, g_ref[...]).astype(bf16)
        acc_ref[...] = x

    h = h_ref[...]
    act = _silu(_dot(h, wg_ref[...])) * _dot(h, wu_ref[...])
    acc_ref[...] += _dot(act.astype(bf16), wd_ref[...])

    @pl.when(j == pl.num_programs(1) - 1)
    def _():
        o_ref[...] = acc_ref[...]


def _ffn_layer(xf, g, wg, wu, wd):
    N, D = xf.shape
    F = wg.shape[1]
    tm = min(1024, N)
    tf = 512
    assert N % tm == 0 and F % tf == 0
    return pl.pallas_call(
        _ffn_kernel,
        out_shape=jax.ShapeDtypeStruct((N, D), f32),
        grid=(N // tm, F // tf),
        in_specs=[
            pl.BlockSpec((tm, D), lambda i, j: (i, 0)),
            pl.BlockSpec((1, D), lambda i, j: (0, 0)),
            pl.BlockSpec((D, tf), lambda i, j: (0, j)),
            pl.BlockSpec((D, tf), lambda i, j: (0, j)),
            pl.BlockSpec((tf, D), lambda i, j: (j, 0)),
        ],
        out_specs=pl.BlockSpec((tm, D), lambda i, j: (i, 0)),
        scratch_shapes=[pltpu.VMEM((tm, D), bf16), pltpu.VMEM((tm, D), f32)],
        compiler_params=_cparams("parallel", "arbitrary"),
        name="ffn_dense",
    )(xf, g.reshape(1, D), wg.astype(bf16), wu.astype(bf16), wd.astype(bf16))


def _router_kernel(x_ref, g_ref, rt_ref, hp_ref, idx_ref, gate_ref):
    h = _rms(x_ref[...], g_ref[...])
    half = h.shape[1] // 2
    h_hi = h.astype(bf16)
    bits = pltpu.bitcast(h_hi.astype(f32), u32)
    hp_ref[...] = pltpu.bitcast((bits[:, :half] & jnp.uint32(0xFFFF0000)) | (bits[:, half:] >> 16), i32)
    h_lo = (h - h_hi.astype(f32)).astype(bf16)
    rt = rt_ref[...]
    rt_hi = rt.astype(bf16)
    rt_lo = (rt - rt_hi.astype(f32)).astype(bf16)
    nt_dot = lambda a, b: lax.dot_general(a, b, (((1,), (1,)), ((), ())), preferred_element_type=f32)
    logits = nt_dot(rt_hi, h_hi) + nt_dot(rt_hi, h_lo) + nt_dot(rt_lo, h_hi)
    ne = logits.shape[0]
    row = lax.broadcasted_iota(i32, logits.shape, 0)
    m1 = jnp.max(logits, axis=0, keepdims=True)
    i1 = jnp.min(jnp.where(logits == m1, row, ne), axis=0, keepdims=True)
    rest = jnp.where(row == i1, -jnp.inf, logits)
    m2 = jnp.max(rest, axis=0, keepdims=True)
    i2 = jnp.min(jnp.where(rest == m2, row, ne), axis=0, keepdims=True)
    e2 = jnp.exp(m2 - m1)
    den = 1.0 + e2
    idx_ref[...] = jnp.concatenate([i1, i2], axis=0)
    gate_ref[...] = jnp.concatenate([1.0 / den, e2 / den], axis=0)


def _router(xf, g, router):
    N, D = xf.shape
    E = router.shape[1]
    tr = min(1024, N)
    assert N % tr == 0 and TOP_K == 2
    return pl.pallas_call(
        _router_kernel,
        out_shape=(jax.ShapeDtypeStruct((N, D // 2), i32),
                   jax.ShapeDtypeStruct((TOP_K, N), i32),
                   jax.ShapeDtypeStruct((TOP_K, N), f32)),
        grid=(N // tr,),
        in_specs=[
            pl.BlockSpec((tr, D), lambda i: (i, 0)),
            pl.BlockSpec((1, D), lambda i: (0, 0)),
            pl.BlockSpec((E, D), lambda i: (0, 0)),
        ],
        out_specs=(pl.BlockSpec((tr, D // 2), lambda i: (i, 0)),
                   pl.BlockSpec((TOP_K, tr), lambda i: (0, i)),
                   pl.BlockSpec((TOP_K, tr), lambda i: (0, i))),
        compiler_params=_cparams("parallel"),
        name="moe_router",
    )(xf, g.reshape(1, D), router.T)


def _sc_gather_rows(table, idx, chunk):
    B = idx.shape[0]
    D = table.shape[1]
    workers = V7X_SC_CORES * V7X_SC_SUBCORES
    per_w = B // workers
    cpw = per_w // chunk
    assert B % (8 * workers) == 0 and per_w % (2 * chunk) == 0 and cpw % 8 == 0
    assert chunk % 8 == 0 and chunk <= SC_MAX_INDEX_VECTOR
    mesh = plsc.VectorSubcoreMesh(core_axis_name="c", subcore_axis_name="s")

    @functools.partial(
        pl.kernel, mesh=mesh, out_type=jax.ShapeDtypeStruct((B, D), table.dtype),
        scratch_types=[pltpu.VMEM((cpw, chunk), i32), pltpu.VMEM((2, chunk, D), table.dtype),
                       pltpu.SemaphoreType.DMA, pltpu.SemaphoreType.DMA])
    def gather(table_hbm, idx_hbm, out_hbm, idx_v, rows_v, sem0, sem1):
        wid = lax.axis_index("s") * V7X_SC_CORES + lax.axis_index("c")
        pltpu.sync_copy(idx_hbm.at[pl.ds(pl.multiple_of(wid * cpw, 8), cpw)], idx_v)
        sems = (sem0, sem1)

        def fetch(c, b):
            return pltpu.make_async_copy(table_hbm.at[idx_v.at[c]], rows_v.at[b], sems[b])

        fetch(0, 0).start()

        @pl.loop(0, cpw, step=2)
        def _(c):
            for b in range(2):
                cur = c + b
                fetch(cur, b).wait()

                @pl.when(cur + 1 < cpw)
                def _():
                    fetch(cur + 1, 1 - b).start()

                pltpu.sync_copy(rows_v.at[b], out_hbm.at[pl.ds(pl.multiple_of(wid * per_w + cur * chunk, 8), chunk)])

    return gather(table, idx.reshape(B // chunk, chunk))


def _sc_scatter_rows(src, slots, n_slots, chunk):
    K, N = slots.shape
    D = src.shape[1]
    workers = V7X_SC_CORES * V7X_SC_SUBCORES
    per_w = N // workers
    cpw = per_w // chunk
    assert N % (8 * workers) == 0 and per_w % (2 * chunk) == 0 and cpw % 8 == 0
    assert chunk % 8 == 0 and chunk <= SC_MAX_INDEX_VECTOR
    mesh = plsc.VectorSubcoreMesh(core_axis_name="c", subcore_axis_name="s")

    @functools.partial(
        pl.kernel, mesh=mesh, out_type=jax.ShapeDtypeStruct((n_slots, D), src.dtype),
        scratch_types=[pltpu.VMEM((K, cpw, chunk), i32), pltpu.VMEM((2, chunk, D), src.dtype),
                       pltpu.SemaphoreType.DMA, pltpu.SemaphoreType.DMA, pltpu.SemaphoreType.DMA,
                       pltpu.SemaphoreType.DMA])
    def scatter(src_hbm, slots_hbm, out_hbm, slot_v, rows_v, lsem0, lsem1, ssem0, ssem1):
        wid = lax.axis_index("s") * V7X_SC_CORES + lax.axis_index("c")
        for k in range(K):
            pltpu.sync_copy(slots_hbm.at[k, pl.ds(pl.multiple_of(wid * cpw, 8), cpw)], slot_v.at[k])
        lsems, ssems = (lsem0, lsem1), (ssem0, ssem1)

        def load(c, b):
            return pltpu.make_async_copy(
                src_hbm.at[pl.ds(pl.multiple_of(wid * per_w + c * chunk, 8), chunk)], rows_v.at[b], lsems[b])

        def store(c, b, k):
            return pltpu.make_async_copy(rows_v.at[b], out_hbm.at[slot_v.at[k].at[c]], ssems[b])

        load(0, 0).start()

        @pl.loop(0, cpw, step=2)
        def _(c):
            for b in range(2):
                cur = c + b
                load(cur, b).wait()
                for k in range(K):
                    store(cur, b, k).start()

                @pl.when(cur >= 1)
                def _():
                    for k in range(K):
                        store(cur - 1, 1 - b, k).wait()

                @pl.when(cur + 1 < cpw)
                def _():
                    load(cur + 1, 1 - b).start()

        for k in range(K):
            store(cpw - 1, 1, k).wait()

    return scatter(src, slots.reshape(K, N // chunk, chunk))


def _gmm_kernel(te_ref, nu_ref, xs_ref, wg_ref, wu_ref, wd_ref, o_ref, h_ref, acc_ref):
    i = pl.program_id(0)
    j = pl.program_id(1)

    @pl.when(i < nu_ref[0])
    def _():
        @pl.when(j == 0)
        def _():
            w = pltpu.bitcast(xs_ref[...], u32)
            left = pltpu.bitcast(w & jnp.uint32(0xFFFF0000), f32)
            right = pltpu.bitcast(w << 16, f32)
            h = jnp.concatenate([left, right], axis=1)
            filled = lax.broadcasted_iota(i32, (h.shape[0], 1), 0) < nu_ref[1 + i]
            h_ref[...] = jnp.where(filled, h, 0.0).astype(bf16)
            acc_ref[...] = jnp.zeros_like(acc_ref)

        h = h_ref[...]
        act = _silu(_dot(h, wg_ref[0].astype(bf16))) * _dot(h, wu_ref[0].astype(bf16))
        acc_ref[...] += _dot(act.astype(bf16), wd_ref[0].astype(bf16))

        @pl.when(j == pl.num_programs(1) - 1)
        def _():
            o_ref[...] = acc_ref[...]

    @pl.when((i >= nu_ref[0]) & (j == 0))
    def _():
        o_ref[...] = jnp.zeros_like(o_ref)


def _gmm(tile_expert, n_used, xs, wg, wu, wd, tm):
    P, half = xs.shape
    D = 2 * half
    F = wg.shape[2]
    tf = 512
    assert P % tm == 0 and F % tf == 0

    nf = F // tf

    def row_map(i, j, te, nu):
        return (jnp.minimum(i, nu[0] - 1), 0)

    def ff(i, j, nu):
        return jnp.where(i < nu[0], j, nf - 1)

    grid_spec = pltpu.PrefetchScalarGridSpec(
        num_scalar_prefetch=2,
        grid=(P // tm, nf),
        in_specs=[
            pl.BlockSpec((tm, half), row_map),
            pl.BlockSpec((1, D, tf), lambda i, j, te, nu: (te[i], 0, ff(i, j, nu))),
            pl.BlockSpec((1, D, tf), lambda i, j, te, nu: (te[i], 0, ff(i, j, nu))),
            pl.BlockSpec((1, tf, D), lambda i, j, te, nu: (te[i], ff(i, j, nu), 0)),
        ],
        out_specs=pl.BlockSpec((tm, D), lambda i, j, te, nu: (i, 0)),
        scratch_shapes=[pltpu.VMEM((tm, D), bf16), pltpu.VMEM((tm, D), f32)],
    )
    return pl.pallas_call(
        _gmm_kernel,
        out_shape=jax.ShapeDtypeStruct((P, D), f32),
        grid_spec=grid_spec,
        compiler_params=_cparams("arbitrary", "arbitrary"),
        name="moe_gmm",
    )(tile_expert, n_used, xs, wg, wu, wd)


def _combine_kernel(x_ref, gate_ref, y_ref, o_ref):
    gate = gate_ref[...]
    acc = x_ref[...]
    for k in range(TOP_K):
        acc = acc + y_ref[k] * gate[:, k:k + 1]
    o_ref[...] = acc


def _combine(xf, gate_tk, y_tok):
    N, D = xf.shape
    tt = min(1024, N)
    assert N % tt == 0
    return pl.pallas_call(
        _combine_kernel,
        out_shape=jax.ShapeDtypeStruct((N, D), f32),
        grid=(N // tt,),
        in_specs=[pl.BlockSpec((tt, D), lambda i: (i, 0)),
                  pl.BlockSpec((tt, TOP_K), lambda i: (i, 0)),
                  pl.BlockSpec((TOP_K, tt, D), lambda i: (0, i, 0))],
        out_specs=pl.BlockSpec((tt, D), lambda i: (i, 0)),
        compiler_params=_cparams("parallel"),
        name="moe_combine",
    )(xf, gate_tk, y_tok)


def _moe_layer(xf, g, router, wg, wu, wd, first_expert):
    N, D = xf.shape
    E = router.shape[1]
    tm = min(1024, N)
    hp, idx, gate = _router(xf, g, router)
    e_flat = idx.reshape(-1)
    onehot = (e_flat[:, None] == jnp.arange(E, dtype=i32)[None, :]).astype(i32)
    csum = jnp.cumsum(onehot, axis=0)
    counts = csum[-1]
    padded = ((counts + tm - 1) // tm) * tm
    ends = jnp.cumsum(padded)
    starts = ends - padded
    pos = (jnp.sum((csum + starts[None, :]) * onehot, axis=1) - 1).astype(i32)
    P = TOP_K * N + E * tm
    n_used = (ends[-1] // tm).astype(i32)
    tile_start = jnp.minimum(jnp.arange(P // tm, dtype=i32), n_used - 1) * tm
    tile_expert = jnp.sum((ends[None, :] <= tile_start[:, None]).astype(i32), axis=1)
    group_end = (starts + counts)[tile_expert]
    tile_rows = jnp.clip(group_end - tile_start, 0, tm).astype(i32)
    xs = _sc_scatter_rows(hp, pos.reshape(TOP_K, N), P, SC_GATHER_CHUNK)
    ys = _gmm(tile_expert + first_expert, jnp.concatenate([n_used.reshape(1), tile_rows]), xs, wg, wu, wd, tm)
    y_tok = _sc_gather_rows(ys, pos, SC_GATHER_CHUNK).reshape(TOP_K, N, D)
    return _combine(xf, gate.T, y_tok)


def _kv_kernel(x_ref, g_ref, w_ref, gk_ref, seg_ref, segt_ref, c_ref, ks_ref, vs_ref, kw_ref, vw_ref,
               stage_ref):
    h = _rms(x_ref[...], g_ref[...]).astype(bf16)
    kv = _dot(h, w_ref[...])
    G, DH = ks_ref.shape[1], ks_ref.shape[3]
    wd = G * DH
    part = lambda p: kv[:, p * wd:(p + 1) * wd]
    seg, segt = seg_ref[...], segt_ref[...]
    nrow = stage_ref.shape[0] // CMP_STRIDE
    for pc in range(2):
        z = part(pc)
        for g in range(G):
            stage_ref[...] = z[:, g * DH:(g + 1) * DH]
            c_ref[pc, 0, g] = jnp.concatenate(
                [stage_ref[pl.ds(l, nrow, stride=CMP_STRIDE), :] for l in range(CMP_STRIDE)], axis=1)
    for k_ref, v_ref, pk, gain in ((ks_ref, vs_ref, 2, gk_ref[0:1, :]), (kw_ref, vw_ref, 4, gk_ref[1:2, :])):
        kn = _head_rms(part(pk), gain, seg, segt).astype(bf16)
        v = part(pk + 1)
        for g in range(G):
            k_ref[0, g] = kn[:, g * DH:(g + 1) * DH]
        for t in range(v_ref.shape[2]):
            vt = v[t * KEY_TILE:(t + 1) * KEY_TILE, :].T
            for g in range(G):
                v_ref[0, g, t] = vt[g * DH:(g + 1) * DH].astype(bf16)


def _kv_project(xf, B, S, norm_kv, w_kv, g_k):
    N, D = xf.shape
    G, DH = N_KV_HEADS, HEAD_DIM
    wd = G * DH
    tk = min(512, S)
    spb = S // tk
    tpk = tk // KEY_TILE
    assert S % tk == 0 and tk % KEY_TILE == 0 and w_kv.shape[1] == 2 * N_BRANCH * wd
    seg, segt = _seg_mats(wd)
    gk = jnp.stack([jnp.tile(g_k[1], G), jnp.tile(g_k[2], G)])
    row = lambda i: (i, 0)
    const = lambda i: (0, 0)
    k_shape = jax.ShapeDtypeStruct((B, G, S, DH), bf16)
    v_shape = jax.ShapeDtypeStruct((B, G, S // KEY_TILE, DH, KEY_TILE), bf16)
    k_spec = pl.BlockSpec((1, G, tk, DH), lambda i: (i // spb, 0, i % spb, 0))
    c_shape = jax.ShapeDtypeStruct((2, B, G, S // CMP_STRIDE, CMP_STRIDE * DH), f32)
    c_spec = pl.BlockSpec((2, 1, G, tk // CMP_STRIDE, CMP_STRIDE * DH), lambda i: (0, i // spb, 0, i % spb, 0))
    v_spec = pl.BlockSpec((1, G, tpk, DH, KEY_TILE), lambda i: (i // spb, 0, i % spb, 0, 0))
    return pl.pallas_call(
        _kv_kernel,
        out_shape=(c_shape,
                   k_shape, v_shape, k_shape, v_shape),
        grid=(N // tk,),
        in_specs=[
            pl.BlockSpec((tk, D), row),
            pl.BlockSpec((1, D), const),
            pl.BlockSpec(w_kv.shape, const),
            pl.BlockSpec((2, wd), const),
            pl.BlockSpec((wd, NH_PAD), const),
            pl.BlockSpec((NH_PAD, wd), const),
        ],
        out_specs=(c_spec, k_spec, v_spec, k_spec, v_spec),
        scratch_shapes=[pltpu.VMEM((tk, DH), f32)],
        compiler_params=_cparams("parallel"),
        name="kv_project",
    )(xf, norm_kv.reshape(1, D), w_kv.astype(bf16), gk, seg, segt)


def _compress_kernel(c_ref, pos_ref, w1_ref, w2_ref, gk_ref, o_ref):
    kv = pl.program_id(0)
    c = c_ref[0, 0, 0]
    a = _dot((c + pos_ref[0, 0:1, :]).astype(bf16), w1_ref[0, 0])
    b = _dot((c + pos_ref[0, 1:2, :]).astype(bf16), w1_ref[0, 1])
    n = c.shape[0]
    hid = a + pltpu.roll(b, n - 1, axis=0)
    out = _dot(_silu(hid).astype(bf16), w2_ref[0])

    @pl.when(kv == 0)
    def _():
        o_ref[0, 0, 0] = _rms(out, gk_ref[...])

    @pl.when(kv != 0)
    def _():
        o_ref[0, 0, 0] = out


def _compress(c, B, S, cmp_pos, cmp_w1, cmp_w2, g_k0):
    G, DH = N_KV_HEADS, HEAD_DIM
    nch = S // CMP_STRIDE
    cw = CMP_STRIDE * DH
    pos = cmp_pos.reshape(2, 2, cw)
    w1 = cmp_w1.reshape(2, 2, cw, CMP_HIDDEN).astype(bf16)
    return pl.pallas_call(
        _compress_kernel,
        out_shape=jax.ShapeDtypeStruct((2, B, G, nch, DH), f32),
        grid=(2, B, G),
        in_specs=[
            pl.BlockSpec((1, 1, 1, nch, cw), lambda k, b, g: (k, b, g, 0, 0)),
            pl.BlockSpec((1, 2, cw), lambda k, b, g: (k, 0, 0)),
            pl.BlockSpec((1, 2, cw, CMP_HIDDEN), lambda k, b, g: (k, 0, 0, 0)),
            pl.BlockSpec((1, CMP_HIDDEN, DH), lambda k, b, g: (k, 0, 0)),
            pl.BlockSpec((1, DH), lambda k, b, g: (0, 0)),
        ],
        out_specs=pl.BlockSpec((1, 1, 1, nch, DH), lambda k, b, g: (k, b, g, 0, 0)),
        compiler_params=_cparams("arbitrary", "arbitrary", "arbitrary"),
        name="kv_compress",
    )(c, pos, w1, cmp_w2.astype(bf16), g_k0.reshape(1, DH))


def _build_shared(xf, B, S, norm_kv, w_kv, g_k, cmp_pos, cmp_w1, cmp_w2):
    c, ks, vst, kw, vwt = _kv_project(xf, B, S, norm_kv, w_kv, g_k)
    cmp = _compress(c, B, S, cmp_pos, cmp_w1, cmp_w2, g_k[0]).astype(bf16)
    kcm = cmp[0]
    vct = cmp[1].transpose(0, 1, 3, 2)
    return kcm, vct, ks, vst, kw, vwt


def _qg_kernel(x_ref, g_ref, wq_ref, wgate_ref, gq_ref, seg_ref, segt_ref, q_ref, gate_ref):
    h = _rms(x_ref[...], g_ref[...]).astype(bf16)
    q_raw = _dot(h, wq_ref[...])
    gate_ref[...] = jax.nn.sigmoid(_dot(h, wgate_ref[...]))
    nt, G, DH, RT = q_ref.shape
    T = q_raw.shape[0] // nt
    r = RT // T

    def tile_phases(t):
        q = _head_rms(q_raw[t * T:(t + 1) * T], gq_ref[...], seg_ref[...], segt_ref[...])
        q = q * (HEAD_DIM ** -0.5 * LOG2E)
        yield
        for g in range(G):
            qt = q[:, g * r * DH:(g + 1) * r * DH].T
            q_ref[t, g] = jnp.concatenate([qt[k * DH:(k + 1) * DH] for k in range(r)], axis=1).astype(bf16)
            yield

    tiles = [tile_phases(t) for t in range(nt)]
    while tiles:
        tiles = [phases for phases in tiles if next(phases, False) is None]


def _qg_project(xf, g, w_qg, g_q):
    N, D = xf.shape
    HD = D
    ng = w_qg.shape[1] - HD
    tq = min(512, N)
    G, T = N_KV_HEADS, Q_BLOCK
    RT = HD // (G * HEAD_DIM) * T
    assert N % tq == 0 and tq % T == 0
    seg, segt = _seg_mats(HD)
    row = lambda i: (i, 0)
    const = lambda i: (0, 0)
    return pl.pallas_call(
        _qg_kernel,
        out_shape=(jax.ShapeDtypeStruct((N // T, G, HEAD_DIM, RT), bf16), jax.ShapeDtypeStruct((N, ng), f32)),
        grid=(N // tq,),
        in_specs=[
            pl.BlockSpec((tq, D), row),
            pl.BlockSpec((1, D), const),
            pl.BlockSpec((D, HD), const),
            pl.BlockSpec((D, ng), const),
            pl.BlockSpec((1, HD), const),
            pl.BlockSpec((HD, NH_PAD), const),
            pl.BlockSpec((NH_PAD, HD), const),
        ],
        out_specs=(pl.BlockSpec((tq // T, G, HEAD_DIM, RT), lambda i: (i, 0, 0, 0)), pl.BlockSpec((tq, ng), row)),
        compiler_params=_cparams("parallel"),
        name="qg_project",
    )(xf, g.reshape(1, D), w_qg[:, :HD].astype(bf16), w_qg[:, HD:].astype(bf16),
      jnp.tile(g_q, HD // HEAD_DIM).reshape(1, HD), seg, segt)


def _fold8(x, op):
    return op(x.reshape(x.shape[0] // 8, 8, x.shape[1]), axis=0)


def _nsa_kernel(q_ref, gt_ref, kc_ref, vct_ref, ks_ref, vst_ref, kw_ref, vwt_ref, cbn_ref, cstep_ref, tb_ref, c2st_ref,
                o_ref, cb_scr, pen_ref, penf_ref, s_scr, m8_scr, l8_scr, acc_scr, part_scr, *, r, nsb, n_sel):
    qb = pl.program_id(1)
    t0 = qb * Q_BLOCK
    T = Q_BLOCK
    RT = r * T
    G = q_ref.shape[1]
    nch = kc_ref.shape[2]
    nkt = s_scr.shape[1] // KEY_TILE - SLC_UNROLL

    def tile_rows(kt):
        return pl.ds(pl.multiple_of(kt * KEY_TILE, KEY_TILE), KEY_TILE)

    def add_block_pen(s, ref, g, kt):
        return jnp.concatenate([s[:SLC_LEN] + ref[g, pl.ds(2 * kt, 1), :],
                                s[SLC_LEN:] + ref[g, pl.ds(2 * kt + 1, 1), :]], axis=0)

    def group_phases(g):
        qT = q_ref[0, g]

        cb_scr[g] = cstep_ref[g, pl.ds(pl.multiple_of(nch - (qb + 1) * CMP_PER_Q, 8), CMP_PAD + nch), :]
        cb_scr[g, pl.ds(pl.multiple_of(qb * CMP_PER_Q, 8), CMP_NEAR), :] = cbn_ref[g]
        s = _dot(kc_ref[0, g], qT) + cb_scr[g, CMP_PAD:CMP_PAD + nch, :]
        yield
        p = jnp.where(s > 0.5 * NEG, jnp.exp2(s - jnp.max(s, axis=0, keepdims=True)), 0.0)
        p = p * (1.0 / jnp.maximum(jnp.sum(p, axis=0, keepdims=True), TINY))
        o_cmp = _dot(vct_ref[0, g], p.astype(bf16))
        yield

        psum = p[:, 0:T]
        for k in range(1, r):
            psum = psum + p[:, k * T:(k + 1) * T]
        hi = psum.astype(bf16)
        lo = (psum - hi.astype(f32)).astype(bf16)
        imp = _dot(c2st_ref[...], hi) + _dot(c2st_ref[...], lo)
        jb = lax.broadcasted_iota(i32, (nsb, T), 0)
        blk_q = jnp.right_shift(t0 + lax.broadcasted_iota(i32, (nsb, T), 1), SLC_LEN.bit_length() - 1)
        forced = (jb == 0) | (jb == blk_q) | (jb == blk_q - 1)
        score = jnp.where(forced, FORCED_SCORE, jnp.where(jb <= blk_q, imp, NEG))
        pen = jnp.full((nsb, T), NEG, f32)
        for _ in range(n_sel):
            mx = jnp.max(score, axis=0, keepdims=True)
            first = jnp.min(jnp.where(score == mx, jb, nsb), axis=0, keepdims=True)
            hit = jb == first
            pen = jnp.where(hit, 0.0, pen)
            score = jnp.where(hit, -jnp.inf, score)
        pen = jnp.concatenate([pen] * r, axis=1)
        pen_ref[g] = pen
        penf_ref[g] = pen + cstep_ref[g, 0:1, :]
        yield

        win = []
        for d in range(WINDOW // KEY_TILE, -1, -1):
            kt = qb - d
            ktc = jnp.maximum(kt, 0)
            tab = tb_ref[g, jnp.where(kt >= 0, d, TB_NONE)]
            win.append((_dot(kw_ref[0, g, tile_rows(ktc), :], qT) + tab, vwt_ref[0, g, ktc]))
        yield
        m8 = _fold8(win[0][0], jnp.max)
        for s_d, _ in win[1:]:
            m8 = jnp.maximum(m8, _fold8(s_d, jnp.max))
        m = jnp.max(m8, axis=0, keepdims=True)
        l8 = jnp.zeros((8, RT), f32)
        acc = jnp.zeros((HEAD_DIM, RT), f32)
        for s_d, v_d in win:
            p_d = jnp.exp2(s_d - m)
            l8 = l8 + _fold8(p_d, jnp.sum)
            acc = acc + _dot(v_d, p_d.astype(bf16))
        o_win = acc * (1.0 / jnp.maximum(jnp.sum(l8, axis=0, keepdims=True), TINY))

        gt = gt_ref[0, g]
        part_scr[g] = gt[0:1] * o_cmp + gt[2:3] * o_win
        yield

        ktp = jnp.maximum(qb - 1, 0)
        s_prev = add_block_pen(_dot(ks_ref[0, g, tile_rows(ktp), :], qT)
                               + tb_ref[g, jnp.where(qb >= 1, TB_PREV, TB_NONE)], pen_ref, g, ktp)
        s_scr[g, tile_rows(ktp), :] = s_prev
        s_own = add_block_pen(_dot(ks_ref[0, g, tile_rows(qb), :], qT) + tb_ref[g, TB_OWN], pen_ref, g, qb)
        s_scr[g, tile_rows(qb), :] = s_own
        m8_scr[g] = jnp.maximum(_fold8(s_prev, jnp.max), _fold8(s_own, jnp.max))
        l8_scr[g] = jnp.zeros((8, RT), f32)
        acc_scr[g] = jnp.zeros((HEAD_DIM, RT), f32)

    groups = [group_phases(g) for g in range(G)]
    while groups:
        groups = [phases for phases in groups if next(phases, False) is None]

    def pass_a(i, c):
        for g in range(G):
            qT = q_ref[0, g]
            m8 = m8_scr[g]
            for u in range(SLC_UNROLL):
                kt = SLC_UNROLL * i + u
                live = kt < qb - 1
                ktc = jnp.minimum(kt, nkt - 1)
                s = add_block_pen(_dot(ks_ref[0, g, tile_rows(ktc), :], qT) + jnp.where(live, 0.0, NEG),
                                  penf_ref, g, ktc)
                s_scr[g, tile_rows(jnp.where(live, kt, nkt + u)), :] = s
                m8 = jnp.maximum(m8, _fold8(s, jnp.max))
            m8_scr[g] = m8
        return c

    lax.fori_loop(0, (jnp.maximum(qb - 1, 0) + SLC_UNROLL - 1) // SLC_UNROLL, pass_a, 0)

    def pass_b(i, c):
        for g in range(G):
            m = jnp.max(m8_scr[g], axis=0, keepdims=True)
            l8 = l8_scr[g]
            acc = acc_scr[g]
            for u in range(SLC_UNROLL):
                kt = SLC_UNROLL * i + u
                ktc = jnp.minimum(kt, qb)
                p_u = jnp.exp2(s_scr[g, tile_rows(ktc), :] - (m + jnp.where(kt <= qb, 0.0, -NEG)))
                l8 = l8 + _fold8(p_u, jnp.sum)
                acc = acc + _dot(vst_ref[0, g, ktc], p_u.astype(bf16))
            l8_scr[g] = l8
            acc_scr[g] = acc
        return c

    lax.fori_loop(0, qb // SLC_UNROLL + 1, pass_b, 0)

    for g in range(G):
        o_slc = acc_scr[g] * (1.0 / jnp.maximum(jnp.sum(l8_scr[g], axis=0, keepdims=True), TINY))
        o_ref[0, g] = (part_scr[g] + gt_ref[0, g][1:2] * o_slc).astype(bf16)


def _bias_tables(rel_bias, S):
    G = N_KV_HEADS
    H = rel_bias.shape[1]
    r = H // G
    T = KEY_TILE
    n = jnp.arange(S + 2 * T, dtype=i32)
    max_exact = N_BUCKETS // 2
    nf = jnp.maximum(n, 1).astype(f32)
    large = max_exact + (jnp.log(nf / max_exact) / math.log(MAX_DISTANCE / max_exact)
                         * (N_BUCKETS - max_exact)).astype(i32)
    bucket = jnp.where(n < max_exact, n, jnp.minimum(large, N_BUCKETS - 1))
    bias1d = rel_bias.astype(f32)[bucket] * LOG2E

    def per_group(tab):
        lead = tab.shape[:-2]
        k = len(lead)
        t = jnp.moveaxis(tab, -1, 0).reshape((G, r) + lead + (T,))
        return jnp.moveaxis(t, 1, k + 1).reshape((G,) + lead + (r * T,))

    def masked(dist, ok):
        return jnp.where(jnp.asarray(ok)[..., None], bias1d[np.maximum(dist, 0)], NEG)

    def toeplitz(first):
        w = bias1d[np.maximum(first - (T - 1) + np.arange(2 * T), 0)]
        skew = jnp.broadcast_to(w[None], (T, 2 * T, H)).reshape(2 * T * T, H)[:T * (2 * T - 1)]
        return skew.reshape(T, 2 * T - 1, H)[:, T - 1:]

    kj, qi = np.arange(T)[:, None], np.arange(T)[None, :]
    ok = [qi - kj >= 0, np.ones((T, T), bool), 2 * T + qi - kj < WINDOW]
    tb = jnp.stack([jnp.where(jnp.asarray(ok[d])[..., None], toeplitz(d * T), NEG) for d in range(3)]
                   + [jnp.full((T, T, H), NEG, f32)])
    cend = (np.arange(CMP_NEAR)[:, None] - CMP_PAD) * CMP_STRIDE + CMP_LEN - 1
    dcn = np.arange(T)[None, :] - cend
    rows = CMP_PAD + S // CMP_STRIDE
    far = jnp.broadcast_to(bias1d[-1][None, None, :], (rows, T, H))
    cstep = jnp.concatenate([far, jnp.full((rows, T, H), NEG, f32)])
    return per_group(tb), per_group(masked(dcn, dcn >= 0)), per_group(cstep)


def _nsa_attention(q, gates_t, shared, tables, B, S):
    kcm, vct, ks, vst, kw, vwt = shared
    tb, cbn, cstep = tables
    G, DH, T = N_KV_HEADS, HEAD_DIM, Q_BLOCK
    RT = q.shape[3]
    r = RT // T
    nqb = S // T
    nch = S // CMP_STRIDE
    nsb = S // SLC_LEN
    nkt = S // KEY_TILE
    cmp_start = np.arange(nch) * CMP_STRIDE
    slc_start = np.arange(nsb) * SLC_LEN
    overlap = np.clip(np.minimum(cmp_start[:, None] + CMP_LEN, slc_start[None, :] + SLC_LEN)
                      - np.maximum(cmp_start[:, None], slc_start[None, :]), 0, None) / CMP_LEN
    overlap[nch - 1] = 0.0
    c2st = jnp.asarray(overlap.T, bf16)
    qmap = lambda b, i: (b * nqb + i, 0, 0, 0)
    bat = lambda b, i: (b, 0, 0, 0)
    bat5 = lambda b, i: (b, 0, 0, 0, 0)
    once = pl.Buffered(1)
    return pl.pallas_call(
        functools.partial(_nsa_kernel, r=r, nsb=nsb, n_sel=min(N_SEL, nsb)),
        out_shape=jax.ShapeDtypeStruct(q.shape, bf16),
        grid=(B, nqb),
        in_specs=[
            pl.BlockSpec((1, G, DH, RT), qmap),
            pl.BlockSpec((1, G, N_BRANCH, RT), qmap),
            pl.BlockSpec((1, G, nch, DH), bat),
            pl.BlockSpec((1, G, DH, nch), bat),
            pl.BlockSpec((1, G, S, DH), bat),
            pl.BlockSpec((1, G, nkt, DH, KEY_TILE), bat5),
            pl.BlockSpec((1, G, S, DH), bat),
            pl.BlockSpec((1, G, nkt, DH, KEY_TILE), bat5),
            pl.BlockSpec(cbn.shape, lambda b, i: (0, 0, 0), pipeline_mode=once),
            pl.BlockSpec(cstep.shape, lambda b, i: (0, 0, 0), pipeline_mode=once),
            pl.BlockSpec(tb.shape, lambda b, i: (0, 0, 0, 0), pipeline_mode=once),
            pl.BlockSpec((nsb, nch), lambda b, i: (0, 0), pipeline_mode=once),
        ],
        out_specs=pl.BlockSpec((1, G, DH, RT), qmap),
        scratch_shapes=[pltpu.VMEM((G, CMP_PAD + nch, RT), f32), pltpu.VMEM((G, nsb, RT), f32),
                        pltpu.VMEM((G, nsb, RT), f32), pltpu.VMEM((G, (nkt + SLC_UNROLL) * KEY_TILE, RT), f32),
                        pltpu.VMEM((G, 8, RT), f32), pltpu.VMEM((G, 8, RT), f32), pltpu.VMEM((G, DH, RT), f32),
                        pltpu.VMEM((G, DH, RT), f32)],
        compiler_params=_cparams("parallel", "arbitrary"),
        name="nsa_attention",
    )(q, gates_t, kcm, vct, ks, vst, kw, vwt, cbn, cstep, tb, c2st)


def _oproj_kernel(x_ref, a_ref, w_ref, o_ref):
    nt, G, DH, RT = a_ref.shape
    T = x_ref.shape[0] // nt
    r = RT // T
    rows = []
    for t in range(nt):
        cols = []
        for g in range(G):
            a = a_ref[t, g].astype(f32)
            cols.append(jnp.concatenate([a[:, k * T:(k + 1) * T] for k in range(r)], axis=0).T)
        rows.append(jnp.concatenate(cols, axis=1))
    attn = jnp.concatenate(rows, axis=0)
    o_ref[...] = x_ref[...] + _dot(attn.astype(bf16), w_ref[...])


def _out_project(xf, attn, w_o):
    N, D = xf.shape
    nt_all, G, DH, RT = attn.shape
    T = N // nt_all
    to = min(512, N)
    assert N % to == 0 and to % T == 0
    row = lambda i: (i, 0)
    return pl.pallas_call(
        _oproj_kernel,
        out_shape=jax.ShapeDtypeStruct((N, D), f32),
        grid=(N // to,),
        in_specs=[pl.BlockSpec((to, D), row), pl.BlockSpec((to // T, G, DH, RT), lambda i: (i, 0, 0, 0)),
                  pl.BlockSpec(w_o.shape, lambda i: (0, 0))],
        out_specs=pl.BlockSpec((to, D), row),
        compiler_params=_cparams("parallel"),
        name="out_project",
    )(xf, attn, w_o.astype(bf16))


def _nsa_layer(xf, B, S, g, w_qg, g_q, w_o, shared, tables):
    N = xf.shape[0]
    G, T = N_KV_HEADS, Q_BLOCK
    q, gates = _qg_project(xf, g, w_qg, g_q)
    r = gates.shape[1] // (G * N_BRANCH)
    gates_t = gates.reshape(N // T, T, G, r, N_BRANCH).transpose(0, 2, 4, 3, 1).reshape(N // T, G, N_BRANCH, r * T)
    attn = _nsa_attention(q, gates_t, shared, tables, B, S)
    return _out_project(xf, attn, w_o)


def kernel(x, rel_bias, norm_mix, norm_ffn, pool_w, pool_scale, norm_kv, w_kv, g_k, cmp_pos, cmp_w1, cmp_w2,
           w_qg, g_q, w_o, ffn_wg, ffn_wu, ffn_wd, router, moe_wg, moe_wu, moe_wd):
    B, S, D = x.shape
    depth = norm_mix.shape[0]
    n_a = depth // 2
    assert S % Q_BLOCK == 0
    xf = x.reshape(B * S, D)
    shared = None
    tables = _bias_tables(rel_bias, S)
    n_exp = moe_wg.shape[1]
    moe_w = [w.reshape((-1,) + w.shape[2:]) for w in (moe_wg, moe_wu, moe_wd)]
    for layer in range(depth):
        if layer < n_a:
            xf = _pool_layer(xf.reshape(B, S, D), norm_mix[layer], pool_w[layer], pool_scale[layer]).reshape(B * S, D)
        else:
            j = layer - n_a
            xf = _nsa_layer(xf, B, S, norm_mix[layer], w_qg[j], g_q[j], w_o[j], shared, tables)
        i = layer // 2
        if layer % 2 == 0:
            xf = _ffn_layer(xf, norm_ffn[layer], ffn_wg[i], ffn_wu[i], ffn_wd[i])
        else:
            xf = _moe_layer(xf, norm_ffn[layer], router[i], *moe_w, first_expert=i * n_exp)
        if layer == n_a - 1:
            shared = _build_shared(xf, B, S, norm_kv, w_kv, g_k, cmp_pos, cmp_w1, cmp_w2)
    return xf.reshape(B, S, D)
```
